```python
import math
import jax
import jax.numpy as jnp
from jax import lax
import numpy as np

D_MODEL = 1024
BATCH = 4
SEQ = 4096
DEPTH = 1

D_MIX = D_MODEL
CONV_WIDTH = D_MIX // 2
CONV_GROUPS = 8
CONV_K = 3
GLA_WIDTH = D_MIX - CONV_WIDTH
GLA_HEADS = 4
GLA_DV = GLA_WIDTH // GLA_HEADS
GLA_DK = GLA_DV // 2
GLA_RANK = 16
GLA_NORMALIZER = 16.0
GLA_CHUNK = 64
N_EXPERTS = 32
TOP_K = 4
D_FF = D_MODEL
SWIGLU_LIMIT = 7.0
SWIGLU_ALPHA = 1.702
MOE_BLOCK = 256
RMS_EPS = 1e-5
D_IN = 3 * CONV_WIDTH + 2 * GLA_HEADS * GLA_DK + 2 * GLA_WIDTH + GLA_RANK

kernel_name = "hybrid_conv_gla_moe_block"


def rmsnorm(x, g):
    xf = x.astype(jnp.float32)
    y = xf * lax.rsqrt(jnp.mean(xf * xf, axis=-1, keepdims=True) + RMS_EPS)
    return (y * g.astype(jnp.float32)).astype(x.dtype)


def gla_chunked(q, k, v, log_a):
    bsz, nh, s, dk = q.shape
    dv = v.shape[-1]
    n = s // GLA_CHUNK
    q = q.reshape(bsz, nh, n, GLA_CHUNK, dk)
    k = k.reshape(bsz, nh, n, GLA_CHUNK, dk)
    v = v.reshape(bsz, nh, n, GLA_CHUNK, dv)
    b = jnp.cumsum(log_a.reshape(bsz, nh, n, GLA_CHUNK, dk), axis=3)
    b_last = b[..., -1:, :]
    q_dec = q * jnp.exp(b)
    k_dec = k * jnp.exp(-b)
    mask = jnp.tril(jnp.ones((GLA_CHUNK, GLA_CHUNK), dtype=bool))
    scores = jnp.where(mask, jnp.einsum('bhncd,bhnjd->bhncj', q_dec, k_dec), 0.0)
    o_intra = jnp.einsum('bhncj,bhnjv->bhncv', scores, v)
    kv = jnp.einsum('bhncd,bhncv->bhndv', k * jnp.exp(b_last - b), v)
    decay = jnp.exp(b_last[..., 0, :])

    def step(state, inp):
        dec, kv_n = inp
        return dec[..., None] * state + kv_n, state

    init = jnp.zeros((bsz, nh, dk, dv), q.dtype)
    _, s_before = lax.scan(step, init, (jnp.moveaxis(decay, 2, 0), jnp.moveaxis(kv, 2, 0)))
    s_before = jnp.moveaxis(s_before, 0, 2)
    o_inter = jnp.einsum('bhncd,bhndv->bhncv', q_dec, s_before)
    return (o_intra + o_inter).reshape(bsz, nh, s, dv)


def hybrid_mixer(h, w_in, conv_w, w_gk_up, b_gk_up, gla_norm_g, w_out):
    bsz, s, _ = h.shape
    sizes = [CONV_WIDTH, CONV_WIDTH, CONV_WIDTH, GLA_HEADS * GLA_DK, GLA_HEADS * GLA_DK,
             GLA_WIDTH, GLA_WIDTH, GLA_RANK]
    splits = [int(v) for v in np.cumsum(sizes)[:-1]]
    proj = h @ w_in
    u_h, gate_b, gate_c, q, k, v, g_out, gk_low = jnp.split(proj, splits, axis=-1)

    u = gate_c * u_h
    u_pad = jnp.pad(u, ((0, 0), (CONV_K - 1, 0), (0, 0)))
    conv = conv_w[0] * u_pad[:, 0:s]
    for i in range(1, CONV_K):
        conv = conv + conv_w[i] * u_pad[:, i:i + s]
    y_conv = gate_b * conv

    gk = gk_low @ w_gk_up + b_gk_up
    log_a = jax.nn.log_sigmoid(gk.astype(jnp.float32)) / GLA_NORMALIZER

    def heads(t, d):
        return t.astype(jnp.float32).reshape(bsz, s, GLA_HEADS, d).transpose(0, 2, 1, 3)

    o = gla_chunked(heads(q, GLA_DK) * (GLA_DK ** -0.5), heads(k, GLA_DK),
                    heads(v, GLA_DV), heads(log_a, GLA_DK))
    o = rmsnorm(o, gla_norm_g)
    o = o.transpose(0, 2, 1, 3).reshape(bsz, s, GLA_WIDTH).astype(h.dtype)
    y_gla = o * jax.nn.silu(g_out)

    return jnp.concatenate([y_conv, y_gla], axis=-1) @ w_out


def moe(h, w_router, b_router, w_gate_up, b_gate_up, w_down, b_down):
    bsz, s, d = h.shape
    t = bsz * s
    xf = h.reshape(t, d)
    logits = (xf @ w_router + b_router).astype(jnp.float32)
    top_v, top_i = lax.top_k(logits, TOP_K)
    gates = jax.nn.softmax(top_v, axis=-1).astype(h.dtype)

    tk = t * TOP_K
    e_flat = top_i.reshape(tk).astype(jnp.int32)
    tok_flat = jnp.repeat(jnp.arange(t, dtype=jnp.int32), TOP_K)
    gate_flat = gates.reshape(tk)

    counts = jnp.zeros((N_EXPERTS,), jnp.int32).at[e_flat].add(1)
    padded = ((counts + MOE_BLOCK - 1) // MOE_BLOCK) * MOE_BLOCK
    pad_end = jnp.cumsum(padded)
    pad_start = pad_end - padded
    grp_start = jnp.cumsum(counts) - counts
    order = jnp.argsort(e_flat, stable=True)
    e_sorted = e_flat[order]
    dest = pad_start[e_sorted] + (jnp.arange(tk, dtype=jnp.int32) - grp_start[e_sorted])

    n_rows = (-(-tk // MOE_BLOCK)) * MOE_BLOCK + N_EXPERTS * MOE_BLOCK
    n_blocks = n_rows // MOE_BLOCK
    row_tok = jnp.full((n_rows,), t, jnp.int32).at[dest].set(tok_flat[order])
    row_gate = jnp.zeros((n_rows,), h.dtype).at[dest].set(gate_flat[order])
    block_expert = jnp.minimum(
        jnp.searchsorted(pad_end, jnp.arange(n_blocks, dtype=jnp.int32) * MOE_BLOCK, side='right'),
        N_EXPERTS - 1).astype(jnp.int32)

    x_pad = jnp.concatenate([xf, jnp.zeros((1, d), xf.dtype)], axis=0)
    x_blocks = x_pad[row_tok].reshape(n_blocks, MOE_BLOCK, d)

    def expert_block(args):
        xb, eid = args
        gu = xb @ w_gate_up[eid] + b_gate_up[eid]
        gate = jnp.minimum(gu[:, :D_FF], SWIGLU_LIMIT)
        up = jnp.clip(gu[:, D_FF:], -SWIGLU_LIMIT, SWIGLU_LIMIT)
        glu = gate * jax.nn.sigmoid(gate * SWIGLU_ALPHA)
        return ((up + 1.0) * glu) @ w_down[eid] + b_down[eid]

    out = lax.map(expert_block, (x_blocks, block_expert)).reshape(n_rows, d)
    y = jnp.zeros((t + 1, d), h.dtype).at[row_tok].add(out * row_gate[:, None])[:t]
    return y.reshape(bsz, s, d)


def setup_inputs(seed: int = 0) -> dict:
    key = jax.random.key(seed)
    ks = jax.random.split(key, 17)
    f32 = jnp.float32

    def nrm(k, shape, scale):
        return jax.random.normal(k, shape, f32) * scale

    return {
        "x": nrm(ks[0], (BATCH, SEQ, D_MODEL), 1.0),
        "norm_mix_g": 1.0 + nrm(ks[1], (DEPTH, D_MODEL), 0.02),
        "w_in": nrm(ks[2], (DEPTH, D_MODEL, D_IN), D_MODEL ** -0.5),
        "conv_w": nrm(ks[3], (DEPTH, CONV_K, CONV_WIDTH), CONV_K ** -0.5),
        "w_gk_up": nrm(ks[4], (DEPTH, GLA_RANK, GLA_HEADS * GLA_DK), GLA_RANK ** -0.5),
        "b_gk_up": nrm(ks[5], (DEPTH, GLA_HEADS * GLA_DK), 0.1),
        "gla_norm_g": 1.0 + nrm(ks[6], (DEPTH, GLA_DV), 0.02),
        "w_out": nrm(ks[7], (DEPTH, D_MIX, D_MODEL), D_MIX ** -0.5),
        "norm_ffn_g": 1.0 + nrm(ks[8], (DEPTH, D_MODEL), 0.02),
        "w_router": nrm(ks[9], (DEPTH, D_MODEL, N_EXPERTS), D_MODEL ** -0.5),
        "b_router": nrm(ks[10], (DEPTH, N_EXPERTS), 0.01),
        "w_gate_up": nrm(ks[11], (DEPTH, N_EXPERTS, D_MODEL, 2 * D_FF), D_MODEL ** -0.5),
        "b_gate_up": nrm(ks[12], (DEPTH, N_EXPERTS, 2 * D_FF), 0.02),
        "w_down": nrm(ks[13], (DEPTH, N_EXPERTS, D_FF, D_MODEL), D_FF ** -0.5),
        "b_down": nrm(ks[14], (DEPTH, N_EXPERTS, D_MODEL), 0.02),
        "norm_final_g": 1.0 + nrm(ks[15], (D_MODEL,), 0.02),
    }


def reference(x, norm_mix_g, w_in, conv_w, w_gk_up, b_gk_up, gla_norm_g, w_out,
              norm_ffn_g, w_router, b_router, w_gate_up, b_gate_up, w_down, b_down,
              norm_final_g):
    for l in range(DEPTH):
        h = rmsnorm(x, norm_mix_g[l])
        x = x + hybrid_mixer(h, w_in[l], conv_w[l], w_gk_up[l], b_gk_up[l],
                             gla_norm_g[l], w_out[l])
        h = rmsnorm(x, norm_ffn_g[l])
        x = x + moe(h, w_router[l], b_router[l], w_gate_up[l], b_gate_up[l],
                    w_down[l], b_down[l])
    return rmsnorm(x, norm_final_g)
```

```python
import functools

import jax
import jax.numpy as jnp
from jax import lax
from jax.experimental import pallas as pl
from jax.experimental.pallas import tpu as pltpu

F32 = jnp.float32
BF16 = jnp.bfloat16

D_MODEL = 1024
CONV_WIDTH = 512
CONV_K = 3
GLA_WIDTH = 512
GLA_HEADS = 4
GLA_DV = 128
GLA_DK = 64
GLA_QK = GLA_HEADS * GLA_DK
GLA_RANK = 16
GLA_NORMALIZER = 16.0
GLA_CHUNK = 64
N_EXPERTS = 32
TOP_K = 4
D_FF = 1024
SWIGLU_LIMIT = 7.0
SWIGLU_ALPHA = 1.702
RMS_EPS = 1e-5

LANES = 128
SUBLANES = 8

OFF_UH = 0
OFF_GB = OFF_UH + CONV_WIDTH
OFF_GC = OFF_GB + CONV_WIDTH
OFF_Q = OFF_GC + CONV_WIDTH
OFF_K = OFF_Q + GLA_QK
OFF_V = OFF_K + GLA_QK
OFF_GO = OFF_V + GLA_WIDTH
OFF_GKL = OFF_GO + GLA_WIDTH
D_IN_PAD = OFF_GKL + LANES

SEQ_TILE = 512
MOE_BLOCK = 256
COMBINE_TILE = 512
NEG_BIG = -1e30
VMEM_LIMIT = 56 * 1024 * 1024


def _rms(x, g):
    return x * lax.rsqrt(jnp.mean(x * x, axis=-1, keepdims=True) + RMS_EPS) * g


def _dot(a, b):
    return jnp.dot(a, b, preferred_element_type=F32)


def _dot_nt(a, b):
    return lax.dot_general(a, b, (((1,), (1,)), ((), ())), preferred_element_type=F32)


def _split_bf16(x):
    hi = x.astype(BF16)
    lo = (x - hi.astype(F32)).astype(BF16)
    return hi, lo


def _mixer_kernel(x_ref, g1_ref, win_ref, convw_ref, wgk_ref, bgk_ref, gng_ref, wout_ref,
                  g2_ref, wrh_ref, wrl_ref, br_ref,
                  x1_ref, h2_ref, meta_ref, cnt_ref,
                  proj_ref, ubuf_ref, la_ref, ycat_ref, state_ref, carry_ref):
    ts = x_ref.shape[1]
    b_idx = pl.program_id(0)
    s_idx = pl.program_id(1)

    @pl.when(s_idx == 0)
    def _():
        state_ref[...] = jnp.zeros_like(state_ref)
        ubuf_ref[0:SUBLANES, :] = jnp.zeros((SUBLANES, CONV_WIDTH), F32)

    @pl.when((s_idx == 0) & (b_idx == 0))
    def _():
        carry_ref[...] = jnp.zeros_like(carry_ref)

    x = x_ref[0]
    h = _rms(x, g1_ref[...]).astype(BF16)
    proj_ref[...] = _dot(h, win_ref[...])

    u = proj_ref[:, OFF_GC:OFF_GC + CONV_WIDTH] * proj_ref[:, OFF_UH:OFF_UH + CONV_WIDTH]
    ubuf_ref[SUBLANES:SUBLANES + ts, :] = u
    u1 = ubuf_ref[pl.ds(SUBLANES - 1, ts), :]
    u2 = ubuf_ref[pl.ds(SUBLANES - 2, ts), :]
    conv = convw_ref[0:1, :] * u2 + convw_ref[1:2, :] * u1 + convw_ref[2:3, :] * u
    ycat_ref[:, 0:CONV_WIDTH] = (proj_ref[:, OFF_GB:OFF_GB + CONV_WIDTH] * conv).astype(BF16)
    ubuf_ref[0:SUBLANES, :] = ubuf_ref[ts:ts + SUBLANES, :]

    gk = _dot(proj_ref[:, OFF_GKL:OFF_GKL + LANES].astype(BF16), wgk_ref[...]) + bgk_ref[...]
    log_sig = jnp.minimum(gk, 0.0) - jnp.log1p(jnp.exp(-jnp.abs(gk)))
    la_ref[...] = log_sig / GLA_NORMALIZER

    ci = lax.broadcasted_iota(jnp.int32, (GLA_CHUNK, GLA_CHUNK), 0)
    cj = lax.broadcasted_iota(jnp.int32, (GLA_CHUNK, GLA_CHUNK), 1)
    tri_incl = (cj <= ci).astype(BF16)
    causal = cj <= ci
    causal4 = jnp.concatenate([causal] * GLA_HEADS, axis=0)
    lane_qk = lax.broadcasted_iota(jnp.int32, (1, GLA_QK), 1)
    head_masks = [((lane_qk >= hd * GLA_DK) & (lane_qk < (hd + 1) * GLA_DK)).astype(F32)
                  for hd in range(GLA_HEADS)]
    gng = gng_ref[...]

    def chunk_body(c, carry):
        r0 = pl.multiple_of(c * GLA_CHUNK, GLA_CHUNK)
        rows = pl.ds(r0, GLA_CHUNK)
        la_hi, la_lo = _split_bf16(la_ref[rows, :])
        bcum = _dot(tri_incl, la_hi) + _dot(tri_incl, la_lo)
        blast = bcum[GLA_CHUNK - 1:GLA_CHUNK, :]
        q = proj_ref[rows, OFF_Q:OFF_Q + GLA_QK] * (GLA_DK ** -0.5)
        k = proj_ref[rows, OFF_K:OFF_K + GLA_QK]
        v = proj_ref[rows, OFF_V:OFF_V + GLA_WIDTH].astype(BF16)
        qd = q * jnp.exp(bcum)
        kd = (k * jnp.exp(-bcum)).astype(BF16)
        kr = k * jnp.exp(blast - bcum)

        q_stack = jnp.concatenate([qd * m for m in head_masks], axis=0).astype(BF16)
        scores = jnp.where(causal4, _dot_nt(q_stack, kd), 0.0).astype(BF16)

        state = state_ref[...]
        o_inter = _dot(qd.astype(BF16), state.astype(BF16))
        o_intra = jnp.concatenate(
            [_dot(scores[hd * GLA_CHUNK:(hd + 1) * GLA_CHUNK, :],
                  v[:, hd * GLA_DV:(hd + 1) * GLA_DV]) for hd in range(GLA_HEADS)], axis=1)
        o = o_inter + o_intra

        kt = jnp.concatenate([kr, jnp.broadcast_to(blast, (GLA_CHUNK, GLA_QK))], axis=0).T
        dcol = jnp.exp(kt[:, GLA_CHUNK:GLA_CHUNK + 1])
        lane_c = lax.broadcasted_iota(jnp.int32, (GLA_QK, 2 * GLA_CHUNK), 1)
        kt_b = jnp.where(lane_c < GLA_CHUNK, kt, 0.0).astype(BF16)
        v_pad = jnp.concatenate([v, jnp.zeros_like(v)], axis=0)
        for hd in range(GLA_HEADS):
            rs = slice(hd * GLA_DK, (hd + 1) * GLA_DK)
            cs = slice(hd * GLA_DV, (hd + 1) * GLA_DV)
            kv = _dot(kt_b[rs, :], v_pad[:, cs])
            state_ref[rs, cs] = dcol[rs, :] * state[rs, cs] + kv

        g_out = proj_ref[rows, OFF_GO:OFF_GO + GLA_WIDTH]
        o_n = jnp.concatenate(
            [_rms(o[:, hd * GLA_DV:(hd + 1) * GLA_DV], gng) for hd in range(GLA_HEADS)], axis=1)
        y = o_n * (g_out * jax.nn.sigmoid(g_out))
        ycat_ref[rows, CONV_WIDTH:CONV_WIDTH + GLA_WIDTH] = y.astype(BF16)
        return carry

    lax.fori_loop(0, ts // GLA_CHUNK, chunk_body, 0)

    x1 = x + _dot(ycat_ref[...], wout_ref[...])
    x1_ref[0] = x1
    h2 = _rms(x1, g2_ref[...])
    h2_ref[0] = h2
    h2_hi, h2_lo = _split_bf16(h2)
    logits = (_dot(h2_hi, wrh_ref[...]) + _dot(h2_lo, wrh_ref[...]) + _dot(h2_hi, wrl_ref[...])
              + br_ref[...])

    lane = lax.broadcasted_iota(jnp.int32, (ts, LANES), 1).astype(F32)
    work = logits
    sel = jnp.zeros((ts, LANES), F32)
    top_v, top_i, top_oh = [], [], []
    for _ in range(TOP_K):
        m = jnp.max(work, axis=-1, keepdims=True)
        idx = jnp.min(jnp.where(work == m, lane, float(LANES)), axis=-1, keepdims=True)
        oh = lane == idx
        top_v.append(m)
        top_i.append(idx)
        top_oh.append(oh)
        sel = sel + oh.astype(F32)
        work = jnp.where(oh, -jnp.inf, work)
    exps = [jnp.exp(tv - top_v[0]) for tv in top_v]
    denom = exps[0] + exps[1] + exps[2] + exps[3]
    gates = [e / denom for e in exps]

    ti = lax.broadcasted_iota(jnp.int32, (ts, ts), 0)
    tj = lax.broadcasted_iota(jnp.int32, (ts, ts), 1)
    strict_lower = (tj < ti).astype(BF16)
    prefix = _dot(strict_lower, sel.astype(BF16)) + carry_ref[0:1, :]
    ranks = [jnp.sum(jnp.where(oh, prefix, 0.0), axis=-1, keepdims=True) for oh in top_oh]
    new_carry = carry_ref[0:1, :] + jnp.sum(sel, axis=0, keepdims=True)
    carry_ref[...] = jnp.broadcast_to(new_carry, carry_ref.shape)
    cnt_ref[...] = jnp.broadcast_to(new_carry, cnt_ref.shape)

    meta = jnp.zeros((ts, LANES), F32)
    for kk in range(TOP_K):
        meta = jnp.where(lane == float(kk), top_i[kk], meta)
        meta = jnp.where(lane == float(TOP_K + kk), gates[kk], meta)
        meta = jnp.where(lane == float(2 * TOP_K + kk), ranks[kk], meta)
    meta_ref[0] = meta


def _mixer_call(x, g1, win, convw, wgk, bgk, gng, wout, g2, wrh, wrl, br):
    bsz, seq, d = x.shape
    ts = SEQ_TILE
    grid = (bsz, seq // ts)

    def const(shape):
        return pl.BlockSpec(shape, lambda b, s: (0,) * len(shape))

    tile = lambda w: pl.BlockSpec((1, ts, w), lambda b, s: (b, s, 0))
    return pl.pallas_call(
        _mixer_kernel,
        grid=grid,
        in_specs=[tile(d), const(g1.shape), const(win.shape), const(convw.shape),
                  const(wgk.shape), const(bgk.shape), const(gng.shape), const(wout.shape),
                  const(g2.shape), const(wrh.shape), const(wrl.shape), const(br.shape)],
        out_specs=[tile(d), tile(d), tile(LANES), const((SUBLANES, LANES))],
        out_shape=[jax.ShapeDtypeStruct((bsz, seq, d), F32),
                   jax.ShapeDtypeStruct((bsz, seq, d), F32),
                   jax.ShapeDtypeStruct((bsz, seq, LANES), F32),
                   jax.ShapeDtypeStruct((SUBLANES, LANES), F32)],
        scratch_shapes=[pltpu.VMEM((ts, D_IN_PAD), F32),
                        pltpu.VMEM((ts + SUBLANES, CONV_WIDTH), F32),
                        pltpu.VMEM((ts, GLA_QK), F32),
                        pltpu.VMEM((ts, D_MODEL), BF16),
                        pltpu.VMEM((GLA_QK, GLA_WIDTH), F32),
                        pltpu.VMEM((SUBLANES, LANES), F32)],
        compiler_params=pltpu.CompilerParams(
            dimension_semantics=("arbitrary", "arbitrary"), vmem_limit_bytes=VMEM_LIMIT),
        name="mixer",
    )(x, g1, win, convw, wgk, bgk, gng, wout, g2, wrh, wrl, br)


CAST_ROWS = 128


def _experts_kernel(bexp_ref, first_ref, nxt_ref, nreal_ref,
                    src_cur_ref, src_nxt_ref, dst_ref, h2_hbm, wgu_hbm, bgu_ref, wd_hbm, bd_ref,
                    slots_hbm,
                    xbuf, obuf, wgu_stage, wd_stage, wgu_bf, wd_bf, act_ref,
                    gsem, ssem, wsem):
    i = pl.program_id(0)
    nreal = nreal_ref[0]
    bm = xbuf.shape[1]
    slot = i % 2

    def weight_copies(e):
        return (pltpu.make_async_copy(wgu_hbm.at[e], wgu_stage, wsem.at[0]),
                pltpu.make_async_copy(wd_hbm.at[e], wd_stage, wsem.at[1]))

    def start_gather(idx_ref, buf_slot):
        def body(j, carry):
            tok = idx_ref[0, 0, j]
            pltpu.make_async_copy(h2_hbm.at[pl.ds(tok, 1), :],
                                  xbuf.at[buf_slot, pl.ds(j, 1), :], gsem.at[buf_slot]).start()
            return carry
        lax.fori_loop(0, bm, body, 0, unroll=8)

    def wait_gather(buf_slot):
        pltpu.make_async_copy(h2_hbm.at[pl.ds(0, bm), :], xbuf.at[buf_slot],
                              gsem.at[buf_slot]).wait()

    def start_scatter(buf_slot):
        def body(j, carry):
            row = dst_ref[0, 0, j]
            pltpu.make_async_copy(obuf.at[buf_slot, pl.ds(j, 1), :],
                                  slots_hbm.at[pl.ds(row, 1), :], ssem.at[buf_slot]).start()
            return carry
        lax.fori_loop(0, bm, body, 0, unroll=8)

    def wait_scatter(buf_slot):
        pltpu.make_async_copy(obuf.at[buf_slot], slots_hbm.at[pl.ds(0, bm), :],
                              ssem.at[buf_slot]).wait()

    @pl.when(i < nreal)
    def _():
        e = bexp_ref[i]

        @pl.when(i == 0)
        def _():
            for cp in weight_copies(e):
                cp.start()
            start_gather(src_cur_ref, 0)

        @pl.when(first_ref[i] == 1)
        def _():
            for cp in weight_copies(e):
                cp.wait()

            def cast_gu(r, carry):
                rows = pl.ds(pl.multiple_of(r * CAST_ROWS, CAST_ROWS), CAST_ROWS)
                wgu_bf[rows, :] = wgu_stage[rows, :].astype(BF16)
                return carry

            def cast_d(r, carry):
                rows = pl.ds(pl.multiple_of(r * CAST_ROWS, CAST_ROWS), CAST_ROWS)
                wd_bf[rows, :] = wd_stage[rows, :].astype(BF16)
                return carry

            lax.fori_loop(0, D_MODEL // CAST_ROWS, cast_gu, 0)
            lax.fori_loop(0, D_FF // CAST_ROWS, cast_d, 0)

            @pl.when(nxt_ref[i] >= 0)
            def _():
                for cp in weight_copies(nxt_ref[i]):
                    cp.start()

        @pl.when(i + 1 < nreal)
        def _():
            start_gather(src_nxt_ref, 1 - slot)

        wait_gather(slot)

        @pl.when(i >= 2)
        def _():
            wait_scatter(slot)

        xb = xbuf[slot].astype(BF16)
        fc = 512
        for f0 in range(0, D_FF, fc):
            gate = _dot(xb, wgu_bf[:, f0:f0 + fc]) + bgu_ref[0, :, f0:f0 + fc]
            up = _dot(xb, wgu_bf[:, D_FF + f0:D_FF + f0 + fc]) + bgu_ref[0, :, D_FF + f0:D_FF + f0 + fc]
            gate = jnp.minimum(gate, SWIGLU_LIMIT)
            up = jnp.clip(up, -SWIGLU_LIMIT, SWIGLU_LIMIT)
            glu = gate * jax.nn.sigmoid(gate * SWIGLU_ALPHA)
            act_ref[:, f0:f0 + fc] = ((up + 1.0) * glu).astype(BF16)
        obuf[slot] = _dot(act_ref[...], wd_bf[...]) + bd_ref[0]
        start_scatter(slot)

        @pl.when(i == nreal - 1)
        def _():
            @pl.when(i >= 1)
            def _():
                wait_scatter(1 - slot)
            wait_scatter(slot)


def _experts_call(bexp, first, nxt, nreal, row_src, row_dst, h2, wgu, bgu, wd, bd, n_slot_rows):
    n_blocks, bm = row_src.shape
    t, d = h2.shape
    row_src3 = row_src.reshape(n_blocks, 1, bm)
    row_dst3 = row_dst.reshape(n_blocks, 1, bm)
    bgu3 = bgu.reshape(N_EXPERTS, 1, 2 * D_FF)
    bd3 = bd.reshape(N_EXPERTS, 1, D_MODEL)

    smem_blk = lambda fn: pl.BlockSpec((1, 1, bm), fn, memory_space=pltpu.SMEM)
    grid_spec = pltpu.PrefetchScalarGridSpec(
        num_scalar_prefetch=4,
        grid=(n_blocks,),
        in_specs=[
            smem_blk(lambda i, be, fi, nx, nr: (i, 0, 0)),
            smem_blk(lambda i, be, fi, nx, nr: (jnp.minimum(i + 1, n_blocks - 1), 0, 0)),
            smem_blk(lambda i, be, fi, nx, nr: (i, 0, 0)),
            pl.BlockSpec(memory_space=pl.ANY),
            pl.BlockSpec(memory_space=pl.ANY),
            pl.BlockSpec((1, 1, 2 * D_FF), lambda i, be, fi, nx, nr: (be[i], 0, 0)),
            pl.BlockSpec(memory_space=pl.ANY),
            pl.BlockSpec((1, 1, D_MODEL), lambda i, be, fi, nx, nr: (be[i], 0, 0)),
        ],
        out_specs=pl.BlockSpec(memory_space=pl.ANY),
        scratch_shapes=[pltpu.VMEM((2, bm, d), F32),
                        pltpu.VMEM((2, bm, d), F32),
                        pltpu.VMEM((D_MODEL, 2 * D_FF), F32),
                        pltpu.VMEM((D_FF, D_MODEL), F32),
                        pltpu.VMEM((D_MODEL, 2 * D_FF), BF16),
                        pltpu.VMEM((D_FF, D_MODEL), BF16),
                        pltpu.VMEM((bm, D_FF), BF16),
                        pltpu.SemaphoreType.DMA((2,)),
                        pltpu.SemaphoreType.DMA((2,)),
                        pltpu.SemaphoreType.DMA((2,))],
    )
    return pl.pallas_call(
        _experts_kernel,
        grid_spec=grid_spec,
        out_shape=jax.ShapeDtypeStruct((n_slot_rows, d), F32),
        compiler_params=pltpu.CompilerParams(
            dimension_semantics=("arbitrary",), vmem_limit_bytes=VMEM_LIMIT,
            has_side_effects=True),
        name="experts",
    )(bexp, first, nxt, nreal, row_src3, row_src3, row_dst3, h2, wgu, bgu3, wd, bd3)


def _combine_kernel(x1_ref, meta_ref, s0_ref, s1_ref, s2_ref, s3_ref, g_ref, out_ref):
    acc = x1_ref[...]
    meta = meta_ref[...]
    for kk, s_ref in enumerate((s0_ref, s1_ref, s2_ref, s3_ref)):
        acc = acc + meta[:, TOP_K + kk:TOP_K + kk + 1] * s_ref[...]
    out_ref[...] = _rms(acc, g_ref[...])


def _combine_call(x1, meta, slots, g):
    t, d = x1.shape
    tb = COMBINE_TILE
    nb = t // tb

    def slot_spec(kk):
        return pl.BlockSpec((tb, d), lambda i: (kk * nb + i, 0))

    return pl.pallas_call(
        _combine_kernel,
        grid=(nb,),
        in_specs=[pl.BlockSpec((tb, d), lambda i: (i, 0)),
                  pl.BlockSpec((tb, LANES), lambda i: (i, 0)),
                  slot_spec(0), slot_spec(1), slot_spec(2), slot_spec(3),
                  pl.BlockSpec((1, d), lambda i: (0, 0))],
        out_specs=pl.BlockSpec((tb, d), lambda i: (i, 0)),
        out_shape=jax.ShapeDtypeStruct((t, d), F32),
        compiler_params=pltpu.CompilerParams(
            dimension_semantics=("arbitrary",), vmem_limit_bytes=VMEM_LIMIT),
        name="combine",
    )(x1, meta, slots, slots, slots, slots, g)


def _routing_tables(meta, counts, t):
    bm = MOE_BLOCK
    tk = t * TOP_K
    n_blocks = tk // bm + N_EXPERTS
    n_rows = n_blocks * bm
    e_idx = meta[:, 0:TOP_K].astype(jnp.int32)
    rank = meta[:, 2 * TOP_K:3 * TOP_K].astype(jnp.int32)
    nblk_e = (counts + bm - 1) // bm
    blk_end = jnp.cumsum(nblk_e)
    blk_start = blk_end - nblk_e
    pad_start = blk_start * bm
    nreal = blk_end[-1]
    pos = pad_start[e_idx] + rank

    blk = jnp.arange(n_blocks, dtype=jnp.int32)
    bexp = jnp.minimum(jnp.searchsorted(blk_end, blk, side='right'), N_EXPERTS - 1).astype(jnp.int32)
    first = (blk == blk_start[bexp]).astype(jnp.int32)
    has = nblk_e > 0
    eids = jnp.arange(N_EXPERTS, dtype=jnp.int32)
    cand = jnp.where(has, eids, N_EXPERTS)
    nxt_after = lax.cummin(cand[::-1])[::-1]
    nxt_e = jnp.concatenate([nxt_after[1:], jnp.full((1,), N_EXPERTS, jnp.int32)])
    nxt_e = jnp.where(nxt_e >= N_EXPERTS, -1, nxt_e)
    nxt = nxt_e[bexp]

    rows = jnp.arange(n_rows, dtype=jnp.int32)
    row_e = bexp[rows // bm]
    dump = tk + row_e * bm + jnp.clip(rows - pad_start[row_e] - counts[row_e], 0, bm - 1)
    slot_id = (jnp.arange(TOP_K, dtype=jnp.int32)[None, :] * t
               + jnp.arange(t, dtype=jnp.int32)[:, None])
    row_dst = dump.at[pos.reshape(-1)].set(slot_id.reshape(-1), unique_indices=True)
    row_src = jnp.where(row_dst < tk, row_dst % t, 0)
    n_slot_rows = tk + N_EXPERTS * bm
    return (bexp, first, nxt, nreal.reshape(1).astype(jnp.int32),
            row_src.reshape(n_blocks, bm), row_dst.reshape(n_blocks, bm), n_slot_rows)


def kernel(x, norm_mix_g, w_in, conv_w, w_gk_up, b_gk_up, gla_norm_g, w_out, norm_ffn_g,
           w_router, b_router, w_gate_up, b_gate_up, w_down, b_down, norm_final_g):
    bsz, seq, d = x.shape
    t = bsz * seq
    assert w_in.shape[0] == 1, "single-layer trunk only"
    l = 0
    d_in = w_in.shape[-1]
    win = jnp.pad(w_in[l], ((0, 0), (0, D_IN_PAD - d_in))).astype(BF16)
    wgk = jnp.pad(w_gk_up[l], ((0, LANES - GLA_RANK), (0, 0))).astype(BF16)
    wr = jnp.pad(w_router[l], ((0, 0), (0, LANES - N_EXPERTS)))
    wrh = wr.astype(BF16)
    wrl = (wr - wrh.astype(F32)).astype(BF16)
    br = jnp.pad(b_router[l], (0, LANES - N_EXPERTS), constant_values=NEG_BIG).reshape(1, LANES)

    x1, h2, meta, cnt = _mixer_call(
        x, norm_mix_g[l].reshape(1, d), win, conv_w[l], wgk, b_gk_up[l].reshape(1, GLA_QK),
        gla_norm_g[l].reshape(1, GLA_DV), w_out[l].astype(BF16), norm_ffn_g[l].reshape(1, d),
        wrh, wrl, br)

    meta2 = meta.reshape(t, LANES)
    counts = cnt[0, :N_EXPERTS].astype(jnp.int32)
    bexp, first, nxt, nreal, row_src, row_dst, n_slot_rows = _routing_tables(meta2, counts, t)
    slots = _experts_call(bexp, first, nxt, nreal, row_src, row_dst, h2.reshape(t, d),
                          w_gate_up[l], b_gate_up[l], w_down[l], b_down[l], n_slot_rows)
    out = _combine_call(x1.reshape(t, d), meta2, slots, norm_final_g.reshape(1, d))
    return out.reshape(bsz, seq, d)
```

```python
import jax
import jax.numpy as jnp
from jax import lax
from jax.experimental import pallas as pl
from jax.experimental.pallas import tpu as pltpu

F32 = jnp.float32
BF16 = jnp.bfloat16
I32 = jnp.int32

D_MODEL = 1024
CONV_WIDTH = 512
CONV_K = 3
GLA_WIDTH = 512
GLA_HEADS = 4
GLA_DV = 128
GLA_DK = 64
GLA_QK = GLA_HEADS * GLA_DK
GLA_RANK = 16
GLA_NORMALIZER = 16.0
GLA_CHUNK = 64
N_EXPERTS = 32
TOP_K = 4
D_FF = 1024
SWIGLU_LIMIT = 7.0
SWIGLU_ALPHA = 1.702
RMS_EPS = 1e-5

LANES = 128
SUBLANES = 8
ROW_TILES = D_MODEL // LANES

OFF_UH = 0
OFF_GB = OFF_UH + CONV_WIDTH
OFF_GC = OFF_GB + CONV_WIDTH
OFF_Q = OFF_GC + CONV_WIDTH
OFF_K = OFF_Q + GLA_QK
OFF_V = OFF_K + GLA_QK
OFF_GO = OFF_V + GLA_WIDTH
OFF_GKL = OFF_GO + GLA_WIDTH
D_IN_PAD = OFF_GKL + LANES

SEQ_TILE = 512
MOE_BLOCK = 256
FF_CHUNK = 256
CAST_ROWS = 128
COMBINE_TILE = 256
NEG_BIG = -1e30
VMEM_LIMIT = 56 * 1024 * 1024


def _rms(x, g):
    return x * lax.rsqrt(jnp.mean(x * x, axis=-1, keepdims=True) + RMS_EPS) * g


def _dot(a, b):
    return jnp.dot(a, b, preferred_element_type=F32)


def _dot_nt(a, b):
    return lax.dot_general(a, b, (((1,), (1,)), ((), ())), preferred_element_type=F32)


def _split_bf16(x):
    hi = x.astype(BF16)
    lo = (x - hi.astype(F32)).astype(BF16)
    return hi, lo


def _mixer_kernel(x_ref, g1_ref, win_ref, convw_ref, wgk_ref, bgk_ref, gng_ref, wout_ref,
                  g2_ref, wrh_ref, wrl_ref, br_ref,
                  x1_ref, h2_ref, meta_ref, cnt_ref,
                  proj_ref, ubuf_ref, la_ref, ycat_ref, state_ref, carry_ref):
    ts = x_ref.shape[1]
    b_idx = pl.program_id(0)
    s_idx = pl.program_id(1)

    @pl.when(s_idx == 0)
    def _():
        state_ref[...] = jnp.zeros_like(state_ref)
        ubuf_ref[0:SUBLANES, :] = jnp.zeros((SUBLANES, CONV_WIDTH), F32)

    @pl.when((s_idx == 0) & (b_idx == 0))
    def _():
        carry_ref[...] = jnp.zeros_like(carry_ref)

    x = x_ref[0]
    h = _rms(x, g1_ref[...]).astype(BF16)
    proj_ref[...] = _dot(h, win_ref[...])

    u = proj_ref[:, OFF_GC:OFF_GC + CONV_WIDTH] * proj_ref[:, OFF_UH:OFF_UH + CONV_WIDTH]
    ubuf_ref[SUBLANES:SUBLANES + ts, :] = u
    u1 = ubuf_ref[pl.ds(SUBLANES - 1, ts), :]
    u2 = ubuf_ref[pl.ds(SUBLANES - 2, ts), :]
    conv = convw_ref[0:1, :] * u2 + convw_ref[1:2, :] * u1 + convw_ref[2:3, :] * u
    ycat_ref[:, 0:CONV_WIDTH] = (proj_ref[:, OFF_GB:OFF_GB + CONV_WIDTH] * conv).astype(BF16)
    ubuf_ref[0:SUBLANES, :] = ubuf_ref[ts:ts + SUBLANES, :]

    gk = _dot(proj_ref[:, OFF_GKL:OFF_GKL + LANES].astype(BF16), wgk_ref[...]) + bgk_ref[...]
    log_sig = jnp.minimum(gk, 0.0) - jnp.log1p(jnp.exp(-jnp.abs(gk)))
    la_ref[...] = log_sig / GLA_NORMALIZER

    ci = lax.broadcasted_iota(I32, (GLA_CHUNK, GLA_CHUNK), 0)
    cj = lax.broadcasted_iota(I32, (GLA_CHUNK, GLA_CHUNK), 1)
    tri_incl = (cj <= ci).astype(BF16)
    causal = cj <= ci
    causal4 = jnp.concatenate([causal] * GLA_HEADS, axis=0)
    lane_qk = lax.broadcasted_iota(I32, (1, GLA_QK), 1)
    head_masks = [((lane_qk >= hd * GLA_DK) & (lane_qk < (hd + 1) * GLA_DK)).astype(F32)
                  for hd in range(GLA_HEADS)]
    gng = gng_ref[...]

    def chunk_body(c, carry):
        r0 = pl.multiple_of(c * GLA_CHUNK, GLA_CHUNK)
        rows = pl.ds(r0, GLA_CHUNK)
        la_hi, la_lo = _split_bf16(la_ref[rows, :])
        bcum = _dot(tri_incl, la_hi) + _dot(tri_incl, la_lo)
        blast = bcum[GLA_CHUNK - 1:GLA_CHUNK, :]
        q = proj_ref[rows, OFF_Q:OFF_Q + GLA_QK] * (GLA_DK ** -0.5)
        k = proj_ref[rows, OFF_K:OFF_K + GLA_QK]
        v = proj_ref[rows, OFF_V:OFF_V + GLA_WIDTH].astype(BF16)
        qd = q * jnp.exp(bcum)
        kd = (k * jnp.exp(-bcum)).astype(BF16)
        kr = k * jnp.exp(blast - bcum)

        q_stack = jnp.concatenate([qd * m for m in head_masks], axis=0).astype(BF16)
        scores = jnp.where(causal4, _dot_nt(q_stack, kd), 0.0).astype(BF16)

        state = state_ref[...]
        o_inter = _dot(qd.astype(BF16), state.astype(BF16))
        o_intra = jnp.concatenate(
            [_dot(scores[hd * GLA_CHUNK:(hd + 1) * GLA_CHUNK, :],
                  v[:, hd * GLA_DV:(hd + 1) * GLA_DV]) for hd in range(GLA_HEADS)], axis=1)
        o = o_inter + o_intra

        kt = jnp.concatenate([kr, jnp.broadcast_to(blast, (GLA_CHUNK, GLA_QK))], axis=0).T
        dcol = jnp.exp(kt[:, GLA_CHUNK:GLA_CHUNK + 1])
        lane_c = lax.broadcasted_iota(I32, (GLA_QK, 2 * GLA_CHUNK), 1)
        kt_b = jnp.where(lane_c < GLA_CHUNK, kt, 0.0).astype(BF16)
        v_pad = jnp.concatenate([v, jnp.zeros_like(v)], axis=0)
        for hd in range(GLA_HEADS):
            rs = slice(hd * GLA_DK, (hd + 1) * GLA_DK)
            cs = slice(hd * GLA_DV, (hd + 1) * GLA_DV)
            kv = _dot(kt_b[rs, :], v_pad[:, cs])
            state_ref[rs, cs] = dcol[rs, :] * state[rs, cs] + kv

        g_out = proj_ref[rows, OFF_GO:OFF_GO + GLA_WIDTH]
        o_n = jnp.concatenate(
            [_rms(o[:, hd * GLA_DV:(hd + 1) * GLA_DV], gng) for hd in range(GLA_HEADS)], axis=1)
        y = o_n * (g_out * jax.nn.sigmoid(g_out))
        ycat_ref[rows, CONV_WIDTH:CONV_WIDTH + GLA_WIDTH] = y.astype(BF16)
        return carry

    lax.fori_loop(0, ts // GLA_CHUNK, chunk_body, 0)

    x1 = x + _dot(ycat_ref[...], wout_ref[...])
    x1_ref[0] = x1
    h2 = _rms(x1, g2_ref[...])
    for j in range(ROW_TILES):
        h2_ref[0, pl.ds(j, ts, stride=ROW_TILES), :] = h2[:, j * LANES:(j + 1) * LANES]
    h2_hi, h2_lo = _split_bf16(h2)
    logits = (_dot(h2_hi, wrh_ref[...]) + _dot(h2_lo, wrh_ref[...]) + _dot(h2_hi, wrl_ref[...])
              + br_ref[...])

    lane = lax.broadcasted_iota(I32, (ts, LANES), 1).astype(F32)
    work = logits
    sel = jnp.zeros((ts, LANES), F32)
    top_v, top_i, top_oh = [], [], []
    for _ in range(TOP_K):
        m = jnp.max(work, axis=-1, keepdims=True)
        idx = jnp.min(jnp.where(work == m, lane, float(LANES)), axis=-1, keepdims=True)
        oh = lane == idx
        top_v.append(m)
        top_i.append(idx)
        top_oh.append(oh)
        sel = sel + oh.astype(F32)
        work = jnp.where(oh, -jnp.inf, work)
    exps = [jnp.exp(tv - top_v[0]) for tv in top_v]
    denom = exps[0] + exps[1] + exps[2] + exps[3]
    gates = [e / denom for e in exps]

    ti = lax.broadcasted_iota(I32, (ts, ts), 0)
    tj = lax.broadcasted_iota(I32, (ts, ts), 1)
    strict_lower = (tj < ti).astype(BF16)
    prefix = _dot(strict_lower, sel.astype(BF16)) + carry_ref[0:1, :]
    ranks = [jnp.sum(jnp.where(oh, prefix, 0.0), axis=-1, keepdims=True) for oh in top_oh]
    new_carry = carry_ref[0:1, :] + jnp.sum(sel, axis=0, keepdims=True)
    carry_ref[...] = jnp.broadcast_to(new_carry, carry_ref.shape)
    cnt_ref[...] = jnp.broadcast_to(new_carry, cnt_ref.shape)

    meta = jnp.zeros((ts, LANES), F32)
    for kk in range(TOP_K):
        meta = jnp.where(lane == float(kk), top_i[kk], meta)
        meta = jnp.where(lane == float(TOP_K + kk), gates[kk], meta)
        meta = jnp.where(lane == float(2 * TOP_K + kk), ranks[kk], meta)
    meta_ref[0] = meta


def _mixer_call(x, g1, win, convw, wgk, bgk, gng, wout, g2, wrh, wrl, br):
    bsz, seq, d = x.shape
    ts = SEQ_TILE
    grid = (bsz, seq // ts)

    def const(shape):
        return pl.BlockSpec(shape, lambda b, s: (0,) * len(shape))

    tile = lambda w: pl.BlockSpec((1, ts, w), lambda b, s: (b, s, 0))
    return pl.pallas_call(
        _mixer_kernel,
        grid=grid,
        in_specs=[tile(d), const(g1.shape), const(win.shape), const(convw.shape),
                  const(wgk.shape), const(bgk.shape), const(gng.shape), const(wout.shape),
                  const(g2.shape), const(wrh.shape), const(wrl.shape), const(br.shape)],
        out_specs=[tile(d),
                   pl.BlockSpec((1, ts * ROW_TILES, LANES), lambda b, s: (b, s, 0)),
                   tile(LANES), const((SUBLANES, LANES))],
        out_shape=[jax.ShapeDtypeStruct((bsz, seq, d), F32),
                   jax.ShapeDtypeStruct((bsz, seq * ROW_TILES, LANES), F32),
                   jax.ShapeDtypeStruct((bsz, seq, LANES), F32),
                   jax.ShapeDtypeStruct((SUBLANES, LANES), F32)],
        scratch_shapes=[pltpu.VMEM((ts, D_IN_PAD), F32),
                        pltpu.VMEM((ts + SUBLANES, CONV_WIDTH), F32),
                        pltpu.VMEM((ts, GLA_QK), F32),
                        pltpu.VMEM((ts, D_MODEL), BF16),
                        pltpu.VMEM((GLA_QK, GLA_WIDTH), F32),
                        pltpu.VMEM((SUBLANES, LANES), F32)],
        compiler_params=pltpu.CompilerParams(
            dimension_semantics=("arbitrary", "arbitrary"), vmem_limit_bytes=VMEM_LIMIT),
        name="mixer",
    )(x, g1, win, convw, wgk, bgk, gng, wout, g2, wrh, wrl, br)


def _experts_kernel(bexp_ref, first_ref, nxt_ref, nval_ref, nreal_ref,
                    src_cur_ref, src_nxt_ref, dst_prev_ref, dst_cur_ref,
                    h2_hbm, wgu_hbm, bgu_ref, wd_hbm, bd_ref,
                    slots_hbm,
                    xbuf, obuf, xb_ref, act_ref, wgu_stage, wd_stage, wgu_bf, wd_bf,
                    gsem, ssem, wsem):
    i = pl.program_id(0)
    nreal = nreal_ref[0]
    bm = xb_ref.shape[0]
    slot = i % 2
    other = 1 - slot

    def weight_copies(e):
        return (pltpu.make_async_copy(wgu_hbm.at[e], wgu_stage, wsem.at[0]),
                pltpu.make_async_copy(wd_hbm.at[e], wd_stage, wsem.at[1]))

    def gather_row(idx_ref, j, buf_slot):
        src_row = pl.multiple_of(idx_ref[0, 0, j], ROW_TILES)
        dst_row = j * ROW_TILES if isinstance(j, int) else pl.multiple_of(j * ROW_TILES, ROW_TILES)
        return pltpu.make_async_copy(h2_hbm.at[pl.ds(src_row, ROW_TILES), :],
                                     xbuf.at[buf_slot, pl.ds(dst_row, ROW_TILES), :],
                                     gsem.at[buf_slot])

    def scatter_row(idx_ref, j, buf_slot):
        dst_row = pl.multiple_of(idx_ref[0, 0, j], ROW_TILES)
        src_row = j * ROW_TILES if isinstance(j, int) else pl.multiple_of(j * ROW_TILES, ROW_TILES)
        return pltpu.make_async_copy(obuf.at[buf_slot, pl.ds(src_row, ROW_TILES), :],
                                     slots_hbm.at[pl.ds(dst_row, ROW_TILES), :], ssem.at[buf_slot])

    def wait_gather(buf_slot):
        pltpu.make_async_copy(h2_hbm.at[pl.ds(0, bm * ROW_TILES), :], xbuf.at[buf_slot],
                              gsem.at[buf_slot]).wait()

    def start_scatter(idx_ref, buf_slot, n_rows):
        @pl.when(n_rows == bm)
        def _():
            for j in range(bm):
                scatter_row(idx_ref, j, buf_slot).start()

        @pl.when(n_rows < bm)
        def _():
            def body(j, carry):
                scatter_row(idx_ref, j, buf_slot).start()
                return carry
            lax.fori_loop(0, n_rows, body, 0)

    def wait_scatter(buf_slot, n_rows):
        n = n_rows * ROW_TILES
        pltpu.make_async_copy(obuf.at[buf_slot, pl.ds(0, n), :], slots_hbm.at[pl.ds(0, n), :],
                              ssem.at[buf_slot]).wait()

    def block_mlp():
        for j in range(ROW_TILES):
            xb_ref[:, j * LANES:(j + 1) * LANES] = (
                xbuf[slot, pl.ds(j, bm, stride=ROW_TILES), :].astype(BF16))
        n_chunks = D_FF // FF_CHUNK
        rows_per_chunk = bm // n_chunks
        for c in range(n_chunks):
            f0 = c * FF_CHUNK
            xb = xb_ref[...]
            gate = _dot(xb, wgu_bf[:, f0:f0 + FF_CHUNK]) + bgu_ref[0, :, f0:f0 + FF_CHUNK]
            up = (_dot(xb, wgu_bf[:, D_FF + f0:D_FF + f0 + FF_CHUNK])
                  + bgu_ref[0, :, D_FF + f0:D_FF + f0 + FF_CHUNK])
            gate = jnp.minimum(gate, SWIGLU_LIMIT)
            up = jnp.clip(up, -SWIGLU_LIMIT, SWIGLU_LIMIT)
            glu = gate * jax.nn.sigmoid(gate * SWIGLU_ALPHA)
            act_ref[:, f0:f0 + FF_CHUNK] = ((up + 1.0) * glu).astype(BF16)
            for j in range(c * rows_per_chunk, (c + 1) * rows_per_chunk):
                gather_row(src_nxt_ref, j, other).start()
        out = _dot(act_ref[...], wd_bf[...]) + bd_ref[0]
        for j in range(ROW_TILES):
            obuf[slot, pl.ds(j, bm, stride=ROW_TILES), :] = out[:, j * LANES:(j + 1) * LANES]

    @pl.when(i < nreal)
    def _():
        e = bexp_ref[i]
        n_prev = nval_ref[jnp.maximum(i - 1, 0)]

        @pl.when(i == 0)
        def _():
            for cp in weight_copies(e):
                cp.start()

            def body(j, carry):
                gather_row(src_cur_ref, j, 0).start()
                return carry
            lax.fori_loop(0, bm, body, 0, unroll=8)

        @pl.when(i > 0)
        def _():
            start_scatter(dst_prev_ref, other, n_prev)

        @pl.when(first_ref[i] == 1)
        def _():
            for cp in weight_copies(e):
                cp.wait()

            def cast_gu(r, carry):
                rows = pl.ds(pl.multiple_of(r * CAST_ROWS, CAST_ROWS), CAST_ROWS)
                wgu_bf[rows, :] = wgu_stage[rows, :].astype(BF16)
                return carry

            def cast_d(r, carry):
                rows = pl.ds(pl.multiple_of(r * CAST_ROWS, CAST_ROWS), CAST_ROWS)
                wd_bf[rows, :] = wd_stage[rows, :].astype(BF16)
                return carry

            lax.fori_loop(0, D_MODEL // CAST_ROWS, cast_gu, 0)
            lax.fori_loop(0, D_FF // CAST_ROWS, cast_d, 0)

            @pl.when(nxt_ref[i] >= 0)
            def _():
                for cp in weight_copies(nxt_ref[i]):
                    cp.start()

        wait_gather(slot)
        block_mlp()

        @pl.when(i > 0)
        def _():
            wait_scatter(other, n_prev)

        @pl.when(i == nreal - 1)
        def _():
            wait_gather(other)
            start_scatter(dst_cur_ref, slot, nval_ref[i])
            wait_scatter(slot, nval_ref[i])


def _experts_call(bexp, first, nxt, nval, nreal, row_src, row_dst, h2, wgu, bgu, wd, bd):
    n_blocks, bm = row_src.shape
    t = h2.shape[0] // ROW_TILES
    row_src3 = (row_src * ROW_TILES).reshape(n_blocks, 1, bm)
    row_dst3 = (row_dst * ROW_TILES).reshape(n_blocks, 1, bm)
    bgu3 = bgu.reshape(N_EXPERTS, 1, 2 * D_FF)
    bd3 = bd.reshape(N_EXPERTS, 1, D_MODEL)

    smem_blk = lambda fn: pl.BlockSpec((1, 1, bm), fn, memory_space=pltpu.SMEM)
    grid_spec = pltpu.PrefetchScalarGridSpec(
        num_scalar_prefetch=5,
        grid=(n_blocks,),
        in_specs=[
            smem_blk(lambda i, be, fi, nx, nv, nr: (jnp.minimum(i, nr[0] - 1), 0, 0)),
            smem_blk(lambda i, be, fi, nx, nv, nr: (jnp.minimum(i + 1, nr[0] - 1), 0, 0)),
            smem_blk(lambda i, be, fi, nx, nv, nr: (jnp.clip(i - 1, 0, nr[0] - 1), 0, 0)),
            smem_blk(lambda i, be, fi, nx, nv, nr: (jnp.minimum(i, nr[0] - 1), 0, 0)),
            pl.BlockSpec(memory_space=pl.ANY),
            pl.BlockSpec(memory_space=pl.ANY),
            pl.BlockSpec((1, 1, 2 * D_FF), lambda i, be, fi, nx, nv, nr: (be[i], 0, 0)),
            pl.BlockSpec(memory_space=pl.ANY),
            pl.BlockSpec((1, 1, D_MODEL), lambda i, be, fi, nx, nv, nr: (be[i], 0, 0)),
        ],
        out_specs=pl.BlockSpec(memory_space=pl.ANY),
        scratch_shapes=[pltpu.VMEM((2, bm * ROW_TILES, LANES), F32),
                        pltpu.VMEM((2, bm * ROW_TILES, LANES), F32),
                        pltpu.VMEM((bm, D_MODEL), BF16),
                        pltpu.VMEM((bm, D_FF), BF16),
                        pltpu.VMEM((D_MODEL, 2 * D_FF), F32),
                        pltpu.VMEM((D_FF, D_MODEL), F32),
                        pltpu.VMEM((D_MODEL, 2 * D_FF), BF16),
                        pltpu.VMEM((D_FF, D_MODEL), BF16),
                        pltpu.SemaphoreType.DMA((2,)),
                        pltpu.SemaphoreType.DMA((2,)),
                        pltpu.SemaphoreType.DMA((2,))],
    )
    return pl.pallas_call(
        _experts_kernel,
        grid_spec=grid_spec,
        out_shape=jax.ShapeDtypeStruct((t * TOP_K * ROW_TILES, LANES), F32),
        compiler_params=pltpu.CompilerParams(
            dimension_semantics=("arbitrary",), vmem_limit_bytes=VMEM_LIMIT,
            has_side_effects=True),
        name="experts",
    )(bexp, first, nxt, nval, nreal, row_src3, row_src3, row_dst3, row_dst3, h2, wgu, bgu3, wd, bd3)


def _combine_kernel(x1_ref, meta_ref, s0_ref, s1_ref, s2_ref, s3_ref, g_ref, out_ref):
    tb = x1_ref.shape[0]
    meta = meta_ref[...]
    gates = [meta[:, TOP_K + kk:TOP_K + kk + 1] for kk in range(TOP_K)]
    ssq = jnp.zeros((tb, 1), F32)
    for j in range(ROW_TILES):
        cols = slice(j * LANES, (j + 1) * LANES)
        acc = x1_ref[:, cols]
        for kk, s_ref in enumerate((s0_ref, s1_ref, s2_ref, s3_ref)):
            acc = acc + gates[kk] * s_ref[pl.ds(j, tb, stride=ROW_TILES), :]
        out_ref[:, cols] = acc
        ssq = ssq + jnp.sum(acc * acc, axis=-1, keepdims=True)
    inv = lax.rsqrt(ssq / D_MODEL + RMS_EPS)
    out_ref[...] = out_ref[...] * inv * g_ref[...]


def _combine_call(x1, meta, slots, g):
    t, d = x1.shape
    tb = COMBINE_TILE
    nb = t // tb

    def slot_spec(kk):
        return pl.BlockSpec((tb * ROW_TILES, LANES), lambda i: (kk * nb + i, 0))

    return pl.pallas_call(
        _combine_kernel,
        grid=(nb,),
        in_specs=[pl.BlockSpec((tb, d), lambda i: (i, 0)),
                  pl.BlockSpec((tb, LANES), lambda i: (i, 0)),
                  slot_spec(0), slot_spec(1), slot_spec(2), slot_spec(3),
                  pl.BlockSpec((1, d), lambda i: (0, 0))],
        out_specs=pl.BlockSpec((tb, d), lambda i: (i, 0)),
        out_shape=jax.ShapeDtypeStruct((t, d), F32),
        compiler_params=pltpu.CompilerParams(
            dimension_semantics=("arbitrary",), vmem_limit_bytes=VMEM_LIMIT),
        name="combine",
    )(x1, meta, slots, slots, slots, slots, g)


def _routing_tables(meta, counts, t):
    bm = MOE_BLOCK
    tk = t * TOP_K
    n_blocks = tk // bm + N_EXPERTS
    e_idx = meta[:, 0:TOP_K].astype(I32)
    rank = meta[:, 2 * TOP_K:3 * TOP_K].astype(I32)
    eids = jnp.arange(N_EXPERTS, dtype=I32)
    nblk_e = (counts + bm - 1) // bm
    blk_end = jnp.sum(jnp.where(eids[None, :] <= eids[:, None], nblk_e[None, :], 0), axis=1)
    blk_start = blk_end - nblk_e
    nreal = blk_end[N_EXPERTS - 1]
    pad_start = blk_start * bm
    pos = jnp.sum(jnp.where(e_idx[..., None] == eids, pad_start, 0), axis=-1) + rank

    blk = jnp.arange(n_blocks, dtype=I32)
    bexp = jnp.minimum(jnp.sum((blk_end[None, :] <= blk[:, None]).astype(I32), axis=1),
                       N_EXPERTS - 1)
    blk_is_e = bexp[:, None] == eids[None, :]
    pick = lambda tab: jnp.sum(jnp.where(blk_is_e, tab[None, :], 0), axis=1)
    first = (blk == pick(blk_start)).astype(I32)
    nxt_e = jnp.sum((blk_end[None, :] <= blk_end[:, None]).astype(I32), axis=1)
    nxt_e = jnp.where(blk_end < nreal, jnp.minimum(nxt_e, N_EXPERTS - 1), -1)
    nxt = pick(nxt_e)
    nvalid = jnp.clip(pick(counts) - (blk - pick(blk_start)) * bm, 1, bm)

    slot_id = jnp.arange(TOP_K, dtype=I32)[None, :] * t + jnp.arange(t, dtype=I32)[:, None]
    row_dst = jnp.zeros((n_blocks * bm,), I32).at[pos.reshape(-1)].set(
        slot_id.reshape(-1), unique_indices=True).reshape(n_blocks, bm)
    col = jnp.arange(bm, dtype=I32)[None, :]
    last_real = jnp.sum(jnp.where(col == nvalid[:, None] - 1, row_dst, 0), axis=1, keepdims=True)
    row_dst = jnp.where(col < nvalid[:, None], row_dst, last_real)
    row_src = row_dst % t
    return bexp, first, nxt, nvalid, nreal.reshape(1).astype(I32), row_src, row_dst


def kernel(x, norm_mix_g, w_in, conv_w, w_gk_up, b_gk_up, gla_norm_g, w_out, norm_ffn_g,
           w_router, b_router, w_gate_up, b_gate_up, w_down, b_down, norm_final_g):
    bsz, seq, d = x.shape
    t = bsz * seq
    assert w_in.shape[0] == 1, "single-layer trunk only"
    l = 0
    d_in = w_in.shape[-1]
    win = jnp.pad(w_in[l], ((0, 0), (0, D_IN_PAD - d_in))).astype(BF16)
    wgk = jnp.pad(w_gk_up[l], ((0, LANES - GLA_RANK), (0, 0))).astype(BF16)
    wr = jnp.pad(w_router[l], ((0, 0), (0, LANES - N_EXPERTS)))
    wrh = wr.astype(BF16)
    wrl = (wr - wrh.astype(F32)).astype(BF16)
    br = jnp.pad(b_router[l], (0, LANES - N_EXPERTS), constant_values=NEG_BIG).reshape(1, LANES)

    x1, h2, meta, cnt = _mixer_call(
        x, norm_mix_g[l].reshape(1, d), win, conv_w[l], wgk, b_gk_up[l].reshape(1, GLA_QK),
        gla_norm_g[l].reshape(1, GLA_DV), w_out[l].astype(BF16), norm_ffn_g[l].reshape(1, d),
        wrh, wrl, br)

    meta2 = meta.reshape(t, LANES)
    counts = cnt[0, :N_EXPERTS].astype(I32)
    bexp, first, nxt, nval, nreal, row_src, row_dst = _routing_tables(meta2, counts, t)
    slots = _experts_call(bexp, first, nxt, nval, nreal, row_src, row_dst,
                          h2.reshape(t * ROW_TILES, LANES),
                          w_gate_up[l], b_gate_up[l], w_down[l], b_down[l])
    out = _combine_call(x1.reshape(t, d), meta2, slots, norm_final_g.reshape(1, d))
    return out.reshape(bsz, seq, d)
```

```python
import jax
import jax.numpy as jnp
from jax import lax
from jax.experimental import pallas as pl
from jax.experimental.pallas import tpu as pltpu

F32 = jnp.float32
BF16 = jnp.bfloat16
I32 = jnp.int32

D_MODEL = 1024
CONV_WIDTH = 512
CONV_K = 3
GLA_WIDTH = 512
GLA_HEADS = 4
GLA_DV = 128
GLA_DK = 64
GLA_QK = GLA_HEADS * GLA_DK
GLA_RANK = 16
GLA_NORMALIZER = 16.0
GLA_CHUNK = 64
N_EXPERTS = 32
TOP_K = 4
D_FF = 1024
SWIGLU_LIMIT = 7.0
SWIGLU_ALPHA = 1.702
RMS_EPS = 1e-5

LANES = 128
SUBLANES = 8
ROW_TILES = D_MODEL // LANES

OFF_UH = 0
OFF_GB = OFF_UH + CONV_WIDTH
OFF_GC = OFF_GB + CONV_WIDTH
OFF_Q = OFF_GC + CONV_WIDTH
OFF_K = OFF_Q + GLA_QK
OFF_V = OFF_K + GLA_QK
OFF_GO = OFF_V + GLA_WIDTH
OFF_GKL = OFF_GO + GLA_WIDTH
D_IN_PAD = OFF_GKL + LANES

SEQ_TILE = 512
MOE_BLOCK = 256
FF_CHUNK = 256
CAST_ROWS = 128
COMBINE_TILE = 256
NEG_BIG = -1e30
VMEM_LIMIT = 56 * 1024 * 1024


def _rms(x, g):
    return x * lax.rsqrt(jnp.mean(x * x, axis=-1, keepdims=True) + RMS_EPS) * g


def _dot(a, b):
    return jnp.dot(a, b, preferred_element_type=F32)


def _dot_nt(a, b):
    return lax.dot_general(a, b, (((1,), (1,)), ((), ())), preferred_element_type=F32)


def _split_bf16(x):
    hi = x.astype(BF16)
    lo = (x - hi.astype(F32)).astype(BF16)
    return hi, lo


def _mixer_kernel(x_ref, g1_ref, win_ref, convw_ref, wgk_ref, bgk_ref, gng_ref, wout_ref,
                  g2_ref, wrh_ref, wrl_ref, br_ref,
                  x1_ref, h2_ref, meta_ref, cnt_ref,
                  proj_ref, ubuf_ref, la_ref, ycat_ref, state_ref, carry_ref):
    ts = x_ref.shape[1]
    b_idx = pl.program_id(0)
    s_idx = pl.program_id(1)

    @pl.when(s_idx == 0)
    def _():
        state_ref[...] = jnp.zeros_like(state_ref)
        ubuf_ref[0:SUBLANES, :] = jnp.zeros((SUBLANES, CONV_WIDTH), F32)

    @pl.when((s_idx == 0) & (b_idx == 0))
    def _():
        carry_ref[...] = jnp.zeros_like(carry_ref)

    x = x_ref[0]
    h = _rms(x, g1_ref[...]).astype(BF16)
    proj_ref[...] = _dot(h, win_ref[...])

    u = proj_ref[:, OFF_GC:OFF_GC + CONV_WIDTH] * proj_ref[:, OFF_UH:OFF_UH + CONV_WIDTH]
    ubuf_ref[SUBLANES:SUBLANES + ts, :] = u
    u1 = ubuf_ref[pl.ds(SUBLANES - 1, ts), :]
    u2 = ubuf_ref[pl.ds(SUBLANES - 2, ts), :]
    conv = convw_ref[0:1, :] * u2 + convw_ref[1:2, :] * u1 + convw_ref[2:3, :] * u
    ycat_ref[:, 0:CONV_WIDTH] = (proj_ref[:, OFF_GB:OFF_GB + CONV_WIDTH] * conv).astype(BF16)
    ubuf_ref[0:SUBLANES, :] = ubuf_ref[ts:ts + SUBLANES, :]

    gk = _dot(proj_ref[:, OFF_GKL:OFF_GKL + LANES].astype(BF16), wgk_ref[...]) + bgk_ref[...]
    log_sig = jnp.minimum(gk, 0.0) - jnp.log1p(jnp.exp(-jnp.abs(gk)))
    la_ref[...] = log_sig / GLA_NORMALIZER

    ci = lax.broadcasted_iota(I32, (GLA_CHUNK, GLA_CHUNK), 0)
    cj = lax.broadcasted_iota(I32, (GLA_CHUNK, GLA_CHUNK), 1)
    tri_incl = (cj <= ci).astype(BF16)
    causal = cj <= ci
    causal4 = jnp.concatenate([causal] * GLA_HEADS, axis=0)
    lane_qk = lax.broadcasted_iota(I32, (1, GLA_QK), 1)
    head_masks = [((lane_qk >= hd * GLA_DK) & (lane_qk < (hd + 1) * GLA_DK)).astype(F32)
                  for hd in range(GLA_HEADS)]
    gng = gng_ref[...]

    def chunk_body(c, carry):
        r0 = pl.multiple_of(c * GLA_CHUNK, GLA_CHUNK)
        rows = pl.ds(r0, GLA_CHUNK)
        la_hi, la_lo = _split_bf16(la_ref[rows, :])
        bcum = _dot(tri_incl, la_hi) + _dot(tri_incl, la_lo)
        blast = bcum[GLA_CHUNK - 1:GLA_CHUNK, :]
        q = proj_ref[rows, OFF_Q:OFF_Q + GLA_QK] * (GLA_DK ** -0.5)
        k = proj_ref[rows, OFF_K:OFF_K + GLA_QK]
        v = proj_ref[rows, OFF_V:OFF_V + GLA_WIDTH].astype(BF16)
        qd = q * jnp.exp(bcum)
        kd = (k * jnp.exp(-bcum)).astype(BF16)
        kr = k * jnp.exp(blast - bcum)

        q_stack = jnp.concatenate([qd * m for m in head_masks], axis=0).astype(BF16)
        scores = jnp.where(causal4, _dot_nt(q_stack, kd), 0.0).astype(BF16)

        state = state_ref[...]
        o_inter = _dot(qd.astype(BF16), state.astype(BF16))
        o_intra = jnp.concatenate(
            [_dot(scores[hd * GLA_CHUNK:(hd + 1) * GLA_CHUNK, :],
                  v[:, hd * GLA_DV:(hd + 1) * GLA_DV]) for hd in range(GLA_HEADS)], axis=1)
        o = o_inter + o_intra

        kt = jnp.concatenate([kr, jnp.broadcast_to(blast, (GLA_CHUNK, GLA_QK))], axis=0).T
        dcol = jnp.exp(kt[:, GLA_CHUNK:GLA_CHUNK + 1])
        lane_c = lax.broadcasted_iota(I32, (GLA_QK, 2 * GLA_CHUNK), 1)
        kt_b = jnp.where(lane_c < GLA_CHUNK, kt, 0.0).astype(BF16)
        v_pad = jnp.concatenate([v, jnp.zeros_like(v)], axis=0)
        for hd in range(GLA_HEADS):
            rs = slice(hd * GLA_DK, (hd + 1) * GLA_DK)
            cs = slice(hd * GLA_DV, (hd + 1) * GLA_DV)
            kv = _dot(kt_b[rs, :], v_pad[:, cs])
            state_ref[rs, cs] = dcol[rs, :] * state[rs, cs] + kv

        g_out = proj_ref[rows, OFF_GO:OFF_GO + GLA_WIDTH]
        o_n = jnp.concatenate(
            [_rms(o[:, hd * GLA_DV:(hd + 1) * GLA_DV], gng) for hd in range(GLA_HEADS)], axis=1)
        y = o_n * (g_out * jax.nn.sigmoid(g_out))
        ycat_ref[rows, CONV_WIDTH:CONV_WIDTH + GLA_WIDTH] = y.astype(BF16)
        return carry

    lax.fori_loop(0, ts // GLA_CHUNK, chunk_body, 0)

    x1 = x + _dot(ycat_ref[...], wout_ref[...])
    x1_ref[0] = x1
    h2 = _rms(x1, g2_ref[...])
    for j in range(ROW_TILES):
        h2_ref[0, pl.ds(j, ts, stride=ROW_TILES), :] = h2[:, j * LANES:(j + 1) * LANES]
    h2_hi, h2_lo = _split_bf16(h2)
    logits = (_dot(h2_hi, wrh_ref[...]) + _dot(h2_lo, wrh_ref[...]) + _dot(h2_hi, wrl_ref[...])
              + br_ref[...])

    lane = lax.broadcasted_iota(I32, (ts, LANES), 1).astype(F32)
    work = logits
    sel = jnp.zeros((ts, LANES), F32)
    top_v, top_i, top_oh = [], [], []
    for _ in range(TOP_K):
        m = jnp.max(work, axis=-1, keepdims=True)
        idx = jnp.min(jnp.where(work == m, lane, float(LANES)), axis=-1, keepdims=True)
        oh = lane == idx
        top_v.append(m)
        top_i.append(idx)
        top_oh.append(oh)
        sel = sel + oh.astype(F32)
        work = jnp.where(oh, -jnp.inf, work)
    exps = [jnp.exp(tv - top_v[0]) for tv in top_v]
    denom = exps[0] + exps[1] + exps[2] + exps[3]
    gates = [e / denom for e in exps]

    ti = lax.broadcasted_iota(I32, (ts, ts), 0)
    tj = lax.broadcasted_iota(I32, (ts, ts), 1)
    strict_lower = (tj < ti).astype(BF16)
    prefix = _dot(strict_lower, sel.astype(BF16)) + carry_ref[0:1, :]
    ranks = [jnp.sum(jnp.where(oh, prefix, 0.0), axis=-1, keepdims=True) for oh in top_oh]
    new_carry = carry_ref[0:1, :] + jnp.sum(sel, axis=0, keepdims=True)
    carry_ref[...] = jnp.broadcast_to(new_carry, carry_ref.shape)
    cnt_ref[...] = jnp.broadcast_to(new_carry, cnt_ref.shape)

    meta = jnp.zeros((ts, LANES), F32)
    for kk in range(TOP_K):
        meta = jnp.where(lane == float(kk), top_i[kk], meta)
        meta = jnp.where(lane == float(TOP_K + kk), gates[kk], meta)
        meta = jnp.where(lane == float(2 * TOP_K + kk), ranks[kk], meta)
    meta_ref[0] = meta


def _mixer_call(x, g1, win, convw, wgk, bgk, gng, wout, g2, wrh, wrl, br):
    bsz, seq, d = x.shape
    ts = SEQ_TILE
    grid = (bsz, seq // ts)

    def const(shape):
        return pl.BlockSpec(shape, lambda b, s: (0,) * len(shape))

    tile = lambda w: pl.BlockSpec((1, ts, w), lambda b, s: (b, s, 0))
    return pl.pallas_call(
        _mixer_kernel,
        grid=grid,
        in_specs=[tile(d), const(g1.shape), const(win.shape), const(convw.shape),
                  const(wgk.shape), const(bgk.shape), const(gng.shape), const(wout.shape),
                  const(g2.shape), const(wrh.shape), const(wrl.shape), const(br.shape)],
        out_specs=[tile(d),
                   pl.BlockSpec((1, ts * ROW_TILES, LANES), lambda b, s: (b, s, 0)),
                   tile(LANES), const((SUBLANES, LANES))],
        out_shape=[jax.ShapeDtypeStruct((bsz, seq, d), F32),
                   jax.ShapeDtypeStruct((bsz, seq * ROW_TILES, LANES), F32),
                   jax.ShapeDtypeStruct((bsz, seq, LANES), F32),
                   jax.ShapeDtypeStruct((SUBLANES, LANES), F32)],
        scratch_shapes=[pltpu.VMEM((ts, D_IN_PAD), F32),
                        pltpu.VMEM((ts + SUBLANES, CONV_WIDTH), F32),
                        pltpu.VMEM((ts, GLA_QK), F32),
                        pltpu.VMEM((ts, D_MODEL), BF16),
                        pltpu.VMEM((GLA_QK, GLA_WIDTH), F32),
                        pltpu.VMEM((SUBLANES, LANES), F32)],
        compiler_params=pltpu.CompilerParams(
            dimension_semantics=("arbitrary", "arbitrary"), vmem_limit_bytes=VMEM_LIMIT),
        name="mixer",
    )(x, g1, win, convw, wgk, bgk, gng, wout, g2, wrh, wrl, br)


def _experts_kernel(bexp_ref, first_ref, nxt_ref, nval_ref, nreal_ref,
                    src_cur_ref, src_nxt_ref, dst_prev_ref, dst_cur_ref,
                    h2_hbm, wgu_hbm, bgu_ref, wd_hbm, bd_ref,
                    slots_hbm,
                    xbuf, obuf, xb_ref, act_ref, wgu_stage, wd_stage, wgu_bf, wd_bf,
                    gsem, ssem, wsem):
    i = pl.program_id(0)
    nreal = nreal_ref[0]
    bm = xb_ref.shape[0]
    slot = i % 2
    other = 1 - slot

    def weight_copies(e):
        return (pltpu.make_async_copy(wgu_hbm.at[e], wgu_stage, wsem.at[0]),
                pltpu.make_async_copy(wd_hbm.at[e], wd_stage, wsem.at[1]))

    def gather_row(idx_ref, j, buf_slot):
        src_row = pl.multiple_of(idx_ref[0, 0, j], ROW_TILES)
        dst_row = j * ROW_TILES if isinstance(j, int) else pl.multiple_of(j * ROW_TILES, ROW_TILES)
        return pltpu.make_async_copy(h2_hbm.at[pl.ds(src_row, ROW_TILES), :],
                                     xbuf.at[buf_slot, pl.ds(dst_row, ROW_TILES), :],
                                     gsem.at[buf_slot])

    def scatter_row(idx_ref, j, buf_slot):
        dst_row = pl.multiple_of(idx_ref[0, 0, j], ROW_TILES)
        src_row = j * ROW_TILES if isinstance(j, int) else pl.multiple_of(j * ROW_TILES, ROW_TILES)
        return pltpu.make_async_copy(obuf.at[buf_slot, pl.ds(src_row, ROW_TILES), :],
                                     slots_hbm.at[pl.ds(dst_row, ROW_TILES), :], ssem.at[buf_slot])

    def wait_gather(buf_slot):
        pltpu.make_async_copy(h2_hbm.at[pl.ds(0, bm * ROW_TILES), :], xbuf.at[buf_slot],
                              gsem.at[buf_slot]).wait()

    def start_scatter(idx_ref, buf_slot, n_rows):
        @pl.when(n_rows == bm)
        def _():
            for j in range(bm):
                scatter_row(idx_ref, j, buf_slot).start(priority=j % 2)

        @pl.when(n_rows < bm)
        def _():
            def body(j, carry):
                scatter_row(idx_ref, j, buf_slot).start()
                return carry
            lax.fori_loop(0, n_rows, body, 0)

    def wait_scatter(buf_slot, n_rows):
        n = n_rows * ROW_TILES
        pltpu.make_async_copy(obuf.at[buf_slot, pl.ds(0, n), :], slots_hbm.at[pl.ds(0, n), :],
                              ssem.at[buf_slot]).wait()

    def block_mlp():
        for j in range(ROW_TILES):
            xb_ref[:, j * LANES:(j + 1) * LANES] = (
                xbuf[slot, pl.ds(j, bm, stride=ROW_TILES), :].astype(BF16))
        n_chunks = D_FF // FF_CHUNK
        rows_per_chunk = bm // n_chunks
        for c in range(n_chunks):
            f0 = c * FF_CHUNK
            xb = xb_ref[...]
            gate = _dot(xb, wgu_bf[:, f0:f0 + FF_CHUNK]) + bgu_ref[0, :, f0:f0 + FF_CHUNK]
            up = (_dot(xb, wgu_bf[:, D_FF + f0:D_FF + f0 + FF_CHUNK])
                  + bgu_ref[0, :, D_FF + f0:D_FF + f0 + FF_CHUNK])
            gate = jnp.minimum(gate, SWIGLU_LIMIT)
            up = jnp.clip(up, -SWIGLU_LIMIT, SWIGLU_LIMIT)
            glu = gate * jax.nn.sigmoid(gate * SWIGLU_ALPHA)
            act_ref[:, f0:f0 + FF_CHUNK] = ((up + 1.0) * glu).astype(BF16)
            for j in range(c * rows_per_chunk, (c + 1) * rows_per_chunk):
                gather_row(src_nxt_ref, j, other).start(priority=j % 2)
        out = _dot(act_ref[...], wd_bf[...]) + bd_ref[0]
        for j in range(ROW_TILES):
            obuf[slot, pl.ds(j, bm, stride=ROW_TILES), :] = out[:, j * LANES:(j + 1) * LANES]

    @pl.when(i < nreal)
    def _():
        e = bexp_ref[i]
        n_prev = nval_ref[jnp.maximum(i - 1, 0)]

        @pl.when(i == 0)
        def _():
            for cp in weight_copies(e):
                cp.start()

            def body(j, carry):
                gather_row(src_cur_ref, j, 0).start()
                return carry
            lax.fori_loop(0, bm, body, 0, unroll=8)

        @pl.when(i > 0)
        def _():
            start_scatter(dst_prev_ref, other, n_prev)

        @pl.when(first_ref[i] == 1)
        def _():
            for cp in weight_copies(e):
                cp.wait()

            def cast_gu(r, carry):
                rows = pl.ds(pl.multiple_of(r * CAST_ROWS, CAST_ROWS), CAST_ROWS)
                wgu_bf[rows, :] = wgu_stage[rows, :].astype(BF16)
                return carry

            def cast_d(r, carry):
                rows = pl.ds(pl.multiple_of(r * CAST_ROWS, CAST_ROWS), CAST_ROWS)
                wd_bf[rows, :] = wd_stage[rows, :].astype(BF16)
                return carry

            lax.fori_loop(0, D_MODEL // CAST_ROWS, cast_gu, 0)
            lax.fori_loop(0, D_FF // CAST_ROWS, cast_d, 0)

            @pl.when(nxt_ref[i] >= 0)
            def _():
                for cp in weight_copies(nxt_ref[i]):
                    cp.start()

        wait_gather(slot)
        block_mlp()

        @pl.when(i > 0)
        def _():
            wait_scatter(other, n_prev)

        @pl.when(i == nreal - 1)
        def _():
            wait_gather(other)
            start_scatter(dst_cur_ref, slot, nval_ref[i])
            wait_scatter(slot, nval_ref[i])


def _experts_call(bexp, first, nxt, nval, nreal, row_src, row_dst, h2, wgu, bgu, wd, bd):
    n_blocks, bm = row_src.shape
    t = h2.shape[0] // ROW_TILES
    row_src3 = (row_src * ROW_TILES).reshape(n_blocks, 1, bm)
    row_dst3 = (row_dst * ROW_TILES).reshape(n_blocks, 1, bm)
    bgu3 = bgu.reshape(N_EXPERTS, 1, 2 * D_FF)
    bd3 = bd.reshape(N_EXPERTS, 1, D_MODEL)

    smem_blk = lambda fn: pl.BlockSpec((1, 1, bm), fn, memory_space=pltpu.SMEM)
    grid_spec = pltpu.PrefetchScalarGridSpec(
        num_scalar_prefetch=5,
        grid=(n_blocks,),
        in_specs=[
            smem_blk(lambda i, be, fi, nx, nv, nr: (jnp.minimum(i, nr[0] - 1), 0, 0)),
            smem_blk(lambda i, be, fi, nx, nv, nr: (jnp.minimum(i + 1, nr[0] - 1), 0, 0)),
            smem_blk(lambda i, be, fi, nx, nv, nr: (jnp.clip(i - 1, 0, nr[0] - 1), 0, 0)),
            smem_blk(lambda i, be, fi, nx, nv, nr: (jnp.minimum(i, nr[0] - 1), 0, 0)),
            pl.BlockSpec(memory_space=pl.ANY),
            pl.BlockSpec(memory_space=pl.ANY),
            pl.BlockSpec((1, 1, 2 * D_FF), lambda i, be, fi, nx, nv, nr: (be[i], 0, 0)),
            pl.BlockSpec(memory_space=pl.ANY),
            pl.BlockSpec((1, 1, D_MODEL), lambda i, be, fi, nx, nv, nr: (be[i], 0, 0)),
        ],
        out_specs=pl.BlockSpec(memory_space=pl.ANY),
        scratch_shapes=[pltpu.VMEM((2, bm * ROW_TILES, LANES), F32),
                        pltpu.VMEM((2, bm * ROW_TILES, LANES), F32),
                        pltpu.VMEM((bm, D_MODEL), BF16),
                        pltpu.VMEM((bm, D_FF), BF16),
                        pltpu.VMEM((D_MODEL, 2 * D_FF), F32),
                        pltpu.VMEM((D_FF, D_MODEL), F32),
                        pltpu.VMEM((D_MODEL, 2 * D_FF), BF16),
                        pltpu.VMEM((D_FF, D_MODEL), BF16),
                        pltpu.SemaphoreType.DMA((2,)),
                        pltpu.SemaphoreType.DMA((2,)),
                        pltpu.SemaphoreType.DMA((2,))],
    )
    return pl.pallas_call(
        _experts_kernel,
        grid_spec=grid_spec,
        out_shape=jax.ShapeDtypeStruct((t * TOP_K * ROW_TILES, LANES), F32),
        compiler_params=pltpu.CompilerParams(
            dimension_semantics=("arbitrary",), vmem_limit_bytes=VMEM_LIMIT,
            has_side_effects=True),
        name="experts",
    )(bexp, first, nxt, nval, nreal, row_src3, row_src3, row_dst3, row_dst3, h2, wgu, bgu3, wd, bd3)


def _combine_kernel(x1_ref, meta_ref, s0_ref, s1_ref, s2_ref, s3_ref, g_ref, out_ref):
    tb = x1_ref.shape[0]
    meta = meta_ref[...]
    gates = [meta[:, TOP_K + kk:TOP_K + kk + 1] for kk in range(TOP_K)]
    ssq = jnp.zeros((tb, 1), F32)
    for j in range(ROW_TILES):
        cols = slice(j * LANES, (j + 1) * LANES)
        acc = x1_ref[:, cols]
        for kk, s_ref in enumerate((s0_ref, s1_ref, s2_ref, s3_ref)):
            acc = acc + gates[kk] * s_ref[pl.ds(j, tb, stride=ROW_TILES), :]
        out_ref[:, cols] = acc
        ssq = ssq + jnp.sum(acc * acc, axis=-1, keepdims=True)
    inv = lax.rsqrt(ssq / D_MODEL + RMS_EPS)
    out_ref[...] = out_ref[...] * inv * g_ref[...]


def _combine_call(x1, meta, slots, g):
    t, d = x1.shape
    tb = COMBINE_TILE
    nb = t // tb

    def slot_spec(kk):
        return pl.BlockSpec((tb * ROW_TILES, LANES), lambda i: (kk * nb + i, 0))

    return pl.pallas_call(
        _combine_kernel,
        grid=(nb,),
        in_specs=[pl.BlockSpec((tb, d), lambda i: (i, 0)),
                  pl.BlockSpec((tb, LANES), lambda i: (i, 0)),
                  slot_spec(0), slot_spec(1), slot_spec(2), slot_spec(3),
                  pl.BlockSpec((1, d), lambda i: (0, 0))],
        out_specs=pl.BlockSpec((tb, d), lambda i: (i, 0)),
        out_shape=jax.ShapeDtypeStruct((t, d), F32),
        compiler_params=pltpu.CompilerParams(
            dimension_semantics=("arbitrary",), vmem_limit_bytes=VMEM_LIMIT),
        name="combine",
    )(x1, meta, slots, slots, slots, slots, g)


def _routing_tables(meta, counts, t):
    bm = MOE_BLOCK
    tk = t * TOP_K
    n_blocks = tk // bm + N_EXPERTS
    e_idx = meta[:, 0:TOP_K].astype(I32)
    rank = meta[:, 2 * TOP_K:3 * TOP_K].astype(I32)
    eids = jnp.arange(N_EXPERTS, dtype=I32)
    nblk_e = (counts + bm - 1) // bm
    blk_end = jnp.sum(jnp.where(eids[None, :] <= eids[:, None], nblk_e[None, :], 0), axis=1)
    blk_start = blk_end - nblk_e
    nreal = blk_end[N_EXPERTS - 1]
    pad_start = blk_start * bm
    pos = jnp.sum(jnp.where(e_idx[..., None] == eids, pad_start, 0), axis=-1) + rank

    blk = jnp.arange(n_blocks, dtype=I32)
    bexp = jnp.minimum(jnp.sum((blk_end[None, :] <= blk[:, None]).astype(I32), axis=1),
                       N_EXPERTS - 1)
    blk_is_e = bexp[:, None] == eids[None, :]
    pick = lambda tab: jnp.sum(jnp.where(blk_is_e, tab[None, :], 0), axis=1)
    first = (blk == pick(blk_start)).astype(I32)
    nxt_e = jnp.sum((blk_end[None, :] <= blk_end[:, None]).astype(I32), axis=1)
    nxt_e = jnp.where(blk_end < nreal, jnp.minimum(nxt_e, N_EXPERTS - 1), -1)
    nxt = pick(nxt_e)
    nvalid = jnp.clip(pick(counts) - (blk - pick(blk_start)) * bm, 1, bm)

    slot_id = jnp.arange(TOP_K, dtype=I32)[None, :] * t + jnp.arange(t, dtype=I32)[:, None]
    row_dst = jnp.zeros((n_blocks * bm,), I32).at[pos.reshape(-1)].set(
        slot_id.reshape(-1), unique_indices=True).reshape(n_blocks, bm)
    col = jnp.arange(bm, dtype=I32)[None, :]
    last_real = jnp.sum(jnp.where(col == nvalid[:, None] - 1, row_dst, 0), axis=1, keepdims=True)
    row_dst = jnp.where(col < nvalid[:, None], row_dst, last_real)
    row_src = row_dst % t
    return bexp, first, nxt, nvalid, nreal.reshape(1).astype(I32), row_src, row_dst


def kernel(x, norm_mix_g, w_in, conv_w, w_gk_up, b_gk_up, gla_norm_g, w_out, norm_ffn_g,
           w_router, b_router, w_gate_up, b_gate_up, w_down, b_down, norm_final_g):
    bsz, seq, d = x.shape
    t = bsz * seq
    assert w_in.shape[0] == 1, "single-layer trunk only"
    l = 0
    d_in = w_in.shape[-1]
    win = jnp.pad(w_in[l], ((0, 0), (0, D_IN_PAD - d_in))).astype(BF16)
    wgk = jnp.pad(w_gk_up[l], ((0, LANES - GLA_RANK), (0, 0))).astype(BF16)
    wr = jnp.pad(w_router[l], ((0, 0), (0, LANES - N_EXPERTS)))
    wrh = wr.astype(BF16)
    wrl = (wr - wrh.astype(F32)).astype(BF16)
    br = jnp.pad(b_router[l], (0, LANES - N_EXPERTS), constant_values=NEG_BIG).reshape(1, LANES)

    x1, h2, meta, cnt = _mixer_call(
        x, norm_mix_g[l].reshape(1, d), win, conv_w[l], wgk, b_gk_up[l].reshape(1, GLA_QK),
        gla_norm_g[l].reshape(1, GLA_DV), w_out[l].astype(BF16), norm_ffn_g[l].reshape(1, d),
        wrh, wrl, br)

    meta2 = meta.reshape(t, LANES)
    counts = cnt[0, :N_EXPERTS].astype(I32)
    bexp, first, nxt, nval, nreal, row_src, row_dst = _routing_tables(meta2, counts, t)
    slots = _experts_call(bexp, first, nxt, nval, nreal, row_src, row_dst,
                          h2.reshape(t * ROW_TILES, LANES),
                          w_gate_up[l], b_gate_up[l], w_down[l], b_down[l])
    out = _combine_call(x1.reshape(t, d), meta2, slots, norm_final_g.reshape(1, d))
    return out.reshape(bsz, seq, d)
```

```python
import jax
import jax.numpy as jnp
from jax import lax
from jax.experimental import pallas as pl
from jax.experimental.pallas import tpu as pltpu

F32 = jnp.float32
BF16 = jnp.bfloat16
I32 = jnp.int32

D_MODEL = 1024
CONV_WIDTH = 512
CONV_K = 3
GLA_WIDTH = 512
GLA_HEADS = 4
GLA_DV = 128
GLA_DK = 64
GLA_QK = GLA_HEADS * GLA_DK
GLA_RANK = 16
GLA_NORMALIZER = 16.0
GLA_CHUNK = 64
N_EXPERTS = 32
TOP_K = 4
D_FF = 1024
SWIGLU_LIMIT = 7.0
SWIGLU_ALPHA = 1.702
RMS_EPS = 1e-5

LANES = 128
SUBLANES = 8
ROW_TILES = D_MODEL // LANES

OFF_UH = 0
OFF_GB = OFF_UH + CONV_WIDTH
OFF_GC = OFF_GB + CONV_WIDTH
OFF_Q = OFF_GC + CONV_WIDTH
OFF_K = OFF_Q + GLA_QK
OFF_V = OFF_K + GLA_QK
OFF_GO = OFF_V + GLA_WIDTH
OFF_GKL = OFF_GO + GLA_WIDTH
D_IN_PAD = OFF_GKL + LANES

SEQ_TILE = 512
MOE_BLOCK = 256
DISPATCH_TILE = 512
FF_CHUNK = 256
CAST_ROWS = 128
COMBINE_TILE = 256
NEG_BIG = -1e30
VMEM_LIMIT = 56 * 1024 * 1024


def _rms(x, g):
    return x * lax.rsqrt(jnp.mean(x * x, axis=-1, keepdims=True) + RMS_EPS) * g


def _dot(a, b):
    return jnp.dot(a, b, preferred_element_type=F32)


def _dot_nt(a, b):
    return lax.dot_general(a, b, (((1,), (1,)), ((), ())), preferred_element_type=F32)


def _split_bf16(x):
    hi = x.astype(BF16)
    lo = (x - hi.astype(F32)).astype(BF16)
    return hi, lo


def _mixer_kernel(x_ref, g1_ref, win_ref, convw_ref, wgk_ref, bgk_ref, gng_ref, wout_ref,
                  g2_ref, wrh_ref, wrl_ref, br_ref,
                  x1_ref, h2_ref, meta_ref, cnt_ref,
                  proj_ref, ubuf_ref, la_ref, ycat_ref, state_ref, carry_ref):
    ts = x_ref.shape[1]
    b_idx = pl.program_id(0)
    s_idx = pl.program_id(1)

    @pl.when(s_idx == 0)
    def _():
        state_ref[...] = jnp.zeros_like(state_ref)
        ubuf_ref[0:SUBLANES, :] = jnp.zeros((SUBLANES, CONV_WIDTH), F32)

    @pl.when((s_idx == 0) & (b_idx == 0))
    def _():
        carry_ref[...] = jnp.zeros_like(carry_ref)

    x = x_ref[0]
    h = _rms(x, g1_ref[...]).astype(BF16)
    proj_ref[...] = _dot(h, win_ref[...])

    u = proj_ref[:, OFF_GC:OFF_GC + CONV_WIDTH] * proj_ref[:, OFF_UH:OFF_UH + CONV_WIDTH]
    ubuf_ref[SUBLANES:SUBLANES + ts, :] = u
    u1 = ubuf_ref[pl.ds(SUBLANES - 1, ts), :]
    u2 = ubuf_ref[pl.ds(SUBLANES - 2, ts), :]
    conv = convw_ref[0:1, :] * u2 + convw_ref[1:2, :] * u1 + convw_ref[2:3, :] * u
    ycat_ref[:, 0:CONV_WIDTH] = (proj_ref[:, OFF_GB:OFF_GB + CONV_WIDTH] * conv).astype(BF16)
    ubuf_ref[0:SUBLANES, :] = ubuf_ref[ts:ts + SUBLANES, :]

    gk = _dot(proj_ref[:, OFF_GKL:OFF_GKL + LANES].astype(BF16), wgk_ref[...]) + bgk_ref[...]
    log_sig = jnp.minimum(gk, 0.0) - jnp.log1p(jnp.exp(-jnp.abs(gk)))
    la_ref[...] = log_sig / GLA_NORMALIZER

    ci = lax.broadcasted_iota(I32, (GLA_CHUNK, GLA_CHUNK), 0)
    cj = lax.broadcasted_iota(I32, (GLA_CHUNK, GLA_CHUNK), 1)
    tri_incl = (cj <= ci).astype(BF16)
    causal = cj <= ci
    causal4 = jnp.concatenate([causal] * GLA_HEADS, axis=0)
    lane_qk = lax.broadcasted_iota(I32, (1, GLA_QK), 1)
    head_masks = [((lane_qk >= hd * GLA_DK) & (lane_qk < (hd + 1) * GLA_DK)).astype(F32)
                  for hd in range(GLA_HEADS)]
    gng = gng_ref[...]

    def chunk_body(c, carry):
        r0 = pl.multiple_of(c * GLA_CHUNK, GLA_CHUNK)
        rows = pl.ds(r0, GLA_CHUNK)
        la_hi, la_lo = _split_bf16(la_ref[rows, :])
        bcum = _dot(tri_incl, la_hi) + _dot(tri_incl, la_lo)
        blast = bcum[GLA_CHUNK - 1:GLA_CHUNK, :]
        q = proj_ref[rows, OFF_Q:OFF_Q + GLA_QK] * (GLA_DK ** -0.5)
        k = proj_ref[rows, OFF_K:OFF_K + GLA_QK]
        v = proj_ref[rows, OFF_V:OFF_V + GLA_WIDTH].astype(BF16)
        qd = q * jnp.exp(bcum)
        kd = (k * jnp.exp(-bcum)).astype(BF16)
        kr = k * jnp.exp(blast - bcum)

        q_stack = jnp.concatenate([qd * m for m in head_masks], axis=0).astype(BF16)
        scores = jnp.where(causal4, _dot_nt(q_stack, kd), 0.0).astype(BF16)

        state = state_ref[...]
        o_inter = _dot(qd.astype(BF16), state.astype(BF16))
        o_intra = jnp.concatenate(
            [_dot(scores[hd * GLA_CHUNK:(hd + 1) * GLA_CHUNK, :],
                  v[:, hd * GLA_DV:(hd + 1) * GLA_DV]) for hd in range(GLA_HEADS)], axis=1)
        o = o_inter + o_intra

        kt = jnp.concatenate([kr, jnp.broadcast_to(blast, (GLA_CHUNK, GLA_QK))], axis=0).T
        dcol = jnp.exp(kt[:, GLA_CHUNK:GLA_CHUNK + 1])
        lane_c = lax.broadcasted_iota(I32, (GLA_QK, 2 * GLA_CHUNK), 1)
        kt_b = jnp.where(lane_c < GLA_CHUNK, kt, 0.0).astype(BF16)
        v_pad = jnp.concatenate([v, jnp.zeros_like(v)], axis=0)
        for hd in range(GLA_HEADS):
            rs = slice(hd * GLA_DK, (hd + 1) * GLA_DK)
            cs = slice(hd * GLA_DV, (hd + 1) * GLA_DV)
            kv = _dot(kt_b[rs, :], v_pad[:, cs])
            state_ref[rs, cs] = dcol[rs, :] * state[rs, cs] + kv

        g_out = proj_ref[rows, OFF_GO:OFF_GO + GLA_WIDTH]
        o_n = jnp.concatenate(
            [_rms(o[:, hd * GLA_DV:(hd + 1) * GLA_DV], gng) for hd in range(GLA_HEADS)], axis=1)
        y = o_n * (g_out * jax.nn.sigmoid(g_out))
        ycat_ref[rows, CONV_WIDTH:CONV_WIDTH + GLA_WIDTH] = y.astype(BF16)
        return carry

    lax.fori_loop(0, ts // GLA_CHUNK, chunk_body, 0)

    x1 = x + _dot(ycat_ref[...], wout_ref[...])
    x1_ref[0] = x1
    h2 = _rms(x1, g2_ref[...])
    for j in range(ROW_TILES):
        h2_ref[0, pl.ds(j, ts, stride=ROW_TILES), :] = h2[:, j * LANES:(j + 1) * LANES]
    h2_hi, h2_lo = _split_bf16(h2)
    logits = (_dot(h2_hi, wrh_ref[...]) + _dot(h2_lo, wrh_ref[...]) + _dot(h2_hi, wrl_ref[...])
              + br_ref[...])

    lane = lax.broadcasted_iota(I32, (ts, LANES), 1).astype(F32)
    work = logits
    sel = jnp.zeros((ts, LANES), F32)
    top_v, top_i, top_oh = [], [], []
    for _ in range(TOP_K):
        m = jnp.max(work, axis=-1, keepdims=True)
        idx = jnp.min(jnp.where(work == m, lane, float(LANES)), axis=-1, keepdims=True)
        oh = lane == idx
        top_v.append(m)
        top_i.append(idx)
        top_oh.append(oh)
        sel = sel + oh.astype(F32)
        work = jnp.where(oh, -jnp.inf, work)
    exps = [jnp.exp(tv - top_v[0]) for tv in top_v]
    denom = exps[0] + exps[1] + exps[2] + exps[3]
    gates = [e / denom for e in exps]

    ti = lax.broadcasted_iota(I32, (ts, ts), 0)
    tj = lax.broadcasted_iota(I32, (ts, ts), 1)
    strict_lower = (tj < ti).astype(BF16)
    prefix = _dot(strict_lower, sel.astype(BF16)) + carry_ref[0:1, :]
    ranks = [jnp.sum(jnp.where(oh, prefix, 0.0), axis=-1, keepdims=True) for oh in top_oh]
    new_carry = carry_ref[0:1, :] + jnp.sum(sel, axis=0, keepdims=True)
    carry_ref[...] = jnp.broadcast_to(new_carry, carry_ref.shape)
    cnt_ref[...] = jnp.broadcast_to(new_carry, cnt_ref.shape)

    meta = jnp.zeros((ts, LANES), F32)
    for kk in range(TOP_K):
        meta = jnp.where(lane == float(kk), top_i[kk], meta)
        meta = jnp.where(lane == float(TOP_K + kk), gates[kk], meta)
        meta = jnp.where(lane == float(2 * TOP_K + kk), ranks[kk], meta)
    meta_ref[0] = meta


def _mixer_call(x, g1, win, convw, wgk, bgk, gng, wout, g2, wrh, wrl, br):
    bsz, seq, d = x.shape
    ts = SEQ_TILE
    grid = (bsz, seq // ts)

    def const(shape):
        return pl.BlockSpec(shape, lambda b, s: (0,) * len(shape))

    tile = lambda w: pl.BlockSpec((1, ts, w), lambda b, s: (b, s, 0))
    return pl.pallas_call(
        _mixer_kernel,
        grid=grid,
        in_specs=[tile(d), const(g1.shape), const(win.shape), const(convw.shape),
                  const(wgk.shape), const(bgk.shape), const(gng.shape), const(wout.shape),
                  const(g2.shape), const(wrh.shape), const(wrl.shape), const(br.shape)],
        out_specs=[tile(d),
                   pl.BlockSpec((1, ts * ROW_TILES, LANES), lambda b, s: (b, s, 0)),
                   tile(LANES), const((SUBLANES, LANES))],
        out_shape=[jax.ShapeDtypeStruct((bsz, seq, d), F32),
                   jax.ShapeDtypeStruct((bsz, seq * ROW_TILES, LANES), F32),
                   jax.ShapeDtypeStruct((bsz, seq, LANES), F32),
                   jax.ShapeDtypeStruct((SUBLANES, LANES), F32)],
        scratch_shapes=[pltpu.VMEM((ts, D_IN_PAD), F32),
                        pltpu.VMEM((ts + SUBLANES, CONV_WIDTH), F32),
                        pltpu.VMEM((ts, GLA_QK), F32),
                        pltpu.VMEM((ts, D_MODEL), BF16),
                        pltpu.VMEM((GLA_QK, GLA_WIDTH), F32),
                        pltpu.VMEM((SUBLANES, LANES), F32)],
        compiler_params=pltpu.CompilerParams(
            dimension_semantics=("arbitrary", "arbitrary"), vmem_limit_bytes=VMEM_LIMIT),
        name="mixer",
    )(x, g1, win, convw, wgk, bgk, gng, wout, g2, wrh, wrl, br)


def _dispatch_kernel(pos_ref, h2_ref, xs_hbm, sem):
    td = h2_ref.shape[0] // ROW_TILES

    def body(tok, carry):
        src = h2_ref.at[pl.ds(pl.multiple_of(tok * ROW_TILES, ROW_TILES), ROW_TILES), :]
        for kk in range(TOP_K):
            row = pl.multiple_of(pos_ref[0, 0, tok * TOP_K + kk], ROW_TILES)
            pltpu.make_async_copy(src, xs_hbm.at[pl.ds(row, ROW_TILES), :], sem.at[0]).start(
                priority=kk % 2)
        return carry

    lax.fori_loop(0, td, body, 0, unroll=2)
    for _ in range(TOP_K):
        pltpu.make_async_copy(h2_ref, xs_hbm.at[pl.ds(0, td * ROW_TILES), :], sem.at[0]).wait()


def _dispatch_call(pos_rows, h2, n_rows):
    t = h2.shape[0] // ROW_TILES
    td = DISPATCH_TILE
    pos3 = pos_rows.reshape(t // td, 1, td * TOP_K)
    return pl.pallas_call(
        _dispatch_kernel,
        grid=(t // td,),
        in_specs=[pl.BlockSpec((1, 1, td * TOP_K), lambda i: (i, 0, 0), memory_space=pltpu.SMEM),
                  pl.BlockSpec((td * ROW_TILES, LANES), lambda i: (i, 0))],
        out_specs=pl.BlockSpec(memory_space=pl.ANY),
        out_shape=jax.ShapeDtypeStruct((n_rows * ROW_TILES, LANES), F32),
        scratch_shapes=[pltpu.SemaphoreType.DMA((1,))],
        compiler_params=pltpu.CompilerParams(
            dimension_semantics=("arbitrary",), vmem_limit_bytes=VMEM_LIMIT,
            has_side_effects=True),
        name="dispatch",
    )(pos3, h2)


def _experts_kernel(bexp_ref, first_ref, nxt_ref, nreal_ref,
                    xs_ref, wgu_hbm, bgu_ref, wd_hbm, bd_ref,
                    ys_ref,
                    xb_ref, act_ref, wgu_stage, wd_stage, wgu_bf, wd_bf, wsem):
    i = pl.program_id(0)
    nreal = nreal_ref[0]
    bm = xb_ref.shape[0]

    def weight_copies(e):
        return (pltpu.make_async_copy(wgu_hbm.at[e], wgu_stage, wsem.at[0]),
                pltpu.make_async_copy(wd_hbm.at[e], wd_stage, wsem.at[1]))

    @pl.when(i >= nreal)
    def _():
        ys_ref[...] = jnp.zeros_like(ys_ref)

    @pl.when(i < nreal)
    def _():
        e = bexp_ref[i]

        @pl.when(i == 0)
        def _():
            for cp in weight_copies(e):
                cp.start()

        @pl.when(first_ref[i] == 1)
        def _():
            for cp in weight_copies(e):
                cp.wait()

            def cast_gu(r, carry):
                rows = pl.ds(pl.multiple_of(r * CAST_ROWS, CAST_ROWS), CAST_ROWS)
                wgu_bf[rows, :] = wgu_stage[rows, :].astype(BF16)
                return carry

            def cast_d(r, carry):
                rows = pl.ds(pl.multiple_of(r * CAST_ROWS, CAST_ROWS), CAST_ROWS)
                wd_bf[rows, :] = wd_stage[rows, :].astype(BF16)
                return carry

            lax.fori_loop(0, D_MODEL // CAST_ROWS, cast_gu, 0)
            lax.fori_loop(0, D_FF // CAST_ROWS, cast_d, 0)

            @pl.when(nxt_ref[i] >= 0)
            def _():
                for cp in weight_copies(nxt_ref[i]):
                    cp.start()

        for j in range(ROW_TILES):
            xb_ref[:, j * LANES:(j + 1) * LANES] = (
                xs_ref[pl.ds(j, bm, stride=ROW_TILES), :].astype(BF16))
        for c in range(D_FF // FF_CHUNK):
            f0 = c * FF_CHUNK
            xb = xb_ref[...]
            gate = _dot(xb, wgu_bf[:, f0:f0 + FF_CHUNK]) + bgu_ref[0, :, f0:f0 + FF_CHUNK]
            up = (_dot(xb, wgu_bf[:, D_FF + f0:D_FF + f0 + FF_CHUNK])
                  + bgu_ref[0, :, D_FF + f0:D_FF + f0 + FF_CHUNK])
            gate = jnp.minimum(gate, SWIGLU_LIMIT)
            up = jnp.clip(up, -SWIGLU_LIMIT, SWIGLU_LIMIT)
            glu = gate * jax.nn.sigmoid(gate * SWIGLU_ALPHA)
            act_ref[:, f0:f0 + FF_CHUNK] = ((up + 1.0) * glu).astype(BF16)
        out = _dot(act_ref[...], wd_bf[...]) + bd_ref[0]
        for j in range(ROW_TILES):
            ys_ref[pl.ds(j, bm, stride=ROW_TILES), :] = out[:, j * LANES:(j + 1) * LANES]


def _experts_call(bexp, first, nxt, nreal, xs, wgu, bgu, wd, bd):
    bm = MOE_BLOCK
    n_blocks = xs.shape[0] // (bm * ROW_TILES)
    bgu3 = bgu.reshape(N_EXPERTS, 1, 2 * D_FF)
    bd3 = bd.reshape(N_EXPERTS, 1, D_MODEL)
    grid_spec = pltpu.PrefetchScalarGridSpec(
        num_scalar_prefetch=4,
        grid=(n_blocks,),
        in_specs=[
            pl.BlockSpec((bm * ROW_TILES, LANES),
                         lambda i, be, fi, nx, nr: (jnp.minimum(i, nr[0] - 1), 0)),
            pl.BlockSpec(memory_space=pl.ANY),
            pl.BlockSpec((1, 1, 2 * D_FF), lambda i, be, fi, nx, nr: (be[i], 0, 0)),
            pl.BlockSpec(memory_space=pl.ANY),
            pl.BlockSpec((1, 1, D_MODEL), lambda i, be, fi, nx, nr: (be[i], 0, 0)),
        ],
        out_specs=pl.BlockSpec((bm * ROW_TILES, LANES), lambda i, be, fi, nx, nr: (i, 0)),
        scratch_shapes=[pltpu.VMEM((bm, D_MODEL), BF16),
                        pltpu.VMEM((bm, D_FF), BF16),
                        pltpu.VMEM((D_MODEL, 2 * D_FF), F32),
                        pltpu.VMEM((D_FF, D_MODEL), F32),
                        pltpu.VMEM((D_MODEL, 2 * D_FF), BF16),
                        pltpu.VMEM((D_FF, D_MODEL), BF16),
                        pltpu.SemaphoreType.DMA((2,))],
    )
    return pl.pallas_call(
        _experts_kernel,
        grid_spec=grid_spec,
        out_shape=jax.ShapeDtypeStruct(xs.shape, F32),
        compiler_params=pltpu.CompilerParams(
            dimension_semantics=("arbitrary",), vmem_limit_bytes=VMEM_LIMIT),
        name="experts",
    )(bexp, first, nxt, nreal, xs, wgu, bgu3, wd, bd3)


def _combine_kernel(pos_cur_ref, pos_nxt_ref, x1_ref, meta_ref, g_ref, ys_hbm, out_ref, gbuf, gsem):
    i = pl.program_id(0)
    n = pl.num_programs(0)
    tb = x1_ref.shape[0]
    slot = i % 2

    def start_gather(pos_ref, buf_slot):
        def body(tok, carry):
            for kk in range(TOP_K):
                row = pl.multiple_of(pos_ref[0, 0, tok * TOP_K + kk], ROW_TILES)
                dst = pl.multiple_of((kk * tb + tok) * ROW_TILES, ROW_TILES)
                pltpu.make_async_copy(ys_hbm.at[pl.ds(row, ROW_TILES), :],
                                      gbuf.at[buf_slot, pl.ds(dst, ROW_TILES), :],
                                      gsem.at[buf_slot]).start(priority=kk % 2)
            return carry
        lax.fori_loop(0, tb, body, 0, unroll=2)

    @pl.when(i == 0)
    def _():
        start_gather(pos_cur_ref, 0)

    @pl.when(i + 1 < n)
    def _():
        start_gather(pos_nxt_ref, 1 - slot)

    pltpu.make_async_copy(ys_hbm.at[pl.ds(0, TOP_K * tb * ROW_TILES), :], gbuf.at[slot],
                          gsem.at[slot]).wait()

    meta = meta_ref[...]
    gates = [meta[:, TOP_K + kk:TOP_K + kk + 1] for kk in range(TOP_K)]
    ssq = jnp.zeros((tb, 1), F32)
    for j in range(ROW_TILES):
        cols = slice(j * LANES, (j + 1) * LANES)
        acc = x1_ref[:, cols]
        for kk in range(TOP_K):
            acc = acc + gates[kk] * gbuf[slot, pl.ds(kk * tb * ROW_TILES + j, tb, stride=ROW_TILES), :]
        out_ref[:, cols] = acc
        ssq = ssq + jnp.sum(acc * acc, axis=-1, keepdims=True)
    inv = lax.rsqrt(ssq / D_MODEL + RMS_EPS)
    out_ref[...] = out_ref[...] * inv * g_ref[...]


def _combine_call(pos_rows, x1, meta, g, ys):
    t, d = x1.shape
    tb = COMBINE_TILE
    nb = t // tb
    pos3 = pos_rows.reshape(nb, 1, tb * TOP_K)
    smem_blk = lambda fn: pl.BlockSpec((1, 1, tb * TOP_K), fn, memory_space=pltpu.SMEM)
    return pl.pallas_call(
        _combine_kernel,
        grid=(nb,),
        in_specs=[smem_blk(lambda i: (i, 0, 0)),
                  smem_blk(lambda i: (jnp.minimum(i + 1, nb - 1), 0, 0)),
                  pl.BlockSpec((tb, d), lambda i: (i, 0)),
                  pl.BlockSpec((tb, LANES), lambda i: (i, 0)),
                  pl.BlockSpec((1, d), lambda i: (0, 0)),
                  pl.BlockSpec(memory_space=pl.ANY)],
        out_specs=pl.BlockSpec((tb, d), lambda i: (i, 0)),
        out_shape=jax.ShapeDtypeStruct((t, d), F32),
        scratch_shapes=[pltpu.VMEM((2, TOP_K * tb * ROW_TILES, LANES), F32),
                        pltpu.SemaphoreType.DMA((2,))],
        compiler_params=pltpu.CompilerParams(
            dimension_semantics=("arbitrary",), vmem_limit_bytes=VMEM_LIMIT),
        name="combine",
    )(pos3, pos3, x1, meta, g, ys)


def _routing_tables(meta, counts, t):
    bm = MOE_BLOCK
    n_blocks = t * TOP_K // bm + N_EXPERTS
    e_idx = meta[:, 0:TOP_K].astype(I32)
    rank = meta[:, 2 * TOP_K:3 * TOP_K].astype(I32)
    eids = jnp.arange(N_EXPERTS, dtype=I32)
    nblk_e = (counts + bm - 1) // bm
    blk_end = jnp.sum(jnp.where(eids[None, :] <= eids[:, None], nblk_e[None, :], 0), axis=1)
    blk_start = blk_end - nblk_e
    nreal = blk_end[N_EXPERTS - 1]
    pad_start = blk_start * bm
    pos = jnp.sum(jnp.where(e_idx[..., None] == eids, pad_start, 0), axis=-1) + rank

    blk = jnp.arange(n_blocks, dtype=I32)
    bexp = jnp.minimum(jnp.sum((blk_end[None, :] <= blk[:, None]).astype(I32), axis=1),
                       N_EXPERTS - 1)
    blk_is_e = bexp[:, None] == eids[None, :]
    pick = lambda tab: jnp.sum(jnp.where(blk_is_e, tab[None, :], 0), axis=1)
    first = (blk == pick(blk_start)).astype(I32)
    nxt_e = jnp.sum((blk_end[None, :] <= blk_end[:, None]).astype(I32), axis=1)
    nxt_e = jnp.where(blk_end < nreal, jnp.minimum(nxt_e, N_EXPERTS - 1), -1)
    nxt = pick(nxt_e)
    return pos, bexp, first, nxt, nreal.reshape(1).astype(I32), n_blocks


def kernel(x, norm_mix_g, w_in, conv_w, w_gk_up, b_gk_up, gla_norm_g, w_out, norm_ffn_g,
           w_router, b_router, w_gate_up, b_gate_up, w_down, b_down, norm_final_g):
    bsz, seq, d = x.shape
    t = bsz * seq
    assert w_in.shape[0] == 1, "single-layer trunk only"
    l = 0
    d_in = w_in.shape[-1]
    win = jnp.pad(w_in[l], ((0, 0), (0, D_IN_PAD - d_in))).astype(BF16)
    wgk = jnp.pad(w_gk_up[l], ((0, LANES - GLA_RANK), (0, 0))).astype(BF16)
    wr = jnp.pad(w_router[l], ((0, 0), (0, LANES - N_EXPERTS)))
    wrh = wr.astype(BF16)
    wrl = (wr - wrh.astype(F32)).astype(BF16)
    br = jnp.pad(b_router[l], (0, LANES - N_EXPERTS), constant_values=NEG_BIG).reshape(1, LANES)

    x1, h2, meta, cnt = _mixer_call(
        x, norm_mix_g[l].reshape(1, d), win, conv_w[l], wgk, b_gk_up[l].reshape(1, GLA_QK),
        gla_norm_g[l].reshape(1, GLA_DV), w_out[l].astype(BF16), norm_ffn_g[l].reshape(1, d),
        wrh, wrl, br)

    meta2 = meta.reshape(t, LANES)
    counts = cnt[0, :N_EXPERTS].astype(I32)
    pos, bexp, first, nxt, nreal, n_blocks = _routing_tables(meta2, counts, t)
    pos_rows = (pos * ROW_TILES).reshape(t * TOP_K)
    xs = _dispatch_call(pos_rows, h2.reshape(t * ROW_TILES, LANES), n_blocks * MOE_BLOCK)
    ys = _experts_call(bexp, first, nxt, nreal, xs, w_gate_up[l], b_gate_up[l], w_down[l], b_down[l])
    out = _combine_call(pos_rows, x1.reshape(t, d), meta2, norm_final_g.reshape(1, d), ys)
    return out.reshape(bsz, seq, d)
```

```python
import jax
import jax.numpy as jnp
from jax import lax
from jax.experimental import pallas as pl
from jax.experimental.pallas import tpu as pltpu

F32 = jnp.float32
BF16 = jnp.bfloat16
I32 = jnp.int32

D_MODEL = 1024
CONV_WIDTH = 512
CONV_K = 3
GLA_WIDTH = 512
GLA_HEADS = 4
GLA_DV = 128
GLA_DK = 64
GLA_QK = GLA_HEADS * GLA_DK
GLA_RANK = 16
GLA_NORMALIZER = 16.0
GLA_CHUNK = 64
N_EXPERTS = 32
TOP_K = 4
D_FF = 1024
SWIGLU_LIMIT = 7.0
SWIGLU_ALPHA = 1.702
RMS_EPS = 1e-5

LANES = 128
SUBLANES = 8
ROW_TILES = D_MODEL // LANES

OFF_UH = 0
OFF_GB = OFF_UH + CONV_WIDTH
OFF_GC = OFF_GB + CONV_WIDTH
OFF_Q = OFF_GC + CONV_WIDTH
OFF_K = OFF_Q + GLA_QK
OFF_V = OFF_K + GLA_QK
OFF_GO = OFF_V + GLA_WIDTH
OFF_GKL = OFF_GO + GLA_WIDTH
D_IN_PAD = OFF_GKL + LANES

SEQ_TILE = 512
MOE_BLOCK = 256
DISPATCH_TILE = 512
FF_CHUNK = 256
CAST_ROWS = 128
COMBINE_TILE = 256
NEG_BIG = -1e30
VMEM_LIMIT = 56 * 1024 * 1024


def _rms(x, g):
    return x * lax.rsqrt(jnp.mean(x * x, axis=-1, keepdims=True) + RMS_EPS) * g


def _dot(a, b):
    return jnp.dot(a, b, preferred_element_type=F32)


def _dot_nt(a, b):
    return lax.dot_general(a, b, (((1,), (1,)), ((), ())), preferred_element_type=F32)


def _split_bf16(x):
    hi = x.astype(BF16)
    lo = (x - hi.astype(F32)).astype(BF16)
    return hi, lo


def _mixer_kernel(x_ref, g1_ref, win_ref, convw_ref, wgk_ref, bgk_ref, gng_ref, wout_ref,
                  g2_ref, wrh_ref, wrl_ref, br_ref,
                  x1_ref, h2_ref, meta_ref, cnt_ref,
                  proj_ref, ubuf_ref, la_ref, ycat_ref, state_ref, carry_ref):
    ts = x_ref.shape[1]
    b_idx = pl.program_id(0)
    s_idx = pl.program_id(1)

    @pl.when(s_idx == 0)
    def _():
        state_ref[...] = jnp.zeros_like(state_ref)
        ubuf_ref[0:SUBLANES, :] = jnp.zeros((SUBLANES, CONV_WIDTH), F32)

    @pl.when((s_idx == 0) & (b_idx == 0))
    def _():
        carry_ref[...] = jnp.zeros_like(carry_ref)

    x = x_ref[0]
    h = _rms(x, g1_ref[...]).astype(BF16)
    proj_ref[...] = _dot(h, win_ref[...])

    u = proj_ref[:, OFF_GC:OFF_GC + CONV_WIDTH] * proj_ref[:, OFF_UH:OFF_UH + CONV_WIDTH]
    ubuf_ref[SUBLANES:SUBLANES + ts, :] = u
    u1 = ubuf_ref[pl.ds(SUBLANES - 1, ts), :]
    u2 = ubuf_ref[pl.ds(SUBLANES - 2, ts), :]
    conv = convw_ref[0:1, :] * u2 + convw_ref[1:2, :] * u1 + convw_ref[2:3, :] * u
    ycat_ref[:, 0:CONV_WIDTH] = (proj_ref[:, OFF_GB:OFF_GB + CONV_WIDTH] * conv).astype(BF16)
    ubuf_ref[0:SUBLANES, :] = ubuf_ref[ts:ts + SUBLANES, :]

    gk = _dot(proj_ref[:, OFF_GKL:OFF_GKL + LANES].astype(BF16), wgk_ref[...]) + bgk_ref[...]
    log_sig = jnp.minimum(gk, 0.0) - jnp.log1p(jnp.exp(-jnp.abs(gk)))
    la_ref[...] = log_sig / GLA_NORMALIZER

    ci = lax.broadcasted_iota(I32, (GLA_CHUNK, GLA_CHUNK), 0)
    cj = lax.broadcasted_iota(I32, (GLA_CHUNK, GLA_CHUNK), 1)
    tri_incl = (cj <= ci).astype(BF16)
    causal = cj <= ci
    causal4 = jnp.concatenate([causal] * GLA_HEADS, axis=0)
    lane_qk = lax.broadcasted_iota(I32, (1, GLA_QK), 1)
    head_masks = [((lane_qk >= hd * GLA_DK) & (lane_qk < (hd + 1) * GLA_DK)).astype(F32)
                  for hd in range(GLA_HEADS)]
    gng = gng_ref[...]

    def chunk_body(c, carry):
        r0 = pl.multiple_of(c * GLA_CHUNK, GLA_CHUNK)
        rows = pl.ds(r0, GLA_CHUNK)
        la_hi, la_lo = _split_bf16(la_ref[rows, :])
        bcum = _dot(tri_incl, la_hi) + _dot(tri_incl, la_lo)
        blast = bcum[GLA_CHUNK - 1:GLA_CHUNK, :]
        q = proj_ref[rows, OFF_Q:OFF_Q + GLA_QK] * (GLA_DK ** -0.5)
        k = proj_ref[rows, OFF_K:OFF_K + GLA_QK]
        v = proj_ref[rows, OFF_V:OFF_V + GLA_WIDTH].astype(BF16)
        qd = q * jnp.exp(bcum)
        kd = (k * jnp.exp(-bcum)).astype(BF16)
        kr = k * jnp.exp(blast - bcum)

        q_stack = jnp.concatenate([qd * m for m in head_masks], axis=0).astype(BF16)
        scores = jnp.where(causal4, _dot_nt(q_stack, kd), 0.0).astype(BF16)

        state = state_ref[...]
        o_inter = _dot(qd.astype(BF16), state.astype(BF16))
        o_intra = jnp.concatenate(
            [_dot(scores[hd * GLA_CHUNK:(hd + 1) * GLA_CHUNK, :],
                  v[:, hd * GLA_DV:(hd + 1) * GLA_DV]) for hd in range(GLA_HEADS)], axis=1)
        o = o_inter + o_intra

        kt = jnp.concatenate([kr, jnp.broadcast_to(blast, (GLA_CHUNK, GLA_QK))], axis=0).T
        dcol = jnp.exp(kt[:, GLA_CHUNK:GLA_CHUNK + 1])
        lane_c = lax.broadcasted_iota(I32, (GLA_QK, 2 * GLA_CHUNK), 1)
        kt_b = jnp.where(lane_c < GLA_CHUNK, kt, 0.0).astype(BF16)
        v_pad = jnp.concatenate([v, jnp.zeros_like(v)], axis=0)
        for hd in range(GLA_HEADS):
            rs = slice(hd * GLA_DK, (hd + 1) * GLA_DK)
            cs = slice(hd * GLA_DV, (hd + 1) * GLA_DV)
            kv = _dot(kt_b[rs, :], v_pad[:, cs])
            state_ref[rs, cs] = dcol[rs, :] * state[rs, cs] + kv

        g_out = proj_ref[rows, OFF_GO:OFF_GO + GLA_WIDTH]
        o_n = jnp.concatenate(
            [_rms(o[:, hd * GLA_DV:(hd + 1) * GLA_DV], gng) for hd in range(GLA_HEADS)], axis=1)
        y = o_n * (g_out * jax.nn.sigmoid(g_out))
        ycat_ref[rows, CONV_WIDTH:CONV_WIDTH + GLA_WIDTH] = y.astype(BF16)
        return carry

    lax.fori_loop(0, ts // GLA_CHUNK, chunk_body, 0)

    x1 = x + _dot(ycat_ref[...], wout_ref[...])
    x1_ref[0] = x1
    h2 = _rms(x1, g2_ref[...])
    for j in range(ROW_TILES):
        h2_ref[0, pl.ds(j, ts, stride=ROW_TILES), :] = h2[:, j * LANES:(j + 1) * LANES]
    h2_hi, h2_lo = _split_bf16(h2)
    logits = (_dot(h2_hi, wrh_ref[...]) + _dot(h2_lo, wrh_ref[...]) + _dot(h2_hi, wrl_ref[...])
              + br_ref[...])

    lane = lax.broadcasted_iota(I32, (ts, LANES), 1).astype(F32)
    work = logits
    sel = jnp.zeros((ts, LANES), F32)
    top_v, top_i, top_oh = [], [], []
    for _ in range(TOP_K):
        m = jnp.max(work, axis=-1, keepdims=True)
        idx = jnp.min(jnp.where(work == m, lane, float(LANES)), axis=-1, keepdims=True)
        oh = lane == idx
        top_v.append(m)
        top_i.append(idx)
        top_oh.append(oh)
        sel = sel + oh.astype(F32)
        work = jnp.where(oh, -jnp.inf, work)
    exps = [jnp.exp(tv - top_v[0]) for tv in top_v]
    denom = exps[0] + exps[1] + exps[2] + exps[3]
    gates = [e / denom for e in exps]

    ti = lax.broadcasted_iota(I32, (ts, ts), 0)
    tj = lax.broadcasted_iota(I32, (ts, ts), 1)
    strict_lower = (tj < ti).astype(BF16)
    prefix = _dot(strict_lower, sel.astype(BF16)) + carry_ref[0:1, :]
    ranks = [jnp.sum(jnp.where(oh, prefix, 0.0), axis=-1, keepdims=True) for oh in top_oh]
    new_carry = carry_ref[0:1, :] + jnp.sum(sel, axis=0, keepdims=True)
    carry_ref[...] = jnp.broadcast_to(new_carry, carry_ref.shape)
    cnt_ref[...] = jnp.broadcast_to(new_carry, cnt_ref.shape)

    meta = jnp.zeros((ts, LANES), F32)
    for kk in range(TOP_K):
        meta = jnp.where(lane == float(kk), top_i[kk], meta)
        meta = jnp.where(lane == float(TOP_K + kk), gates[kk], meta)
        meta = jnp.where(lane == float(2 * TOP_K + kk), ranks[kk], meta)
    meta_ref[0] = meta


def _mixer_call(x, g1, win, convw, wgk, bgk, gng, wout, g2, wrh, wrl, br):
    bsz, seq, d = x.shape
    ts = SEQ_TILE
    grid = (bsz, seq // ts)

    def const(shape):
        return pl.BlockSpec(shape, lambda b, s: (0,) * len(shape))

    tile = lambda w: pl.BlockSpec((1, ts, w), lambda b, s: (b, s, 0))
    return pl.pallas_call(
        _mixer_kernel,
        grid=grid,
        in_specs=[tile(d), const(g1.shape), const(win.shape), const(convw.shape),
                  const(wgk.shape), const(bgk.shape), const(gng.shape), const(wout.shape),
                  const(g2.shape), const(wrh.shape), const(wrl.shape), const(br.shape)],
        out_specs=[tile(d),
                   pl.BlockSpec((1, ts * ROW_TILES, LANES), lambda b, s: (b, s, 0)),
                   tile(LANES), const((SUBLANES, LANES))],
        out_shape=[jax.ShapeDtypeStruct((bsz, seq, d), F32),
                   jax.ShapeDtypeStruct((bsz, seq * ROW_TILES, LANES), F32),
                   jax.ShapeDtypeStruct((bsz, seq, LANES), F32),
                   jax.ShapeDtypeStruct((SUBLANES, LANES), F32)],
        scratch_shapes=[pltpu.VMEM((ts, D_IN_PAD), F32),
                        pltpu.VMEM((ts + SUBLANES, CONV_WIDTH), F32),
                        pltpu.VMEM((ts, GLA_QK), F32),
                        pltpu.VMEM((ts, D_MODEL), BF16),
                        pltpu.VMEM((GLA_QK, GLA_WIDTH), F32),
                        pltpu.VMEM((SUBLANES, LANES), F32)],
        compiler_params=pltpu.CompilerParams(
            dimension_semantics=("arbitrary", "arbitrary"), vmem_limit_bytes=VMEM_LIMIT),
        name="mixer",
    )(x, g1, win, convw, wgk, bgk, gng, wout, g2, wrh, wrl, br)


def _dispatch_kernel(pad_lo_ref, pad_n_ref, pos_ref, h2_ref, xs_hbm, zbuf, sem, zsem):
    td = h2_ref.shape[0] // ROW_TILES

    @pl.when(pl.program_id(0) == 0)
    def _():
        zbuf[...] = jnp.zeros_like(zbuf)

        def fill(g, carry):
            n = pl.multiple_of(pad_n_ref[g], ROW_TILES)

            @pl.when(n > 0)
            def _():
                lo = pl.multiple_of(pad_lo_ref[g], ROW_TILES)
                cp = pltpu.make_async_copy(zbuf.at[pl.ds(0, n), :], xs_hbm.at[pl.ds(lo, n), :],
                                           zsem.at[0])
                cp.start()
                cp.wait()
            return carry

        lax.fori_loop(0, pad_lo_ref.shape[0], fill, 0)

    def body(tok, carry):
        src = h2_ref.at[pl.ds(pl.multiple_of(tok * ROW_TILES, ROW_TILES), ROW_TILES), :]
        for kk in range(TOP_K):
            row = pl.multiple_of(pos_ref[0, 0, tok * TOP_K + kk], ROW_TILES)
            pltpu.make_async_copy(src, xs_hbm.at[pl.ds(row, ROW_TILES), :], sem.at[0]).start(
                priority=kk % 2)
        return carry

    lax.fori_loop(0, td, body, 0, unroll=2)
    for _ in range(TOP_K):
        pltpu.make_async_copy(h2_ref, xs_hbm.at[pl.ds(0, td * ROW_TILES), :], sem.at[0]).wait()


def _dispatch_call(pad_lo, pad_n, pos_rows, h2, n_rows):
    t = h2.shape[0] // ROW_TILES
    td = DISPATCH_TILE
    pos3 = pos_rows.reshape(t // td, 1, td * TOP_K)
    grid_spec = pltpu.PrefetchScalarGridSpec(
        num_scalar_prefetch=2,
        grid=(t // td,),
        in_specs=[pl.BlockSpec((1, 1, td * TOP_K), lambda i, lo, n: (i, 0, 0),
                               memory_space=pltpu.SMEM),
                  pl.BlockSpec((td * ROW_TILES, LANES), lambda i, lo, n: (i, 0))],
        out_specs=pl.BlockSpec(memory_space=pl.ANY),
        scratch_shapes=[pltpu.VMEM((MOE_BLOCK * ROW_TILES, LANES), F32),
                        pltpu.SemaphoreType.DMA((1,)),
                        pltpu.SemaphoreType.DMA((1,))],
    )
    return pl.pallas_call(
        _dispatch_kernel,
        grid_spec=grid_spec,
        out_shape=jax.ShapeDtypeStruct((n_rows * ROW_TILES, LANES), F32),
        compiler_params=pltpu.CompilerParams(
            dimension_semantics=("arbitrary",), vmem_limit_bytes=VMEM_LIMIT,
            has_side_effects=True),
        name="dispatch",
    )(pad_lo, pad_n, pos3, h2)


def _experts_kernel(bexp_ref, first_ref, nxt_ref, nreal_ref,
                    xs_ref, wgu_hbm, bgu_ref, wd_hbm, bd_ref,
                    ys_ref,
                    xb_ref, act_ref, wgu_stage, wd_stage, wgu_bf, wd_bf, wsem):
    i = pl.program_id(0)
    nreal = nreal_ref[0]
    bm = xb_ref.shape[0]

    def weight_copies(e):
        return (pltpu.make_async_copy(wgu_hbm.at[e], wgu_stage, wsem.at[0]),
                pltpu.make_async_copy(wd_hbm.at[e], wd_stage, wsem.at[1]))

    @pl.when(i >= nreal)
    def _():
        ys_ref[...] = jnp.zeros_like(ys_ref)

    @pl.when(i < nreal)
    def _():
        e = bexp_ref[i]

        @pl.when(i == 0)
        def _():
            for cp in weight_copies(e):
                cp.start(priority=1)

        @pl.when(first_ref[i] == 1)
        def _():
            for cp in weight_copies(e):
                cp.wait()

            def cast_gu(r, carry):
                rows = pl.ds(pl.multiple_of(r * CAST_ROWS, CAST_ROWS), CAST_ROWS)
                wgu_bf[rows, :] = wgu_stage[rows, :].astype(BF16)
                return carry

            def cast_d(r, carry):
                rows = pl.ds(pl.multiple_of(r * CAST_ROWS, CAST_ROWS), CAST_ROWS)
                wd_bf[rows, :] = wd_stage[rows, :].astype(BF16)
                return carry

            lax.fori_loop(0, D_MODEL // CAST_ROWS, cast_gu, 0)
            lax.fori_loop(0, D_FF // CAST_ROWS, cast_d, 0)

            @pl.when(nxt_ref[i] >= 0)
            def _():
                for cp in weight_copies(nxt_ref[i]):
                    cp.start(priority=1)

        for j in range(ROW_TILES):
            xb_ref[:, j * LANES:(j + 1) * LANES] = (
                xs_ref[pl.ds(j, bm, stride=ROW_TILES), :].astype(BF16))
        for c in range(D_FF // FF_CHUNK):
            f0 = c * FF_CHUNK
            xb = xb_ref[...]
            gate = _dot(xb, wgu_bf[:, f0:f0 + FF_CHUNK]) + bgu_ref[0, :, f0:f0 + FF_CHUNK]
            up = (_dot(xb, wgu_bf[:, D_FF + f0:D_FF + f0 + FF_CHUNK])
                  + bgu_ref[0, :, D_FF + f0:D_FF + f0 + FF_CHUNK])
            gate = jnp.minimum(gate, SWIGLU_LIMIT)
            up = jnp.clip(up, -SWIGLU_LIMIT, SWIGLU_LIMIT)
            glu = gate * jax.nn.sigmoid(gate * SWIGLU_ALPHA)
            act_ref[:, f0:f0 + FF_CHUNK] = ((up + 1.0) * glu).astype(BF16)
        out = _dot(act_ref[...], wd_bf[...]) + bd_ref[0]
        for j in range(ROW_TILES):
            ys_ref[pl.ds(j, bm, stride=ROW_TILES), :] = out[:, j * LANES:(j + 1) * LANES]


def _experts_call(bexp, first, nxt, nreal, xs, wgu, bgu, wd, bd):
    bm = MOE_BLOCK
    n_blocks = xs.shape[0] // (bm * ROW_TILES)
    bgu3 = bgu.reshape(N_EXPERTS, 1, 2 * D_FF)
    bd3 = bd.reshape(N_EXPERTS, 1, D_MODEL)
    grid_spec = pltpu.PrefetchScalarGridSpec(
        num_scalar_prefetch=4,
        grid=(n_blocks,),
        in_specs=[
            pl.BlockSpec((bm * ROW_TILES, LANES),
                         lambda i, be, fi, nx, nr: (jnp.minimum(i, nr[0] - 1), 0)),
            pl.BlockSpec(memory_space=pl.ANY),
            pl.BlockSpec((1, 1, 2 * D_FF), lambda i, be, fi, nx, nr: (be[i], 0, 0)),
            pl.BlockSpec(memory_space=pl.ANY),
            pl.BlockSpec((1, 1, D_MODEL), lambda i, be, fi, nx, nr: (be[i], 0, 0)),
        ],
        out_specs=pl.BlockSpec((bm * ROW_TILES, LANES), lambda i, be, fi, nx, nr: (i, 0)),
        scratch_shapes=[pltpu.VMEM((bm, D_MODEL), BF16),
                        pltpu.VMEM((bm, D_FF), BF16),
                        pltpu.VMEM((D_MODEL, 2 * D_FF), F32),
                        pltpu.VMEM((D_FF, D_MODEL), F32),
                        pltpu.VMEM((D_MODEL, 2 * D_FF), BF16),
                        pltpu.VMEM((D_FF, D_MODEL), BF16),
                        pltpu.SemaphoreType.DMA((2,))],
    )
    return pl.pallas_call(
        _experts_kernel,
        grid_spec=grid_spec,
        out_shape=jax.ShapeDtypeStruct(xs.shape, F32),
        compiler_params=pltpu.CompilerParams(
            dimension_semantics=("arbitrary",), vmem_limit_bytes=VMEM_LIMIT),
        name="experts",
    )(bexp, first, nxt, nreal, xs, wgu, bgu3, wd, bd3)


def _combine_kernel(pos_cur_ref, pos_nxt_ref, x1_ref, meta_ref, g_ref, ys_hbm, out_ref, gbuf, gsem):
    i = pl.program_id(0)
    n = pl.num_programs(0)
    tb = x1_ref.shape[0]
    slot = i % 2

    def start_gather(pos_ref, buf_slot):
        def body(tok, carry):
            for kk in range(TOP_K):
                row = pl.multiple_of(pos_ref[0, 0, tok * TOP_K + kk], ROW_TILES)
                dst = pl.multiple_of((kk * tb + tok) * ROW_TILES, ROW_TILES)
                pltpu.make_async_copy(ys_hbm.at[pl.ds(row, ROW_TILES), :],
                                      gbuf.at[buf_slot, pl.ds(dst, ROW_TILES), :],
                                      gsem.at[buf_slot]).start(priority=kk % 2)
            return carry
        lax.fori_loop(0, tb, body, 0, unroll=2)

    @pl.when(i == 0)
    def _():
        start_gather(pos_cur_ref, 0)

    @pl.when(i + 1 < n)
    def _():
        start_gather(pos_nxt_ref, 1 - slot)

    pltpu.make_async_copy(ys_hbm.at[pl.ds(0, TOP_K * tb * ROW_TILES), :], gbuf.at[slot],
                          gsem.at[slot]).wait()

    meta = meta_ref[...]
    gates = [meta[:, TOP_K + kk:TOP_K + kk + 1] for kk in range(TOP_K)]
    ssq = jnp.zeros((tb, 1), F32)
    for j in range(ROW_TILES):
        cols = slice(j * LANES, (j + 1) * LANES)
        acc = x1_ref[:, cols]
        for kk in range(TOP_K):
            acc = acc + gates[kk] * gbuf[slot, pl.ds(kk * tb * ROW_TILES + j, tb, stride=ROW_TILES), :]
        out_ref[:, cols] = acc
        ssq = ssq + jnp.sum(acc * acc, axis=-1, keepdims=True)
    inv = lax.rsqrt(ssq / D_MODEL + RMS_EPS)
    out_ref[...] = out_ref[...] * inv * g_ref[...]


def _combine_call(pos_rows, x1, meta, g, ys):
    t, d = x1.shape
    tb = COMBINE_TILE
    nb = t // tb
    pos3 = pos_rows.reshape(nb, 1, tb * TOP_K)
    smem_blk = lambda fn: pl.BlockSpec((1, 1, tb * TOP_K), fn, memory_space=pltpu.SMEM)
    return pl.pallas_call(
        _combine_kernel,
        grid=(nb,),
        in_specs=[smem_blk(lambda i: (i, 0, 0)),
                  smem_blk(lambda i: (jnp.minimum(i + 1, nb - 1), 0, 0)),
                  pl.BlockSpec((tb, d), lambda i: (i, 0)),
                  pl.BlockSpec((tb, LANES), lambda i: (i, 0)),
                  pl.BlockSpec((1, d), lambda i: (0, 0)),
                  pl.BlockSpec(memory_space=pl.ANY)],
        out_specs=pl.BlockSpec((tb, d), lambda i: (i, 0)),
        out_shape=jax.ShapeDtypeStruct((t, d), F32),
        scratch_shapes=[pltpu.VMEM((2, TOP_K * tb * ROW_TILES, LANES), F32),
                        pltpu.SemaphoreType.DMA((2,))],
        compiler_params=pltpu.CompilerParams(
            dimension_semantics=("arbitrary",), vmem_limit_bytes=VMEM_LIMIT),
        name="combine",
    )(pos3, pos3, x1, meta, g, ys)


def _routing_tables(meta, counts, t):
    bm = MOE_BLOCK
    n_blocks = t * TOP_K // bm + N_EXPERTS
    e_idx = meta[:, 0:TOP_K].astype(I32)
    rank = meta[:, 2 * TOP_K:3 * TOP_K].astype(I32)
    eids = jnp.arange(N_EXPERTS, dtype=I32)
    nblk_e = (counts + bm - 1) // bm
    blk_end = jnp.sum(jnp.where(eids[None, :] <= eids[:, None], nblk_e[None, :], 0), axis=1)
    blk_start = blk_end - nblk_e
    nreal = blk_end[N_EXPERTS - 1]
    pad_start = blk_start * bm
    pos = jnp.sum(jnp.where(e_idx[..., None] == eids, pad_start, 0), axis=-1) + rank

    blk = jnp.arange(n_blocks, dtype=I32)
    bexp = jnp.minimum(jnp.sum((blk_end[None, :] <= blk[:, None]).astype(I32), axis=1),
                       N_EXPERTS - 1)
    blk_is_e = bexp[:, None] == eids[None, :]
    pick = lambda tab: jnp.sum(jnp.where(blk_is_e, tab[None, :], 0), axis=1)
    first = (blk == pick(blk_start)).astype(I32)
    nxt_e = jnp.sum((blk_end[None, :] <= blk_end[:, None]).astype(I32), axis=1)
    nxt_e = jnp.where(blk_end < nreal, jnp.minimum(nxt_e, N_EXPERTS - 1), -1)
    nxt = pick(nxt_e)
    tail_blk = jnp.arange(N_EXPERTS, dtype=I32) + nreal
    pad_lo = jnp.concatenate([pad_start + counts, jnp.minimum(tail_blk, n_blocks - 1) * bm])
    pad_n = jnp.concatenate([nblk_e * bm - counts, jnp.where(tail_blk < n_blocks, bm, 0)])
    return (pos, bexp, first, nxt, nreal.reshape(1).astype(I32), n_blocks,
            pad_lo * ROW_TILES, pad_n * ROW_TILES)


def kernel(x, norm_mix_g, w_in, conv_w, w_gk_up, b_gk_up, gla_norm_g, w_out, norm_ffn_g,
           w_router, b_router, w_gate_up, b_gate_up, w_down, b_down, norm_final_g):
    bsz, seq, d = x.shape
    t = bsz * seq
    assert w_in.shape[0] == 1, "single-layer trunk only"
    l = 0
    d_in = w_in.shape[-1]
    win = jnp.pad(w_in[l], ((0, 0), (0, D_IN_PAD - d_in))).astype(BF16)
    wgk = jnp.pad(w_gk_up[l], ((0, LANES - GLA_RANK), (0, 0))).astype(BF16)
    wr = jnp.pad(w_router[l], ((0, 0), (0, LANES - N_EXPERTS)))
    wrh = wr.astype(BF16)
    wrl = (wr - wrh.astype(F32)).astype(BF16)
    br = jnp.pad(b_router[l], (0, LANES - N_EXPERTS), constant_values=NEG_BIG).reshape(1, LANES)

    x1, h2, meta, cnt = _mixer_call(
        x, norm_mix_g[l].reshape(1, d), win, conv_w[l], wgk, b_gk_up[l].reshape(1, GLA_QK),
        gla_norm_g[l].reshape(1, GLA_DV), w_out[l].astype(BF16), norm_ffn_g[l].reshape(1, d),
        wrh, wrl, br)

    meta2 = meta.reshape(t, LANES)
    counts = cnt[0, :N_EXPERTS].astype(I32)
    pos, bexp, first, nxt, nreal, n_blocks, pad_lo, pad_n = _routing_tables(meta2, counts, t)
    pos_rows = (pos * ROW_TILES).reshape(t * TOP_K)
    xs = _dispatch_call(pad_lo, pad_n, pos_rows, h2.reshape(t * ROW_TILES, LANES),
                        n_blocks * MOE_BLOCK)
    ys = _experts_call(bexp, first, nxt, nreal, xs, w_gate_up[l], b_gate_up[l], w_down[l], b_down[l])
    out = _combine_call(pos_rows, x1.reshape(t, d), meta2, norm_final_g.reshape(1, d), ys)
    return out.reshape(bsz, seq, d)
```

```python
import jax
import jax.numpy as jnp
from jax import lax
from jax.experimental import pallas as pl
from jax.experimental.pallas import tpu as pltpu

F32 = jnp.float32
BF16 = jnp.bfloat16
I32 = jnp.int32

D_MODEL = 1024
CONV_WIDTH = 512
CONV_K = 3
GLA_WIDTH = 512
GLA_HEADS = 4
GLA_DV = 128
GLA_DK = 64
GLA_QK = GLA_HEADS * GLA_DK
GLA_RANK = 16
GLA_NORMALIZER = 16.0
GLA_CHUNK = 64
N_EXPERTS = 32
TOP_K = 4
D_FF = 1024
SWIGLU_LIMIT = 7.0
SWIGLU_ALPHA = 1.702
RMS_EPS = 1e-5

LANES = 128
SUBLANES = 8
ROW_TILES = D_MODEL // LANES
ROUTE_ROWS = 16

OFF_UH = 0
OFF_GB = OFF_UH + CONV_WIDTH
OFF_GC = OFF_GB + CONV_WIDTH
OFF_Q = OFF_GC + CONV_WIDTH
OFF_K = OFF_Q + GLA_QK
OFF_V = OFF_K + GLA_QK
OFF_GO = OFF_V + GLA_WIDTH
OFF_GKL = OFF_GO + GLA_WIDTH
D_IN_PAD = OFF_GKL + LANES

SEQ_TILE = 512
MOE_BLOCK = 256
DISPATCH_TILE = 512
FF_CHUNK = 256
CAST_ROWS = 128
COMBINE_TILE = 256
NEG_BIG = -1e30
VMEM_LIMIT = 56 * 1024 * 1024


def _rms(x, g):
    return x * lax.rsqrt(jnp.mean(x * x, axis=-1, keepdims=True) + RMS_EPS) * g


def _dot(a, b):
    return jnp.dot(a, b, preferred_element_type=F32)


def _dot_nt(a, b):
    return lax.dot_general(a, b, (((1,), (1,)), ((), ())), preferred_element_type=F32)


def _split_bf16(x):
    hi = x.astype(BF16)
    lo = (x - hi.astype(F32)).astype(BF16)
    return hi, lo


def _mixer_kernel(x_ref, g1_ref, win_ref, convw_ref, wgk_ref, bgk_ref, gng_ref, wout_ref,
                  g2_ref, wrh_ref, wrl_ref, br_ref,
                  x1_ref, h2_ref, meta_ref, route_ref, cnt_ref,
                  proj_ref, ubuf_ref, la_ref, ycat_ref, state_ref, carry_ref):
    ts = x_ref.shape[1]
    b_idx = pl.program_id(0)
    s_idx = pl.program_id(1)

    @pl.when(s_idx == 0)
    def _():
        state_ref[...] = jnp.zeros_like(state_ref)
        ubuf_ref[0:SUBLANES, :] = jnp.zeros((SUBLANES, CONV_WIDTH), F32)

    @pl.when((s_idx == 0) & (b_idx == 0))
    def _():
        carry_ref[...] = jnp.zeros_like(carry_ref)

    x = x_ref[0]
    h = _rms(x, g1_ref[...]).astype(BF16)
    proj_ref[...] = _dot(h, win_ref[...])

    u = proj_ref[:, OFF_GC:OFF_GC + CONV_WIDTH] * proj_ref[:, OFF_UH:OFF_UH + CONV_WIDTH]
    ubuf_ref[SUBLANES:SUBLANES + ts, :] = u
    u1 = ubuf_ref[pl.ds(SUBLANES - 1, ts), :]
    u2 = ubuf_ref[pl.ds(SUBLANES - 2, ts), :]
    conv = convw_ref[0:1, :] * u2 + convw_ref[1:2, :] * u1 + convw_ref[2:3, :] * u
    ycat_ref[:, 0:CONV_WIDTH] = (proj_ref[:, OFF_GB:OFF_GB + CONV_WIDTH] * conv).astype(BF16)
    ubuf_ref[0:SUBLANES, :] = ubuf_ref[ts:ts + SUBLANES, :]

    gk = _dot(proj_ref[:, OFF_GKL:OFF_GKL + LANES].astype(BF16), wgk_ref[...]) + bgk_ref[...]
    log_sig = jnp.minimum(gk, 0.0) - jnp.log1p(jnp.exp(-jnp.abs(gk)))
    la_ref[...] = log_sig / GLA_NORMALIZER

    ci = lax.broadcasted_iota(I32, (GLA_CHUNK, GLA_CHUNK), 0)
    cj = lax.broadcasted_iota(I32, (GLA_CHUNK, GLA_CHUNK), 1)
    tri_incl = (cj <= ci).astype(BF16)
    causal = cj <= ci
    causal4 = jnp.concatenate([causal] * GLA_HEADS, axis=0)
    lane_qk = lax.broadcasted_iota(I32, (1, GLA_QK), 1)
    head_masks = [((lane_qk >= hd * GLA_DK) & (lane_qk < (hd + 1) * GLA_DK)).astype(F32)
                  for hd in range(GLA_HEADS)]
    gng = gng_ref[...]

    def chunk_body(c, carry):
        r0 = pl.multiple_of(c * GLA_CHUNK, GLA_CHUNK)
        rows = pl.ds(r0, GLA_CHUNK)
        la_hi, la_lo = _split_bf16(la_ref[rows, :])
        bcum = _dot(tri_incl, la_hi) + _dot(tri_incl, la_lo)
        blast = bcum[GLA_CHUNK - 1:GLA_CHUNK, :]
        q = proj_ref[rows, OFF_Q:OFF_Q + GLA_QK] * (GLA_DK ** -0.5)
        k = proj_ref[rows, OFF_K:OFF_K + GLA_QK]
        v = proj_ref[rows, OFF_V:OFF_V + GLA_WIDTH].astype(BF16)
        qd = q * jnp.exp(bcum)
        kd = (k * jnp.exp(-bcum)).astype(BF16)
        kr = k * jnp.exp(blast - bcum)

        q_stack = jnp.concatenate([qd * m for m in head_masks], axis=0).astype(BF16)
        scores = jnp.where(causal4, _dot_nt(q_stack, kd), 0.0).astype(BF16)

        state = state_ref[...]
        o_inter = _dot(qd.astype(BF16), state.astype(BF16))
        o_intra = jnp.concatenate(
            [_dot(scores[hd * GLA_CHUNK:(hd + 1) * GLA_CHUNK, :],
                  v[:, hd * GLA_DV:(hd + 1) * GLA_DV]) for hd in range(GLA_HEADS)], axis=1)
        o = o_inter + o_intra

        kt = jnp.concatenate([kr, jnp.broadcast_to(blast, (GLA_CHUNK, GLA_QK))], axis=0).T
        dcol = jnp.exp(kt[:, GLA_CHUNK:GLA_CHUNK + 1])
        lane_c = lax.broadcasted_iota(I32, (GLA_QK, 2 * GLA_CHUNK), 1)
        kt_b = jnp.where(lane_c < GLA_CHUNK, kt, 0.0).astype(BF16)
        v_pad = jnp.concatenate([v, jnp.zeros_like(v)], axis=0)
        for hd in range(GLA_HEADS):
            rs = slice(hd * GLA_DK, (hd + 1) * GLA_DK)
            cs = slice(hd * GLA_DV, (hd + 1) * GLA_DV)
            kv = _dot(kt_b[rs, :], v_pad[:, cs])
            state_ref[rs, cs] = dcol[rs, :] * state[rs, cs] + kv

        g_out = proj_ref[rows, OFF_GO:OFF_GO + GLA_WIDTH]
        o_n = jnp.concatenate(
            [_rms(o[:, hd * GLA_DV:(hd + 1) * GLA_DV], gng) for hd in range(GLA_HEADS)], axis=1)
        y = o_n * (g_out * jax.nn.sigmoid(g_out))
        ycat_ref[rows, CONV_WIDTH:CONV_WIDTH + GLA_WIDTH] = y.astype(BF16)
        return carry

    lax.fori_loop(0, ts // GLA_CHUNK, chunk_body, 0, unroll=True)

    x1 = x + _dot(ycat_ref[...], wout_ref[...])
    x1_ref[0] = x1
    h2 = _rms(x1, g2_ref[...])
    for j in range(ROW_TILES):
        h2_ref[0, pl.ds(j, ts, stride=ROW_TILES), :] = h2[:, j * LANES:(j + 1) * LANES]
    h2_hi, h2_lo = _split_bf16(h2)
    logits = (_dot(h2_hi, wrh_ref[...]) + _dot(h2_lo, wrh_ref[...]) + _dot(h2_hi, wrl_ref[...])
              + br_ref[...])

    lane = lax.broadcasted_iota(I32, (ts, LANES), 1).astype(F32)
    work = logits
    sel = jnp.zeros((ts, LANES), F32)
    top_v, top_i, top_oh = [], [], []
    for _ in range(TOP_K):
        m = jnp.max(work, axis=-1, keepdims=True)
        idx = jnp.min(jnp.where(work == m, lane, float(LANES)), axis=-1, keepdims=True)
        oh = lane == idx
        top_v.append(m)
        top_i.append(idx)
        top_oh.append(oh)
        sel = sel + oh.astype(F32)
        work = jnp.where(oh, -jnp.inf, work)
    exps = [jnp.exp(tv - top_v[0]) for tv in top_v]
    denom = exps[0] + exps[1] + exps[2] + exps[3]
    gates = [e / denom for e in exps]

    ti = lax.broadcasted_iota(I32, (ts, ts), 0)
    tj = lax.broadcasted_iota(I32, (ts, ts), 1)
    strict_lower = (tj < ti).astype(BF16)
    prefix = _dot(strict_lower, sel.astype(BF16)) + carry_ref[0:1, :]
    ranks = [jnp.sum(jnp.where(oh, prefix, 0.0), axis=-1, keepdims=True) for oh in top_oh]
    new_carry = carry_ref[0:1, :] + jnp.sum(sel, axis=0, keepdims=True)
    carry_ref[...] = jnp.broadcast_to(new_carry, carry_ref.shape)
    cnt_ref[...] = jnp.broadcast_to(new_carry, cnt_ref.shape)

    meta = jnp.zeros((ts, LANES), F32)
    for kk in range(TOP_K):
        meta = jnp.where(lane == float(kk), top_i[kk], meta)
        meta = jnp.where(lane == float(TOP_K + kk), gates[kk], meta)
        meta = jnp.where(lane == float(2 * TOP_K + kk), ranks[kk], meta)
    meta_ref[0] = meta
    route_ref[...] = meta.T[0:ROUTE_ROWS, :]


def _mixer_call(x, g1, win, convw, wgk, bgk, gng, wout, g2, wrh, wrl, br):
    bsz, seq, d = x.shape
    ts = SEQ_TILE
    grid = (bsz, seq // ts)

    def const(shape):
        return pl.BlockSpec(shape, lambda b, s: (0,) * len(shape))

    tile = lambda w: pl.BlockSpec((1, ts, w), lambda b, s: (b, s, 0))
    return pl.pallas_call(
        _mixer_kernel,
        grid=grid,
        in_specs=[tile(d), const(g1.shape), const(win.shape), const(convw.shape),
                  const(wgk.shape), const(bgk.shape), const(gng.shape), const(wout.shape),
                  const(g2.shape), const(wrh.shape), const(wrl.shape), const(br.shape)],
        out_specs=[tile(d),
                   pl.BlockSpec((1, ts * ROW_TILES, LANES), lambda b, s: (b, s, 0)),
                   tile(LANES),
                   pl.BlockSpec((ROUTE_ROWS, ts), lambda b, s: (0, b * (seq // ts) + s)),
                   const((SUBLANES, LANES))],
        out_shape=[jax.ShapeDtypeStruct((bsz, seq, d), F32),
                   jax.ShapeDtypeStruct((bsz, seq * ROW_TILES, LANES), F32),
                   jax.ShapeDtypeStruct((bsz, seq, LANES), F32),
                   jax.ShapeDtypeStruct((ROUTE_ROWS, bsz * seq), F32),
                   jax.ShapeDtypeStruct((SUBLANES, LANES), F32)],
        scratch_shapes=[pltpu.VMEM((ts, D_IN_PAD), F32),
                        pltpu.VMEM((ts + SUBLANES, CONV_WIDTH), F32),
                        pltpu.VMEM((ts, GLA_QK), F32),
                        pltpu.VMEM((ts, D_MODEL), BF16),
                        pltpu.VMEM((GLA_QK, GLA_WIDTH), F32),
                        pltpu.VMEM((SUBLANES, LANES), F32)],
        compiler_params=pltpu.CompilerParams(
            dimension_semantics=("arbitrary", "arbitrary"), vmem_limit_bytes=VMEM_LIMIT),
        name="mixer",
    )(x, g1, win, convw, wgk, bgk, gng, wout, g2, wrh, wrl, br)


def _dispatch_kernel(pad_lo_ref, pad_n_ref, pos_ref, h2_ref, xs_hbm, zbuf, sem, zsem):
    td = h2_ref.shape[0] // ROW_TILES

    def zero_fill(wait):
        def fill(g, carry):
            n = pl.multiple_of(pad_n_ref[g], ROW_TILES)

            @pl.when(n > 0)
            def _():
                lo = pl.multiple_of(pad_lo_ref[g], ROW_TILES)
                cp = pltpu.make_async_copy(zbuf.at[pl.ds(0, n), :], xs_hbm.at[pl.ds(lo, n), :],
                                           zsem.at[0])
                if wait:
                    cp.wait()
                else:
                    cp.start()
            return carry

        lax.fori_loop(0, pad_lo_ref.shape[0], fill, 0)

    @pl.when(pl.program_id(0) == 0)
    def _():
        zbuf[...] = jnp.zeros_like(zbuf)
        zero_fill(wait=False)

    def body(tok, carry):
        src = h2_ref.at[pl.ds(pl.multiple_of(tok * ROW_TILES, ROW_TILES), ROW_TILES), :]
        for kk in range(TOP_K):
            row = pl.multiple_of(pos_ref[kk, tok], ROW_TILES)
            pltpu.make_async_copy(src, xs_hbm.at[pl.ds(row, ROW_TILES), :], sem.at[0]).start(
                priority=kk % 2)
        return carry

    lax.fori_loop(0, td, body, 0, unroll=2)
    for _ in range(TOP_K):
        pltpu.make_async_copy(h2_ref, xs_hbm.at[pl.ds(0, td * ROW_TILES), :], sem.at[0]).wait()

    @pl.when(pl.program_id(0) == 0)
    def _():
        zero_fill(wait=True)


def _dispatch_call(pad_lo, pad_n, pos_rows, h2, n_rows):
    t = h2.shape[0] // ROW_TILES
    td = DISPATCH_TILE
    grid_spec = pltpu.PrefetchScalarGridSpec(
        num_scalar_prefetch=2,
        grid=(t // td,),
        in_specs=[pl.BlockSpec((TOP_K, td), lambda i, lo, n: (0, i), memory_space=pltpu.SMEM),
                  pl.BlockSpec((td * ROW_TILES, LANES), lambda i, lo, n: (i, 0))],
        out_specs=pl.BlockSpec(memory_space=pl.ANY),
        scratch_shapes=[pltpu.VMEM((MOE_BLOCK * ROW_TILES, LANES), F32),
                        pltpu.SemaphoreType.DMA((1,)),
                        pltpu.SemaphoreType.DMA((1,))],
    )
    return pl.pallas_call(
        _dispatch_kernel,
        grid_spec=grid_spec,
        out_shape=jax.ShapeDtypeStruct((n_rows * ROW_TILES, LANES), F32),
        compiler_params=pltpu.CompilerParams(
            dimension_semantics=("arbitrary",), vmem_limit_bytes=VMEM_LIMIT,
            has_side_effects=True),
        name="dispatch",
    )(pad_lo, pad_n, pos_rows, h2)


def _experts_kernel(bexp_ref, first_ref, nxt_ref, nreal_ref,
                    xs_ref, wgu_hbm, bgu_ref, wd_hbm, bd_ref,
                    ys_ref,
                    xb_ref, act_ref, wgu_stage, wd_stage, wgu_bf, wd_bf, wsem):
    i = pl.program_id(0)
    nreal = nreal_ref[0]
    bm = xb_ref.shape[0]

    def weight_copies(e):
        return (pltpu.make_async_copy(wgu_hbm.at[e], wgu_stage, wsem.at[0]),
                pltpu.make_async_copy(wd_hbm.at[e], wd_stage, wsem.at[1]))

    @pl.when(i >= nreal)
    def _():
        ys_ref[...] = jnp.zeros_like(ys_ref)

    @pl.when(i < nreal)
    def _():
        e = bexp_ref[i]

        @pl.when(i == 0)
        def _():
            for cp in weight_copies(e):
                cp.start(priority=1)

        @pl.when(first_ref[i] == 1)
        def _():
            for cp in weight_copies(e):
                cp.wait()

            def cast_gu(r, carry):
                rows = pl.ds(pl.multiple_of(r * CAST_ROWS, CAST_ROWS), CAST_ROWS)
                wgu_bf[rows, :] = wgu_stage[rows, :].astype(BF16)
                return carry

            def cast_d(r, carry):
                rows = pl.ds(pl.multiple_of(r * CAST_ROWS, CAST_ROWS), CAST_ROWS)
                wd_bf[rows, :] = wd_stage[rows, :].astype(BF16)
                return carry

            lax.fori_loop(0, D_MODEL // CAST_ROWS, cast_gu, 0)
            lax.fori_loop(0, D_FF // CAST_ROWS, cast_d, 0)

            @pl.when(nxt_ref[i] >= 0)
            def _():
                for cp in weight_copies(nxt_ref[i]):
                    cp.start(priority=1)

        for j in range(ROW_TILES):
            xb_ref[:, j * LANES:(j + 1) * LANES] = (
                xs_ref[pl.ds(j, bm, stride=ROW_TILES), :].astype(BF16))
        for c in range(D_FF // FF_CHUNK):
            f0 = c * FF_CHUNK
            xb = xb_ref[...]
            gate = _dot(xb, wgu_bf[:, f0:f0 + FF_CHUNK]) + bgu_ref[0, :, f0:f0 + FF_CHUNK]
            up = (_dot(xb, wgu_bf[:, D_FF + f0:D_FF + f0 + FF_CHUNK])
                  + bgu_ref[0, :, D_FF + f0:D_FF + f0 + FF_CHUNK])
            gate = jnp.minimum(gate, SWIGLU_LIMIT)
            up = jnp.clip(up, -SWIGLU_LIMIT, SWIGLU_LIMIT)
            glu = gate * jax.nn.sigmoid(gate * SWIGLU_ALPHA)
            act_ref[:, f0:f0 + FF_CHUNK] = ((up + 1.0) * glu).astype(BF16)
        out = _dot(act_ref[...], wd_bf[...]) + bd_ref[0]
        for j in range(ROW_TILES):
            ys_ref[pl.ds(j, bm, stride=ROW_TILES), :] = out[:, j * LANES:(j + 1) * LANES]


def _experts_call(bexp, first, nxt, nreal, xs, wgu, bgu, wd, bd):
    bm = MOE_BLOCK
    n_blocks = xs.shape[0] // (bm * ROW_TILES)
    bgu3 = bgu.reshape(N_EXPERTS, 1, 2 * D_FF)
    bd3 = bd.reshape(N_EXPERTS, 1, D_MODEL)
    grid_spec = pltpu.PrefetchScalarGridSpec(
        num_scalar_prefetch=4,
        grid=(n_blocks,),
        in_specs=[
            pl.BlockSpec((bm * ROW_TILES, LANES),
                         lambda i, be, fi, nx, nr: (jnp.minimum(i, nr[0] - 1), 0)),
            pl.BlockSpec(memory_space=pl.ANY),
            pl.BlockSpec((1, 1, 2 * D_FF), lambda i, be, fi, nx, nr: (be[i], 0, 0)),
            pl.BlockSpec(memory_space=pl.ANY),
            pl.BlockSpec((1, 1, D_MODEL), lambda i, be, fi, nx, nr: (be[i], 0, 0)),
        ],
        out_specs=pl.BlockSpec((bm * ROW_TILES, LANES), lambda i, be, fi, nx, nr: (i, 0)),
        scratch_shapes=[pltpu.VMEM((bm, D_MODEL), BF16),
                        pltpu.VMEM((bm, D_FF), BF16),
                        pltpu.VMEM((D_MODEL, 2 * D_FF), F32),
                        pltpu.VMEM((D_FF, D_MODEL), F32),
                        pltpu.VMEM((D_MODEL, 2 * D_FF), BF16),
                        pltpu.VMEM((D_FF, D_MODEL), BF16),
                        pltpu.SemaphoreType.DMA((2,))],
    )
    return pl.pallas_call(
        _experts_kernel,
        grid_spec=grid_spec,
        out_shape=jax.ShapeDtypeStruct(xs.shape, F32),
        compiler_params=pltpu.CompilerParams(
            dimension_semantics=("arbitrary",), vmem_limit_bytes=VMEM_LIMIT),
        name="experts",
    )(bexp, first, nxt, nreal, xs, wgu, bgu3, wd, bd3)


def _combine_kernel(pos_cur_ref, pos_nxt_ref, x1_ref, meta_ref, g_ref, ys_hbm, out_ref, gbuf, gsem):
    i = pl.program_id(0)
    n = pl.num_programs(0)
    tb = x1_ref.shape[0]
    slot = i % 2

    def start_gather(pos_ref, buf_slot):
        def body(tok, carry):
            for kk in range(TOP_K):
                row = pl.multiple_of(pos_ref[kk, tok], ROW_TILES)
                dst = pl.multiple_of((kk * tb + tok) * ROW_TILES, ROW_TILES)
                pltpu.make_async_copy(ys_hbm.at[pl.ds(row, ROW_TILES), :],
                                      gbuf.at[buf_slot, pl.ds(dst, ROW_TILES), :],
                                      gsem.at[buf_slot]).start(priority=kk % 2)
            return carry
        lax.fori_loop(0, tb, body, 0, unroll=2)

    @pl.when(i == 0)
    def _():
        start_gather(pos_cur_ref, 0)

    @pl.when(i + 1 < n)
    def _():
        start_gather(pos_nxt_ref, 1 - slot)

    pltpu.make_async_copy(ys_hbm.at[pl.ds(0, TOP_K * tb * ROW_TILES), :], gbuf.at[slot],
                          gsem.at[slot]).wait()

    meta = meta_ref[...]
    gates = [meta[:, TOP_K + kk:TOP_K + kk + 1] for kk in range(TOP_K)]
    ssq = jnp.zeros((tb, 1), F32)
    for j in range(ROW_TILES):
        cols = slice(j * LANES, (j + 1) * LANES)
        acc = x1_ref[:, cols]
        for kk in range(TOP_K):
            acc = acc + gates[kk] * gbuf[slot, pl.ds(kk * tb * ROW_TILES + j, tb, stride=ROW_TILES), :]
        out_ref[:, cols] = acc
        ssq = ssq + jnp.sum(acc * acc, axis=-1, keepdims=True)
    inv = lax.rsqrt(ssq / D_MODEL + RMS_EPS)
    out_ref[...] = out_ref[...] * inv * g_ref[...]


def _combine_call(pos_rows, x1, meta, g, ys):
    t, d = x1.shape
    tb = COMBINE_TILE
    nb = t // tb
    smem_blk = lambda fn: pl.BlockSpec((TOP_K, tb), fn, memory_space=pltpu.SMEM)
    return pl.pallas_call(
        _combine_kernel,
        grid=(nb,),
        in_specs=[smem_blk(lambda i: (0, i)),
                  smem_blk(lambda i: (0, jnp.minimum(i + 1, nb - 1))),
                  pl.BlockSpec((tb, d), lambda i: (i, 0)),
                  pl.BlockSpec((tb, LANES), lambda i: (i, 0)),
                  pl.BlockSpec((1, d), lambda i: (0, 0)),
                  pl.BlockSpec(memory_space=pl.ANY)],
        out_specs=pl.BlockSpec((tb, d), lambda i: (i, 0)),
        out_shape=jax.ShapeDtypeStruct((t, d), F32),
        scratch_shapes=[pltpu.VMEM((2, TOP_K * tb * ROW_TILES, LANES), F32),
                        pltpu.SemaphoreType.DMA((2,))],
        compiler_params=pltpu.CompilerParams(
            dimension_semantics=("arbitrary",), vmem_limit_bytes=VMEM_LIMIT),
        name="combine",
    )(pos_rows, pos_rows, x1, meta, g, ys)


def _routing_tables(route, counts, t):
    bm = MOE_BLOCK
    n_blocks = t * TOP_K // bm + N_EXPERTS
    e_idx = route[0:TOP_K].astype(I32)
    rank = route[2 * TOP_K:3 * TOP_K].astype(I32)
    eids = jnp.arange(N_EXPERTS, dtype=I32)
    nblk_e = (counts + bm - 1) // bm
    blk_end = jnp.sum(jnp.where(eids[None, :] <= eids[:, None], nblk_e[None, :], 0), axis=1)
    blk_start = blk_end - nblk_e
    nreal = blk_end[N_EXPERTS - 1]
    pad_start = blk_start * bm
    pos = rank
    for e in range(N_EXPERTS):
        pos = pos + jnp.where(e_idx == e, pad_start[e], 0)

    blk = jnp.arange(n_blocks, dtype=I32)
    bexp = jnp.minimum(jnp.sum((blk_end[None, :] <= blk[:, None]).astype(I32), axis=1),
                       N_EXPERTS - 1)
    blk_is_e = bexp[:, None] == eids[None, :]
    pick = lambda tab: jnp.sum(jnp.where(blk_is_e, tab[None, :], 0), axis=1)
    first = (blk == pick(blk_start)).astype(I32)
    nxt_e = jnp.sum((blk_end[None, :] <= blk_end[:, None]).astype(I32), axis=1)
    nxt_e = jnp.where(blk_end < nreal, jnp.minimum(nxt_e, N_EXPERTS - 1), -1)
    nxt = pick(nxt_e)
    tail_blk = jnp.arange(N_EXPERTS, dtype=I32) + nreal
    pad_lo = jnp.concatenate([pad_start + counts, jnp.minimum(tail_blk, n_blocks - 1) * bm])
    pad_n = jnp.concatenate([nblk_e * bm - counts, jnp.where(tail_blk < n_blocks, bm, 0)])
    return (pos, bexp, first, nxt, nreal.reshape(1).astype(I32), n_blocks,
            pad_lo * ROW_TILES, pad_n * ROW_TILES)


def kernel(x, norm_mix_g, w_in, conv_w, w_gk_up, b_gk_up, gla_norm_g, w_out, norm_ffn_g,
           w_router, b_router, w_gate_up, b_gate_up, w_down, b_down, norm_final_g):
    bsz, seq, d = x.shape
    t = bsz * seq
    assert w_in.shape[0] == 1, "single-layer trunk only"
    l = 0
    d_in = w_in.shape[-1]
    win = jnp.pad(w_in[l], ((0, 0), (0, D_IN_PAD - d_in))).astype(BF16)
    wgk = jnp.pad(w_gk_up[l], ((0, LANES - GLA_RANK), (0, 0))).astype(BF16)
    wr = jnp.pad(w_router[l], ((0, 0), (0, LANES - N_EXPERTS)))
    wrh = wr.astype(BF16)
    wrl = (wr - wrh.astype(F32)).astype(BF16)
    br = jnp.pad(b_router[l], (0, LANES - N_EXPERTS), constant_values=NEG_BIG).reshape(1, LANES)

    x1, h2, meta, route, cnt = _mixer_call(
        x, norm_mix_g[l].reshape(1, d), win, conv_w[l], wgk, b_gk_up[l].reshape(1, GLA_QK),
        gla_norm_g[l].reshape(1, GLA_DV), w_out[l].astype(BF16), norm_ffn_g[l].reshape(1, d),
        wrh, wrl, br)

    meta2 = meta.reshape(t, LANES)
    counts = cnt[0, :N_EXPERTS].astype(I32)
    pos, bexp, first, nxt, nreal, n_blocks, pad_lo, pad_n = _routing_tables(route, counts, t)
    pos_rows = pos * ROW_TILES
    xs = _dispatch_call(pad_lo, pad_n, pos_rows, h2.reshape(t * ROW_TILES, LANES),
                        n_blocks * MOE_BLOCK)
    ys = _experts_call(bexp, first, nxt, nreal, xs, w_gate_up[l], b_gate_up[l], w_down[l], b_down[l])
    out = _combine_call(pos_rows, x1.reshape(t, d), meta2, norm_final_g.reshape(1, d), ys)
    return out.reshape(bsz, seq, d)
```

```python
import jax
import jax.numpy as jnp
from jax import lax
from jax.experimental import pallas as pl
from jax.experimental.pallas import tpu as pltpu

F32 = jnp.float32
BF16 = jnp.bfloat16
I32 = jnp.int32

D_MODEL = 1024
CONV_WIDTH = 512
CONV_K = 3
GLA_WIDTH = 512
GLA_HEADS = 4
GLA_DV = 128
GLA_DK = 64
GLA_QK = GLA_HEADS * GLA_DK
GLA_RANK = 16
GLA_NORMALIZER = 16.0
GLA_CHUNK = 64
N_EXPERTS = 32
TOP_K = 4
D_FF = 1024
SWIGLU_LIMIT = 7.0
SWIGLU_ALPHA = 1.702
RMS_EPS = 1e-5

LANES = 128
SUBLANES = 8
ROW_TILES = D_MODEL // LANES
ROUTE_ROWS = 16

OFF_UH = 0
OFF_GB = OFF_UH + CONV_WIDTH
OFF_GC = OFF_GB + CONV_WIDTH
OFF_Q = OFF_GC + CONV_WIDTH
OFF_K = OFF_Q + GLA_QK
OFF_V = OFF_K + GLA_QK
OFF_GO = OFF_V + GLA_WIDTH
OFF_GKL = OFF_GO + GLA_WIDTH
D_IN_PAD = OFF_GKL + LANES

SEQ_TILE = 512
MOE_BLOCK = 512
DISPATCH_TILE = 512
FF_CHUNK = 256
CAST_ROWS = 128
COMBINE_TILE = 256
NEG_BIG = -1e30
VMEM_LIMIT = 56 * 1024 * 1024


def _rms(x, g):
    return x * lax.rsqrt(jnp.mean(x * x, axis=-1, keepdims=True) + RMS_EPS) * g


def _dot(a, b):
    return jnp.dot(a, b, preferred_element_type=F32)


def _dot_nt(a, b):
    return lax.dot_general(a, b, (((1,), (1,)), ((), ())), preferred_element_type=F32)


def _split_bf16(x):
    hi = x.astype(BF16)
    lo = (x - hi.astype(F32)).astype(BF16)
    return hi, lo


def _mixer_kernel(x_ref, g1_ref, win_ref, convw_ref, wgk_ref, bgk_ref, gng_ref, wout_ref,
                  g2_ref, wrh_ref, wrl_ref, br_ref,
                  x1_ref, h2_ref, meta_ref, route_ref, cnt_ref,
                  proj_ref, ubuf_ref, la_ref, ycat_ref, state_ref, carry_ref):
    ts = x_ref.shape[1]
    b_idx = pl.program_id(0)
    s_idx = pl.program_id(1)

    @pl.when(s_idx == 0)
    def _():
        state_ref[...] = jnp.zeros_like(state_ref)
        ubuf_ref[0:SUBLANES, :] = jnp.zeros((SUBLANES, CONV_WIDTH), F32)

    @pl.when((s_idx == 0) & (b_idx == 0))
    def _():
        carry_ref[...] = jnp.zeros_like(carry_ref)

    x = x_ref[0]
    h = _rms(x, g1_ref[...]).astype(BF16)
    proj_ref[...] = _dot(h, win_ref[...])

    u = proj_ref[:, OFF_GC:OFF_GC + CONV_WIDTH] * proj_ref[:, OFF_UH:OFF_UH + CONV_WIDTH]
    ubuf_ref[SUBLANES:SUBLANES + ts, :] = u
    u1 = ubuf_ref[pl.ds(SUBLANES - 1, ts), :]
    u2 = ubuf_ref[pl.ds(SUBLANES - 2, ts), :]
    conv = convw_ref[0:1, :] * u2 + convw_ref[1:2, :] * u1 + convw_ref[2:3, :] * u
    ycat_ref[:, 0:CONV_WIDTH] = (proj_ref[:, OFF_GB:OFF_GB + CONV_WIDTH] * conv).astype(BF16)
    ubuf_ref[0:SUBLANES, :] = ubuf_ref[ts:ts + SUBLANES, :]

    gk = _dot(proj_ref[:, OFF_GKL:OFF_GKL + LANES].astype(BF16), wgk_ref[...]) + bgk_ref[...]
    log_sig = jnp.minimum(gk, 0.0) - jnp.log1p(jnp.exp(-jnp.abs(gk)))
    la_ref[...] = log_sig / GLA_NORMALIZER

    ci = lax.broadcasted_iota(I32, (GLA_CHUNK, GLA_CHUNK), 0)
    cj = lax.broadcasted_iota(I32, (GLA_CHUNK, GLA_CHUNK), 1)
    tri_incl = (cj <= ci).astype(BF16)
    causal = cj <= ci
    causal4 = jnp.concatenate([causal] * GLA_HEADS, axis=0)
    lane_qk = lax.broadcasted_iota(I32, (1, GLA_QK), 1)
    head_masks = [((lane_qk >= hd * GLA_DK) & (lane_qk < (hd + 1) * GLA_DK)).astype(F32)
                  for hd in range(GLA_HEADS)]
    gng = gng_ref[...]

    def chunk_body(c, carry):
        r0 = pl.multiple_of(c * GLA_CHUNK, GLA_CHUNK)
        rows = pl.ds(r0, GLA_CHUNK)
        la_hi, la_lo = _split_bf16(la_ref[rows, :])
        bcum = _dot(tri_incl, la_hi) + _dot(tri_incl, la_lo)
        blast = bcum[GLA_CHUNK - 1:GLA_CHUNK, :]
        q = proj_ref[rows, OFF_Q:OFF_Q + GLA_QK] * (GLA_DK ** -0.5)
        k = proj_ref[rows, OFF_K:OFF_K + GLA_QK]
        v = proj_ref[rows, OFF_V:OFF_V + GLA_WIDTH].astype(BF16)
        qd = q * jnp.exp(bcum)
        kd = (k * jnp.exp(-bcum)).astype(BF16)
        kr = k * jnp.exp(blast - bcum)

        q_stack = jnp.concatenate([qd * m for m in head_masks], axis=0).astype(BF16)
        scores = jnp.where(causal4, _dot_nt(q_stack, kd), 0.0).astype(BF16)

        state = state_ref[...]
        o_inter = _dot(qd.astype(BF16), state.astype(BF16))
        o_intra = jnp.concatenate(
            [_dot(scores[hd * GLA_CHUNK:(hd + 1) * GLA_CHUNK, :],
                  v[:, hd * GLA_DV:(hd + 1) * GLA_DV]) for hd in range(GLA_HEADS)], axis=1)
        o = o_inter + o_intra

        kt = jnp.concatenate([kr, jnp.broadcast_to(blast, (GLA_CHUNK, GLA_QK))], axis=0).T
        dcol = jnp.exp(kt[:, GLA_CHUNK:GLA_CHUNK + 1])
        lane_c = lax.broadcasted_iota(I32, (GLA_QK, 2 * GLA_CHUNK), 1)
        kt_b = jnp.where(lane_c < GLA_CHUNK, kt, 0.0).astype(BF16)
        v_pad = jnp.concatenate([v, jnp.zeros_like(v)], axis=0)
        for hd in range(GLA_HEADS):
            rs = slice(hd * GLA_DK, (hd + 1) * GLA_DK)
            cs = slice(hd * GLA_DV, (hd + 1) * GLA_DV)
            kv = _dot(kt_b[rs, :], v_pad[:, cs])
            state_ref[rs, cs] = dcol[rs, :] * state[rs, cs] + kv

        g_out = proj_ref[rows, OFF_GO:OFF_GO + GLA_WIDTH]
        o_n = jnp.concatenate(
            [_rms(o[:, hd * GLA_DV:(hd + 1) * GLA_DV], gng) for hd in range(GLA_HEADS)], axis=1)
        y = o_n * (g_out * jax.nn.sigmoid(g_out))
        ycat_ref[rows, CONV_WIDTH:CONV_WIDTH + GLA_WIDTH] = y.astype(BF16)
        return carry

    lax.fori_loop(0, ts // GLA_CHUNK, chunk_body, 0, unroll=True)

    x1 = x + _dot(ycat_ref[...], wout_ref[...])
    x1_ref[0] = x1
    h2 = _rms(x1, g2_ref[...])
    for j in range(ROW_TILES):
        h2_ref[0, pl.ds(j, ts, stride=ROW_TILES), :] = h2[:, j * LANES:(j + 1) * LANES]
    h2_hi, h2_lo = _split_bf16(h2)
    logits = (_dot(h2_hi, wrh_ref[...]) + _dot(h2_lo, wrh_ref[...]) + _dot(h2_hi, wrl_ref[...])
              + br_ref[...])

    lane = lax.broadcasted_iota(I32, (ts, LANES), 1).astype(F32)
    work = logits
    sel = jnp.zeros((ts, LANES), F32)
    top_v, top_i, top_oh = [], [], []
    for _ in range(TOP_K):
        m = jnp.max(work, axis=-1, keepdims=True)
        idx = jnp.min(jnp.where(work == m, lane, float(LANES)), axis=-1, keepdims=True)
        oh = lane == idx
        top_v.append(m)
        top_i.append(idx)
        top_oh.append(oh)
        sel = sel + oh.astype(F32)
        work = jnp.where(oh, -jnp.inf, work)
    exps = [jnp.exp(tv - top_v[0]) for tv in top_v]
    denom = exps[0] + exps[1] + exps[2] + exps[3]
    gates = [e / denom for e in exps]

    ti = lax.broadcasted_iota(I32, (ts, ts), 0)
    tj = lax.broadcasted_iota(I32, (ts, ts), 1)
    strict_lower = (tj < ti).astype(BF16)
    prefix = _dot(strict_lower, sel.astype(BF16)) + carry_ref[0:1, :]
    ranks = [jnp.sum(jnp.where(oh, prefix, 0.0), axis=-1, keepdims=True) for oh in top_oh]
    new_carry = carry_ref[0:1, :] + jnp.sum(sel, axis=0, keepdims=True)
    carry_ref[...] = jnp.broadcast_to(new_carry, carry_ref.shape)
    cnt_ref[...] = jnp.broadcast_to(new_carry, cnt_ref.shape)

    meta = jnp.zeros((ts, LANES), F32)
    for kk in range(TOP_K):
        meta = jnp.where(lane == float(kk), top_i[kk], meta)
        meta = jnp.where(lane == float(TOP_K + kk), gates[kk], meta)
        meta = jnp.where(lane == float(2 * TOP_K + kk), ranks[kk], meta)
    meta_ref[0] = meta
    route_ref[...] = meta.T[0:ROUTE_ROWS, :]


def _mixer_call(x, g1, win, convw, wgk, bgk, gng, wout, g2, wrh, wrl, br):
    bsz, seq, d = x.shape
    ts = SEQ_TILE
    grid = (bsz, seq // ts)

    def const(shape):
        return pl.BlockSpec(shape, lambda b, s: (0,) * len(shape))

    tile = lambda w: pl.BlockSpec((1, ts, w), lambda b, s: (b, s, 0))
    return pl.pallas_call(
        _mixer_kernel,
        grid=grid,
        in_specs=[tile(d), const(g1.shape), const(win.shape), const(convw.shape),
                  const(wgk.shape), const(bgk.shape), const(gng.shape), const(wout.shape),
                  const(g2.shape), const(wrh.shape), const(wrl.shape), const(br.shape)],
        out_specs=[tile(d),
                   pl.BlockSpec((1, ts * ROW_TILES, LANES), lambda b, s: (b, s, 0)),
                   tile(LANES),
                   pl.BlockSpec((ROUTE_ROWS, ts), lambda b, s: (0, b * (seq // ts) + s)),
                   const((SUBLANES, LANES))],
        out_shape=[jax.ShapeDtypeStruct((bsz, seq, d), F32),
                   jax.ShapeDtypeStruct((bsz, seq * ROW_TILES, LANES), F32),
                   jax.ShapeDtypeStruct((bsz, seq, LANES), F32),
                   jax.ShapeDtypeStruct((ROUTE_ROWS, bsz * seq), F32),
                   jax.ShapeDtypeStruct((SUBLANES, LANES), F32)],
        scratch_shapes=[pltpu.VMEM((ts, D_IN_PAD), F32),
                        pltpu.VMEM((ts + SUBLANES, CONV_WIDTH), F32),
                        pltpu.VMEM((ts, GLA_QK), F32),
                        pltpu.VMEM((ts, D_MODEL), BF16),
                        pltpu.VMEM((GLA_QK, GLA_WIDTH), F32),
                        pltpu.VMEM((SUBLANES, LANES), F32)],
        compiler_params=pltpu.CompilerParams(
            dimension_semantics=("arbitrary", "arbitrary"), vmem_limit_bytes=VMEM_LIMIT),
        name="mixer",
    )(x, g1, win, convw, wgk, bgk, gng, wout, g2, wrh, wrl, br)


def _dispatch_kernel(pad_lo_ref, pad_n_ref, pos_ref, h2_ref, xs_hbm, zbuf, sem, zsem):
    td = h2_ref.shape[0] // ROW_TILES

    def zero_fill(wait):
        def fill(g, carry):
            n = pl.multiple_of(pad_n_ref[g], ROW_TILES)

            @pl.when(n > 0)
            def _():
                lo = pl.multiple_of(pad_lo_ref[g], ROW_TILES)
                cp = pltpu.make_async_copy(zbuf.at[pl.ds(0, n), :], xs_hbm.at[pl.ds(lo, n), :],
                                           zsem.at[0])
                if wait:
                    cp.wait()
                else:
                    cp.start()
            return carry

        lax.fori_loop(0, pad_lo_ref.shape[0], fill, 0)

    @pl.when(pl.program_id(0) == 0)
    def _():
        zbuf[...] = jnp.zeros_like(zbuf)
        zero_fill(wait=False)

    def body(tok, carry):
        src = h2_ref.at[pl.ds(pl.multiple_of(tok * ROW_TILES, ROW_TILES), ROW_TILES), :]
        for kk in range(TOP_K):
            row = pl.multiple_of(pos_ref[kk, tok], ROW_TILES)
            pltpu.make_async_copy(src, xs_hbm.at[pl.ds(row, ROW_TILES), :], sem.at[0]).start(
                priority=kk % 2)
        return carry

    lax.fori_loop(0, td, body, 0, unroll=2)
    for _ in range(TOP_K):
        pltpu.make_async_copy(h2_ref, xs_hbm.at[pl.ds(0, td * ROW_TILES), :], sem.at[0]).wait()

    @pl.when(pl.program_id(0) == 0)
    def _():
        zero_fill(wait=True)


def _dispatch_call(pad_lo, pad_n, pos_rows, h2, n_rows):
    t = h2.shape[0] // ROW_TILES
    td = DISPATCH_TILE
    grid_spec = pltpu.PrefetchScalarGridSpec(
        num_scalar_prefetch=2,
        grid=(t // td,),
        in_specs=[pl.BlockSpec((TOP_K, td), lambda i, lo, n: (0, i), memory_space=pltpu.SMEM),
                  pl.BlockSpec((td * ROW_TILES, LANES), lambda i, lo, n: (i, 0))],
        out_specs=pl.BlockSpec(memory_space=pl.ANY),
        scratch_shapes=[pltpu.VMEM((MOE_BLOCK * ROW_TILES, LANES), F32),
                        pltpu.SemaphoreType.DMA((1,)),
                        pltpu.SemaphoreType.DMA((1,))],
    )
    return pl.pallas_call(
        _dispatch_kernel,
        grid_spec=grid_spec,
        out_shape=jax.ShapeDtypeStruct((n_rows * ROW_TILES, LANES), F32),
        compiler_params=pltpu.CompilerParams(
            dimension_semantics=("arbitrary",), vmem_limit_bytes=VMEM_LIMIT,
            has_side_effects=True),
        name="dispatch",
    )(pad_lo, pad_n, pos_rows, h2)


def _experts_kernel(bexp_ref, first_ref, nxt_ref, nreal_ref,
                    xs_ref, wgu_hbm, bgu_ref, wd_hbm, bd_ref,
                    ys_ref,
                    xb_ref, act_ref, wgu_stage, wd_stage, wgu_bf, wd_bf, wsem):
    i = pl.program_id(0)
    nreal = nreal_ref[0]
    bm = xb_ref.shape[0]

    def weight_copies(e):
        return (pltpu.make_async_copy(wgu_hbm.at[e], wgu_stage, wsem.at[0]),
                pltpu.make_async_copy(wd_hbm.at[e], wd_stage, wsem.at[1]))

    @pl.when(i >= nreal)
    def _():
        ys_ref[...] = jnp.zeros_like(ys_ref)

    @pl.when(i < nreal)
    def _():
        e = bexp_ref[i]

        @pl.when(i == 0)
        def _():
            for cp in weight_copies(e):
                cp.start(priority=1)

        @pl.when(first_ref[i] == 1)
        def _():
            for cp in weight_copies(e):
                cp.wait()

            def cast_gu(r, carry):
                rows = pl.ds(pl.multiple_of(r * CAST_ROWS, CAST_ROWS), CAST_ROWS)
                wgu_bf[rows, :] = wgu_stage[rows, :].astype(BF16)
                return carry

            def cast_d(r, carry):
                rows = pl.ds(pl.multiple_of(r * CAST_ROWS, CAST_ROWS), CAST_ROWS)
                wd_bf[rows, :] = wd_stage[rows, :].astype(BF16)
                return carry

            lax.fori_loop(0, D_MODEL // CAST_ROWS, cast_gu, 0)
            lax.fori_loop(0, D_FF // CAST_ROWS, cast_d, 0)

            @pl.when(nxt_ref[i] >= 0)
            def _():
                for cp in weight_copies(nxt_ref[i]):
                    cp.start(priority=1)

        for j in range(ROW_TILES):
            xb_ref[:, j * LANES:(j + 1) * LANES] = (
                xs_ref[pl.ds(j, bm, stride=ROW_TILES), :].astype(BF16))
        for c in range(D_FF // FF_CHUNK):
            f0 = c * FF_CHUNK
            xb = xb_ref[...]
            gate = _dot(xb, wgu_bf[:, f0:f0 + FF_CHUNK]) + bgu_ref[0, :, f0:f0 + FF_CHUNK]
            up = (_dot(xb, wgu_bf[:, D_FF + f0:D_FF + f0 + FF_CHUNK])
                  + bgu_ref[0, :, D_FF + f0:D_FF + f0 + FF_CHUNK])
            gate = jnp.minimum(gate, SWIGLU_LIMIT)
            up = jnp.clip(up, -SWIGLU_LIMIT, SWIGLU_LIMIT)
            glu = gate * jax.nn.sigmoid(gate * SWIGLU_ALPHA)
            act_ref[:, f0:f0 + FF_CHUNK] = ((up + 1.0) * glu).astype(BF16)
        out = _dot(act_ref[...], wd_bf[...]) + bd_ref[0]
        for j in range(ROW_TILES):
            ys_ref[pl.ds(j, bm, stride=ROW_TILES), :] = out[:, j * LANES:(j + 1) * LANES]


def _experts_call(bexp, first, nxt, nreal, xs, wgu, bgu, wd, bd):
    bm = MOE_BLOCK
    n_blocks = xs.shape[0] // (bm * ROW_TILES)
    bgu3 = bgu.reshape(N_EXPERTS, 1, 2 * D_FF)
    bd3 = bd.reshape(N_EXPERTS, 1, D_MODEL)
    grid_spec = pltpu.PrefetchScalarGridSpec(
        num_scalar_prefetch=4,
        grid=(n_blocks,),
        in_specs=[
            pl.BlockSpec((bm * ROW_TILES, LANES),
                         lambda i, be, fi, nx, nr: (jnp.minimum(i, nr[0] - 1), 0)),
            pl.BlockSpec(memory_space=pl.ANY),
            pl.BlockSpec((1, 1, 2 * D_FF), lambda i, be, fi, nx, nr: (be[i], 0, 0)),
            pl.BlockSpec(memory_space=pl.ANY),
            pl.BlockSpec((1, 1, D_MODEL), lambda i, be, fi, nx, nr: (be[i], 0, 0)),
        ],
        out_specs=pl.BlockSpec((bm * ROW_TILES, LANES), lambda i, be, fi, nx, nr: (i, 0)),
        scratch_shapes=[pltpu.VMEM((bm, D_MODEL), BF16),
                        pltpu.VMEM((bm, D_FF), BF16),
                        pltpu.VMEM((D_MODEL, 2 * D_FF), F32),
                        pltpu.VMEM((D_FF, D_MODEL), F32),
                        pltpu.VMEM((D_MODEL, 2 * D_FF), BF16),
                        pltpu.VMEM((D_FF, D_MODEL), BF16),
                        pltpu.SemaphoreType.DMA((2,))],
    )
    return pl.pallas_call(
        _experts_kernel,
        grid_spec=grid_spec,
        out_shape=jax.ShapeDtypeStruct(xs.shape, F32),
        compiler_params=pltpu.CompilerParams(
            dimension_semantics=("arbitrary",), vmem_limit_bytes=VMEM_LIMIT),
        name="experts",
    )(bexp, first, nxt, nreal, xs, wgu, bgu3, wd, bd3)


def _combine_kernel(pos_cur_ref, pos_nxt_ref, x1_ref, meta_ref, g_ref, ys_hbm, out_ref, gbuf, gsem):
    i = pl.program_id(0)
    n = pl.num_programs(0)
    tb = x1_ref.shape[0]
    slot = i % 2

    def start_gather(pos_ref, buf_slot):
        def body(tok, carry):
            for kk in range(TOP_K):
                row = pl.multiple_of(pos_ref[kk, tok], ROW_TILES)
                dst = pl.multiple_of((kk * tb + tok) * ROW_TILES, ROW_TILES)
                pltpu.make_async_copy(ys_hbm.at[pl.ds(row, ROW_TILES), :],
                                      gbuf.at[buf_slot, pl.ds(dst, ROW_TILES), :],
                                      gsem.at[buf_slot]).start(priority=kk % 2)
            return carry
        lax.fori_loop(0, tb, body, 0, unroll=2)

    @pl.when(i == 0)
    def _():
        start_gather(pos_cur_ref, 0)

    @pl.when(i + 1 < n)
    def _():
        start_gather(pos_nxt_ref, 1 - slot)

    pltpu.make_async_copy(ys_hbm.at[pl.ds(0, TOP_K * tb * ROW_TILES), :], gbuf.at[slot],
                          gsem.at[slot]).wait()

    meta = meta_ref[...]
    gates = [meta[:, TOP_K + kk:TOP_K + kk + 1] for kk in range(TOP_K)]
    ssq = jnp.zeros((tb, 1), F32)
    for j in range(ROW_TILES):
        cols = slice(j * LANES, (j + 1) * LANES)
        acc = x1_ref[:, cols]
        for kk in range(TOP_K):
            acc = acc + gates[kk] * gbuf[slot, pl.ds(kk * tb * ROW_TILES + j, tb, stride=ROW_TILES), :]
        out_ref[:, cols] = acc
        ssq = ssq + jnp.sum(acc * acc, axis=-1, keepdims=True)
    inv = lax.rsqrt(ssq / D_MODEL + RMS_EPS)
    out_ref[...] = out_ref[...] * inv * g_ref[...]


def _combine_call(pos_rows, x1, meta, g, ys):
    t, d = x1.shape
    tb = COMBINE_TILE
    nb = t // tb
    smem_blk = lambda fn: pl.BlockSpec((TOP_K, tb), fn, memory_space=pltpu.SMEM)
    return pl.pallas_call(
        _combine_kernel,
        grid=(nb,),
        in_specs=[smem_blk(lambda i: (0, i)),
                  smem_blk(lambda i: (0, jnp.minimum(i + 1, nb - 1))),
                  pl.BlockSpec((tb, d), lambda i: (i, 0)),
                  pl.BlockSpec((tb, LANES), lambda i: (i, 0)),
                  pl.BlockSpec((1, d), lambda i: (0, 0)),
                  pl.BlockSpec(memory_space=pl.ANY)],
        out_specs=pl.BlockSpec((tb, d), lambda i: (i, 0)),
        out_shape=jax.ShapeDtypeStruct((t, d), F32),
        scratch_shapes=[pltpu.VMEM((2, TOP_K * tb * ROW_TILES, LANES), F32),
                        pltpu.SemaphoreType.DMA((2,))],
        compiler_params=pltpu.CompilerParams(
            dimension_semantics=("arbitrary",), vmem_limit_bytes=VMEM_LIMIT),
        name="combine",
    )(pos_rows, pos_rows, x1, meta, g, ys)


def _routing_tables(route, counts, t):
    bm = MOE_BLOCK
    n_blocks = t * TOP_K // bm + N_EXPERTS
    e_idx = route[0:TOP_K].astype(I32)
    rank = route[2 * TOP_K:3 * TOP_K].astype(I32)
    eids = jnp.arange(N_EXPERTS, dtype=I32)
    nblk_e = (counts + bm - 1) // bm
    blk_end = jnp.sum(jnp.where(eids[None, :] <= eids[:, None], nblk_e[None, :], 0), axis=1)
    blk_start = blk_end - nblk_e
    nreal = blk_end[N_EXPERTS - 1]
    pad_start = blk_start * bm
    pos = rank
    for e in range(N_EXPERTS):
        pos = pos + jnp.where(e_idx == e, pad_start[e], 0)

    blk = jnp.arange(n_blocks, dtype=I32)
    bexp = jnp.minimum(jnp.sum((blk_end[None, :] <= blk[:, None]).astype(I32), axis=1),
                       N_EXPERTS - 1)
    blk_is_e = bexp[:, None] == eids[None, :]
    pick = lambda tab: jnp.sum(jnp.where(blk_is_e, tab[None, :], 0), axis=1)
    first = (blk == pick(blk_start)).astype(I32)
    nxt_e = jnp.sum((blk_end[None, :] <= blk_end[:, None]).astype(I32), axis=1)
    nxt_e = jnp.where(blk_end < nreal, jnp.minimum(nxt_e, N_EXPERTS - 1), -1)
    nxt = pick(nxt_e)
    tail_blk = jnp.arange(N_EXPERTS, dtype=I32) + nreal
    pad_lo = jnp.concatenate([pad_start + counts, jnp.minimum(tail_blk, n_blocks - 1) * bm])
    pad_n = jnp.concatenate([nblk_e * bm - counts, jnp.where(tail_blk < n_blocks, bm, 0)])
    return (pos, bexp, first, nxt, nreal.reshape(1).astype(I32), n_blocks,
            pad_lo * ROW_TILES, pad_n * ROW_TILES)


def kernel(x, norm_mix_g, w_in, conv_w, w_gk_up, b_gk_up, gla_norm_g, w_out, norm_ffn_g,
           w_router, b_router, w_gate_up, b_gate_up, w_down, b_down, norm_final_g):
    bsz, seq, d = x.shape
    t = bsz * seq
    assert w_in.shape[0] == 1, "single-layer trunk only"
    l = 0
    d_in = w_in.shape[-1]
    win = jnp.pad(w_in[l], ((0, 0), (0, D_IN_PAD - d_in))).astype(BF16)
    wgk = jnp.pad(w_gk_up[l], ((0, LANES - GLA_RANK), (0, 0))).astype(BF16)
    wr = jnp.pad(w_router[l], ((0, 0), (0, LANES - N_EXPERTS)))
    wrh = wr.astype(BF16)
    wrl = (wr - wrh.astype(F32)).astype(BF16)
    br = jnp.pad(b_router[l], (0, LANES - N_EXPERTS), constant_values=NEG_BIG).reshape(1, LANES)

    x1, h2, meta, route, cnt = _mixer_call(
        x, norm_mix_g[l].reshape(1, d), win, conv_w[l], wgk, b_gk_up[l].reshape(1, GLA_QK),
        gla_norm_g[l].reshape(1, GLA_DV), w_out[l].astype(BF16), norm_ffn_g[l].reshape(1, d),
        wrh, wrl, br)

    meta2 = meta.reshape(t, LANES)
    counts = cnt[0, :N_EXPERTS].astype(I32)
    pos, bexp, first, nxt, nreal, n_blocks, pad_lo, pad_n = _routing_tables(route, counts, t)
    pos_rows = pos * ROW_TILES
    xs = _dispatch_call(pad_lo, pad_n, pos_rows, h2.reshape(t * ROW_TILES, LANES),
                        n_blocks * MOE_BLOCK)
    ys = _experts_call(bexp, first, nxt, nreal, xs, w_gate_up[l], b_gate_up[l], w_down[l], b_down[l])
    out = _combine_call(pos_rows, x1.reshape(t, d), meta2, norm_final_g.reshape(1, d), ys)
    return out.reshape(bsz, seq, d)
```

```python
import jax
import jax.numpy as jnp
from jax import lax
from jax.experimental import pallas as pl
from jax.experimental.pallas import tpu as pltpu

F32 = jnp.float32
BF16 = jnp.bfloat16
I32 = jnp.int32

D_MODEL = 1024
CONV_WIDTH = 512
CONV_K = 3
GLA_WIDTH = 512
GLA_HEADS = 4
GLA_DV = 128
GLA_DK = 64
GLA_QK = GLA_HEADS * GLA_DK
GLA_RANK = 16
GLA_NORMALIZER = 16.0
GLA_CHUNK = 64
N_EXPERTS = 32
TOP_K = 4
D_FF = 1024
SWIGLU_LIMIT = 7.0
SWIGLU_ALPHA = 1.702
RMS_EPS = 1e-5

LANES = 128
SUBLANES = 8
ROW_TILES = D_MODEL // LANES
ROUTE_ROWS = 16

OFF_UH = 0
OFF_GB = OFF_UH + CONV_WIDTH
OFF_GC = OFF_GB + CONV_WIDTH
OFF_Q = OFF_GC + CONV_WIDTH
OFF_K = OFF_Q + GLA_QK
OFF_V = OFF_K + GLA_QK
OFF_GO = OFF_V + GLA_WIDTH
OFF_GKL = OFF_GO + GLA_WIDTH
D_IN_PAD = OFF_GKL + LANES

SEQ_TILE = 512
MOE_BLOCK = 512
DISPATCH_TILE = 512
FF_CHUNK = 256
CAST_ROWS = 128
COMBINE_TILE = 256
NEG_BIG = -1e30
VMEM_LIMIT = 56 * 1024 * 1024


def _rms(x, g):
    return x * lax.rsqrt(jnp.mean(x * x, axis=-1, keepdims=True) + RMS_EPS) * g


def _dot(a, b):
    return jnp.dot(a, b, preferred_element_type=F32)


def _dot_nt(a, b):
    return lax.dot_general(a, b, (((1,), (1,)), ((), ())), preferred_element_type=F32)


def _split_bf16(x):
    hi = x.astype(BF16)
    lo = (x - hi.astype(F32)).astype(BF16)
    return hi, lo


def _mixer_kernel(x_ref, g1_ref, win_ref, convw_ref, wgk_ref, bgk_ref, gng_ref, wout_ref,
                  g2_ref, wrhl_ref, br_ref,
                  x1_ref, h2_ref, meta_ref, route_ref, cnt_ref,
                  proj_ref, ubuf_ref, la_ref, ycat_ref, state_ref, carry_ref):
    ts = x_ref.shape[1]
    b_idx = pl.program_id(0)
    s_idx = pl.program_id(1)

    @pl.when(s_idx == 0)
    def _():
        state_ref[...] = jnp.zeros_like(state_ref)
        ubuf_ref[0:SUBLANES, :] = jnp.zeros((SUBLANES, CONV_WIDTH), F32)

    @pl.when((s_idx == 0) & (b_idx == 0))
    def _():
        carry_ref[...] = jnp.zeros_like(carry_ref)

    x = x_ref[0]
    h = _rms(x, g1_ref[...]).astype(BF16)
    proj_ref[...] = _dot(h, win_ref[...])

    u = proj_ref[:, OFF_GC:OFF_GC + CONV_WIDTH] * proj_ref[:, OFF_UH:OFF_UH + CONV_WIDTH]
    ubuf_ref[SUBLANES:SUBLANES + ts, :] = u
    u1 = ubuf_ref[pl.ds(SUBLANES - 1, ts), :]
    u2 = ubuf_ref[pl.ds(SUBLANES - 2, ts), :]
    conv = convw_ref[0:1, :] * u2 + convw_ref[1:2, :] * u1 + convw_ref[2:3, :] * u
    ycat_ref[:, 0:CONV_WIDTH] = (proj_ref[:, OFF_GB:OFF_GB + CONV_WIDTH] * conv).astype(BF16)
    ubuf_ref[0:SUBLANES, :] = ubuf_ref[ts:ts + SUBLANES, :]

    gk = _dot(proj_ref[:, OFF_GKL:OFF_GKL + LANES].astype(BF16), wgk_ref[...]) + bgk_ref[...]
    log_sig = jnp.minimum(gk, 0.0) - jnp.log1p(jnp.exp(-jnp.abs(gk)))
    la_ref[...] = log_sig / GLA_NORMALIZER

    ci = lax.broadcasted_iota(I32, (GLA_CHUNK, GLA_CHUNK), 0)
    cj = lax.broadcasted_iota(I32, (GLA_CHUNK, GLA_CHUNK), 1)
    tri_incl = (cj <= ci).astype(BF16)
    causal = cj <= ci
    causal4 = jnp.concatenate([causal] * GLA_HEADS, axis=0)
    lane_qk = lax.broadcasted_iota(I32, (1, GLA_QK), 1)
    head_masks = [((lane_qk >= hd * GLA_DK) & (lane_qk < (hd + 1) * GLA_DK)).astype(F32)
                  for hd in range(GLA_HEADS)]
    gng = gng_ref[...]

    def chunk_body(c, carry):
        r0 = pl.multiple_of(c * GLA_CHUNK, GLA_CHUNK)
        rows = pl.ds(r0, GLA_CHUNK)
        la_hi, la_lo = _split_bf16(la_ref[rows, :])
        bcum = _dot(tri_incl, la_hi) + _dot(tri_incl, la_lo)
        blast = bcum[GLA_CHUNK - 1:GLA_CHUNK, :]
        q = proj_ref[rows, OFF_Q:OFF_Q + GLA_QK] * (GLA_DK ** -0.5)
        k = proj_ref[rows, OFF_K:OFF_K + GLA_QK]
        v = proj_ref[rows, OFF_V:OFF_V + GLA_WIDTH].astype(BF16)
        qd = q * jnp.exp(bcum)
        kd = (k * jnp.exp(-bcum)).astype(BF16)
        kr = k * jnp.exp(blast - bcum)

        q_stack = jnp.concatenate([qd * m for m in head_masks], axis=0).astype(BF16)
        scores = jnp.where(causal4, _dot_nt(q_stack, kd), 0.0).astype(BF16)

        state = state_ref[...]
        o_inter = _dot(qd.astype(BF16), state.astype(BF16))
        o_intra = jnp.concatenate(
            [_dot(scores[hd * GLA_CHUNK:(hd + 1) * GLA_CHUNK, :],
                  v[:, hd * GLA_DV:(hd + 1) * GLA_DV]) for hd in range(GLA_HEADS)], axis=1)
        o = o_inter + o_intra

        kt = jnp.concatenate([kr, jnp.broadcast_to(blast, (GLA_CHUNK, GLA_QK))], axis=0).T
        dcol = jnp.exp(kt[:, GLA_CHUNK:GLA_CHUNK + 1])
        lane_c = lax.broadcasted_iota(I32, (GLA_QK, 2 * GLA_CHUNK), 1)
        kt_b = jnp.where(lane_c < GLA_CHUNK, kt, 0.0).astype(BF16)
        v_pad = jnp.concatenate([v, jnp.zeros_like(v)], axis=0)
        for hd in range(GLA_HEADS):
            rs = slice(hd * GLA_DK, (hd + 1) * GLA_DK)
            cs = slice(hd * GLA_DV, (hd + 1) * GLA_DV)
            kv = _dot(kt_b[rs, :], v_pad[:, cs])
            state_ref[rs, cs] = dcol[rs, :] * state[rs, cs] + kv

        g_out = proj_ref[rows, OFF_GO:OFF_GO + GLA_WIDTH]
        o_n = jnp.concatenate(
            [_rms(o[:, hd * GLA_DV:(hd + 1) * GLA_DV], gng) for hd in range(GLA_HEADS)], axis=1)
        y = o_n * (g_out * jax.nn.sigmoid(g_out))
        ycat_ref[rows, CONV_WIDTH:CONV_WIDTH + GLA_WIDTH] = y.astype(BF16)
        return carry

    lax.fori_loop(0, ts // GLA_CHUNK, chunk_body, 0, unroll=True)

    x1 = x + _dot(ycat_ref[...], wout_ref[...])
    x1_ref[0] = x1
    h2 = _rms(x1, g2_ref[...])
    for j in range(ROW_TILES):
        h2_ref[0, pl.ds(j, ts, stride=ROW_TILES), :] = h2[:, j * LANES:(j + 1) * LANES]
    h2_hi, h2_lo = _split_bf16(h2)
    hi_terms = _dot(h2_hi, wrhl_ref[...])
    logits = (hi_terms[:, 0:LANES] + hi_terms[:, LANES:2 * LANES]
              + _dot(h2_lo, wrhl_ref[:, 0:LANES]) + br_ref[...])

    lane = lax.broadcasted_iota(I32, (ts, LANES), 1).astype(F32)
    work = logits
    sel = jnp.zeros((ts, LANES), F32)
    top_v, top_i, top_oh = [], [], []
    for _ in range(TOP_K):
        m = jnp.max(work, axis=-1, keepdims=True)
        idx = jnp.min(jnp.where(work == m, lane, float(LANES)), axis=-1, keepdims=True)
        oh = lane == idx
        top_v.append(m)
        top_i.append(idx)
        top_oh.append(oh)
        sel = sel + oh.astype(F32)
        work = jnp.where(oh, -jnp.inf, work)
    exps = [jnp.exp(tv - top_v[0]) for tv in top_v]
    denom = exps[0] + exps[1] + exps[2] + exps[3]
    gates = [e / denom for e in exps]

    ti = lax.broadcasted_iota(I32, (ts, ts), 0)
    tj = lax.broadcasted_iota(I32, (ts, ts), 1)
    strict_lower = (tj < ti).astype(BF16)
    prefix = _dot(strict_lower, sel.astype(BF16)) + carry_ref[0:1, :]
    ranks = [jnp.sum(jnp.where(oh, prefix, 0.0), axis=-1, keepdims=True) for oh in top_oh]
    new_carry = carry_ref[0:1, :] + jnp.sum(sel, axis=0, keepdims=True)
    carry_ref[...] = jnp.broadcast_to(new_carry, carry_ref.shape)
    cnt_ref[...] = jnp.broadcast_to(new_carry, cnt_ref.shape)

    meta = jnp.zeros((ts, LANES), F32)
    for kk in range(TOP_K):
        meta = jnp.where(lane == float(kk), top_i[kk], meta)
        meta = jnp.where(lane == float(TOP_K + kk), gates[kk], meta)
        meta = jnp.where(lane == float(2 * TOP_K + kk), ranks[kk], meta)
    meta_ref[0] = meta
    route_ref[...] = meta.T[0:ROUTE_ROWS, :]


def _mixer_call(x, g1, win, convw, wgk, bgk, gng, wout, g2, wrhl, br):
    bsz, seq, d = x.shape
    ts = SEQ_TILE
    grid = (bsz, seq // ts)

    def const(shape):
        return pl.BlockSpec(shape, lambda b, s: (0,) * len(shape))

    tile = lambda w: pl.BlockSpec((1, ts, w), lambda b, s: (b, s, 0))
    return pl.pallas_call(
        _mixer_kernel,
        grid=grid,
        in_specs=[tile(d), const(g1.shape), const(win.shape), const(convw.shape),
                  const(wgk.shape), const(bgk.shape), const(gng.shape), const(wout.shape),
                  const(g2.shape), const(wrhl.shape), const(br.shape)],
        out_specs=[tile(d),
                   pl.BlockSpec((1, ts * ROW_TILES, LANES), lambda b, s: (b, s, 0)),
                   tile(LANES),
                   pl.BlockSpec((ROUTE_ROWS, ts), lambda b, s: (0, b * (seq // ts) + s)),
                   const((SUBLANES, LANES))],
        out_shape=[jax.ShapeDtypeStruct((bsz, seq, d), F32),
                   jax.ShapeDtypeStruct((bsz, seq * ROW_TILES, LANES), F32),
                   jax.ShapeDtypeStruct((bsz, seq, LANES), F32),
                   jax.ShapeDtypeStruct((ROUTE_ROWS, bsz * seq), F32),
                   jax.ShapeDtypeStruct((SUBLANES, LANES), F32)],
        scratch_shapes=[pltpu.VMEM((ts, D_IN_PAD), F32),
                        pltpu.VMEM((ts + SUBLANES, CONV_WIDTH), F32),
                        pltpu.VMEM((ts, GLA_QK), F32),
                        pltpu.VMEM((ts, D_MODEL), BF16),
                        pltpu.VMEM((GLA_QK, GLA_WIDTH), F32),
                        pltpu.VMEM((SUBLANES, LANES), F32)],
        compiler_params=pltpu.CompilerParams(
            dimension_semantics=("arbitrary", "arbitrary"), vmem_limit_bytes=VMEM_LIMIT),
        name="mixer",
    )(x, g1, win, convw, wgk, bgk, gng, wout, g2, wrhl, br)


def _dispatch_kernel(pad_lo_ref, pad_n_ref, pos_ref, h2_ref, xs_hbm, zbuf, sem, zsem):
    td = h2_ref.shape[0] // ROW_TILES

    def zero_fill(wait):
        def fill(g, carry):
            n = pl.multiple_of(pad_n_ref[g], ROW_TILES)

            @pl.when(n > 0)
            def _():
                lo = pl.multiple_of(pad_lo_ref[g], ROW_TILES)
                cp = pltpu.make_async_copy(zbuf.at[pl.ds(0, n), :], xs_hbm.at[pl.ds(lo, n), :],
                                           zsem.at[0])
                if wait:
                    cp.wait()
                else:
                    cp.start()
            return carry

        lax.fori_loop(0, pad_lo_ref.shape[0], fill, 0)

    @pl.when(pl.program_id(0) == 0)
    def _():
        zbuf[...] = jnp.zeros_like(zbuf)
        zero_fill(wait=False)

    def body(tok, carry):
        src = h2_ref.at[pl.ds(pl.multiple_of(tok * ROW_TILES, ROW_TILES), ROW_TILES), :]
        for kk in range(TOP_K):
            row = pl.multiple_of(pos_ref[kk, tok], ROW_TILES)
            pltpu.make_async_copy(src, xs_hbm.at[pl.ds(row, ROW_TILES), :], sem.at[0]).start(
                priority=kk % 2)
        return carry

    lax.fori_loop(0, td, body, 0, unroll=8)
    for _ in range(TOP_K):
        pltpu.make_async_copy(h2_ref, xs_hbm.at[pl.ds(0, td * ROW_TILES), :], sem.at[0]).wait()

    @pl.when(pl.program_id(0) == 0)
    def _():
        zero_fill(wait=True)


def _dispatch_call(pad_lo, pad_n, pos_rows, h2, n_rows):
    t = h2.shape[0] // ROW_TILES
    td = DISPATCH_TILE
    grid_spec = pltpu.PrefetchScalarGridSpec(
        num_scalar_prefetch=2,
        grid=(t // td,),
        in_specs=[pl.BlockSpec((TOP_K, td), lambda i, lo, n: (0, i), memory_space=pltpu.SMEM),
                  pl.BlockSpec((td * ROW_TILES, LANES), lambda i, lo, n: (i, 0))],
        out_specs=pl.BlockSpec(memory_space=pl.ANY),
        scratch_shapes=[pltpu.VMEM((MOE_BLOCK * ROW_TILES, LANES), F32),
                        pltpu.SemaphoreType.DMA((1,)),
                        pltpu.SemaphoreType.DMA((1,))],
    )
    return pl.pallas_call(
        _dispatch_kernel,
        grid_spec=grid_spec,
        out_shape=jax.ShapeDtypeStruct((n_rows * ROW_TILES, LANES), F32),
        compiler_params=pltpu.CompilerParams(
            dimension_semantics=("arbitrary",), vmem_limit_bytes=VMEM_LIMIT,
            has_side_effects=True),
        name="dispatch",
    )(pad_lo, pad_n, pos_rows, h2)


def _experts_kernel(bexp_ref, first_ref, nxt_ref, nval_ref, nreal_ref,
                    xs_ref, wgu_hbm, bgu_ref, wd_hbm, bd_ref,
                    ys_ref,
                    xb_ref, act_ref, wgu_stage, wd_stage, wgu_bf, wd_bf, wsem):
    i = pl.program_id(0)
    nreal = nreal_ref[0]
    bm = xb_ref.shape[0]

    def weight_copies(e):
        return (pltpu.make_async_copy(wgu_hbm.at[e], wgu_stage, wsem.at[0]),
                pltpu.make_async_copy(wd_hbm.at[e], wd_stage, wsem.at[1]))

    @pl.when(i >= nreal)
    def _():
        ys_ref[...] = jnp.zeros_like(ys_ref)

    @pl.when(i < nreal)
    def _():
        e = bexp_ref[i]

        @pl.when(i == 0)
        def _():
            for cp in weight_copies(e):
                cp.start(priority=1)

        @pl.when(first_ref[i] == 1)
        def _():
            for cp in weight_copies(e):
                cp.wait()

            def cast_gu(r, carry):
                rows = pl.ds(pl.multiple_of(r * CAST_ROWS, CAST_ROWS), CAST_ROWS)
                wgu_bf[rows, :] = wgu_stage[rows, :].astype(BF16)
                return carry

            def cast_d(r, carry):
                rows = pl.ds(pl.multiple_of(r * CAST_ROWS, CAST_ROWS), CAST_ROWS)
                wd_bf[rows, :] = wd_stage[rows, :].astype(BF16)
                return carry

            lax.fori_loop(0, D_MODEL // CAST_ROWS, cast_gu, 0)
            lax.fori_loop(0, D_FF // CAST_ROWS, cast_d, 0)

            @pl.when(nxt_ref[i] >= 0)
            def _():
                for cp in weight_copies(nxt_ref[i]):
                    cp.start(priority=1)

        def mlp(rows):
            for j in range(ROW_TILES):
                xb_ref[0:rows, j * LANES:(j + 1) * LANES] = (
                    xs_ref[pl.ds(j, rows, stride=ROW_TILES), :].astype(BF16))
            for c in range(D_FF // FF_CHUNK):
                f0 = c * FF_CHUNK
                xb = xb_ref[0:rows, :]
                gate = _dot(xb, wgu_bf[:, f0:f0 + FF_CHUNK]) + bgu_ref[0, :, f0:f0 + FF_CHUNK]
                up = (_dot(xb, wgu_bf[:, D_FF + f0:D_FF + f0 + FF_CHUNK])
                      + bgu_ref[0, :, D_FF + f0:D_FF + f0 + FF_CHUNK])
                gate = jnp.minimum(gate, SWIGLU_LIMIT)
                up = jnp.clip(up, -SWIGLU_LIMIT, SWIGLU_LIMIT)
                glu = gate * jax.nn.sigmoid(gate * SWIGLU_ALPHA)
                act_ref[0:rows, f0:f0 + FF_CHUNK] = ((up + 1.0) * glu).astype(BF16)
            out = _dot(act_ref[0:rows, :], wd_bf[...]) + bd_ref[0]
            for j in range(ROW_TILES):
                ys_ref[pl.ds(j, rows, stride=ROW_TILES), :] = out[:, j * LANES:(j + 1) * LANES]

        @pl.when(nval_ref[i] > bm // 2)
        def _():
            mlp(bm)

        @pl.when(nval_ref[i] <= bm // 2)
        def _():
            mlp(bm // 2)
            ys_ref[bm // 2 * ROW_TILES:bm * ROW_TILES, :] = jnp.zeros(
                (bm // 2 * ROW_TILES, LANES), F32)


def _experts_call(bexp, first, nxt, nval, nreal, xs, wgu, bgu, wd, bd):
    bm = MOE_BLOCK
    n_blocks = xs.shape[0] // (bm * ROW_TILES)
    bgu3 = bgu.reshape(N_EXPERTS, 1, 2 * D_FF)
    bd3 = bd.reshape(N_EXPERTS, 1, D_MODEL)
    grid_spec = pltpu.PrefetchScalarGridSpec(
        num_scalar_prefetch=5,
        grid=(n_blocks,),
        in_specs=[
            pl.BlockSpec((bm * ROW_TILES, LANES),
                         lambda i, be, fi, nx, nv, nr: (jnp.minimum(i, nr[0] - 1), 0)),
            pl.BlockSpec(memory_space=pl.ANY),
            pl.BlockSpec((1, 1, 2 * D_FF), lambda i, be, fi, nx, nv, nr: (be[i], 0, 0)),
            pl.BlockSpec(memory_space=pl.ANY),
            pl.BlockSpec((1, 1, D_MODEL), lambda i, be, fi, nx, nv, nr: (be[i], 0, 0)),
        ],
        out_specs=pl.BlockSpec((bm * ROW_TILES, LANES), lambda i, be, fi, nx, nv, nr: (i, 0)),
        scratch_shapes=[pltpu.VMEM((bm, D_MODEL), BF16),
                        pltpu.VMEM((bm, D_FF), BF16),
                        pltpu.VMEM((D_MODEL, 2 * D_FF), F32),
                        pltpu.VMEM((D_FF, D_MODEL), F32),
                        pltpu.VMEM((D_MODEL, 2 * D_FF), BF16),
                        pltpu.VMEM((D_FF, D_MODEL), BF16),
                        pltpu.SemaphoreType.DMA((2,))],
    )
    return pl.pallas_call(
        _experts_kernel,
        grid_spec=grid_spec,
        out_shape=jax.ShapeDtypeStruct(xs.shape, F32),
        compiler_params=pltpu.CompilerParams(
            dimension_semantics=("arbitrary",), vmem_limit_bytes=VMEM_LIMIT),
        name="experts",
    )(bexp, first, nxt, nval, nreal, xs, wgu, bgu3, wd, bd3)


def _combine_kernel(pos_cur_ref, pos_nxt_ref, x1_ref, meta_ref, g_ref, ys_hbm, out_ref, gbuf, gsem):
    i = pl.program_id(0)
    n = pl.num_programs(0)
    tb = x1_ref.shape[0]
    slot = i % 2

    def start_gather(pos_ref, buf_slot):
        def body(tok, carry):
            for kk in range(TOP_K):
                row = pl.multiple_of(pos_ref[kk, tok], ROW_TILES)
                dst = pl.multiple_of((kk * tb + tok) * ROW_TILES, ROW_TILES)
                pltpu.make_async_copy(ys_hbm.at[pl.ds(row, ROW_TILES), :],
                                      gbuf.at[buf_slot, pl.ds(dst, ROW_TILES), :],
                                      gsem.at[buf_slot]).start(priority=kk % 2)
            return carry
        lax.fori_loop(0, tb, body, 0, unroll=2)

    @pl.when(i == 0)
    def _():
        start_gather(pos_cur_ref, 0)

    @pl.when(i + 1 < n)
    def _():
        start_gather(pos_nxt_ref, 1 - slot)

    pltpu.make_async_copy(ys_hbm.at[pl.ds(0, TOP_K * tb * ROW_TILES), :], gbuf.at[slot],
                          gsem.at[slot]).wait()

    meta = meta_ref[...]
    gates = [meta[:, TOP_K + kk:TOP_K + kk + 1] for kk in range(TOP_K)]
    ssq = jnp.zeros((tb, 1), F32)
    for j in range(ROW_TILES):
        cols = slice(j * LANES, (j + 1) * LANES)
        acc = x1_ref[:, cols]
        for kk in range(TOP_K):
            acc = acc + gates[kk] * gbuf[slot, pl.ds(kk * tb * ROW_TILES + j, tb, stride=ROW_TILES), :]
        out_ref[:, cols] = acc
        ssq = ssq + jnp.sum(acc * acc, axis=-1, keepdims=True)
    inv = lax.rsqrt(ssq / D_MODEL + RMS_EPS)
    out_ref[...] = out_ref[...] * inv * g_ref[...]


def _combine_call(pos_rows, x1, meta, g, ys):
    t, d = x1.shape
    tb = COMBINE_TILE
    nb = t // tb
    smem_blk = lambda fn: pl.BlockSpec((TOP_K, tb), fn, memory_space=pltpu.SMEM)
    return pl.pallas_call(
        _combine_kernel,
        grid=(nb,),
        in_specs=[smem_blk(lambda i: (0, i)),
                  smem_blk(lambda i: (0, jnp.minimum(i + 1, nb - 1))),
                  pl.BlockSpec((tb, d), lambda i: (i, 0)),
                  pl.BlockSpec((tb, LANES), lambda i: (i, 0)),
                  pl.BlockSpec((1, d), lambda i: (0, 0)),
                  pl.BlockSpec(memory_space=pl.ANY)],
        out_specs=pl.BlockSpec((tb, d), lambda i: (i, 0)),
        out_shape=jax.ShapeDtypeStruct((t, d), F32),
        scratch_shapes=[pltpu.VMEM((2, TOP_K * tb * ROW_TILES, LANES), F32),
                        pltpu.SemaphoreType.DMA((2,))],
        compiler_params=pltpu.CompilerParams(
            dimension_semantics=("arbitrary",), vmem_limit_bytes=VMEM_LIMIT),
        name="combine",
    )(pos_rows, pos_rows, x1, meta, g, ys)


def _routing_tables(route, counts, t):
    bm = MOE_BLOCK
    n_blocks = t * TOP_K // bm + N_EXPERTS
    e_idx = route[0:TOP_K].astype(I32)
    rank = route[2 * TOP_K:3 * TOP_K].astype(I32)
    eids = jnp.arange(N_EXPERTS, dtype=I32)
    nblk_e = (counts + bm - 1) // bm
    blk_end = jnp.sum(jnp.where(eids[None, :] <= eids[:, None], nblk_e[None, :], 0), axis=1)
    blk_start = blk_end - nblk_e
    nreal = blk_end[N_EXPERTS - 1]
    pad_start = blk_start * bm
    pos = rank
    for e in range(N_EXPERTS):
        pos = pos + jnp.where(e_idx == e, pad_start[e], 0)

    blk = jnp.arange(n_blocks, dtype=I32)
    bexp = jnp.minimum(jnp.sum((blk_end[None, :] <= blk[:, None]).astype(I32), axis=1),
                       N_EXPERTS - 1)
    blk_is_e = bexp[:, None] == eids[None, :]
    pick = lambda tab: jnp.sum(jnp.where(blk_is_e, tab[None, :], 0), axis=1)
    first = (blk == pick(blk_start)).astype(I32)
    nxt_e = jnp.sum((blk_end[None, :] <= blk_end[:, None]).astype(I32), axis=1)
    nxt_e = jnp.where(blk_end < nreal, jnp.minimum(nxt_e, N_EXPERTS - 1), -1)
    nxt = pick(nxt_e)
    nval = jnp.clip(pick(counts) - (blk - pick(blk_start)) * bm, 0, bm)
    tail_blk = jnp.arange(N_EXPERTS, dtype=I32) + nreal
    pad_lo = jnp.concatenate([pad_start + counts, jnp.minimum(tail_blk, n_blocks - 1) * bm])
    pad_n = jnp.concatenate([nblk_e * bm - counts, jnp.where(tail_blk < n_blocks, bm, 0)])
    return (pos, bexp, first, nxt, nval, nreal.reshape(1).astype(I32), n_blocks,
            pad_lo * ROW_TILES, pad_n * ROW_TILES)


def kernel(x, norm_mix_g, w_in, conv_w, w_gk_up, b_gk_up, gla_norm_g, w_out, norm_ffn_g,
           w_router, b_router, w_gate_up, b_gate_up, w_down, b_down, norm_final_g):
    bsz, seq, d = x.shape
    t = bsz * seq
    assert w_in.shape[0] == 1, "single-layer trunk only"
    l = 0
    d_in = w_in.shape[-1]
    win = jnp.pad(w_in[l], ((0, 0), (0, D_IN_PAD - d_in))).astype(BF16)
    wgk = jnp.pad(w_gk_up[l], ((0, LANES - GLA_RANK), (0, 0))).astype(BF16)
    wr = jnp.pad(w_router[l], ((0, 0), (0, LANES - N_EXPERTS)))
    wrh = wr.astype(BF16)
    wrhl = jnp.concatenate([wrh, (wr - wrh.astype(F32)).astype(BF16)], axis=1)
    br = jnp.pad(b_router[l], (0, LANES - N_EXPERTS), constant_values=NEG_BIG).reshape(1, LANES)

    x1, h2, meta, route, cnt = _mixer_call(
        x, norm_mix_g[l].reshape(1, d), win, conv_w[l], wgk, b_gk_up[l].reshape(1, GLA_QK),
        gla_norm_g[l].reshape(1, GLA_DV), w_out[l].astype(BF16), norm_ffn_g[l].reshape(1, d),
        wrhl, br)

    meta2 = meta.reshape(t, LANES)
    counts = cnt[0, :N_EXPERTS].astype(I32)
    pos, bexp, first, nxt, nval, nreal, n_blocks, pad_lo, pad_n = _routing_tables(route, counts, t)
    pos_rows = pos * ROW_TILES
    xs = _dispatch_call(pad_lo, pad_n, pos_rows, h2.reshape(t * ROW_TILES, LANES),
                        n_blocks * MOE_BLOCK)
    ys = _experts_call(bexp, first, nxt, nval, nreal, xs, w_gate_up[l], b_gate_up[l], w_down[l],
                       b_down[l])
    out = _combine_call(pos_rows, x1.reshape(t, d), meta2, norm_final_g.reshape(1, d), ys)
    return out.reshape(bsz, seq, d)
```

```python
import jax
import jax.numpy as jnp
from jax import lax
from jax.experimental import pallas as pl
from jax.experimental.pallas import tpu as pltpu

F32 = jnp.float32
BF16 = jnp.bfloat16
I32 = jnp.int32

D_MODEL = 1024
CONV_WIDTH = 512
CONV_K = 3
GLA_WIDTH = 512
GLA_HEADS = 4
GLA_DV = 128
GLA_DK = 64
GLA_QK = GLA_HEADS * GLA_DK
GLA_RANK = 16
GLA_NORMALIZER = 16.0
GLA_CHUNK = 64
N_EXPERTS = 32
TOP_K = 4
D_FF = 1024
SWIGLU_LIMIT = 7.0
SWIGLU_ALPHA = 1.702
RMS_EPS = 1e-5

LANES = 128
SUBLANES = 8
ROW_TILES = D_MODEL // LANES
ROUTE_ROWS = 16

OFF_UH = 0
OFF_GB = OFF_UH + CONV_WIDTH
OFF_GC = OFF_GB + CONV_WIDTH
OFF_Q = OFF_GC + CONV_WIDTH
OFF_K = OFF_Q + GLA_QK
OFF_V = OFF_K + GLA_QK
OFF_GO = OFF_V + GLA_WIDTH
OFF_GKL = OFF_GO + GLA_WIDTH
D_IN_PAD = OFF_GKL + LANES

SEQ_TILE = 512
MOE_BLOCK = 512
DISPATCH_TILE = 512
FF_CHUNK = 256
CAST_ROWS = 128
NEG_BIG = -1e30
VMEM_LIMIT = 56 * 1024 * 1024


def _rms(x, g):
    return x * lax.rsqrt(jnp.mean(x * x, axis=-1, keepdims=True) + RMS_EPS) * g


def _dot(a, b):
    return jnp.dot(a, b, preferred_element_type=F32)


def _dot_nt(a, b):
    return lax.dot_general(a, b, (((1,), (1,)), ((), ())), preferred_element_type=F32)


def _split_bf16(x):
    hi = x.astype(BF16)
    lo = (x - hi.astype(F32)).astype(BF16)
    return hi, lo


def _mixer_kernel(x_ref, g1_ref, win_ref, convw_ref, wgk_ref, bgk_ref, gng_ref, wout_ref,
                  g2_ref, wrhl_ref, br_ref,
                  x1_ref, h2_ref, meta_ref, route_ref, seg_ref, cnt_ref,
                  proj_ref, ubuf_ref, la_ref, ycat_ref, state_ref, carry_ref):
    ts = x_ref.shape[1]
    b_idx = pl.program_id(0)
    s_idx = pl.program_id(1)

    @pl.when(s_idx == 0)
    def _():
        state_ref[...] = jnp.zeros_like(state_ref)
        ubuf_ref[0:SUBLANES, :] = jnp.zeros((SUBLANES, CONV_WIDTH), F32)

    @pl.when((s_idx == 0) & (b_idx == 0))
    def _():
        carry_ref[...] = jnp.zeros_like(carry_ref)

    x = x_ref[0]
    h = _rms(x, g1_ref[...]).astype(BF16)
    proj_ref[...] = _dot(h, win_ref[...])

    u = proj_ref[:, OFF_GC:OFF_GC + CONV_WIDTH] * proj_ref[:, OFF_UH:OFF_UH + CONV_WIDTH]
    ubuf_ref[SUBLANES:SUBLANES + ts, :] = u
    u1 = ubuf_ref[pl.ds(SUBLANES - 1, ts), :]
    u2 = ubuf_ref[pl.ds(SUBLANES - 2, ts), :]
    conv = convw_ref[0:1, :] * u2 + convw_ref[1:2, :] * u1 + convw_ref[2:3, :] * u
    ycat_ref[:, 0:CONV_WIDTH] = (proj_ref[:, OFF_GB:OFF_GB + CONV_WIDTH] * conv).astype(BF16)
    ubuf_ref[0:SUBLANES, :] = ubuf_ref[ts:ts + SUBLANES, :]

    gk = _dot(proj_ref[:, OFF_GKL:OFF_GKL + LANES].astype(BF16), wgk_ref[...]) + bgk_ref[...]
    log_sig = jnp.minimum(gk, 0.0) - jnp.log1p(jnp.exp(-jnp.abs(gk)))
    la_ref[...] = log_sig / GLA_NORMALIZER

    ci = lax.broadcasted_iota(I32, (GLA_CHUNK, GLA_CHUNK), 0)
    cj = lax.broadcasted_iota(I32, (GLA_CHUNK, GLA_CHUNK), 1)
    tri_incl = (cj <= ci).astype(BF16)
    causal = cj <= ci
    causal4 = jnp.concatenate([causal] * GLA_HEADS, axis=0)
    lane_qk = lax.broadcasted_iota(I32, (1, GLA_QK), 1)
    head_masks = [((lane_qk >= hd * GLA_DK) & (lane_qk < (hd + 1) * GLA_DK)).astype(F32)
                  for hd in range(GLA_HEADS)]
    gng = gng_ref[...]

    def chunk_body(c, carry):
        r0 = pl.multiple_of(c * GLA_CHUNK, GLA_CHUNK)
        rows = pl.ds(r0, GLA_CHUNK)
        la_hi, la_lo = _split_bf16(la_ref[rows, :])
        bcum = _dot(tri_incl, la_hi) + _dot(tri_incl, la_lo)
        blast = bcum[GLA_CHUNK - 1:GLA_CHUNK, :]
        q = proj_ref[rows, OFF_Q:OFF_Q + GLA_QK] * (GLA_DK ** -0.5)
        k = proj_ref[rows, OFF_K:OFF_K + GLA_QK]
        v = proj_ref[rows, OFF_V:OFF_V + GLA_WIDTH].astype(BF16)
        qd = q * jnp.exp(bcum)
        kd = (k * jnp.exp(-bcum)).astype(BF16)
        kr = k * jnp.exp(blast - bcum)

        q_stack = jnp.concatenate([qd * m for m in head_masks], axis=0).astype(BF16)
        scores = jnp.where(causal4, _dot_nt(q_stack, kd), 0.0).astype(BF16)

        state = state_ref[...]
        o_inter = _dot(qd.astype(BF16), state.astype(BF16))
        o_intra = jnp.concatenate(
            [_dot(scores[hd * GLA_CHUNK:(hd + 1) * GLA_CHUNK, :],
                  v[:, hd * GLA_DV:(hd + 1) * GLA_DV]) for hd in range(GLA_HEADS)], axis=1)
        o = o_inter + o_intra

        kt = jnp.concatenate([kr, jnp.broadcast_to(blast, (GLA_CHUNK, GLA_QK))], axis=0).T
        dcol = jnp.exp(kt[:, GLA_CHUNK:GLA_CHUNK + 1])
        lane_c = lax.broadcasted_iota(I32, (GLA_QK, 2 * GLA_CHUNK), 1)
        kt_b = jnp.where(lane_c < GLA_CHUNK, kt, 0.0).astype(BF16)
        v_pad = jnp.concatenate([v, jnp.zeros_like(v)], axis=0)
        for hd in range(GLA_HEADS):
            rs = slice(hd * GLA_DK, (hd + 1) * GLA_DK)
            cs = slice(hd * GLA_DV, (hd + 1) * GLA_DV)
            kv = _dot(kt_b[rs, :], v_pad[:, cs])
            state_ref[rs, cs] = dcol[rs, :] * state[rs, cs] + kv

        g_out = proj_ref[rows, OFF_GO:OFF_GO + GLA_WIDTH]
        o_n = jnp.concatenate(
            [_rms(o[:, hd * GLA_DV:(hd + 1) * GLA_DV], gng) for hd in range(GLA_HEADS)], axis=1)
        y = o_n * (g_out * jax.nn.sigmoid(g_out))
        ycat_ref[rows, CONV_WIDTH:CONV_WIDTH + GLA_WIDTH] = y.astype(BF16)
        return carry

    lax.fori_loop(0, ts // GLA_CHUNK, chunk_body, 0, unroll=True)

    x1 = x + _dot(ycat_ref[...], wout_ref[...])
    x1_ref[0] = x1
    h2 = _rms(x1, g2_ref[...])
    for j in range(ROW_TILES):
        h2_ref[0, pl.ds(j, ts, stride=ROW_TILES), :] = h2[:, j * LANES:(j + 1) * LANES]
    h2_hi, h2_lo = _split_bf16(h2)
    hi_terms = _dot(h2_hi, wrhl_ref[...])
    logits = (hi_terms[:, 0:LANES] + hi_terms[:, LANES:2 * LANES]
              + _dot(h2_lo, wrhl_ref[:, 0:LANES]) + br_ref[...])

    lane = lax.broadcasted_iota(I32, (ts, LANES), 1).astype(F32)
    work = logits
    sel = jnp.zeros((ts, LANES), F32)
    top_v, top_i, top_oh = [], [], []
    for _ in range(TOP_K):
        m = jnp.max(work, axis=-1, keepdims=True)
        idx = jnp.min(jnp.where(work == m, lane, float(LANES)), axis=-1, keepdims=True)
        oh = lane == idx
        top_v.append(m)
        top_i.append(idx)
        top_oh.append(oh)
        sel = sel + oh.astype(F32)
        work = jnp.where(oh, -jnp.inf, work)
    exps = [jnp.exp(tv - top_v[0]) for tv in top_v]
    denom = exps[0] + exps[1] + exps[2] + exps[3]
    gates = [e / denom for e in exps]

    ti = lax.broadcasted_iota(I32, (ts, ts), 0)
    tj = lax.broadcasted_iota(I32, (ts, ts), 1)
    strict_lower = (tj < ti).astype(BF16)
    local = _dot(strict_lower, sel.astype(BF16))
    carry = carry_ref[0:1, :]
    tile_cnt = jnp.sum(sel, axis=0, keepdims=True)
    lane_row = lax.broadcasted_iota(I32, (1, LANES), 1)
    seg_start = tile_cnt
    shift = 1
    while shift < LANES:
        seg_start = seg_start + jnp.where(lane_row >= shift, pltpu.roll(seg_start, shift, 1), 0.0)
        shift *= 2
    seg_start = seg_start - tile_cnt
    ranks = [jnp.sum(jnp.where(oh, local + carry, 0.0), axis=-1, keepdims=True) for oh in top_oh]
    slots = [jnp.sum(jnp.where(oh, local + seg_start, 0.0), axis=-1, keepdims=True)
             for oh in top_oh]
    new_carry = carry + tile_cnt
    carry_ref[...] = jnp.broadcast_to(new_carry, carry_ref.shape)
    cnt_ref[...] = jnp.broadcast_to(new_carry, cnt_ref.shape)
    seg_ref[0, 0:1, :] = carry
    seg_ref[0, 1:2, :] = tile_cnt
    seg_ref[0, 2:3, :] = seg_start
    seg_ref[0, 3:SUBLANES, :] = jnp.zeros((SUBLANES - 3, LANES), F32)

    meta = jnp.zeros((ts, LANES), F32)
    for kk in range(TOP_K):
        meta = jnp.where(lane == float(kk), top_i[kk], meta)
        meta = jnp.where(lane == float(TOP_K + kk), gates[kk], meta)
        meta = jnp.where(lane == float(2 * TOP_K + kk), ranks[kk], meta)
        meta = jnp.where(lane == float(3 * TOP_K + kk), slots[kk], meta)
    meta_ref[0] = meta
    route_ref[...] = meta.T[0:ROUTE_ROWS, :]


def _mixer_call(x, g1, win, convw, wgk, bgk, gng, wout, g2, wrhl, br):
    bsz, seq, d = x.shape
    ts = SEQ_TILE
    grid = (bsz, seq // ts)

    def const(shape):
        return pl.BlockSpec(shape, lambda b, s: (0,) * len(shape))

    tile = lambda w: pl.BlockSpec((1, ts, w), lambda b, s: (b, s, 0))
    return pl.pallas_call(
        _mixer_kernel,
        grid=grid,
        in_specs=[tile(d), const(g1.shape), const(win.shape), const(convw.shape),
                  const(wgk.shape), const(bgk.shape), const(gng.shape), const(wout.shape),
                  const(g2.shape), const(wrhl.shape), const(br.shape)],
        out_specs=[tile(d),
                   pl.BlockSpec((1, ts * ROW_TILES, LANES), lambda b, s: (b, s, 0)),
                   tile(LANES),
                   pl.BlockSpec((ROUTE_ROWS, ts), lambda b, s: (0, b * (seq // ts) + s)),
                   pl.BlockSpec((1, SUBLANES, LANES), lambda b, s: (b * (seq // ts) + s, 0, 0)),
                   const((SUBLANES, LANES))],
        out_shape=[jax.ShapeDtypeStruct((bsz, seq, d), F32),
                   jax.ShapeDtypeStruct((bsz, seq * ROW_TILES, LANES), F32),
                   jax.ShapeDtypeStruct((bsz, seq, LANES), F32),
                   jax.ShapeDtypeStruct((ROUTE_ROWS, bsz * seq), F32),
                   jax.ShapeDtypeStruct((bsz * seq // ts, SUBLANES, LANES), F32),
                   jax.ShapeDtypeStruct((SUBLANES, LANES), F32)],
        scratch_shapes=[pltpu.VMEM((ts, D_IN_PAD), F32),
                        pltpu.VMEM((ts + SUBLANES, CONV_WIDTH), F32),
                        pltpu.VMEM((ts, GLA_QK), F32),
                        pltpu.VMEM((ts, D_MODEL), BF16),
                        pltpu.VMEM((GLA_QK, GLA_WIDTH), F32),
                        pltpu.VMEM((SUBLANES, LANES), F32)],
        compiler_params=pltpu.CompilerParams(
            dimension_semantics=("arbitrary", "arbitrary"), vmem_limit_bytes=VMEM_LIMIT),
        name="mixer",
    )(x, g1, win, convw, wgk, bgk, gng, wout, g2, wrhl, br)


def _dispatch_kernel(pad_lo_ref, pad_n_ref, pos_ref, h2_ref, xs_hbm, zbuf, sem, zsem):
    td = h2_ref.shape[0] // ROW_TILES

    def zero_fill(wait):
        def fill(g, carry):
            n = pl.multiple_of(pad_n_ref[g], ROW_TILES)

            @pl.when(n > 0)
            def _():
                lo = pl.multiple_of(pad_lo_ref[g], ROW_TILES)
                cp = pltpu.make_async_copy(zbuf.at[pl.ds(0, n), :], xs_hbm.at[pl.ds(lo, n), :],
                                           zsem.at[0])
                if wait:
                    cp.wait()
                else:
                    cp.start()
            return carry

        lax.fori_loop(0, pad_lo_ref.shape[0], fill, 0)

    @pl.when(pl.program_id(0) == 0)
    def _():
        zbuf[...] = jnp.zeros_like(zbuf)
        zero_fill(wait=False)

    def body(tok, carry):
        src = h2_ref.at[pl.ds(pl.multiple_of(tok * ROW_TILES, ROW_TILES), ROW_TILES), :]
        for kk in range(TOP_K):
            row = pl.multiple_of(pos_ref[kk, tok], ROW_TILES)
            pltpu.make_async_copy(src, xs_hbm.at[pl.ds(row, ROW_TILES), :], sem.at[0]).start(
                priority=kk % 2)
        return carry

    lax.fori_loop(0, td, body, 0, unroll=8)
    for _ in range(TOP_K):
        pltpu.make_async_copy(h2_ref, xs_hbm.at[pl.ds(0, td * ROW_TILES), :], sem.at[0]).wait()

    @pl.when(pl.program_id(0) == 0)
    def _():
        zero_fill(wait=True)


def _dispatch_call(pad_lo, pad_n, pos_rows, h2, n_rows):
    t = h2.shape[0] // ROW_TILES
    td = DISPATCH_TILE
    grid_spec = pltpu.PrefetchScalarGridSpec(
        num_scalar_prefetch=2,
        grid=(t // td,),
        in_specs=[pl.BlockSpec((TOP_K, td), lambda i, lo, n: (0, i), memory_space=pltpu.SMEM),
                  pl.BlockSpec((td * ROW_TILES, LANES), lambda i, lo, n: (i, 0))],
        out_specs=pl.BlockSpec(memory_space=pl.ANY),
        scratch_shapes=[pltpu.VMEM((MOE_BLOCK * ROW_TILES, LANES), F32),
                        pltpu.SemaphoreType.DMA((1,)),
                        pltpu.SemaphoreType.DMA((1,))],
    )
    return pl.pallas_call(
        _dispatch_kernel,
        grid_spec=grid_spec,
        out_shape=jax.ShapeDtypeStruct((n_rows * ROW_TILES, LANES), F32),
        compiler_params=pltpu.CompilerParams(
            dimension_semantics=("arbitrary",), vmem_limit_bytes=VMEM_LIMIT,
            has_side_effects=True),
        name="dispatch",
    )(pad_lo, pad_n, pos_rows, h2)


def _experts_kernel(bexp_ref, first_ref, nxt_ref, nval_ref, nreal_ref,
                    xs_ref, wgu_hbm, bgu_ref, wd_hbm, bd_ref,
                    ys_ref,
                    xb_ref, act_ref, wgu_stage, wd_stage, wgu_bf, wd_bf, wsem):
    i = pl.program_id(0)
    nreal = nreal_ref[0]
    bm = xb_ref.shape[0]

    def weight_copies(e):
        return (pltpu.make_async_copy(wgu_hbm.at[e], wgu_stage, wsem.at[0]),
                pltpu.make_async_copy(wd_hbm.at[e], wd_stage, wsem.at[1]))

    @pl.when(i >= nreal)
    def _():
        ys_ref[...] = jnp.zeros_like(ys_ref)

    @pl.when(i < nreal)
    def _():
        e = bexp_ref[i]

        @pl.when(i == 0)
        def _():
            for cp in weight_copies(e):
                cp.start(priority=1)

        @pl.when(first_ref[i] == 1)
        def _():
            for cp in weight_copies(e):
                cp.wait()

            def cast_gu(r, carry):
                rows = pl.ds(pl.multiple_of(r * CAST_ROWS, CAST_ROWS), CAST_ROWS)
                wgu_bf[rows, :] = wgu_stage[rows, :].astype(BF16)
                return carry

            def cast_d(r, carry):
                rows = pl.ds(pl.multiple_of(r * CAST_ROWS, CAST_ROWS), CAST_ROWS)
                wd_bf[rows, :] = wd_stage[rows, :].astype(BF16)
                return carry

            lax.fori_loop(0, D_MODEL // CAST_ROWS, cast_gu, 0)
            lax.fori_loop(0, D_FF // CAST_ROWS, cast_d, 0)

            @pl.when(nxt_ref[i] >= 0)
            def _():
                for cp in weight_copies(nxt_ref[i]):
                    cp.start(priority=1)

        def mlp(rows):
            for j in range(ROW_TILES):
                xb_ref[0:rows, j * LANES:(j + 1) * LANES] = (
                    xs_ref[pl.ds(j, rows, stride=ROW_TILES), :].astype(BF16))
            for c in range(D_FF // FF_CHUNK):
                f0 = c * FF_CHUNK
                xb = xb_ref[0:rows, :]
                gate = _dot(xb, wgu_bf[:, f0:f0 + FF_CHUNK]) + bgu_ref[0, :, f0:f0 + FF_CHUNK]
                up = (_dot(xb, wgu_bf[:, D_FF + f0:D_FF + f0 + FF_CHUNK])
                      + bgu_ref[0, :, D_FF + f0:D_FF + f0 + FF_CHUNK])
                gate = jnp.minimum(gate, SWIGLU_LIMIT)
                up = jnp.clip(up, -SWIGLU_LIMIT, SWIGLU_LIMIT)
                glu = gate * jax.nn.sigmoid(gate * SWIGLU_ALPHA)
                act_ref[0:rows, f0:f0 + FF_CHUNK] = ((up + 1.0) * glu).astype(BF16)
            out = _dot(act_ref[0:rows, :], wd_bf[...]) + bd_ref[0]
            for j in range(ROW_TILES):
                ys_ref[pl.ds(j, rows, stride=ROW_TILES), :] = out[:, j * LANES:(j + 1) * LANES]

        @pl.when(nval_ref[i] > bm // 2)
        def _():
            mlp(bm)

        @pl.when(nval_ref[i] <= bm // 2)
        def _():
            mlp(bm // 2)
            ys_ref[bm // 2 * ROW_TILES:bm * ROW_TILES, :] = jnp.zeros(
                (bm // 2 * ROW_TILES, LANES), F32)


def _experts_call(bexp, first, nxt, nval, nreal, xs, wgu, bgu, wd, bd):
    bm = MOE_BLOCK
    n_blocks = xs.shape[0] // (bm * ROW_TILES)
    bgu3 = bgu.reshape(N_EXPERTS, 1, 2 * D_FF)
    bd3 = bd.reshape(N_EXPERTS, 1, D_MODEL)
    grid_spec = pltpu.PrefetchScalarGridSpec(
        num_scalar_prefetch=5,
        grid=(n_blocks,),
        in_specs=[
            pl.BlockSpec((bm * ROW_TILES, LANES),
                         lambda i, be, fi, nx, nv, nr: (jnp.minimum(i, nr[0] - 1), 0)),
            pl.BlockSpec(memory_space=pl.ANY),
            pl.BlockSpec((1, 1, 2 * D_FF), lambda i, be, fi, nx, nv, nr: (be[i], 0, 0)),
            pl.BlockSpec(memory_space=pl.ANY),
            pl.BlockSpec((1, 1, D_MODEL), lambda i, be, fi, nx, nv, nr: (be[i], 0, 0)),
        ],
        out_specs=pl.BlockSpec((bm * ROW_TILES, LANES), lambda i, be, fi, nx, nv, nr: (i, 0)),
        scratch_shapes=[pltpu.VMEM((bm, D_MODEL), BF16),
                        pltpu.VMEM((bm, D_FF), BF16),
                        pltpu.VMEM((D_MODEL, 2 * D_FF), F32),
                        pltpu.VMEM((D_FF, D_MODEL), F32),
                        pltpu.VMEM((D_MODEL, 2 * D_FF), BF16),
                        pltpu.VMEM((D_FF, D_MODEL), BF16),
                        pltpu.SemaphoreType.DMA((2,))],
    )
    return pl.pallas_call(
        _experts_kernel,
        grid_spec=grid_spec,
        out_shape=jax.ShapeDtypeStruct(xs.shape, F32),
        compiler_params=pltpu.CompilerParams(
            dimension_semantics=("arbitrary",), vmem_limit_bytes=VMEM_LIMIT),
        name="experts",
    )(bexp, first, nxt, nval, nreal, xs, wgu, bgu3, wd, bd3)


def _combine_kernel(src_ref, dst_ref, len_ref, x1_ref, meta_ref, g_ref, ys_hbm, out_ref,
                    ybuf, ysort_ref, sem):
    i = pl.program_id(0)
    n = pl.num_programs(0)
    tb = x1_ref.shape[0]
    n_rows = TOP_K * tb
    slot = i % 2

    def start_runs(tile, buf_slot):
        def body(e, carry):
            g = tile * N_EXPERTS + e
            ln = pl.multiple_of(len_ref[g], ROW_TILES)

            @pl.when(ln > 0)
            def _():
                src = pl.multiple_of(src_ref[g], ROW_TILES)
                dst = pl.multiple_of(dst_ref[g], ROW_TILES)
                pltpu.make_async_copy(ys_hbm.at[pl.ds(src, ln), :],
                                      ybuf.at[buf_slot, pl.ds(dst, ln), :], sem.at[buf_slot]).start()
            return carry
        lax.fori_loop(0, N_EXPERTS, body, 0)

    @pl.when(i == 0)
    def _():
        start_runs(0, 0)

    @pl.when(i + 1 < n)
    def _():
        start_runs(i + 1, 1 - slot)

    pltpu.make_async_copy(ys_hbm.at[pl.ds(0, n_rows * ROW_TILES), :], ybuf.at[slot],
                          sem.at[slot]).wait()

    for j in range(ROW_TILES):
        ysort_ref[:, j * LANES:(j + 1) * LANES] = (
            ybuf[slot, pl.ds(j, n_rows, stride=ROW_TILES), :].astype(BF16))

    meta = meta_ref[...]
    col = lax.broadcasted_iota(I32, (tb, n_rows), 1).astype(F32)
    weights = jnp.zeros((tb, n_rows), F32)
    for kk in range(TOP_K):
        weights = jnp.where(col == meta[:, 3 * TOP_K + kk:3 * TOP_K + kk + 1],
                            meta[:, TOP_K + kk:TOP_K + kk + 1], weights)
    acc = x1_ref[...] + _dot(weights.astype(BF16), ysort_ref[...])
    out_ref[...] = _rms(acc, g_ref[...])


def _combine_call(seg_src, seg_dst, seg_len, x1, meta, g, ys):
    t, d = x1.shape
    tb = SEQ_TILE
    nb = t // tb
    grid_spec = pltpu.PrefetchScalarGridSpec(
        num_scalar_prefetch=3,
        grid=(nb,),
        in_specs=[pl.BlockSpec((tb, d), lambda i, a, b, c: (i, 0)),
                  pl.BlockSpec((tb, LANES), lambda i, a, b, c: (i, 0)),
                  pl.BlockSpec((1, d), lambda i, a, b, c: (0, 0)),
                  pl.BlockSpec(memory_space=pl.ANY)],
        out_specs=pl.BlockSpec((tb, d), lambda i, a, b, c: (i, 0)),
        scratch_shapes=[pltpu.VMEM((2, TOP_K * tb * ROW_TILES, LANES), F32),
                        pltpu.VMEM((TOP_K * tb, D_MODEL), BF16),
                        pltpu.SemaphoreType.DMA((2,))],
    )
    return pl.pallas_call(
        _combine_kernel,
        grid_spec=grid_spec,
        out_shape=jax.ShapeDtypeStruct((t, d), F32),
        compiler_params=pltpu.CompilerParams(
            dimension_semantics=("arbitrary",), vmem_limit_bytes=VMEM_LIMIT),
        name="combine",
    )(seg_src, seg_dst, seg_len, x1, meta, g, ys)


def _routing_tables(route, counts, t):
    bm = MOE_BLOCK
    n_blocks = t * TOP_K // bm + N_EXPERTS
    e_idx = route[0:TOP_K].astype(I32)
    rank = route[2 * TOP_K:3 * TOP_K].astype(I32)
    eids = jnp.arange(N_EXPERTS, dtype=I32)
    nblk_e = (counts + bm - 1) // bm
    blk_end = jnp.sum(jnp.where(eids[None, :] <= eids[:, None], nblk_e[None, :], 0), axis=1)
    blk_start = blk_end - nblk_e
    nreal = blk_end[N_EXPERTS - 1]
    pad_start = blk_start * bm
    pos = rank
    for e in range(N_EXPERTS):
        pos = pos + jnp.where(e_idx == e, pad_start[e], 0)

    blk = jnp.arange(n_blocks, dtype=I32)
    bexp = jnp.minimum(jnp.sum((blk_end[None, :] <= blk[:, None]).astype(I32), axis=1),
                       N_EXPERTS - 1)
    blk_is_e = bexp[:, None] == eids[None, :]
    pick = lambda tab: jnp.sum(jnp.where(blk_is_e, tab[None, :], 0), axis=1)
    first = (blk == pick(blk_start)).astype(I32)
    nxt_e = jnp.sum((blk_end[None, :] <= blk_end[:, None]).astype(I32), axis=1)
    nxt_e = jnp.where(blk_end < nreal, jnp.minimum(nxt_e, N_EXPERTS - 1), -1)
    nxt = pick(nxt_e)
    nval = jnp.clip(pick(counts) - (blk - pick(blk_start)) * bm, 0, bm)
    tail_blk = jnp.arange(N_EXPERTS, dtype=I32) + nreal
    pad_lo = jnp.concatenate([pad_start + counts, jnp.minimum(tail_blk, n_blocks - 1) * bm])
    pad_n = jnp.concatenate([nblk_e * bm - counts, jnp.where(tail_blk < n_blocks, bm, 0)])
    return (pos, pad_start, bexp, first, nxt, nval, nreal.reshape(1).astype(I32), n_blocks,
            pad_lo * ROW_TILES, pad_n * ROW_TILES)


def kernel(x, norm_mix_g, w_in, conv_w, w_gk_up, b_gk_up, gla_norm_g, w_out, norm_ffn_g,
           w_router, b_router, w_gate_up, b_gate_up, w_down, b_down, norm_final_g):
    bsz, seq, d = x.shape
    t = bsz * seq
    assert w_in.shape[0] == 1, "single-layer trunk only"
    l = 0
    d_in = w_in.shape[-1]
    win = jnp.pad(w_in[l], ((0, 0), (0, D_IN_PAD - d_in))).astype(BF16)
    wgk = jnp.pad(w_gk_up[l], ((0, LANES - GLA_RANK), (0, 0))).astype(BF16)
    wr = jnp.pad(w_router[l], ((0, 0), (0, LANES - N_EXPERTS)))
    wrh = wr.astype(BF16)
    wrhl = jnp.concatenate([wrh, (wr - wrh.astype(F32)).astype(BF16)], axis=1)
    br = jnp.pad(b_router[l], (0, LANES - N_EXPERTS), constant_values=NEG_BIG).reshape(1, LANES)

    x1, h2, meta, route, seg, cnt = _mixer_call(
        x, norm_mix_g[l].reshape(1, d), win, conv_w[l], wgk, b_gk_up[l].reshape(1, GLA_QK),
        gla_norm_g[l].reshape(1, GLA_DV), w_out[l].astype(BF16), norm_ffn_g[l].reshape(1, d),
        wrhl, br)

    meta2 = meta.reshape(t, LANES)
    counts = cnt[0, :N_EXPERTS].astype(I32)
    (pos, pos_base, bexp, first, nxt, nval, nreal, n_blocks, pad_lo,
     pad_n) = _routing_tables(route, counts, t)
    pos_rows = pos * ROW_TILES
    xs = _dispatch_call(pad_lo, pad_n, pos_rows, h2.reshape(t * ROW_TILES, LANES),
                        n_blocks * MOE_BLOCK)
    ys = _experts_call(bexp, first, nxt, nval, nreal, xs, w_gate_up[l], b_gate_up[l], w_down[l],
                       b_down[l])
    pad_start = pos_base
    seg_src = ((pad_start[None, :] + seg[:, 0, :N_EXPERTS].astype(I32)) * ROW_TILES).reshape(-1)
    seg_len = (seg[:, 1, :N_EXPERTS].astype(I32) * ROW_TILES).reshape(-1)
    seg_dst = (seg[:, 2, :N_EXPERTS].astype(I32) * ROW_TILES).reshape(-1)
    out = _combine_call(seg_src, seg_dst, seg_len, x1.reshape(t, d), meta2,
                        norm_final_g.reshape(1, d), ys)
    return out.reshape(bsz, seq, d)
```

```python
import jax
import jax.numpy as jnp
from jax import lax
from jax.experimental import pallas as pl
from jax.experimental.pallas import tpu as pltpu

F32 = jnp.float32
BF16 = jnp.bfloat16
I32 = jnp.int32

D_MODEL = 1024
CONV_WIDTH = 512
CONV_K = 3
GLA_WIDTH = 512
GLA_HEADS = 4
GLA_DV = 128
GLA_DK = 64
GLA_QK = GLA_HEADS * GLA_DK
GLA_RANK = 16
GLA_NORMALIZER = 16.0
GLA_CHUNK = 64
N_EXPERTS = 32
TOP_K = 4
D_FF = 1024
SWIGLU_LIMIT = 7.0
SWIGLU_ALPHA = 1.702
RMS_EPS = 1e-5

LANES = 128
SUBLANES = 8
ROW_TILES = D_MODEL // LANES
ROUTE_ROWS = 16

OFF_UH = 0
OFF_GB = OFF_UH + CONV_WIDTH
OFF_GC = OFF_GB + CONV_WIDTH
OFF_Q = OFF_GC + CONV_WIDTH
OFF_K = OFF_Q + GLA_QK
OFF_V = OFF_K + GLA_QK
OFF_GO = OFF_V + GLA_WIDTH
OFF_GKL = OFF_GO + GLA_WIDTH
D_IN_PAD = OFF_GKL + LANES

SEQ_TILE = 512
MOE_BLOCK = 512
FF_CHUNK = 256
CAST_ROWS = 128
NEG_BIG = -1e30
VMEM_LIMIT = 56 * 1024 * 1024


def _rms(x, g):
    return x * lax.rsqrt(jnp.mean(x * x, axis=-1, keepdims=True) + RMS_EPS) * g


def _dot(a, b):
    return jnp.dot(a, b, preferred_element_type=F32)


def _dot_nt(a, b):
    return lax.dot_general(a, b, (((1,), (1,)), ((), ())), preferred_element_type=F32)


def _split_bf16(x):
    hi = x.astype(BF16)
    lo = (x - hi.astype(F32)).astype(BF16)
    return hi, lo


def _mixer_kernel(x_ref, g1_ref, win_ref, convw_ref, wgk_ref, bgk_ref, gng_ref, wout_ref,
                  g2_ref, wrhl_ref, br_ref,
                  x1_ref, h2_ref, meta_ref, route_ref, seg_ref, cnt_ref,
                  proj_ref, ubuf_ref, la_ref, ycat_ref, state_ref, carry_ref):
    ts = x_ref.shape[1]
    b_idx = pl.program_id(0)
    s_idx = pl.program_id(1)

    @pl.when(s_idx == 0)
    def _():
        state_ref[...] = jnp.zeros_like(state_ref)
        ubuf_ref[0:SUBLANES, :] = jnp.zeros((SUBLANES, CONV_WIDTH), F32)

    @pl.when((s_idx == 0) & (b_idx == 0))
    def _():
        carry_ref[...] = jnp.zeros_like(carry_ref)

    x = x_ref[0]
    h = _rms(x, g1_ref[...]).astype(BF16)
    proj_ref[...] = _dot(h, win_ref[...])

    u = proj_ref[:, OFF_GC:OFF_GC + CONV_WIDTH] * proj_ref[:, OFF_UH:OFF_UH + CONV_WIDTH]
    ubuf_ref[SUBLANES:SUBLANES + ts, :] = u
    u1 = ubuf_ref[pl.ds(SUBLANES - 1, ts), :]
    u2 = ubuf_ref[pl.ds(SUBLANES - 2, ts), :]
    conv = convw_ref[0:1, :] * u2 + convw_ref[1:2, :] * u1 + convw_ref[2:3, :] * u
    ycat_ref[:, 0:CONV_WIDTH] = (proj_ref[:, OFF_GB:OFF_GB + CONV_WIDTH] * conv).astype(BF16)
    ubuf_ref[0:SUBLANES, :] = ubuf_ref[ts:ts + SUBLANES, :]

    gk = _dot(proj_ref[:, OFF_GKL:OFF_GKL + LANES].astype(BF16), wgk_ref[...]) + bgk_ref[...]
    log_sig = jnp.minimum(gk, 0.0) - jnp.log1p(jnp.exp(-jnp.abs(gk)))
    la_ref[...] = log_sig / GLA_NORMALIZER

    ci = lax.broadcasted_iota(I32, (GLA_CHUNK, GLA_CHUNK), 0)
    cj = lax.broadcasted_iota(I32, (GLA_CHUNK, GLA_CHUNK), 1)
    tri_incl = (cj <= ci).astype(BF16)
    causal = cj <= ci
    causal4 = jnp.concatenate([causal] * GLA_HEADS, axis=0)
    lane_qk = lax.broadcasted_iota(I32, (1, GLA_QK), 1)
    head_masks = [((lane_qk >= hd * GLA_DK) & (lane_qk < (hd + 1) * GLA_DK)).astype(F32)
                  for hd in range(GLA_HEADS)]
    gng = gng_ref[...]

    def chunk_body(c, carry):
        r0 = pl.multiple_of(c * GLA_CHUNK, GLA_CHUNK)
        rows = pl.ds(r0, GLA_CHUNK)
        la_hi, la_lo = _split_bf16(la_ref[rows, :])
        bcum = _dot(tri_incl, la_hi) + _dot(tri_incl, la_lo)
        blast = bcum[GLA_CHUNK - 1:GLA_CHUNK, :]
        q = proj_ref[rows, OFF_Q:OFF_Q + GLA_QK] * (GLA_DK ** -0.5)
        k = proj_ref[rows, OFF_K:OFF_K + GLA_QK]
        v = proj_ref[rows, OFF_V:OFF_V + GLA_WIDTH].astype(BF16)
        qd = q * jnp.exp(bcum)
        kd = (k * jnp.exp(-bcum)).astype(BF16)
        kr = k * jnp.exp(blast - bcum)

        q_stack = jnp.concatenate([qd * m for m in head_masks], axis=0).astype(BF16)
        scores = jnp.where(causal4, _dot_nt(q_stack, kd), 0.0).astype(BF16)

        state = state_ref[...]
        o_inter = _dot(qd.astype(BF16), state.astype(BF16))
        o_intra = jnp.concatenate(
            [_dot(scores[hd * GLA_CHUNK:(hd + 1) * GLA_CHUNK, :],
                  v[:, hd * GLA_DV:(hd + 1) * GLA_DV]) for hd in range(GLA_HEADS)], axis=1)
        o = o_inter + o_intra

        kt = jnp.concatenate([kr, jnp.broadcast_to(blast, (GLA_CHUNK, GLA_QK))], axis=0).T
        dcol = jnp.exp(kt[:, GLA_CHUNK:GLA_CHUNK + 1])
        lane_c = lax.broadcasted_iota(I32, (GLA_QK, 2 * GLA_CHUNK), 1)
        kt_b = jnp.where(lane_c < GLA_CHUNK, kt, 0.0).astype(BF16)
        v_pad = jnp.concatenate([v, jnp.zeros_like(v)], axis=0)
        for hd in range(GLA_HEADS):
            rs = slice(hd * GLA_DK, (hd + 1) * GLA_DK)
            cs = slice(hd * GLA_DV, (hd + 1) * GLA_DV)
            kv = _dot(kt_b[rs, :], v_pad[:, cs])
            state_ref[rs, cs] = dcol[rs, :] * state[rs, cs] + kv

        g_out = proj_ref[rows, OFF_GO:OFF_GO + GLA_WIDTH]
        o_n = jnp.concatenate(
            [_rms(o[:, hd * GLA_DV:(hd + 1) * GLA_DV], gng) for hd in range(GLA_HEADS)], axis=1)
        y = o_n * (g_out * jax.nn.sigmoid(g_out))
        ycat_ref[rows, CONV_WIDTH:CONV_WIDTH + GLA_WIDTH] = y.astype(BF16)
        return carry

    lax.fori_loop(0, ts // GLA_CHUNK, chunk_body, 0, unroll=True)

    x1 = x + _dot(ycat_ref[...], wout_ref[...])
    x1_ref[0] = x1
    h2 = _rms(x1, g2_ref[...])
    h2_hi, h2_lo = _split_bf16(h2)
    h2_ref[0] = h2_hi
    hi_terms = _dot(h2_hi, wrhl_ref[...])
    logits = (hi_terms[:, 0:LANES] + hi_terms[:, LANES:2 * LANES]
              + _dot(h2_lo, wrhl_ref[:, 0:LANES]) + br_ref[...])

    lane = lax.broadcasted_iota(I32, (ts, LANES), 1).astype(F32)
    work = logits
    sel = jnp.zeros((ts, LANES), F32)
    top_v, top_i, top_oh = [], [], []
    for _ in range(TOP_K):
        m = jnp.max(work, axis=-1, keepdims=True)
        idx = jnp.min(jnp.where(work == m, lane, float(LANES)), axis=-1, keepdims=True)
        oh = lane == idx
        top_v.append(m)
        top_i.append(idx)
        top_oh.append(oh)
        sel = sel + oh.astype(F32)
        work = jnp.where(oh, -jnp.inf, work)
    exps = [jnp.exp(tv - top_v[0]) for tv in top_v]
    denom = exps[0] + exps[1] + exps[2] + exps[3]
    gates = [e / denom for e in exps]

    ti = lax.broadcasted_iota(I32, (ts, ts), 0)
    tj = lax.broadcasted_iota(I32, (ts, ts), 1)
    strict_lower = (tj < ti).astype(BF16)
    local = _dot(strict_lower, sel.astype(BF16))
    carry = carry_ref[0:1, :]
    tile_cnt = jnp.sum(sel, axis=0, keepdims=True)
    lane_row = lax.broadcasted_iota(I32, (1, LANES), 1)
    seg_start = tile_cnt
    shift = 1
    while shift < LANES:
        seg_start = seg_start + jnp.where(lane_row >= shift, pltpu.roll(seg_start, shift, 1), 0.0)
        shift *= 2
    seg_start = seg_start - tile_cnt
    ranks = [jnp.sum(jnp.where(oh, local + carry, 0.0), axis=-1, keepdims=True) for oh in top_oh]
    slots = [jnp.sum(jnp.where(oh, local + seg_start, 0.0), axis=-1, keepdims=True)
             for oh in top_oh]
    new_carry = carry + tile_cnt
    carry_ref[...] = jnp.broadcast_to(new_carry, carry_ref.shape)
    cnt_ref[...] = jnp.broadcast_to(new_carry, cnt_ref.shape)
    seg_ref[0, 0:1, :] = carry
    seg_ref[0, 1:2, :] = tile_cnt
    seg_ref[0, 2:3, :] = seg_start
    seg_ref[0, 3:SUBLANES, :] = jnp.zeros((SUBLANES - 3, LANES), F32)

    meta = jnp.zeros((ts, LANES), F32)
    for kk in range(TOP_K):
        meta = jnp.where(lane == float(kk), top_i[kk], meta)
        meta = jnp.where(lane == float(TOP_K + kk), gates[kk], meta)
        meta = jnp.where(lane == float(2 * TOP_K + kk), ranks[kk], meta)
        meta = jnp.where(lane == float(3 * TOP_K + kk), slots[kk], meta)
    meta_ref[0] = meta
    route_ref[...] = meta.T[0:ROUTE_ROWS, :]


def _mixer_call(x, g1, win, convw, wgk, bgk, gng, wout, g2, wrhl, br):
    bsz, seq, d = x.shape
    ts = SEQ_TILE
    grid = (bsz, seq // ts)

    def const(shape):
        return pl.BlockSpec(shape, lambda b, s: (0,) * len(shape))

    tile = lambda w: pl.BlockSpec((1, ts, w), lambda b, s: (b, s, 0))
    return pl.pallas_call(
        _mixer_kernel,
        grid=grid,
        in_specs=[tile(d), const(g1.shape), const(win.shape), const(convw.shape),
                  const(wgk.shape), const(bgk.shape), const(gng.shape), const(wout.shape),
                  const(g2.shape), const(wrhl.shape), const(br.shape)],
        out_specs=[tile(d),
                   tile(d),
                   tile(LANES),
                   pl.BlockSpec((ROUTE_ROWS, ts), lambda b, s: (0, b * (seq // ts) + s)),
                   pl.BlockSpec((1, SUBLANES, LANES), lambda b, s: (b * (seq // ts) + s, 0, 0)),
                   const((SUBLANES, LANES))],
        out_shape=[jax.ShapeDtypeStruct((bsz, seq, d), F32),
                   jax.ShapeDtypeStruct((bsz, seq, d), BF16),
                   jax.ShapeDtypeStruct((bsz, seq, LANES), F32),
                   jax.ShapeDtypeStruct((ROUTE_ROWS, bsz * seq), F32),
                   jax.ShapeDtypeStruct((bsz * seq // ts, SUBLANES, LANES), F32),
                   jax.ShapeDtypeStruct((SUBLANES, LANES), F32)],
        scratch_shapes=[pltpu.VMEM((ts, D_IN_PAD), F32),
                        pltpu.VMEM((ts + SUBLANES, CONV_WIDTH), F32),
                        pltpu.VMEM((ts, GLA_QK), F32),
                        pltpu.VMEM((ts, D_MODEL), BF16),
                        pltpu.VMEM((GLA_QK, GLA_WIDTH), F32),
                        pltpu.VMEM((SUBLANES, LANES), F32)],
        compiler_params=pltpu.CompilerParams(
            dimension_semantics=("arbitrary", "arbitrary"), vmem_limit_bytes=VMEM_LIMIT),
        name="mixer",
    )(x, g1, win, convw, wgk, bgk, gng, wout, g2, wrhl, br)


def _dispatch_kernel(run_xs_ref, run_buf_ref, run_len_ref, pad_lo_ref, pad_n_ref,
                     route_ref, h2_ref, xs_hbm, sbuf, zbuf, sem, zsem):
    i = pl.program_id(0)
    n = pl.num_programs(0)
    td = h2_ref.shape[0]
    n_rows = TOP_K * td
    slot = i % 2

    def zero_fill(wait):
        def fill(g, carry):
            ln = pl.multiple_of(pad_n_ref[g], ROW_TILES)

            @pl.when(ln > 0)
            def _():
                lo = pl.multiple_of(pad_lo_ref[g], ROW_TILES)
                cp = pltpu.make_async_copy(zbuf.at[pl.ds(0, ln), :], xs_hbm.at[pl.ds(lo, ln), :],
                                           zsem.at[0])
                if wait:
                    cp.wait()
                else:
                    cp.start()
            return carry

        lax.fori_loop(0, pad_lo_ref.shape[0], fill, 0)

    def wait_runs(buf_slot):
        pltpu.make_async_copy(sbuf.at[buf_slot], xs_hbm.at[pl.ds(0, n_rows * ROW_TILES), :],
                              sem.at[buf_slot]).wait()

    @pl.when(i == 0)
    def _():
        zbuf[...] = jnp.zeros_like(zbuf)
        zero_fill(wait=False)

    @pl.when(i >= 2)
    def _():
        wait_runs(slot)

    route = route_ref[...]
    row = lax.broadcasted_iota(I32, (n_rows, td), 0).astype(F32)
    pick = jnp.zeros((n_rows, td), F32)
    for kk in range(TOP_K):
        pick = jnp.where(row == route[3 * TOP_K + kk:3 * TOP_K + kk + 1, :], 1.0, pick)
    rows_sorted = _dot(pick.astype(BF16), h2_ref[...])
    for j in range(ROW_TILES):
        sbuf[slot, pl.ds(j, n_rows, stride=ROW_TILES), :] = rows_sorted[:, j * LANES:(j + 1) * LANES]

    def body(e, carry):
        g = i * N_EXPERTS + e
        ln = pl.multiple_of(run_len_ref[g], ROW_TILES)

        @pl.when(ln > 0)
        def _():
            src = pl.multiple_of(run_buf_ref[g], ROW_TILES)
            dst = pl.multiple_of(run_xs_ref[g], ROW_TILES)
            pltpu.make_async_copy(sbuf.at[slot, pl.ds(src, ln), :], xs_hbm.at[pl.ds(dst, ln), :],
                                  sem.at[slot]).start()
        return carry

    lax.fori_loop(0, N_EXPERTS, body, 0)

    @pl.when(i == n - 1)
    def _():
        @pl.when(i >= 1)
        def _():
            wait_runs(1 - slot)
        wait_runs(slot)
        zero_fill(wait=True)


def _dispatch_call(run_xs, run_buf, run_len, pad_lo, pad_n, route, h2, n_rows):
    t, d = h2.shape
    td = SEQ_TILE
    grid_spec = pltpu.PrefetchScalarGridSpec(
        num_scalar_prefetch=5,
        grid=(t // td,),
        in_specs=[pl.BlockSpec((ROUTE_ROWS, td), lambda i, *_: (0, i)),
                  pl.BlockSpec((td, d), lambda i, *_: (i, 0))],
        out_specs=pl.BlockSpec(memory_space=pl.ANY),
        scratch_shapes=[pltpu.VMEM((2, TOP_K * td * ROW_TILES, LANES), F32),
                        pltpu.VMEM((MOE_BLOCK * ROW_TILES, LANES), F32),
                        pltpu.SemaphoreType.DMA((2,)),
                        pltpu.SemaphoreType.DMA((1,))],
    )
    return pl.pallas_call(
        _dispatch_kernel,
        grid_spec=grid_spec,
        out_shape=jax.ShapeDtypeStruct((n_rows * ROW_TILES, LANES), F32),
        compiler_params=pltpu.CompilerParams(
            dimension_semantics=("arbitrary",), vmem_limit_bytes=VMEM_LIMIT,
            has_side_effects=True),
        name="dispatch",
    )(run_xs, run_buf, run_len, pad_lo, pad_n, route, h2)


def _experts_kernel(bexp_ref, first_ref, nxt_ref, nval_ref, nreal_ref,
                    xs_ref, wgu_hbm, bgu_ref, wd_hbm, bd_ref,
                    ys_ref,
                    xb_ref, act_ref, wgu_stage, wd_stage, wgu_bf, wd_bf, wsem):
    i = pl.program_id(0)
    nreal = nreal_ref[0]
    bm = xb_ref.shape[0]

    def weight_copies(e):
        return (pltpu.make_async_copy(wgu_hbm.at[e], wgu_stage, wsem.at[0]),
                pltpu.make_async_copy(wd_hbm.at[e], wd_stage, wsem.at[1]))

    @pl.when(i >= nreal)
    def _():
        ys_ref[...] = jnp.zeros_like(ys_ref)

    @pl.when(i < nreal)
    def _():
        e = bexp_ref[i]

        @pl.when(i == 0)
        def _():
            for cp in weight_copies(e):
                cp.start(priority=1)

        @pl.when(first_ref[i] == 1)
        def _():
            for cp in weight_copies(e):
                cp.wait()

            def cast_gu(r, carry):
                rows = pl.ds(pl.multiple_of(r * CAST_ROWS, CAST_ROWS), CAST_ROWS)
                wgu_bf[rows, :] = wgu_stage[rows, :].astype(BF16)
                return carry

            def cast_d(r, carry):
                rows = pl.ds(pl.multiple_of(r * CAST_ROWS, CAST_ROWS), CAST_ROWS)
                wd_bf[rows, :] = wd_stage[rows, :].astype(BF16)
                return carry

            lax.fori_loop(0, D_MODEL // CAST_ROWS, cast_gu, 0)
            lax.fori_loop(0, D_FF // CAST_ROWS, cast_d, 0)

            @pl.when(nxt_ref[i] >= 0)
            def _():
                for cp in weight_copies(nxt_ref[i]):
                    cp.start(priority=1)

        def mlp(rows):
            for j in range(ROW_TILES):
                xb_ref[0:rows, j * LANES:(j + 1) * LANES] = (
                    xs_ref[pl.ds(j, rows, stride=ROW_TILES), :].astype(BF16))
            for c in range(D_FF // FF_CHUNK):
                f0 = c * FF_CHUNK
                xb = xb_ref[0:rows, :]
                gate = _dot(xb, wgu_bf[:, f0:f0 + FF_CHUNK]) + bgu_ref[0, :, f0:f0 + FF_CHUNK]
                up = (_dot(xb, wgu_bf[:, D_FF + f0:D_FF + f0 + FF_CHUNK])
                      + bgu_ref[0, :, D_FF + f0:D_FF + f0 + FF_CHUNK])
                gate = jnp.minimum(gate, SWIGLU_LIMIT)
                up = jnp.clip(up, -SWIGLU_LIMIT, SWIGLU_LIMIT)
                glu = gate * jax.nn.sigmoid(gate * SWIGLU_ALPHA)
                act_ref[0:rows, f0:f0 + FF_CHUNK] = ((up + 1.0) * glu).astype(BF16)
            out = _dot(act_ref[0:rows, :], wd_bf[...]) + bd_ref[0]
            for j in range(ROW_TILES):
                ys_ref[pl.ds(j, rows, stride=ROW_TILES), :] = out[:, j * LANES:(j + 1) * LANES]

        @pl.when(nval_ref[i] > bm // 2)
        def _():
            mlp(bm)

        @pl.when(nval_ref[i] <= bm // 2)
        def _():
            mlp(bm // 2)
            ys_ref[bm // 2 * ROW_TILES:bm * ROW_TILES, :] = jnp.zeros(
                (bm // 2 * ROW_TILES, LANES), F32)


def _experts_call(bexp, first, nxt, nval, nreal, xs, wgu, bgu, wd, bd):
    bm = MOE_BLOCK
    n_blocks = xs.shape[0] // (bm * ROW_TILES)
    bgu3 = bgu.reshape(N_EXPERTS, 1, 2 * D_FF)
    bd3 = bd.reshape(N_EXPERTS, 1, D_MODEL)
    grid_spec = pltpu.PrefetchScalarGridSpec(
        num_scalar_prefetch=5,
        grid=(n_blocks,),
        in_specs=[
            pl.BlockSpec((bm * ROW_TILES, LANES),
                         lambda i, be, fi, nx, nv, nr: (jnp.minimum(i, nr[0] - 1), 0)),
            pl.BlockSpec(memory_space=pl.ANY),
            pl.BlockSpec((1, 1, 2 * D_FF), lambda i, be, fi, nx, nv, nr: (be[i], 0, 0)),
            pl.BlockSpec(memory_space=pl.ANY),
            pl.BlockSpec((1, 1, D_MODEL), lambda i, be, fi, nx, nv, nr: (be[i], 0, 0)),
        ],
        out_specs=pl.BlockSpec((bm * ROW_TILES, LANES), lambda i, be, fi, nx, nv, nr: (i, 0)),
        scratch_shapes=[pltpu.VMEM((bm, D_MODEL), BF16),
                        pltpu.VMEM((bm, D_FF), BF16),
                        pltpu.VMEM((D_MODEL, 2 * D_FF), F32),
                        pltpu.VMEM((D_FF, D_MODEL), F32),
                        pltpu.VMEM((D_MODEL, 2 * D_FF), BF16),
                        pltpu.VMEM((D_FF, D_MODEL), BF16),
                        pltpu.SemaphoreType.DMA((2,))],
    )
    return pl.pallas_call(
        _experts_kernel,
        grid_spec=grid_spec,
        out_shape=jax.ShapeDtypeStruct(xs.shape, F32),
        compiler_params=pltpu.CompilerParams(
            dimension_semantics=("arbitrary",), vmem_limit_bytes=VMEM_LIMIT),
        name="experts",
    )(bexp, first, nxt, nval, nreal, xs, wgu, bgu3, wd, bd3)


def _combine_kernel(src_ref, dst_ref, len_ref, x1_ref, meta_ref, g_ref, ys_hbm, out_ref,
                    ybuf, ysort_ref, sem):
    i = pl.program_id(0)
    n = pl.num_programs(0)
    tb = x1_ref.shape[0]
    n_rows = TOP_K * tb
    slot = i % 2

    def start_runs(tile, buf_slot):
        def body(e, carry):
            g = tile * N_EXPERTS + e
            ln = pl.multiple_of(len_ref[g], ROW_TILES)

            @pl.when(ln > 0)
            def _():
                src = pl.multiple_of(src_ref[g], ROW_TILES)
                dst = pl.multiple_of(dst_ref[g], ROW_TILES)
                pltpu.make_async_copy(ys_hbm.at[pl.ds(src, ln), :],
                                      ybuf.at[buf_slot, pl.ds(dst, ln), :], sem.at[buf_slot]).start()
            return carry
        lax.fori_loop(0, N_EXPERTS, body, 0)

    @pl.when(i == 0)
    def _():
        start_runs(0, 0)

    @pl.when(i + 1 < n)
    def _():
        start_runs(i + 1, 1 - slot)

    pltpu.make_async_copy(ys_hbm.at[pl.ds(0, n_rows * ROW_TILES), :], ybuf.at[slot],
                          sem.at[slot]).wait()

    for j in range(ROW_TILES):
        ysort_ref[:, j * LANES:(j + 1) * LANES] = (
            ybuf[slot, pl.ds(j, n_rows, stride=ROW_TILES), :].astype(BF16))

    meta = meta_ref[...]
    col = lax.broadcasted_iota(I32, (tb, n_rows), 1).astype(F32)
    weights = jnp.zeros((tb, n_rows), F32)
    for kk in range(TOP_K):
        weights = jnp.where(col == meta[:, 3 * TOP_K + kk:3 * TOP_K + kk + 1],
                            meta[:, TOP_K + kk:TOP_K + kk + 1], weights)
    acc = x1_ref[...] + _dot(weights.astype(BF16), ysort_ref[...])
    out_ref[...] = _rms(acc, g_ref[...])


def _combine_call(seg_src, seg_dst, seg_len, x1, meta, g, ys):
    t, d = x1.shape
    tb = SEQ_TILE
    nb = t // tb
    grid_spec = pltpu.PrefetchScalarGridSpec(
        num_scalar_prefetch=3,
        grid=(nb,),
        in_specs=[pl.BlockSpec((tb, d), lambda i, a, b, c: (i, 0)),
                  pl.BlockSpec((tb, LANES), lambda i, a, b, c: (i, 0)),
                  pl.BlockSpec((1, d), lambda i, a, b, c: (0, 0)),
                  pl.BlockSpec(memory_space=pl.ANY)],
        out_specs=pl.BlockSpec((tb, d), lambda i, a, b, c: (i, 0)),
        scratch_shapes=[pltpu.VMEM((2, TOP_K * tb * ROW_TILES, LANES), F32),
                        pltpu.VMEM((TOP_K * tb, D_MODEL), BF16),
                        pltpu.SemaphoreType.DMA((2,))],
    )
    return pl.pallas_call(
        _combine_kernel,
        grid_spec=grid_spec,
        out_shape=jax.ShapeDtypeStruct((t, d), F32),
        compiler_params=pltpu.CompilerParams(
            dimension_semantics=("arbitrary",), vmem_limit_bytes=VMEM_LIMIT),
        name="combine",
    )(seg_src, seg_dst, seg_len, x1, meta, g, ys)


def _routing_tables(counts, t):
    bm = MOE_BLOCK
    n_blocks = t * TOP_K // bm + N_EXPERTS
    eids = jnp.arange(N_EXPERTS, dtype=I32)
    nblk_e = (counts + bm - 1) // bm
    blk_end = jnp.sum(jnp.where(eids[None, :] <= eids[:, None], nblk_e[None, :], 0), axis=1)
    blk_start = blk_end - nblk_e
    nreal = blk_end[N_EXPERTS - 1]
    pad_start = blk_start * bm
    blk = jnp.arange(n_blocks, dtype=I32)
    bexp = jnp.minimum(jnp.sum((blk_end[None, :] <= blk[:, None]).astype(I32), axis=1),
                       N_EXPERTS - 1)
    blk_is_e = bexp[:, None] == eids[None, :]
    pick = lambda tab: jnp.sum(jnp.where(blk_is_e, tab[None, :], 0), axis=1)
    first = (blk == pick(blk_start)).astype(I32)
    nxt_e = jnp.sum((blk_end[None, :] <= blk_end[:, None]).astype(I32), axis=1)
    nxt_e = jnp.where(blk_end < nreal, jnp.minimum(nxt_e, N_EXPERTS - 1), -1)
    nxt = pick(nxt_e)
    nval = jnp.clip(pick(counts) - (blk - pick(blk_start)) * bm, 0, bm)
    tail_blk = jnp.arange(N_EXPERTS, dtype=I32) + nreal
    pad_lo = jnp.concatenate([pad_start + counts, jnp.minimum(tail_blk, n_blocks - 1) * bm])
    pad_n = jnp.concatenate([nblk_e * bm - counts, jnp.where(tail_blk < n_blocks, bm, 0)])
    return (pad_start, bexp, first, nxt, nval, nreal.reshape(1).astype(I32), n_blocks,
            pad_lo * ROW_TILES, pad_n * ROW_TILES)


def kernel(x, norm_mix_g, w_in, conv_w, w_gk_up, b_gk_up, gla_norm_g, w_out, norm_ffn_g,
           w_router, b_router, w_gate_up, b_gate_up, w_down, b_down, norm_final_g):
    bsz, seq, d = x.shape
    t = bsz * seq
    assert w_in.shape[0] == 1, "single-layer trunk only"
    l = 0
    d_in = w_in.shape[-1]
    win = jnp.pad(w_in[l], ((0, 0), (0, D_IN_PAD - d_in))).astype(BF16)
    wgk = jnp.pad(w_gk_up[l], ((0, LANES - GLA_RANK), (0, 0))).astype(BF16)
    wr = jnp.pad(w_router[l], ((0, 0), (0, LANES - N_EXPERTS)))
    wrh = wr.astype(BF16)
    wrhl = jnp.concatenate([wrh, (wr - wrh.astype(F32)).astype(BF16)], axis=1)
    br = jnp.pad(b_router[l], (0, LANES - N_EXPERTS), constant_values=NEG_BIG).reshape(1, LANES)

    x1, h2, meta, route, seg, cnt = _mixer_call(
        x, norm_mix_g[l].reshape(1, d), win, conv_w[l], wgk, b_gk_up[l].reshape(1, GLA_QK),
        gla_norm_g[l].reshape(1, GLA_DV), w_out[l].astype(BF16), norm_ffn_g[l].reshape(1, d),
        wrhl, br)

    meta2 = meta.reshape(t, LANES)
    counts = cnt[0, :N_EXPERTS].astype(I32)
    (pad_start, bexp, first, nxt, nval, nreal, n_blocks, pad_lo,
     pad_n) = _routing_tables(counts, t)
    seg_src = ((pad_start[None, :] + seg[:, 0, :N_EXPERTS].astype(I32)) * ROW_TILES).reshape(-1)
    seg_len = (seg[:, 1, :N_EXPERTS].astype(I32) * ROW_TILES).reshape(-1)
    seg_dst = (seg[:, 2, :N_EXPERTS].astype(I32) * ROW_TILES).reshape(-1)
    xs = _dispatch_call(seg_src, seg_dst, seg_len, pad_lo, pad_n, route, h2.reshape(t, d),
                        n_blocks * MOE_BLOCK)
    ys = _experts_call(bexp, first, nxt, nval, nreal, xs, w_gate_up[l], b_gate_up[l], w_down[l],
                       b_down[l])
    out = _combine_call(seg_src, seg_dst, seg_len, x1.reshape(t, d), meta2,
                        norm_final_g.reshape(1, d), ys)
    return out.reshape(bsz, seq, d)
```

```python
import jax
import jax.numpy as jnp
from jax import lax
from jax.experimental import pallas as pl
from jax.experimental.pallas import tpu as pltpu

F32 = jnp.float32
BF16 = jnp.bfloat16
I32 = jnp.int32

D_MODEL = 1024
CONV_WIDTH = 512
CONV_K = 3
GLA_WIDTH = 512
GLA_HEADS = 4
GLA_DV = 128
GLA_DK = 64
GLA_QK = GLA_HEADS * GLA_DK
GLA_RANK = 16
GLA_NORMALIZER = 16.0
GLA_CHUNK = 64
N_EXPERTS = 32
TOP_K = 4
D_FF = 1024
SWIGLU_LIMIT = 7.0
SWIGLU_ALPHA = 1.702
RMS_EPS = 1e-5

LANES = 128
SUBLANES = 8
ROW_TILES = D_MODEL // LANES
ROUTE_ROWS = 16

OFF_UH = 0
OFF_GB = OFF_UH + CONV_WIDTH
OFF_GC = OFF_GB + CONV_WIDTH
OFF_Q = OFF_GC + CONV_WIDTH
OFF_K = OFF_Q + GLA_QK
OFF_V = OFF_K + GLA_QK
OFF_GO = OFF_V + GLA_WIDTH
OFF_GKL = OFF_GO + GLA_WIDTH
D_IN_PAD = OFF_GKL + LANES

SEQ_TILE = 512
SORT_TILE = 256
MOE_BLOCK = 512
FF_CHUNK = 256
CAST_ROWS = 128
NEG_BIG = -1e30
VMEM_LIMIT = 56 * 1024 * 1024


def _rms(x, g):
    return x * lax.rsqrt(jnp.mean(x * x, axis=-1, keepdims=True) + RMS_EPS) * g


def _dot(a, b):
    return jnp.dot(a, b, preferred_element_type=F32)


def _dot_nt(a, b):
    return lax.dot_general(a, b, (((1,), (1,)), ((), ())), preferred_element_type=F32)


def _split_bf16(x):
    hi = x.astype(BF16)
    lo = (x - hi.astype(F32)).astype(BF16)
    return hi, lo


def _mixer_kernel(x_ref, g1_ref, win_ref, convw_ref, wgk_ref, bgk_ref, gng_ref, wout_ref,
                  g2_ref, wrhl_ref, br_ref,
                  x1_ref, h2_ref, meta_ref, route_ref, seg_ref, cnt_ref,
                  proj_ref, ubuf_ref, la_ref, ycat_ref, state_ref, carry_ref):
    ts = x_ref.shape[1]
    b_idx = pl.program_id(0)
    s_idx = pl.program_id(1)

    @pl.when(s_idx == 0)
    def _():
        state_ref[...] = jnp.zeros_like(state_ref)
        ubuf_ref[0:SUBLANES, :] = jnp.zeros((SUBLANES, CONV_WIDTH), F32)

    @pl.when((s_idx == 0) & (b_idx == 0))
    def _():
        carry_ref[...] = jnp.zeros_like(carry_ref)

    x = x_ref[0]
    h = _rms(x, g1_ref[...]).astype(BF16)
    proj_ref[...] = _dot(h, win_ref[...])

    u = proj_ref[:, OFF_GC:OFF_GC + CONV_WIDTH] * proj_ref[:, OFF_UH:OFF_UH + CONV_WIDTH]
    ubuf_ref[SUBLANES:SUBLANES + ts, :] = u
    u1 = ubuf_ref[pl.ds(SUBLANES - 1, ts), :]
    u2 = ubuf_ref[pl.ds(SUBLANES - 2, ts), :]
    conv = convw_ref[0:1, :] * u2 + convw_ref[1:2, :] * u1 + convw_ref[2:3, :] * u
    ycat_ref[:, 0:CONV_WIDTH] = (proj_ref[:, OFF_GB:OFF_GB + CONV_WIDTH] * conv).astype(BF16)
    ubuf_ref[0:SUBLANES, :] = ubuf_ref[ts:ts + SUBLANES, :]

    gk = _dot(proj_ref[:, OFF_GKL:OFF_GKL + LANES].astype(BF16), wgk_ref[...]) + bgk_ref[...]
    log_sig = jnp.minimum(gk, 0.0) - jnp.log1p(jnp.exp(-jnp.abs(gk)))
    la_ref[...] = log_sig / GLA_NORMALIZER

    ci = lax.broadcasted_iota(I32, (GLA_CHUNK, GLA_CHUNK), 0)
    cj = lax.broadcasted_iota(I32, (GLA_CHUNK, GLA_CHUNK), 1)
    tri_incl = (cj <= ci).astype(BF16)
    causal = cj <= ci
    causal4 = jnp.concatenate([causal] * GLA_HEADS, axis=0)
    lane_qk = lax.broadcasted_iota(I32, (1, GLA_QK), 1)
    head_masks = [((lane_qk >= hd * GLA_DK) & (lane_qk < (hd + 1) * GLA_DK)).astype(F32)
                  for hd in range(GLA_HEADS)]
    gng = gng_ref[...]

    def chunk_body(c, carry):
        r0 = pl.multiple_of(c * GLA_CHUNK, GLA_CHUNK)
        rows = pl.ds(r0, GLA_CHUNK)
        la_hi, la_lo = _split_bf16(la_ref[rows, :])
        bcum = _dot(tri_incl, la_hi) + _dot(tri_incl, la_lo)
        blast = bcum[GLA_CHUNK - 1:GLA_CHUNK, :]
        q = proj_ref[rows, OFF_Q:OFF_Q + GLA_QK] * (GLA_DK ** -0.5)
        k = proj_ref[rows, OFF_K:OFF_K + GLA_QK]
        v = proj_ref[rows, OFF_V:OFF_V + GLA_WIDTH].astype(BF16)
        qd = q * jnp.exp(bcum)
        kd = (k * jnp.exp(-bcum)).astype(BF16)
        kr = k * jnp.exp(blast - bcum)

        q_stack = jnp.concatenate([qd * m for m in head_masks], axis=0).astype(BF16)
        scores = jnp.where(causal4, _dot_nt(q_stack, kd), 0.0).astype(BF16)

        state = state_ref[...]
        o_inter = _dot(qd.astype(BF16), state.astype(BF16))
        o_intra = jnp.concatenate(
            [_dot(scores[hd * GLA_CHUNK:(hd + 1) * GLA_CHUNK, :],
                  v[:, hd * GLA_DV:(hd + 1) * GLA_DV]) for hd in range(GLA_HEADS)], axis=1)
        o = o_inter + o_intra

        kt = jnp.concatenate([kr, jnp.broadcast_to(blast, (GLA_CHUNK, GLA_QK))], axis=0).T
        dcol = jnp.exp(kt[:, GLA_CHUNK:GLA_CHUNK + 1])
        lane_c = lax.broadcasted_iota(I32, (GLA_QK, 2 * GLA_CHUNK), 1)
        kt_b = jnp.where(lane_c < GLA_CHUNK, kt, 0.0).astype(BF16)
        v_pad = jnp.concatenate([v, jnp.zeros_like(v)], axis=0)
        for hd in range(GLA_HEADS):
            rs = slice(hd * GLA_DK, (hd + 1) * GLA_DK)
            cs = slice(hd * GLA_DV, (hd + 1) * GLA_DV)
            kv = _dot(kt_b[rs, :], v_pad[:, cs])
            state_ref[rs, cs] = dcol[rs, :] * state[rs, cs] + kv

        g_out = proj_ref[rows, OFF_GO:OFF_GO + GLA_WIDTH]
        o_n = jnp.concatenate(
            [_rms(o[:, hd * GLA_DV:(hd + 1) * GLA_DV], gng) for hd in range(GLA_HEADS)], axis=1)
        y = o_n * (g_out * jax.nn.sigmoid(g_out))
        ycat_ref[rows, CONV_WIDTH:CONV_WIDTH + GLA_WIDTH] = y.astype(BF16)
        return carry

    lax.fori_loop(0, ts // GLA_CHUNK, chunk_body, 0, unroll=True)

    x1 = x + _dot(ycat_ref[...], wout_ref[...])
    x1_ref[0] = x1
    h2 = _rms(x1, g2_ref[...])
    h2_hi, h2_lo = _split_bf16(h2)
    h2_ref[0] = h2_hi
    hi_terms = _dot(h2_hi, wrhl_ref[...])
    logits = (hi_terms[:, 0:LANES] + hi_terms[:, LANES:2 * LANES]
              + _dot(h2_lo, wrhl_ref[:, 0:LANES]) + br_ref[...])

    lane = lax.broadcasted_iota(I32, (ts, LANES), 1).astype(F32)
    work = logits
    sel = jnp.zeros((ts, LANES), F32)
    top_v, top_i, top_oh = [], [], []
    for _ in range(TOP_K):
        m = jnp.max(work, axis=-1, keepdims=True)
        idx = jnp.min(jnp.where(work == m, lane, float(LANES)), axis=-1, keepdims=True)
        oh = lane == idx
        top_v.append(m)
        top_i.append(idx)
        top_oh.append(oh)
        sel = sel + oh.astype(F32)
        work = jnp.where(oh, -jnp.inf, work)
    exps = [jnp.exp(tv - top_v[0]) for tv in top_v]
    denom = exps[0] + exps[1] + exps[2] + exps[3]
    gates = [e / denom for e in exps]

    ti = lax.broadcasted_iota(I32, (ts, ts), 0)
    tj = lax.broadcasted_iota(I32, (ts, ts), 1)
    strict_lower = (tj < ti).astype(BF16)
    local = _dot(strict_lower, sel.astype(BF16))
    carry = carry_ref[0:1, :]
    ranks = [jnp.sum(jnp.where(oh, local + carry, 0.0), axis=-1, keepdims=True) for oh in top_oh]

    lane_row = lax.broadcasted_iota(I32, (1, LANES), 1)
    slot_base = []
    before = jnp.zeros((1, LANES), F32)
    for u in range(ts // SORT_TILE):
        sub_cnt = jnp.sum(sel[u * SORT_TILE:(u + 1) * SORT_TILE, :], axis=0, keepdims=True)
        seg_start = sub_cnt
        shift = 1
        while shift < LANES:
            seg_start = seg_start + jnp.where(lane_row >= shift,
                                              pltpu.roll(seg_start, shift, 1), 0.0)
            shift *= 2
        seg_start = seg_start - sub_cnt
        seg_ref[u, 0:1, :] = carry + before
        seg_ref[u, 1:2, :] = sub_cnt
        seg_ref[u, 2:3, :] = seg_start
        seg_ref[u, 3:SUBLANES, :] = jnp.zeros((SUBLANES - 3, LANES), F32)
        slot_base.append(jnp.broadcast_to(seg_start - before, (SORT_TILE, LANES)))
        before = before + sub_cnt
    slot_base = jnp.concatenate(slot_base, axis=0)
    slots = [jnp.sum(jnp.where(oh, local + slot_base, 0.0), axis=-1, keepdims=True)
             for oh in top_oh]
    new_carry = carry + before
    carry_ref[...] = jnp.broadcast_to(new_carry, carry_ref.shape)
    cnt_ref[...] = jnp.broadcast_to(new_carry, cnt_ref.shape)

    meta = jnp.zeros((ts, LANES), F32)
    for kk in range(TOP_K):
        meta = jnp.where(lane == float(kk), top_i[kk], meta)
        meta = jnp.where(lane == float(TOP_K + kk), gates[kk], meta)
        meta = jnp.where(lane == float(2 * TOP_K + kk), ranks[kk], meta)
        meta = jnp.where(lane == float(3 * TOP_K + kk), slots[kk], meta)
    meta_ref[0] = meta
    route_ref[...] = meta.T[0:ROUTE_ROWS, :]


def _mixer_call(x, g1, win, convw, wgk, bgk, gng, wout, g2, wrhl, br):
    bsz, seq, d = x.shape
    ts = SEQ_TILE
    grid = (bsz, seq // ts)

    def const(shape):
        return pl.BlockSpec(shape, lambda b, s: (0,) * len(shape))

    tile = lambda w: pl.BlockSpec((1, ts, w), lambda b, s: (b, s, 0))
    return pl.pallas_call(
        _mixer_kernel,
        grid=grid,
        in_specs=[tile(d), const(g1.shape), const(win.shape), const(convw.shape),
                  const(wgk.shape), const(bgk.shape), const(gng.shape), const(wout.shape),
                  const(g2.shape), const(wrhl.shape), const(br.shape)],
        out_specs=[tile(d),
                   tile(d),
                   tile(LANES),
                   pl.BlockSpec((ROUTE_ROWS, ts), lambda b, s: (0, b * (seq // ts) + s)),
                   pl.BlockSpec((ts // SORT_TILE, SUBLANES, LANES),
                                lambda b, s: (b * (seq // ts) + s, 0, 0)),
                   const((SUBLANES, LANES))],
        out_shape=[jax.ShapeDtypeStruct((bsz, seq, d), F32),
                   jax.ShapeDtypeStruct((bsz, seq, d), BF16),
                   jax.ShapeDtypeStruct((bsz, seq, LANES), F32),
                   jax.ShapeDtypeStruct((ROUTE_ROWS, bsz * seq), F32),
                   jax.ShapeDtypeStruct((bsz * seq // SORT_TILE, SUBLANES, LANES), F32),
                   jax.ShapeDtypeStruct((SUBLANES, LANES), F32)],
        scratch_shapes=[pltpu.VMEM((ts, D_IN_PAD), F32),
                        pltpu.VMEM((ts + SUBLANES, CONV_WIDTH), F32),
                        pltpu.VMEM((ts, GLA_QK), F32),
                        pltpu.VMEM((ts, D_MODEL), BF16),
                        pltpu.VMEM((GLA_QK, GLA_WIDTH), F32),
                        pltpu.VMEM((SUBLANES, LANES), F32)],
        compiler_params=pltpu.CompilerParams(
            dimension_semantics=("arbitrary", "arbitrary"), vmem_limit_bytes=VMEM_LIMIT),
        name="mixer",
    )(x, g1, win, convw, wgk, bgk, gng, wout, g2, wrhl, br)


def _dispatch_kernel(run_xs_ref, run_buf_ref, run_len_ref, pad_lo_ref, pad_n_ref,
                     route_ref, h2_ref, xs_hbm, sbuf, zbuf, sem, zsem):
    i = pl.program_id(0)
    n = pl.num_programs(0)
    td = h2_ref.shape[0]
    n_rows = TOP_K * td
    slot = i % 2

    def zero_fill(wait):
        def fill(g, carry):
            ln = pl.multiple_of(pad_n_ref[g], ROW_TILES)

            @pl.when(ln > 0)
            def _():
                lo = pl.multiple_of(pad_lo_ref[g], ROW_TILES)
                cp = pltpu.make_async_copy(zbuf.at[pl.ds(0, ln), :], xs_hbm.at[pl.ds(lo, ln), :],
                                           zsem.at[0])
                if wait:
                    cp.wait()
                else:
                    cp.start()
            return carry

        lax.fori_loop(0, pad_lo_ref.shape[0], fill, 0)

    def wait_runs(buf_slot):
        pltpu.make_async_copy(sbuf.at[buf_slot], xs_hbm.at[pl.ds(0, n_rows * ROW_TILES), :],
                              sem.at[buf_slot]).wait()

    @pl.when(i == 0)
    def _():
        zbuf[...] = jnp.zeros_like(zbuf)
        zero_fill(wait=False)

    @pl.when(i >= 2)
    def _():
        wait_runs(slot)

    route = route_ref[...]
    row = lax.broadcasted_iota(I32, (n_rows, td), 0).astype(F32)
    pick = jnp.zeros((n_rows, td), F32)
    for kk in range(TOP_K):
        pick = jnp.where(row == route[3 * TOP_K + kk:3 * TOP_K + kk + 1, :], 1.0, pick)
    rows_sorted = _dot(pick.astype(BF16), h2_ref[...])
    for j in range(ROW_TILES):
        sbuf[slot, pl.ds(j, n_rows, stride=ROW_TILES), :] = rows_sorted[:, j * LANES:(j + 1) * LANES]

    def body(e, carry):
        g = i * N_EXPERTS + e
        ln = pl.multiple_of(run_len_ref[g], ROW_TILES)

        @pl.when(ln > 0)
        def _():
            src = pl.multiple_of(run_buf_ref[g], ROW_TILES)
            dst = pl.multiple_of(run_xs_ref[g], ROW_TILES)
            pltpu.make_async_copy(sbuf.at[slot, pl.ds(src, ln), :], xs_hbm.at[pl.ds(dst, ln), :],
                                  sem.at[slot]).start()
        return carry

    lax.fori_loop(0, N_EXPERTS, body, 0)

    @pl.when(i == n - 1)
    def _():
        @pl.when(i >= 1)
        def _():
            wait_runs(1 - slot)
        wait_runs(slot)
        zero_fill(wait=True)


def _dispatch_call(run_xs, run_buf, run_len, pad_lo, pad_n, route, h2, n_rows):
    t, d = h2.shape
    td = SORT_TILE
    grid_spec = pltpu.PrefetchScalarGridSpec(
        num_scalar_prefetch=5,
        grid=(t // td,),
        in_specs=[pl.BlockSpec((ROUTE_ROWS, td), lambda i, *_: (0, i)),
                  pl.BlockSpec((td, d), lambda i, *_: (i, 0))],
        out_specs=pl.BlockSpec(memory_space=pl.ANY),
        scratch_shapes=[pltpu.VMEM((2, TOP_K * td * ROW_TILES, LANES), F32),
                        pltpu.VMEM((MOE_BLOCK * ROW_TILES, LANES), F32),
                        pltpu.SemaphoreType.DMA((2,)),
                        pltpu.SemaphoreType.DMA((1,))],
    )
    return pl.pallas_call(
        _dispatch_kernel,
        grid_spec=grid_spec,
        out_shape=jax.ShapeDtypeStruct((n_rows * ROW_TILES, LANES), F32),
        compiler_params=pltpu.CompilerParams(
            dimension_semantics=("arbitrary",), vmem_limit_bytes=VMEM_LIMIT,
            has_side_effects=True),
        name="dispatch",
    )(run_xs, run_buf, run_len, pad_lo, pad_n, route, h2)


def _experts_kernel(bexp_ref, first_ref, nxt_ref, nval_ref, nreal_ref,
                    xs_ref, wgu_hbm, bgu_ref, wd_hbm, bd_ref,
                    ys_ref,
                    xb_ref, act_ref, wgu_stage, wd_stage, wgu_bf, wd_bf, wsem):
    i = pl.program_id(0)
    nreal = nreal_ref[0]
    bm = xb_ref.shape[0]

    def weight_copies(e):
        return (pltpu.make_async_copy(wgu_hbm.at[e], wgu_stage, wsem.at[0]),
                pltpu.make_async_copy(wd_hbm.at[e], wd_stage, wsem.at[1]))

    @pl.when(i >= nreal)
    def _():
        ys_ref[...] = jnp.zeros_like(ys_ref)

    @pl.when(i < nreal)
    def _():
        e = bexp_ref[i]

        @pl.when(i == 0)
        def _():
            for cp in weight_copies(e):
                cp.start(priority=1)

        @pl.when(first_ref[i] == 1)
        def _():
            for cp in weight_copies(e):
                cp.wait()

            def cast_gu(r, carry):
                rows = pl.ds(pl.multiple_of(r * CAST_ROWS, CAST_ROWS), CAST_ROWS)
                wgu_bf[rows, :] = wgu_stage[rows, :].astype(BF16)
                return carry

            def cast_d(r, carry):
                rows = pl.ds(pl.multiple_of(r * CAST_ROWS, CAST_ROWS), CAST_ROWS)
                wd_bf[rows, :] = wd_stage[rows, :].astype(BF16)
                return carry

            lax.fori_loop(0, D_MODEL // CAST_ROWS, cast_gu, 0)
            lax.fori_loop(0, D_FF // CAST_ROWS, cast_d, 0)

            @pl.when(nxt_ref[i] >= 0)
            def _():
                for cp in weight_copies(nxt_ref[i]):
                    cp.start(priority=1)

        def mlp(rows):
            for j in range(ROW_TILES):
                xb_ref[0:rows, j * LANES:(j + 1) * LANES] = (
                    xs_ref[pl.ds(j, rows, stride=ROW_TILES), :].astype(BF16))
            for c in range(D_FF // FF_CHUNK):
                f0 = c * FF_CHUNK
                xb = xb_ref[0:rows, :]
                gate = _dot(xb, wgu_bf[:, f0:f0 + FF_CHUNK]) + bgu_ref[0, :, f0:f0 + FF_CHUNK]
                up = (_dot(xb, wgu_bf[:, D_FF + f0:D_FF + f0 + FF_CHUNK])
                      + bgu_ref[0, :, D_FF + f0:D_FF + f0 + FF_CHUNK])
                gate = jnp.minimum(gate, SWIGLU_LIMIT)
                up = jnp.clip(up, -SWIGLU_LIMIT, SWIGLU_LIMIT)
                glu = gate * jax.nn.sigmoid(gate * SWIGLU_ALPHA)
                act_ref[0:rows, f0:f0 + FF_CHUNK] = ((up + 1.0) * glu).astype(BF16)
            out = _dot(act_ref[0:rows, :], wd_bf[...]) + bd_ref[0]
            for j in range(ROW_TILES):
                ys_ref[pl.ds(j, rows, stride=ROW_TILES), :] = out[:, j * LANES:(j + 1) * LANES]

        @pl.when(nval_ref[i] > bm // 2)
        def _():
            mlp(bm)

        @pl.when(nval_ref[i] <= bm // 2)
        def _():
            mlp(bm // 2)
            ys_ref[bm // 2 * ROW_TILES:bm * ROW_TILES, :] = jnp.zeros(
                (bm // 2 * ROW_TILES, LANES), F32)


def _experts_call(bexp, first, nxt, nval, nreal, xs, wgu, bgu, wd, bd):
    bm = MOE_BLOCK
    n_blocks = xs.shape[0] // (bm * ROW_TILES)
    bgu3 = bgu.reshape(N_EXPERTS, 1, 2 * D_FF)
    bd3 = bd.reshape(N_EXPERTS, 1, D_MODEL)
    grid_spec = pltpu.PrefetchScalarGridSpec(
        num_scalar_prefetch=5,
        grid=(n_blocks,),
        in_specs=[
            pl.BlockSpec((bm * ROW_TILES, LANES),
                         lambda i, be, fi, nx, nv, nr: (jnp.minimum(i, nr[0] - 1), 0)),
            pl.BlockSpec(memory_space=pl.ANY),
            pl.BlockSpec((1, 1, 2 * D_FF), lambda i, be, fi, nx, nv, nr: (be[i], 0, 0)),
            pl.BlockSpec(memory_space=pl.ANY),
            pl.BlockSpec((1, 1, D_MODEL), lambda i, be, fi, nx, nv, nr: (be[i], 0, 0)),
        ],
        out_specs=pl.BlockSpec((bm * ROW_TILES, LANES), lambda i, be, fi, nx, nv, nr: (i, 0)),
        scratch_shapes=[pltpu.VMEM((bm, D_MODEL), BF16),
                        pltpu.VMEM((bm, D_FF), BF16),
                        pltpu.VMEM((D_MODEL, 2 * D_FF), F32),
                        pltpu.VMEM((D_FF, D_MODEL), F32),
                        pltpu.VMEM((D_MODEL, 2 * D_FF), BF16),
                        pltpu.VMEM((D_FF, D_MODEL), BF16),
                        pltpu.SemaphoreType.DMA((2,))],
    )
    return pl.pallas_call(
        _experts_kernel,
        grid_spec=grid_spec,
        out_shape=jax.ShapeDtypeStruct(xs.shape, F32),
        compiler_params=pltpu.CompilerParams(
            dimension_semantics=("arbitrary",), vmem_limit_bytes=VMEM_LIMIT),
        name="experts",
    )(bexp, first, nxt, nval, nreal, xs, wgu, bgu3, wd, bd3)


def _combine_kernel(src_ref, dst_ref, len_ref, x1_ref, meta_ref, g_ref, ys_hbm, out_ref,
                    ybuf, ysort_ref, sem):
    i = pl.program_id(0)
    n = pl.num_programs(0)
    tb = x1_ref.shape[0]
    n_rows = TOP_K * tb
    slot = i % 2

    def start_runs(tile, buf_slot):
        def body(e, carry):
            g = tile * N_EXPERTS + e
            ln = pl.multiple_of(len_ref[g], ROW_TILES)

            @pl.when(ln > 0)
            def _():
                src = pl.multiple_of(src_ref[g], ROW_TILES)
                dst = pl.multiple_of(dst_ref[g], ROW_TILES)
                pltpu.make_async_copy(ys_hbm.at[pl.ds(src, ln), :],
                                      ybuf.at[buf_slot, pl.ds(dst, ln), :], sem.at[buf_slot]).start()
            return carry
        lax.fori_loop(0, N_EXPERTS, body, 0)

    @pl.when(i == 0)
    def _():
        start_runs(0, 0)

    @pl.when(i + 1 < n)
    def _():
        start_runs(i + 1, 1 - slot)

    pltpu.make_async_copy(ys_hbm.at[pl.ds(0, n_rows * ROW_TILES), :], ybuf.at[slot],
                          sem.at[slot]).wait()

    for j in range(ROW_TILES):
        ysort_ref[:, j * LANES:(j + 1) * LANES] = (
            ybuf[slot, pl.ds(j, n_rows, stride=ROW_TILES), :].astype(BF16))

    meta = meta_ref[...]
    col = lax.broadcasted_iota(I32, (tb, n_rows), 1).astype(F32)
    weights = jnp.zeros((tb, n_rows), F32)
    for kk in range(TOP_K):
        weights = jnp.where(col == meta[:, 3 * TOP_K + kk:3 * TOP_K + kk + 1],
                            meta[:, TOP_K + kk:TOP_K + kk + 1], weights)
    acc = x1_ref[...] + _dot(weights.astype(BF16), ysort_ref[...])
    out_ref[...] = _rms(acc, g_ref[...])


def _combine_call(seg_src, seg_dst, seg_len, x1, meta, g, ys):
    t, d = x1.shape
    tb = SORT_TILE
    nb = t // tb
    grid_spec = pltpu.PrefetchScalarGridSpec(
        num_scalar_prefetch=3,
        grid=(nb,),
        in_specs=[pl.BlockSpec((tb, d), lambda i, a, b, c: (i, 0)),
                  pl.BlockSpec((tb, LANES), lambda i, a, b, c: (i, 0)),
                  pl.BlockSpec((1, d), lambda i, a, b, c: (0, 0)),
                  pl.BlockSpec(memory_space=pl.ANY)],
        out_specs=pl.BlockSpec((tb, d), lambda i, a, b, c: (i, 0)),
        scratch_shapes=[pltpu.VMEM((2, TOP_K * tb * ROW_TILES, LANES), F32),
                        pltpu.VMEM((TOP_K * tb, D_MODEL), BF16),
                        pltpu.SemaphoreType.DMA((2,))],
    )
    return pl.pallas_call(
        _combine_kernel,
        grid_spec=grid_spec,
        out_shape=jax.ShapeDtypeStruct((t, d), F32),
        compiler_params=pltpu.CompilerParams(
            dimension_semantics=("arbitrary",), vmem_limit_bytes=VMEM_LIMIT),
        name="combine",
    )(seg_src, seg_dst, seg_len, x1, meta, g, ys)


def _routing_tables(counts, t):
    bm = MOE_BLOCK
    n_blocks = t * TOP_K // bm + N_EXPERTS
    eids = jnp.arange(N_EXPERTS, dtype=I32)
    nblk_e = (counts + bm - 1) // bm
    blk_end = jnp.sum(jnp.where(eids[None, :] <= eids[:, None], nblk_e[None, :], 0), axis=1)
    blk_start = blk_end - nblk_e
    nreal = blk_end[N_EXPERTS - 1]
    pad_start = blk_start * bm
    blk = jnp.arange(n_blocks, dtype=I32)
    bexp = jnp.minimum(jnp.sum((blk_end[None, :] <= blk[:, None]).astype(I32), axis=1),
                       N_EXPERTS - 1)
    blk_is_e = bexp[:, None] == eids[None, :]
    pick = lambda tab: jnp.sum(jnp.where(blk_is_e, tab[None, :], 0), axis=1)
    first = (blk == pick(blk_start)).astype(I32)
    nxt_e = jnp.sum((blk_end[None, :] <= blk_end[:, None]).astype(I32), axis=1)
    nxt_e = jnp.where(blk_end < nreal, jnp.minimum(nxt_e, N_EXPERTS - 1), -1)
    nxt = pick(nxt_e)
    nval = jnp.clip(pick(counts) - (blk - pick(blk_start)) * bm, 0, bm)
    tail_blk = jnp.arange(N_EXPERTS, dtype=I32) + nreal
    pad_lo = jnp.concatenate([pad_start + counts, jnp.minimum(tail_blk, n_blocks - 1) * bm])
    pad_n = jnp.concatenate([nblk_e * bm - counts, jnp.where(tail_blk < n_blocks, bm, 0)])
    return (pad_start, bexp, first, nxt, nval, nreal.reshape(1).astype(I32), n_blocks,
            pad_lo * ROW_TILES, pad_n * ROW_TILES)


def kernel(x, norm_mix_g, w_in, conv_w, w_gk_up, b_gk_up, gla_norm_g, w_out, norm_ffn_g,
           w_router, b_router, w_gate_up, b_gate_up, w_down, b_down, norm_final_g):
    bsz, seq, d = x.shape
    t = bsz * seq
    assert w_in.shape[0] == 1, "single-layer trunk only"
    l = 0
    d_in = w_in.shape[-1]
    win = jnp.pad(w_in[l], ((0, 0), (0, D_IN_PAD - d_in))).astype(BF16)
    wgk = jnp.pad(w_gk_up[l], ((0, LANES - GLA_RANK), (0, 0))).astype(BF16)
    wr = jnp.pad(w_router[l], ((0, 0), (0, LANES - N_EXPERTS)))
    wrh = wr.astype(BF16)
    wrhl = jnp.concatenate([wrh, (wr - wrh.astype(F32)).astype(BF16)], axis=1)
    br = jnp.pad(b_router[l], (0, LANES - N_EXPERTS), constant_values=NEG_BIG).reshape(1, LANES)

    x1, h2, meta, route, seg, cnt = _mixer_call(
        x, norm_mix_g[l].reshape(1, d), win, conv_w[l], wgk, b_gk_up[l].reshape(1, GLA_QK),
        gla_norm_g[l].reshape(1, GLA_DV), w_out[l].astype(BF16), norm_ffn_g[l].reshape(1, d),
        wrhl, br)

    meta2 = meta.reshape(t, LANES)
    counts = cnt[0, :N_EXPERTS].astype(I32)
    (pad_start, bexp, first, nxt, nval, nreal, n_blocks, pad_lo,
     pad_n) = _routing_tables(counts, t)
    seg_src = ((pad_start[None, :] + seg[:, 0, :N_EXPERTS].astype(I32)) * ROW_TILES).reshape(-1)
    seg_len = (seg[:, 1, :N_EXPERTS].astype(I32) * ROW_TILES).reshape(-1)
    seg_dst = (seg[:, 2, :N_EXPERTS].astype(I32) * ROW_TILES).reshape(-1)
    xs = _dispatch_call(seg_src, seg_dst, seg_len, pad_lo, pad_n, route, h2.reshape(t, d),
                        n_blocks * MOE_BLOCK)
    ys = _experts_call(bexp, first, nxt, nval, nreal, xs, w_gate_up[l], b_gate_up[l], w_down[l],
                       b_down[l])
    out = _combine_call(seg_src, seg_dst, seg_len, x1.reshape(t, d), meta2,
                        norm_final_g.reshape(1, d), ys)
    return out.reshape(bsz, seq, d)
```

```python
import jax
import jax.numpy as jnp
from jax import lax
from jax.experimental import pallas as pl
from jax.experimental.pallas import tpu as pltpu

F32 = jnp.float32
BF16 = jnp.bfloat16
I32 = jnp.int32
U32 = jnp.uint32

D_MODEL = 1024
CONV_WIDTH = 512
CONV_K = 3
GLA_WIDTH = 512
GLA_HEADS = 4
GLA_DV = 128
GLA_DK = 64
GLA_QK = GLA_HEADS * GLA_DK
GLA_RANK = 16
GLA_NORMALIZER = 16.0
GLA_CHUNK = 64
N_EXPERTS = 32
TOP_K = 4
D_FF = 1024
SWIGLU_LIMIT = 7.0
SWIGLU_ALPHA = 1.702
RMS_EPS = 1e-5

LANES = 128
SUBLANES = 8
ROW_TILES = D_MODEL // LANES // 2
ROUTE_ROWS = 16

OFF_UH = 0
OFF_GB = OFF_UH + CONV_WIDTH
OFF_GC = OFF_GB + CONV_WIDTH
OFF_Q = OFF_GC + CONV_WIDTH
OFF_K = OFF_Q + GLA_QK
OFF_V = OFF_K + GLA_QK
OFF_GO = OFF_V + GLA_WIDTH
OFF_GKL = OFF_GO + GLA_WIDTH
D_IN_PAD = OFF_GKL + LANES

SEQ_TILE = 512
SORT_TILE = 512
MOE_BLOCK = 512
FF_CHUNK = 256
CAST_ROWS = 128
NEG_BIG = -1e30
VMEM_LIMIT = 56 * 1024 * 1024


def _rms(x, g):
    return x * lax.rsqrt(jnp.mean(x * x, axis=-1, keepdims=True) + RMS_EPS) * g


def _dot(a, b):
    return jnp.dot(a, b, preferred_element_type=F32)


def _dot_nt(a, b):
    return lax.dot_general(a, b, (((1,), (1,)), ((), ())), preferred_element_type=F32)


def _pack_rows(x):
    half = x.shape[1] // 2
    xr = x.astype(BF16).astype(F32)
    lo = lax.bitcast_convert_type(xr[:, :half], U32) >> 16
    hi = lax.bitcast_convert_type(xr[:, half:], U32) & jnp.uint32(0xFFFF0000)
    return hi | lo


def _unpack_words(w):
    lo = lax.bitcast_convert_type(w << 16, F32).astype(BF16)
    hi = lax.bitcast_convert_type(w & jnp.uint32(0xFFFF0000), F32).astype(BF16)
    return lo, hi


def _split_bf16(x):
    hi = x.astype(BF16)
    lo = (x - hi.astype(F32)).astype(BF16)
    return hi, lo


def _mixer_kernel(x_ref, g1_ref, win_ref, convw_ref, wgk_ref, bgk_ref, gng_ref, wout_ref,
                  g2_ref, wrhl_ref, br_ref,
                  x1_ref, h2_ref, meta_ref, route_ref, seg_ref, cnt_ref,
                  proj_ref, ubuf_ref, la_ref, ycat_ref, state_ref, carry_ref):
    ts = x_ref.shape[1]
    b_idx = pl.program_id(0)
    s_idx = pl.program_id(1)

    @pl.when(s_idx == 0)
    def _():
        state_ref[...] = jnp.zeros_like(state_ref)
        ubuf_ref[0:SUBLANES, :] = jnp.zeros((SUBLANES, CONV_WIDTH), F32)

    @pl.when((s_idx == 0) & (b_idx == 0))
    def _():
        carry_ref[...] = jnp.zeros_like(carry_ref)

    x = x_ref[0]
    h = _rms(x, g1_ref[...]).astype(BF16)
    proj_ref[...] = _dot(h, win_ref[...])

    u = proj_ref[:, OFF_GC:OFF_GC + CONV_WIDTH] * proj_ref[:, OFF_UH:OFF_UH + CONV_WIDTH]
    ubuf_ref[SUBLANES:SUBLANES + ts, :] = u
    u1 = ubuf_ref[pl.ds(SUBLANES - 1, ts), :]
    u2 = ubuf_ref[pl.ds(SUBLANES - 2, ts), :]
    conv = convw_ref[0:1, :] * u2 + convw_ref[1:2, :] * u1 + convw_ref[2:3, :] * u
    ycat_ref[:, 0:CONV_WIDTH] = (proj_ref[:, OFF_GB:OFF_GB + CONV_WIDTH] * conv).astype(BF16)
    ubuf_ref[0:SUBLANES, :] = ubuf_ref[ts:ts + SUBLANES, :]

    gk = _dot(proj_ref[:, OFF_GKL:OFF_GKL + LANES].astype(BF16), wgk_ref[...]) + bgk_ref[...]
    log_sig = jnp.minimum(gk, 0.0) - jnp.log1p(jnp.exp(-jnp.abs(gk)))
    la_ref[...] = log_sig / GLA_NORMALIZER

    ci = lax.broadcasted_iota(I32, (GLA_CHUNK, GLA_CHUNK), 0)
    cj = lax.broadcasted_iota(I32, (GLA_CHUNK, GLA_CHUNK), 1)
    tri_incl = (cj <= ci).astype(BF16)
    causal = cj <= ci
    causal4 = jnp.concatenate([causal] * GLA_HEADS, axis=0)
    lane_qk = lax.broadcasted_iota(I32, (1, GLA_QK), 1)
    head_masks = [((lane_qk >= hd * GLA_DK) & (lane_qk < (hd + 1) * GLA_DK)).astype(F32)
                  for hd in range(GLA_HEADS)]
    gng = gng_ref[...]

    def chunk_body(c, carry):
        r0 = pl.multiple_of(c * GLA_CHUNK, GLA_CHUNK)
        rows = pl.ds(r0, GLA_CHUNK)
        la_hi, la_lo = _split_bf16(la_ref[rows, :])
        bcum = _dot(tri_incl, la_hi) + _dot(tri_incl, la_lo)
        blast = bcum[GLA_CHUNK - 1:GLA_CHUNK, :]
        q = proj_ref[rows, OFF_Q:OFF_Q + GLA_QK] * (GLA_DK ** -0.5)
        k = proj_ref[rows, OFF_K:OFF_K + GLA_QK]
        v = proj_ref[rows, OFF_V:OFF_V + GLA_WIDTH].astype(BF16)
        qd = q * jnp.exp(bcum)
        kd = (k * jnp.exp(-bcum)).astype(BF16)
        kr = k * jnp.exp(blast - bcum)

        q_stack = jnp.concatenate([qd * m for m in head_masks], axis=0).astype(BF16)
        scores = jnp.where(causal4, _dot_nt(q_stack, kd), 0.0).astype(BF16)

        state = state_ref[...]
        o_inter = _dot(qd.astype(BF16), state.astype(BF16))
        o_intra = jnp.concatenate(
            [_dot(scores[hd * GLA_CHUNK:(hd + 1) * GLA_CHUNK, :],
                  v[:, hd * GLA_DV:(hd + 1) * GLA_DV]) for hd in range(GLA_HEADS)], axis=1)
        o = o_inter + o_intra

        kt = jnp.concatenate([kr, jnp.broadcast_to(blast, (GLA_CHUNK, GLA_QK))], axis=0).T
        dcol = jnp.exp(kt[:, GLA_CHUNK:GLA_CHUNK + 1])
        lane_c = lax.broadcasted_iota(I32, (GLA_QK, 2 * GLA_CHUNK), 1)
        kt_b = jnp.where(lane_c < GLA_CHUNK, kt, 0.0).astype(BF16)
        v_pad = jnp.concatenate([v, jnp.zeros_like(v)], axis=0)
        for hd in range(GLA_HEADS):
            rs = slice(hd * GLA_DK, (hd + 1) * GLA_DK)
            cs = slice(hd * GLA_DV, (hd + 1) * GLA_DV)
            kv = _dot(kt_b[rs, :], v_pad[:, cs])
            state_ref[rs, cs] = dcol[rs, :] * state[rs, cs] + kv

        g_out = proj_ref[rows, OFF_GO:OFF_GO + GLA_WIDTH]
        o_n = jnp.concatenate(
            [_rms(o[:, hd * GLA_DV:(hd + 1) * GLA_DV], gng) for hd in range(GLA_HEADS)], axis=1)
        y = o_n * (g_out * jax.nn.sigmoid(g_out))
        ycat_ref[rows, CONV_WIDTH:CONV_WIDTH + GLA_WIDTH] = y.astype(BF16)
        return carry

    lax.fori_loop(0, ts // GLA_CHUNK, chunk_body, 0, unroll=True)

    x1 = x + _dot(ycat_ref[...], wout_ref[...])
    x1_ref[0] = x1
    h2 = _rms(x1, g2_ref[...])
    h2_hi, h2_lo = _split_bf16(h2)
    h2_ref[0] = h2_hi
    hi_terms = _dot(h2_hi, wrhl_ref[...])
    logits = (hi_terms[:, 0:LANES] + hi_terms[:, LANES:2 * LANES]
              + _dot(h2_lo, wrhl_ref[:, 0:LANES]) + br_ref[...])

    lane = lax.broadcasted_iota(I32, (ts, LANES), 1).astype(F32)
    work = logits
    sel = jnp.zeros((ts, LANES), F32)
    top_v, top_i, top_oh = [], [], []
    for _ in range(TOP_K):
        m = jnp.max(work, axis=-1, keepdims=True)
        idx = jnp.min(jnp.where(work == m, lane, float(LANES)), axis=-1, keepdims=True)
        oh = lane == idx
        top_v.append(m)
        top_i.append(idx)
        top_oh.append(oh)
        sel = sel + oh.astype(F32)
        work = jnp.where(oh, -jnp.inf, work)
    exps = [jnp.exp(tv - top_v[0]) for tv in top_v]
    denom = exps[0] + exps[1] + exps[2] + exps[3]
    gates = [e / denom for e in exps]

    ti = lax.broadcasted_iota(I32, (ts, ts), 0)
    tj = lax.broadcasted_iota(I32, (ts, ts), 1)
    strict_lower = (tj < ti).astype(BF16)
    local = _dot(strict_lower, sel.astype(BF16))
    carry = carry_ref[0:1, :]
    ranks = [jnp.sum(jnp.where(oh, local + carry, 0.0), axis=-1, keepdims=True) for oh in top_oh]

    lane_row = lax.broadcasted_iota(I32, (1, LANES), 1)
    slot_base = []
    before = jnp.zeros((1, LANES), F32)
    for u in range(ts // SORT_TILE):
        sub_cnt = jnp.sum(sel[u * SORT_TILE:(u + 1) * SORT_TILE, :], axis=0, keepdims=True)
        seg_start = sub_cnt
        shift = 1
        while shift < LANES:
            seg_start = seg_start + jnp.where(lane_row >= shift,
                                              pltpu.roll(seg_start, shift, 1), 0.0)
            shift *= 2
        seg_start = seg_start - sub_cnt
        seg_ref[u, 0:1, :] = carry + before
        seg_ref[u, 1:2, :] = sub_cnt
        seg_ref[u, 2:3, :] = seg_start
        seg_ref[u, 3:SUBLANES, :] = jnp.zeros((SUBLANES - 3, LANES), F32)
        slot_base.append(jnp.broadcast_to(seg_start - before, (SORT_TILE, LANES)))
        before = before + sub_cnt
    slot_base = jnp.concatenate(slot_base, axis=0)
    slots = [jnp.sum(jnp.where(oh, local + slot_base, 0.0), axis=-1, keepdims=True)
             for oh in top_oh]
    new_carry = carry + before
    carry_ref[...] = jnp.broadcast_to(new_carry, carry_ref.shape)
    cnt_ref[...] = jnp.broadcast_to(new_carry, cnt_ref.shape)

    meta = jnp.zeros((ts, LANES), F32)
    for kk in range(TOP_K):
        meta = jnp.where(lane == float(kk), top_i[kk], meta)
        meta = jnp.where(lane == float(TOP_K + kk), gates[kk], meta)
        meta = jnp.where(lane == float(2 * TOP_K + kk), ranks[kk], meta)
        meta = jnp.where(lane == float(3 * TOP_K + kk), slots[kk], meta)
    meta_ref[0] = meta
    route_ref[...] = meta.T[0:ROUTE_ROWS, :]


def _mixer_call(x, g1, win, convw, wgk, bgk, gng, wout, g2, wrhl, br):
    bsz, seq, d = x.shape
    ts = SEQ_TILE
    grid = (bsz, seq // ts)

    def const(shape):
        return pl.BlockSpec(shape, lambda b, s: (0,) * len(shape))

    tile = lambda w: pl.BlockSpec((1, ts, w), lambda b, s: (b, s, 0))
    return pl.pallas_call(
        _mixer_kernel,
        grid=grid,
        in_specs=[tile(d), const(g1.shape), const(win.shape), const(convw.shape),
                  const(wgk.shape), const(bgk.shape), const(gng.shape), const(wout.shape),
                  const(g2.shape), const(wrhl.shape), const(br.shape)],
        out_specs=[tile(d),
                   tile(d),
                   tile(LANES),
                   pl.BlockSpec((ROUTE_ROWS, ts), lambda b, s: (0, b * (seq // ts) + s)),
                   pl.BlockSpec((ts // SORT_TILE, SUBLANES, LANES),
                                lambda b, s: (b * (seq // ts) + s, 0, 0)),
                   const((SUBLANES, LANES))],
        out_shape=[jax.ShapeDtypeStruct((bsz, seq, d), F32),
                   jax.ShapeDtypeStruct((bsz, seq, d), BF16),
                   jax.ShapeDtypeStruct((bsz, seq, LANES), F32),
                   jax.ShapeDtypeStruct((ROUTE_ROWS, bsz * seq), F32),
                   jax.ShapeDtypeStruct((bsz * seq // SORT_TILE, SUBLANES, LANES), F32),
                   jax.ShapeDtypeStruct((SUBLANES, LANES), F32)],
        scratch_shapes=[pltpu.VMEM((ts, D_IN_PAD), F32),
                        pltpu.VMEM((ts + SUBLANES, CONV_WIDTH), F32),
                        pltpu.VMEM((ts, GLA_QK), F32),
                        pltpu.VMEM((ts, D_MODEL), BF16),
                        pltpu.VMEM((GLA_QK, GLA_WIDTH), F32),
                        pltpu.VMEM((SUBLANES, LANES), F32)],
        compiler_params=pltpu.CompilerParams(
            dimension_semantics=("arbitrary", "arbitrary"), vmem_limit_bytes=VMEM_LIMIT),
        name="mixer",
    )(x, g1, win, convw, wgk, bgk, gng, wout, g2, wrhl, br)


def _dispatch_kernel(run_xs_ref, run_buf_ref, run_len_ref, pad_lo_ref, pad_n_ref,
                     route_ref, h2_ref, xs_hbm, sbuf, zbuf, sem, zsem):
    i = pl.program_id(0)
    n = pl.num_programs(0)
    td = h2_ref.shape[0]
    n_rows = TOP_K * td
    slot = i % 2

    def zero_fill(wait):
        def fill(g, carry):
            ln = pl.multiple_of(pad_n_ref[g], ROW_TILES)

            @pl.when(ln > 0)
            def _():
                lo = pl.multiple_of(pad_lo_ref[g], ROW_TILES)
                cp = pltpu.make_async_copy(zbuf.at[pl.ds(0, ln), :], xs_hbm.at[pl.ds(lo, ln), :],
                                           zsem.at[0])
                if wait:
                    cp.wait()
                else:
                    cp.start()
            return carry

        lax.fori_loop(0, pad_lo_ref.shape[0], fill, 0)

    def wait_runs(buf_slot):
        pltpu.make_async_copy(sbuf.at[buf_slot], xs_hbm.at[pl.ds(0, n_rows * ROW_TILES), :],
                              sem.at[buf_slot]).wait()

    @pl.when(i == 0)
    def _():
        zbuf[...] = jnp.zeros_like(zbuf)
        zero_fill(wait=False)

    @pl.when(i >= 2)
    def _():
        wait_runs(slot)

    route = route_ref[...]
    row = lax.broadcasted_iota(I32, (n_rows, td), 0).astype(F32)
    pick = jnp.zeros((n_rows, td), F32)
    for kk in range(TOP_K):
        pick = jnp.where(row == route[3 * TOP_K + kk:3 * TOP_K + kk + 1, :], 1.0, pick)
    rows_sorted = _dot(pick.astype(BF16), h2_ref[...])
    words = _pack_rows(rows_sorted)
    for j in range(ROW_TILES):
        sbuf[slot, pl.ds(j, n_rows, stride=ROW_TILES), :] = words[:, j * LANES:(j + 1) * LANES]

    def body(e, carry):
        g = i * N_EXPERTS + e
        ln = pl.multiple_of(run_len_ref[g], ROW_TILES)

        @pl.when(ln > 0)
        def _():
            src = pl.multiple_of(run_buf_ref[g], ROW_TILES)
            dst = pl.multiple_of(run_xs_ref[g], ROW_TILES)
            pltpu.make_async_copy(sbuf.at[slot, pl.ds(src, ln), :], xs_hbm.at[pl.ds(dst, ln), :],
                                  sem.at[slot]).start()
        return carry

    lax.fori_loop(0, N_EXPERTS, body, 0)

    @pl.when(i == n - 1)
    def _():
        @pl.when(i >= 1)
        def _():
            wait_runs(1 - slot)
        wait_runs(slot)
        zero_fill(wait=True)


def _dispatch_call(run_xs, run_buf, run_len, pad_lo, pad_n, route, h2, n_rows):
    t, d = h2.shape
    td = SORT_TILE
    grid_spec = pltpu.PrefetchScalarGridSpec(
        num_scalar_prefetch=5,
        grid=(t // td,),
        in_specs=[pl.BlockSpec((ROUTE_ROWS, td), lambda i, *_: (0, i)),
                  pl.BlockSpec((td, d), lambda i, *_: (i, 0))],
        out_specs=pl.BlockSpec(memory_space=pl.ANY),
        scratch_shapes=[pltpu.VMEM((2, TOP_K * td * ROW_TILES, LANES), U32),
                        pltpu.VMEM((MOE_BLOCK * ROW_TILES, LANES), U32),
                        pltpu.SemaphoreType.DMA((2,)),
                        pltpu.SemaphoreType.DMA((1,))],
    )
    return pl.pallas_call(
        _dispatch_kernel,
        grid_spec=grid_spec,
        out_shape=jax.ShapeDtypeStruct((n_rows * ROW_TILES, LANES), U32),
        compiler_params=pltpu.CompilerParams(
            dimension_semantics=("arbitrary",), vmem_limit_bytes=VMEM_LIMIT,
            has_side_effects=True),
        name="dispatch",
    )(run_xs, run_buf, run_len, pad_lo, pad_n, route, h2)


def _experts_kernel(bexp_ref, first_ref, nxt_ref, nval_ref, nreal_ref,
                    xs_ref, wgu_hbm, bgu_ref, wd_hbm, bd_ref,
                    ys_ref,
                    xb_ref, act_ref, wgu_stage, wd_stage, wgu_bf, wd_bf, wsem):
    i = pl.program_id(0)
    nreal = nreal_ref[0]
    bm = xb_ref.shape[0]

    def weight_copies(e):
        return (pltpu.make_async_copy(wgu_hbm.at[e], wgu_stage, wsem.at[0]),
                pltpu.make_async_copy(wd_hbm.at[e], wd_stage, wsem.at[1]))

    @pl.when(i >= nreal)
    def _():
        ys_ref[...] = jnp.zeros_like(ys_ref)

    @pl.when(i < nreal)
    def _():
        e = bexp_ref[i]

        @pl.when(i == 0)
        def _():
            for cp in weight_copies(e):
                cp.start(priority=1)

        @pl.when(first_ref[i] == 1)
        def _():
            for cp in weight_copies(e):
                cp.wait()

            def cast_gu(r, carry):
                rows = pl.ds(pl.multiple_of(r * CAST_ROWS, CAST_ROWS), CAST_ROWS)
                wgu_bf[rows, :] = wgu_stage[rows, :].astype(BF16)
                return carry

            def cast_d(r, carry):
                rows = pl.ds(pl.multiple_of(r * CAST_ROWS, CAST_ROWS), CAST_ROWS)
                wd_bf[rows, :] = wd_stage[rows, :].astype(BF16)
                return carry

            lax.fori_loop(0, D_MODEL // CAST_ROWS, cast_gu, 0)
            lax.fori_loop(0, D_FF // CAST_ROWS, cast_d, 0)

            @pl.when(nxt_ref[i] >= 0)
            def _():
                for cp in weight_copies(nxt_ref[i]):
                    cp.start(priority=1)

        def mlp(rows):
            for j in range(ROW_TILES):
                lo, hi = _unpack_words(xs_ref[pl.ds(j, rows, stride=ROW_TILES), :])
                xb_ref[0:rows, j * LANES:(j + 1) * LANES] = lo
                xb_ref[0:rows, D_MODEL // 2 + j * LANES:D_MODEL // 2 + (j + 1) * LANES] = hi
            for c in range(D_FF // FF_CHUNK):
                f0 = c * FF_CHUNK
                xb = xb_ref[0:rows, :]
                gate = _dot(xb, wgu_bf[:, f0:f0 + FF_CHUNK]) + bgu_ref[0, :, f0:f0 + FF_CHUNK]
                up = (_dot(xb, wgu_bf[:, D_FF + f0:D_FF + f0 + FF_CHUNK])
                      + bgu_ref[0, :, D_FF + f0:D_FF + f0 + FF_CHUNK])
                gate = jnp.minimum(gate, SWIGLU_LIMIT)
                up = jnp.clip(up, -SWIGLU_LIMIT, SWIGLU_LIMIT)
                glu = gate * jax.nn.sigmoid(gate * SWIGLU_ALPHA)
                act_ref[0:rows, f0:f0 + FF_CHUNK] = ((up + 1.0) * glu).astype(BF16)
            out = _pack_rows(_dot(act_ref[0:rows, :], wd_bf[...]) + bd_ref[0])
            for j in range(ROW_TILES):
                ys_ref[pl.ds(j, rows, stride=ROW_TILES), :] = out[:, j * LANES:(j + 1) * LANES]

        @pl.when(nval_ref[i] > bm // 2)
        def _():
            mlp(bm)

        @pl.when(nval_ref[i] <= bm // 2)
        def _():
            mlp(bm // 2)
            ys_ref[bm // 2 * ROW_TILES:bm * ROW_TILES, :] = jnp.zeros(
                (bm // 2 * ROW_TILES, LANES), U32)


def _experts_call(bexp, first, nxt, nval, nreal, xs, wgu, bgu, wd, bd):
    bm = MOE_BLOCK
    n_blocks = xs.shape[0] // (bm * ROW_TILES)
    bgu3 = bgu.reshape(N_EXPERTS, 1, 2 * D_FF)
    bd3 = bd.reshape(N_EXPERTS, 1, D_MODEL)
    grid_spec = pltpu.PrefetchScalarGridSpec(
        num_scalar_prefetch=5,
        grid=(n_blocks,),
        in_specs=[
            pl.BlockSpec((bm * ROW_TILES, LANES),
                         lambda i, be, fi, nx, nv, nr: (jnp.minimum(i, nr[0] - 1), 0)),
            pl.BlockSpec(memory_space=pl.ANY),
            pl.BlockSpec((1, 1, 2 * D_FF), lambda i, be, fi, nx, nv, nr: (be[i], 0, 0)),
            pl.BlockSpec(memory_space=pl.ANY),
            pl.BlockSpec((1, 1, D_MODEL), lambda i, be, fi, nx, nv, nr: (be[i], 0, 0)),
        ],
        out_specs=pl.BlockSpec((bm * ROW_TILES, LANES), lambda i, be, fi, nx, nv, nr: (i, 0)),
        scratch_shapes=[pltpu.VMEM((bm, D_MODEL), BF16),
                        pltpu.VMEM((bm, D_FF), BF16),
                        pltpu.VMEM((D_MODEL, 2 * D_FF), F32),
                        pltpu.VMEM((D_FF, D_MODEL), F32),
                        pltpu.VMEM((D_MODEL, 2 * D_FF), BF16),
                        pltpu.VMEM((D_FF, D_MODEL), BF16),
                        pltpu.SemaphoreType.DMA((2,))],
    )
    return pl.pallas_call(
        _experts_kernel,
        grid_spec=grid_spec,
        out_shape=jax.ShapeDtypeStruct(xs.shape, U32),
        compiler_params=pltpu.CompilerParams(
            dimension_semantics=("arbitrary",), vmem_limit_bytes=VMEM_LIMIT),
        name="experts",
    )(bexp, first, nxt, nval, nreal, xs, wgu, bgu3, wd, bd3)


def _combine_kernel(src_ref, dst_ref, len_ref, x1_ref, meta_ref, g_ref, ys_hbm, out_ref,
                    ybuf, ysort_ref, sem):
    i = pl.program_id(0)
    n = pl.num_programs(0)
    tb = x1_ref.shape[0]
    n_rows = TOP_K * tb
    slot = i % 2

    def start_runs(tile, buf_slot):
        def body(e, carry):
            g = tile * N_EXPERTS + e
            ln = pl.multiple_of(len_ref[g], ROW_TILES)

            @pl.when(ln > 0)
            def _():
                src = pl.multiple_of(src_ref[g], ROW_TILES)
                dst = pl.multiple_of(dst_ref[g], ROW_TILES)
                pltpu.make_async_copy(ys_hbm.at[pl.ds(src, ln), :],
                                      ybuf.at[buf_slot, pl.ds(dst, ln), :], sem.at[buf_slot]).start()
            return carry
        lax.fori_loop(0, N_EXPERTS, body, 0)

    @pl.when(i == 0)
    def _():
        start_runs(0, 0)

    @pl.when(i + 1 < n)
    def _():
        start_runs(i + 1, 1 - slot)

    pltpu.make_async_copy(ys_hbm.at[pl.ds(0, n_rows * ROW_TILES), :], ybuf.at[slot],
                          sem.at[slot]).wait()

    for j in range(ROW_TILES):
        lo, hi = _unpack_words(ybuf[slot, pl.ds(j, n_rows, stride=ROW_TILES), :])
        ysort_ref[:, j * LANES:(j + 1) * LANES] = lo
        ysort_ref[:, D_MODEL // 2 + j * LANES:D_MODEL // 2 + (j + 1) * LANES] = hi

    meta = meta_ref[...]
    col = lax.broadcasted_iota(I32, (tb, n_rows), 1).astype(F32)
    weights = jnp.zeros((tb, n_rows), F32)
    for kk in range(TOP_K):
        weights = jnp.where(col == meta[:, 3 * TOP_K + kk:3 * TOP_K + kk + 1],
                            meta[:, TOP_K + kk:TOP_K + kk + 1], weights)
    acc = x1_ref[...] + _dot(weights.astype(BF16), ysort_ref[...])
    out_ref[...] = _rms(acc, g_ref[...])


def _combine_call(seg_src, seg_dst, seg_len, x1, meta, g, ys):
    t, d = x1.shape
    tb = SORT_TILE
    nb = t // tb
    grid_spec = pltpu.PrefetchScalarGridSpec(
        num_scalar_prefetch=3,
        grid=(nb,),
        in_specs=[pl.BlockSpec((tb, d), lambda i, a, b, c: (i, 0)),
                  pl.BlockSpec((tb, LANES), lambda i, a, b, c: (i, 0)),
                  pl.BlockSpec((1, d), lambda i, a, b, c: (0, 0)),
                  pl.BlockSpec(memory_space=pl.ANY)],
        out_specs=pl.BlockSpec((tb, d), lambda i, a, b, c: (i, 0)),
        scratch_shapes=[pltpu.VMEM((2, TOP_K * tb * ROW_TILES, LANES), U32),
                        pltpu.VMEM((TOP_K * tb, D_MODEL), BF16),
                        pltpu.SemaphoreType.DMA((2,))],
    )
    return pl.pallas_call(
        _combine_kernel,
        grid_spec=grid_spec,
        out_shape=jax.ShapeDtypeStruct((t, d), F32),
        compiler_params=pltpu.CompilerParams(
            dimension_semantics=("arbitrary",), vmem_limit_bytes=VMEM_LIMIT),
        name="combine",
    )(seg_src, seg_dst, seg_len, x1, meta, g, ys)


def _routing_tables(counts, t):
    bm = MOE_BLOCK
    n_blocks = t * TOP_K // bm + N_EXPERTS
    eids = jnp.arange(N_EXPERTS, dtype=I32)
    nblk_e = (counts + bm - 1) // bm
    blk_end = jnp.sum(jnp.where(eids[None, :] <= eids[:, None], nblk_e[None, :], 0), axis=1)
    blk_start = blk_end - nblk_e
    nreal = blk_end[N_EXPERTS - 1]
    pad_start = blk_start * bm
    blk = jnp.arange(n_blocks, dtype=I32)
    bexp = jnp.minimum(jnp.sum((blk_end[None, :] <= blk[:, None]).astype(I32), axis=1),
                       N_EXPERTS - 1)
    blk_is_e = bexp[:, None] == eids[None, :]
    pick = lambda tab: jnp.sum(jnp.where(blk_is_e, tab[None, :], 0), axis=1)
    first = (blk == pick(blk_start)).astype(I32)
    nxt_e = jnp.sum((blk_end[None, :] <= blk_end[:, None]).astype(I32), axis=1)
    nxt_e = jnp.where(blk_end < nreal, jnp.minimum(nxt_e, N_EXPERTS - 1), -1)
    nxt = pick(nxt_e)
    nval = jnp.clip(pick(counts) - (blk - pick(blk_start)) * bm, 0, bm)
    tail_blk = jnp.arange(N_EXPERTS, dtype=I32) + nreal
    pad_lo = jnp.concatenate([pad_start + counts, jnp.minimum(tail_blk, n_blocks - 1) * bm])
    pad_n = jnp.concatenate([nblk_e * bm - counts, jnp.where(tail_blk < n_blocks, bm, 0)])
    return (pad_start, bexp, first, nxt, nval, nreal.reshape(1).astype(I32), n_blocks,
            pad_lo * ROW_TILES, pad_n * ROW_TILES)


def kernel(x, norm_mix_g, w_in, conv_w, w_gk_up, b_gk_up, gla_norm_g, w_out, norm_ffn_g,
           w_router, b_router, w_gate_up, b_gate_up, w_down, b_down, norm_final_g):
    bsz, seq, d = x.shape
    t = bsz * seq
    assert w_in.shape[0] == 1, "single-layer trunk only"
    l = 0
    d_in = w_in.shape[-1]
    win = jnp.pad(w_in[l], ((0, 0), (0, D_IN_PAD - d_in))).astype(BF16)
    wgk = jnp.pad(w_gk_up[l], ((0, LANES - GLA_RANK), (0, 0))).astype(BF16)
    wr = jnp.pad(w_router[l], ((0, 0), (0, LANES - N_EXPERTS)))
    wrh = wr.astype(BF16)
    wrhl = jnp.concatenate([wrh, (wr - wrh.astype(F32)).astype(BF16)], axis=1)
    br = jnp.pad(b_router[l], (0, LANES - N_EXPERTS), constant_values=NEG_BIG).reshape(1, LANES)

    x1, h2, meta, route, seg, cnt = _mixer_call(
        x, norm_mix_g[l].reshape(1, d), win, conv_w[l], wgk, b_gk_up[l].reshape(1, GLA_QK),
        gla_norm_g[l].reshape(1, GLA_DV), w_out[l].astype(BF16), norm_ffn_g[l].reshape(1, d),
        wrhl, br)

    meta2 = meta.reshape(t, LANES)
    counts = cnt[0, :N_EXPERTS].astype(I32)
    (pad_start, bexp, first, nxt, nval, nreal, n_blocks, pad_lo,
     pad_n) = _routing_tables(counts, t)
    seg_src = ((pad_start[None, :] + seg[:, 0, :N_EXPERTS].astype(I32)) * ROW_TILES).reshape(-1)
    seg_len = (seg[:, 1, :N_EXPERTS].astype(I32) * ROW_TILES).reshape(-1)
    seg_dst = (seg[:, 2, :N_EXPERTS].astype(I32) * ROW_TILES).reshape(-1)
    xs = _dispatch_call(seg_src, seg_dst, seg_len, pad_lo, pad_n, route, h2.reshape(t, d),
                        n_blocks * MOE_BLOCK)
    ys = _experts_call(bexp, first, nxt, nval, nreal, xs, w_gate_up[l], b_gate_up[l], w_down[l],
                       b_down[l])
    out = _combine_call(seg_src, seg_dst, seg_len, x1.reshape(t, d), meta2,
                        norm_final_g.reshape(1, d), ys)
    return out.reshape(bsz, seq, d)
```

```python
import jax
import jax.numpy as jnp
from jax import lax
from jax.experimental import pallas as pl
from jax.experimental.pallas import tpu as pltpu

F32 = jnp.float32
BF16 = jnp.bfloat16
I32 = jnp.int32
U32 = jnp.uint32

D_MODEL = 1024
CONV_WIDTH = 512
CONV_K = 3
GLA_WIDTH = 512
GLA_HEADS = 4
GLA_DV = 128
GLA_DK = 64
GLA_QK = GLA_HEADS * GLA_DK
GLA_RANK = 16
GLA_NORMALIZER = 16.0
GLA_CHUNK = 64
N_EXPERTS = 32
TOP_K = 4
D_FF = 1024
SWIGLU_LIMIT = 7.0
SWIGLU_ALPHA = 1.702
RMS_EPS = 1e-5

LANES = 128
SUBLANES = 8
ROW_TILES = D_MODEL // LANES // 2
ROUTE_ROWS = 16

OFF_UH = 0
OFF_GB = OFF_UH + CONV_WIDTH
OFF_GC = OFF_GB + CONV_WIDTH
OFF_Q = OFF_GC + CONV_WIDTH
OFF_K = OFF_Q + GLA_QK
OFF_V = OFF_K + GLA_QK
OFF_GO = OFF_V + GLA_WIDTH
OFF_GKL = OFF_GO + GLA_WIDTH
D_IN_PAD = OFF_GKL + LANES

SEQ_TILE = 512
SORT_TILE = 256
MOE_BLOCK = 512
FF_CHUNK = 256
CAST_ROWS = 128
NEG_BIG = -1e30
VMEM_LIMIT = 56 * 1024 * 1024


def _rms(x, g):
    return x * lax.rsqrt(jnp.mean(x * x, axis=-1, keepdims=True) + RMS_EPS) * g


def _dot(a, b):
    return jnp.dot(a, b, preferred_element_type=F32)


def _dot_nt(a, b):
    return lax.dot_general(a, b, (((1,), (1,)), ((), ())), preferred_element_type=F32)


def _pack_rows(x):
    half = x.shape[1] // 2
    xr = x.astype(BF16).astype(F32)
    lo = lax.bitcast_convert_type(xr[:, :half], U32) >> 16
    hi = lax.bitcast_convert_type(xr[:, half:], U32) & jnp.uint32(0xFFFF0000)
    return hi | lo


def _unpack_words(w):
    lo = lax.bitcast_convert_type(w << 16, F32).astype(BF16)
    hi = lax.bitcast_convert_type(w & jnp.uint32(0xFFFF0000), F32).astype(BF16)
    return lo, hi


def _split_bf16(x):
    hi = x.astype(BF16)
    lo = (x - hi.astype(F32)).astype(BF16)
    return hi, lo


def _mixer_kernel(x_ref, g1_ref, win_ref, convw_ref, wgk_ref, bgk_ref, gng_ref, wout_ref,
                  g2_ref, wrhl_ref, br_ref,
                  x1_ref, h2_ref, meta_ref, route_ref, seg_ref, cnt_ref,
                  proj_ref, ubuf_ref, la_ref, ycat_ref, state_ref, carry_ref):
    ts = x_ref.shape[1]
    b_idx = pl.program_id(0)
    s_idx = pl.program_id(1)

    @pl.when(s_idx == 0)
    def _():
        state_ref[...] = jnp.zeros_like(state_ref)
        ubuf_ref[0:SUBLANES, :] = jnp.zeros((SUBLANES, CONV_WIDTH), F32)

    @pl.when((s_idx == 0) & (b_idx == 0))
    def _():
        carry_ref[...] = jnp.zeros_like(carry_ref)

    x = x_ref[0]
    h = _rms(x, g1_ref[...]).astype(BF16)
    proj_ref[...] = _dot(h, win_ref[...])

    u = proj_ref[:, OFF_GC:OFF_GC + CONV_WIDTH] * proj_ref[:, OFF_UH:OFF_UH + CONV_WIDTH]
    ubuf_ref[SUBLANES:SUBLANES + ts, :] = u
    u1 = ubuf_ref[pl.ds(SUBLANES - 1, ts), :]
    u2 = ubuf_ref[pl.ds(SUBLANES - 2, ts), :]
    conv = convw_ref[0:1, :] * u2 + convw_ref[1:2, :] * u1 + convw_ref[2:3, :] * u
    ycat_ref[:, 0:CONV_WIDTH] = (proj_ref[:, OFF_GB:OFF_GB + CONV_WIDTH] * conv).astype(BF16)
    ubuf_ref[0:SUBLANES, :] = ubuf_ref[ts:ts + SUBLANES, :]

    gk = _dot(proj_ref[:, OFF_GKL:OFF_GKL + LANES].astype(BF16), wgk_ref[...]) + bgk_ref[...]
    log_sig = jnp.minimum(gk, 0.0) - jnp.log1p(jnp.exp(-jnp.abs(gk)))
    la_ref[...] = log_sig / GLA_NORMALIZER

    ci = lax.broadcasted_iota(I32, (GLA_CHUNK, GLA_CHUNK), 0)
    cj = lax.broadcasted_iota(I32, (GLA_CHUNK, GLA_CHUNK), 1)
    tri_incl = (cj <= ci).astype(BF16)
    causal = cj <= ci
    causal4 = jnp.concatenate([causal] * GLA_HEADS, axis=0)
    lane_qk = lax.broadcasted_iota(I32, (1, GLA_QK), 1)
    head_masks = [((lane_qk >= hd * GLA_DK) & (lane_qk < (hd + 1) * GLA_DK)).astype(F32)
                  for hd in range(GLA_HEADS)]
    gng = gng_ref[...]

    def chunk_body(c, carry):
        r0 = pl.multiple_of(c * GLA_CHUNK, GLA_CHUNK)
        rows = pl.ds(r0, GLA_CHUNK)
        la_hi, la_lo = _split_bf16(la_ref[rows, :])
        bcum = _dot(tri_incl, la_hi) + _dot(tri_incl, la_lo)
        blast = bcum[GLA_CHUNK - 1:GLA_CHUNK, :]
        q = proj_ref[rows, OFF_Q:OFF_Q + GLA_QK] * (GLA_DK ** -0.5)
        k = proj_ref[rows, OFF_K:OFF_K + GLA_QK]
        v = proj_ref[rows, OFF_V:OFF_V + GLA_WIDTH].astype(BF16)
        qd = q * jnp.exp(bcum)
        kd = (k * jnp.exp(-bcum)).astype(BF16)
        kr = k * jnp.exp(blast - bcum)

        q_stack = jnp.concatenate([qd * m for m in head_masks], axis=0).astype(BF16)
        scores = jnp.where(causal4, _dot_nt(q_stack, kd), 0.0).astype(BF16)

        state = state_ref[...]
        o_inter = _dot(qd.astype(BF16), state.astype(BF16))
        o_intra = jnp.concatenate(
            [_dot(scores[hd * GLA_CHUNK:(hd + 1) * GLA_CHUNK, :],
                  v[:, hd * GLA_DV:(hd + 1) * GLA_DV]) for hd in range(GLA_HEADS)], axis=1)
        o = o_inter + o_intra

        kt = jnp.concatenate([kr, jnp.broadcast_to(blast, (GLA_CHUNK, GLA_QK))], axis=0).T
        dcol = jnp.exp(kt[:, GLA_CHUNK:GLA_CHUNK + 1])
        lane_c = lax.broadcasted_iota(I32, (GLA_QK, 2 * GLA_CHUNK), 1)
        kt_b = jnp.where(lane_c < GLA_CHUNK, kt, 0.0).astype(BF16)
        v_pad = jnp.concatenate([v, jnp.zeros_like(v)], axis=0)
        for hd in range(GLA_HEADS):
            rs = slice(hd * GLA_DK, (hd + 1) * GLA_DK)
            cs = slice(hd * GLA_DV, (hd + 1) * GLA_DV)
            kv = _dot(kt_b[rs, :], v_pad[:, cs])
            state_ref[rs, cs] = dcol[rs, :] * state[rs, cs] + kv

        g_out = proj_ref[rows, OFF_GO:OFF_GO + GLA_WIDTH]
        o_n = jnp.concatenate(
            [_rms(o[:, hd * GLA_DV:(hd + 1) * GLA_DV], gng) for hd in range(GLA_HEADS)], axis=1)
        y = o_n * (g_out * jax.nn.sigmoid(g_out))
        ycat_ref[rows, CONV_WIDTH:CONV_WIDTH + GLA_WIDTH] = y.astype(BF16)
        return carry

    lax.fori_loop(0, ts // GLA_CHUNK, chunk_body, 0, unroll=True)

    x1 = x + _dot(ycat_ref[...], wout_ref[...])
    x1_ref[0] = x1
    h2 = _rms(x1, g2_ref[...])
    h2_hi, h2_lo = _split_bf16(h2)
    h2_ref[0] = h2_hi
    hi_terms = _dot(h2_hi, wrhl_ref[...])
    logits = (hi_terms[:, 0:LANES] + hi_terms[:, LANES:2 * LANES]
              + _dot(h2_lo, wrhl_ref[:, 0:LANES]) + br_ref[...])

    lane = lax.broadcasted_iota(I32, (ts, LANES), 1).astype(F32)
    work = logits
    sel = jnp.zeros((ts, LANES), F32)
    top_v, top_i, top_oh = [], [], []
    for _ in range(TOP_K):
        m = jnp.max(work, axis=-1, keepdims=True)
        idx = jnp.min(jnp.where(work == m, lane, float(LANES)), axis=-1, keepdims=True)
        oh = lane == idx
        top_v.append(m)
        top_i.append(idx)
        top_oh.append(oh)
        sel = sel + oh.astype(F32)
        work = jnp.where(oh, -jnp.inf, work)
    exps = [jnp.exp(tv - top_v[0]) for tv in top_v]
    denom = exps[0] + exps[1] + exps[2] + exps[3]
    gates = [e / denom for e in exps]

    ti = lax.broadcasted_iota(I32, (ts, ts), 0)
    tj = lax.broadcasted_iota(I32, (ts, ts), 1)
    strict_lower = (tj < ti).astype(BF16)
    local = _dot(strict_lower, sel.astype(BF16))
    carry = carry_ref[0:1, :]
    ranks = [jnp.sum(jnp.where(oh, local + carry, 0.0), axis=-1, keepdims=True) for oh in top_oh]

    lane_row = lax.broadcasted_iota(I32, (1, LANES), 1)
    slot_base = []
    before = jnp.zeros((1, LANES), F32)
    for u in range(ts // SORT_TILE):
        sub_cnt = jnp.sum(sel[u * SORT_TILE:(u + 1) * SORT_TILE, :], axis=0, keepdims=True)
        seg_start = sub_cnt
        shift = 1
        while shift < LANES:
            seg_start = seg_start + jnp.where(lane_row >= shift,
                                              pltpu.roll(seg_start, shift, 1), 0.0)
            shift *= 2
        seg_start = seg_start - sub_cnt
        seg_ref[u, 0:1, :] = carry + before
        seg_ref[u, 1:2, :] = sub_cnt
        seg_ref[u, 2:3, :] = seg_start
        seg_ref[u, 3:SUBLANES, :] = jnp.zeros((SUBLANES - 3, LANES), F32)
        slot_base.append(jnp.broadcast_to(seg_start - before, (SORT_TILE, LANES)))
        before = before + sub_cnt
    slot_base = jnp.concatenate(slot_base, axis=0)
    slots = [jnp.sum(jnp.where(oh, local + slot_base, 0.0), axis=-1, keepdims=True)
             for oh in top_oh]
    new_carry = carry + before
    carry_ref[...] = jnp.broadcast_to(new_carry, carry_ref.shape)
    cnt_ref[...] = jnp.broadcast_to(new_carry, cnt_ref.shape)

    meta = jnp.zeros((ts, LANES), F32)
    for kk in range(TOP_K):
        meta = jnp.where(lane == float(kk), top_i[kk], meta)
        meta = jnp.where(lane == float(TOP_K + kk), gates[kk], meta)
        meta = jnp.where(lane == float(2 * TOP_K + kk), ranks[kk], meta)
        meta = jnp.where(lane == float(3 * TOP_K + kk), slots[kk], meta)
    meta_ref[0] = meta
    route_ref[...] = meta.T[0:ROUTE_ROWS, :]


def _mixer_call(x, g1, win, convw, wgk, bgk, gng, wout, g2, wrhl, br):
    bsz, seq, d = x.shape
    ts = SEQ_TILE
    grid = (bsz, seq // ts)

    def const(shape):
        return pl.BlockSpec(shape, lambda b, s: (0,) * len(shape))

    tile = lambda w: pl.BlockSpec((1, ts, w), lambda b, s: (b, s, 0))
    return pl.pallas_call(
        _mixer_kernel,
        grid=grid,
        in_specs=[tile(d), const(g1.shape), const(win.shape), const(convw.shape),
                  const(wgk.shape), const(bgk.shape), const(gng.shape), const(wout.shape),
                  const(g2.shape), const(wrhl.shape), const(br.shape)],
        out_specs=[tile(d),
                   tile(d),
                   tile(LANES),
                   pl.BlockSpec((ROUTE_ROWS, ts), lambda b, s: (0, b * (seq // ts) + s)),
                   pl.BlockSpec((ts // SORT_TILE, SUBLANES, LANES),
                                lambda b, s: (b * (seq // ts) + s, 0, 0)),
                   const((SUBLANES, LANES))],
        out_shape=[jax.ShapeDtypeStruct((bsz, seq, d), F32),
                   jax.ShapeDtypeStruct((bsz, seq, d), BF16),
                   jax.ShapeDtypeStruct((bsz, seq, LANES), F32),
                   jax.ShapeDtypeStruct((ROUTE_ROWS, bsz * seq), F32),
                   jax.ShapeDtypeStruct((bsz * seq // SORT_TILE, SUBLANES, LANES), F32),
                   jax.ShapeDtypeStruct((SUBLANES, LANES), F32)],
        scratch_shapes=[pltpu.VMEM((ts, D_IN_PAD), F32),
                        pltpu.VMEM((ts + SUBLANES, CONV_WIDTH), F32),
                        pltpu.VMEM((ts, GLA_QK), F32),
                        pltpu.VMEM((ts, D_MODEL), BF16),
                        pltpu.VMEM((GLA_QK, GLA_WIDTH), F32),
                        pltpu.VMEM((SUBLANES, LANES), F32)],
        compiler_params=pltpu.CompilerParams(
            dimension_semantics=("arbitrary", "arbitrary"), vmem_limit_bytes=VMEM_LIMIT),
        name="mixer",
    )(x, g1, win, convw, wgk, bgk, gng, wout, g2, wrhl, br)


def _dispatch_kernel(run_xs_ref, run_buf_ref, run_len_ref, pad_lo_ref, pad_n_ref,
                     route_ref, h2_ref, xs_hbm, sbuf, zbuf, sem, zsem):
    i = pl.program_id(0)
    n = pl.num_programs(0)
    td = h2_ref.shape[0]
    n_rows = TOP_K * td
    slot = i % 2

    def zero_fill(wait):
        def fill(g, carry):
            ln = pl.multiple_of(pad_n_ref[g], ROW_TILES)

            @pl.when(ln > 0)
            def _():
                lo = pl.multiple_of(pad_lo_ref[g], ROW_TILES)
                cp = pltpu.make_async_copy(zbuf.at[pl.ds(0, ln), :], xs_hbm.at[pl.ds(lo, ln), :],
                                           zsem.at[0])
                if wait:
                    cp.wait()
                else:
                    cp.start()
            return carry

        lax.fori_loop(0, pad_lo_ref.shape[0], fill, 0)

    def wait_runs(buf_slot):
        pltpu.make_async_copy(sbuf.at[buf_slot], xs_hbm.at[pl.ds(0, n_rows * ROW_TILES), :],
                              sem.at[buf_slot]).wait()

    @pl.when(i == 0)
    def _():
        zbuf[...] = jnp.zeros_like(zbuf)
        zero_fill(wait=False)

    @pl.when(i >= 2)
    def _():
        wait_runs(slot)

    route = route_ref[...]
    row = lax.broadcasted_iota(I32, (n_rows, td), 0).astype(F32)
    pick = jnp.zeros((n_rows, td), F32)
    for kk in range(TOP_K):
        pick = jnp.where(row == route[3 * TOP_K + kk:3 * TOP_K + kk + 1, :], 1.0, pick)
    rows_sorted = _dot(pick.astype(BF16), h2_ref[...])
    words = _pack_rows(rows_sorted)
    for j in range(ROW_TILES):
        sbuf[slot, pl.ds(j, n_rows, stride=ROW_TILES), :] = words[:, j * LANES:(j + 1) * LANES]

    def body(e, carry):
        g = i * N_EXPERTS + e
        ln = pl.multiple_of(run_len_ref[g], ROW_TILES)

        @pl.when(ln > 0)
        def _():
            src = pl.multiple_of(run_buf_ref[g], ROW_TILES)
            dst = pl.multiple_of(run_xs_ref[g], ROW_TILES)
            pltpu.make_async_copy(sbuf.at[slot, pl.ds(src, ln), :], xs_hbm.at[pl.ds(dst, ln), :],
                                  sem.at[slot]).start()
        return carry

    lax.fori_loop(0, N_EXPERTS, body, 0)

    @pl.when(i == n - 1)
    def _():
        @pl.when(i >= 1)
        def _():
            wait_runs(1 - slot)
        wait_runs(slot)
        zero_fill(wait=True)


def _dispatch_call(run_xs, run_buf, run_len, pad_lo, pad_n, route, h2, n_rows):
    t, d = h2.shape
    td = SORT_TILE
    grid_spec = pltpu.PrefetchScalarGridSpec(
        num_scalar_prefetch=5,
        grid=(t // td,),
        in_specs=[pl.BlockSpec((ROUTE_ROWS, td), lambda i, *_: (0, i)),
                  pl.BlockSpec((td, d), lambda i, *_: (i, 0))],
        out_specs=pl.BlockSpec(memory_space=pl.ANY),
        scratch_shapes=[pltpu.VMEM((2, TOP_K * td * ROW_TILES, LANES), U32),
                        pltpu.VMEM((MOE_BLOCK * ROW_TILES, LANES), U32),
                        pltpu.SemaphoreType.DMA((2,)),
                        pltpu.SemaphoreType.DMA((1,))],
    )
    return pl.pallas_call(
        _dispatch_kernel,
        grid_spec=grid_spec,
        out_shape=jax.ShapeDtypeStruct((n_rows * ROW_TILES, LANES), U32),
        compiler_params=pltpu.CompilerParams(
            dimension_semantics=("arbitrary",), vmem_limit_bytes=VMEM_LIMIT,
            has_side_effects=True),
        name="dispatch",
    )(run_xs, run_buf, run_len, pad_lo, pad_n, route, h2)


def _experts_kernel(bexp_ref, first_ref, nxt_ref, nval_ref, nreal_ref,
                    xs_ref, wgu_hbm, bgu_ref, wd_hbm, bd_ref,
                    ys_ref,
                    xb_ref, act_ref, wgu_stage, wd_stage, wgu_bf, wd_bf, wsem):
    i = pl.program_id(0)
    nreal = nreal_ref[0]
    bm = xb_ref.shape[0]

    def weight_copies(e):
        return (pltpu.make_async_copy(wgu_hbm.at[e], wgu_stage, wsem.at[0]),
                pltpu.make_async_copy(wd_hbm.at[e], wd_stage, wsem.at[1]))

    @pl.when(i >= nreal)
    def _():
        ys_ref[...] = jnp.zeros_like(ys_ref)

    @pl.when(i < nreal)
    def _():
        e = bexp_ref[i]

        @pl.when(i == 0)
        def _():
            for cp in weight_copies(e):
                cp.start(priority=1)

        @pl.when(first_ref[i] == 1)
        def _():
            for cp in weight_copies(e):
                cp.wait()

            def cast_gu(r, carry):
                rows = pl.ds(pl.multiple_of(r * CAST_ROWS, CAST_ROWS), CAST_ROWS)
                wgu_bf[rows, :] = wgu_stage[rows, :].astype(BF16)
                return carry

            def cast_d(r, carry):
                rows = pl.ds(pl.multiple_of(r * CAST_ROWS, CAST_ROWS), CAST_ROWS)
                wd_bf[rows, :] = wd_stage[rows, :].astype(BF16)
                return carry

            lax.fori_loop(0, D_MODEL // CAST_ROWS, cast_gu, 0)
            lax.fori_loop(0, D_FF // CAST_ROWS, cast_d, 0)

            @pl.when(nxt_ref[i] >= 0)
            def _():
                for cp in weight_copies(nxt_ref[i]):
                    cp.start(priority=1)

        def mlp(rows):
            for j in range(ROW_TILES):
                lo, hi = _unpack_words(xs_ref[pl.ds(j, rows, stride=ROW_TILES), :])
                xb_ref[0:rows, j * LANES:(j + 1) * LANES] = lo
                xb_ref[0:rows, D_MODEL // 2 + j * LANES:D_MODEL // 2 + (j + 1) * LANES] = hi
            for c in range(D_FF // FF_CHUNK):
                f0 = c * FF_CHUNK
                xb = xb_ref[0:rows, :]
                gate = _dot(xb, wgu_bf[:, f0:f0 + FF_CHUNK]) + bgu_ref[0, :, f0:f0 + FF_CHUNK]
                up = (_dot(xb, wgu_bf[:, D_FF + f0:D_FF + f0 + FF_CHUNK])
                      + bgu_ref[0, :, D_FF + f0:D_FF + f0 + FF_CHUNK])
                gate = jnp.minimum(gate, SWIGLU_LIMIT)
                up = jnp.clip(up, -SWIGLU_LIMIT, SWIGLU_LIMIT)
                glu = gate * jax.nn.sigmoid(gate * SWIGLU_ALPHA)
                act_ref[0:rows, f0:f0 + FF_CHUNK] = ((up + 1.0) * glu).astype(BF16)
            out = _pack_rows(_dot(act_ref[0:rows, :], wd_bf[...]) + bd_ref[0])
            for j in range(ROW_TILES):
                ys_ref[pl.ds(j, rows, stride=ROW_TILES), :] = out[:, j * LANES:(j + 1) * LANES]

        @pl.when(nval_ref[i] > bm // 2)
        def _():
            mlp(bm)

        @pl.when(nval_ref[i] <= bm // 2)
        def _():
            mlp(bm // 2)
            ys_ref[bm // 2 * ROW_TILES:bm * ROW_TILES, :] = jnp.zeros(
                (bm // 2 * ROW_TILES, LANES), U32)


def _experts_call(bexp, first, nxt, nval, nreal, xs, wgu, bgu, wd, bd):
    bm = MOE_BLOCK
    n_blocks = xs.shape[0] // (bm * ROW_TILES)
    bgu3 = bgu.reshape(N_EXPERTS, 1, 2 * D_FF)
    bd3 = bd.reshape(N_EXPERTS, 1, D_MODEL)
    grid_spec = pltpu.PrefetchScalarGridSpec(
        num_scalar_prefetch=5,
        grid=(n_blocks,),
        in_specs=[
            pl.BlockSpec((bm * ROW_TILES, LANES),
                         lambda i, be, fi, nx, nv, nr: (jnp.minimum(i, nr[0] - 1), 0)),
            pl.BlockSpec(memory_space=pl.ANY),
            pl.BlockSpec((1, 1, 2 * D_FF), lambda i, be, fi, nx, nv, nr: (be[i], 0, 0)),
            pl.BlockSpec(memory_space=pl.ANY),
            pl.BlockSpec((1, 1, D_MODEL), lambda i, be, fi, nx, nv, nr: (be[i], 0, 0)),
        ],
        out_specs=pl.BlockSpec((bm * ROW_TILES, LANES), lambda i, be, fi, nx, nv, nr: (i, 0)),
        scratch_shapes=[pltpu.VMEM((bm, D_MODEL), BF16),
                        pltpu.VMEM((bm, D_FF), BF16),
                        pltpu.VMEM((D_MODEL, 2 * D_FF), F32),
                        pltpu.VMEM((D_FF, D_MODEL), F32),
                        pltpu.VMEM((D_MODEL, 2 * D_FF), BF16),
                        pltpu.VMEM((D_FF, D_MODEL), BF16),
                        pltpu.SemaphoreType.DMA((2,))],
    )
    return pl.pallas_call(
        _experts_kernel,
        grid_spec=grid_spec,
        out_shape=jax.ShapeDtypeStruct(xs.shape, U32),
        compiler_params=pltpu.CompilerParams(
            dimension_semantics=("arbitrary",), vmem_limit_bytes=VMEM_LIMIT),
        name="experts",
    )(bexp, first, nxt, nval, nreal, xs, wgu, bgu3, wd, bd3)


def _combine_kernel(src_ref, dst_ref, len_ref, x1_ref, meta_ref, g_ref, ys_hbm, out_ref,
                    ybuf, ysort_ref, sem):
    i = pl.program_id(0)
    n = pl.num_programs(0)
    tb = x1_ref.shape[0]
    n_rows = TOP_K * tb
    slot = i % 2

    def start_runs(tile, buf_slot):
        def body(e, carry):
            g = tile * N_EXPERTS + e
            ln = pl.multiple_of(len_ref[g], ROW_TILES)

            @pl.when(ln > 0)
            def _():
                src = pl.multiple_of(src_ref[g], ROW_TILES)
                dst = pl.multiple_of(dst_ref[g], ROW_TILES)
                pltpu.make_async_copy(ys_hbm.at[pl.ds(src, ln), :],
                                      ybuf.at[buf_slot, pl.ds(dst, ln), :], sem.at[buf_slot]).start()
            return carry
        lax.fori_loop(0, N_EXPERTS, body, 0)

    @pl.when(i == 0)
    def _():
        start_runs(0, 0)

    @pl.when(i + 1 < n)
    def _():
        start_runs(i + 1, 1 - slot)

    pltpu.make_async_copy(ys_hbm.at[pl.ds(0, n_rows * ROW_TILES), :], ybuf.at[slot],
                          sem.at[slot]).wait()

    for j in range(ROW_TILES):
        lo, hi = _unpack_words(ybuf[slot, pl.ds(j, n_rows, stride=ROW_TILES), :])
        ysort_ref[:, j * LANES:(j + 1) * LANES] = lo
        ysort_ref[:, D_MODEL // 2 + j * LANES:D_MODEL // 2 + (j + 1) * LANES] = hi

    meta = meta_ref[...]
    col = lax.broadcasted_iota(I32, (tb, n_rows), 1).astype(F32)
    weights = jnp.zeros((tb, n_rows), F32)
    for kk in range(TOP_K):
        weights = jnp.where(col == meta[:, 3 * TOP_K + kk:3 * TOP_K + kk + 1],
                            meta[:, TOP_K + kk:TOP_K + kk + 1], weights)
    acc = x1_ref[...] + _dot(weights.astype(BF16), ysort_ref[...])
    out_ref[...] = _rms(acc, g_ref[...])


def _combine_call(seg_src, seg_dst, seg_len, x1, meta, g, ys):
    t, d = x1.shape
    tb = SORT_TILE
    nb = t // tb
    grid_spec = pltpu.PrefetchScalarGridSpec(
        num_scalar_prefetch=3,
        grid=(nb,),
        in_specs=[pl.BlockSpec((tb, d), lambda i, a, b, c: (i, 0)),
                  pl.BlockSpec((tb, LANES), lambda i, a, b, c: (i, 0)),
                  pl.BlockSpec((1, d), lambda i, a, b, c: (0, 0)),
                  pl.BlockSpec(memory_space=pl.ANY)],
        out_specs=pl.BlockSpec((tb, d), lambda i, a, b, c: (i, 0)),
        scratch_shapes=[pltpu.VMEM((2, TOP_K * tb * ROW_TILES, LANES), U32),
                        pltpu.VMEM((TOP_K * tb, D_MODEL), BF16),
                        pltpu.SemaphoreType.DMA((2,))],
    )
    return pl.pallas_call(
        _combine_kernel,
        grid_spec=grid_spec,
        out_shape=jax.ShapeDtypeStruct((t, d), F32),
        compiler_params=pltpu.CompilerParams(
            dimension_semantics=("arbitrary",), vmem_limit_bytes=VMEM_LIMIT),
        name="combine",
    )(seg_src, seg_dst, seg_len, x1, meta, g, ys)


def _routing_tables(counts, t):
    bm = MOE_BLOCK
    n_blocks = t * TOP_K // bm + N_EXPERTS
    eids = jnp.arange(N_EXPERTS, dtype=I32)
    nblk_e = (counts + bm - 1) // bm
    blk_end = jnp.sum(jnp.where(eids[None, :] <= eids[:, None], nblk_e[None, :], 0), axis=1)
    blk_start = blk_end - nblk_e
    nreal = blk_end[N_EXPERTS - 1]
    pad_start = blk_start * bm
    blk = jnp.arange(n_blocks, dtype=I32)
    bexp = jnp.minimum(jnp.sum((blk_end[None, :] <= blk[:, None]).astype(I32), axis=1),
                       N_EXPERTS - 1)
    blk_is_e = bexp[:, None] == eids[None, :]
    pick = lambda tab: jnp.sum(jnp.where(blk_is_e, tab[None, :], 0), axis=1)
    first = (blk == pick(blk_start)).astype(I32)
    nxt_e = jnp.sum((blk_end[None, :] <= blk_end[:, None]).astype(I32), axis=1)
    nxt_e = jnp.where(blk_end < nreal, jnp.minimum(nxt_e, N_EXPERTS - 1), -1)
    nxt = pick(nxt_e)
    nval = jnp.clip(pick(counts) - (blk - pick(blk_start)) * bm, 0, bm)
    tail_blk = jnp.arange(N_EXPERTS, dtype=I32) + nreal
    pad_lo = jnp.concatenate([pad_start + counts, jnp.minimum(tail_blk, n_blocks - 1) * bm])
    pad_n = jnp.concatenate([nblk_e * bm - counts, jnp.where(tail_blk < n_blocks, bm, 0)])
    return (pad_start, bexp, first, nxt, nval, nreal.reshape(1).astype(I32), n_blocks,
            pad_lo * ROW_TILES, pad_n * ROW_TILES)


def kernel(x, norm_mix_g, w_in, conv_w, w_gk_up, b_gk_up, gla_norm_g, w_out, norm_ffn_g,
           w_router, b_router, w_gate_up, b_gate_up, w_down, b_down, norm_final_g):
    bsz, seq, d = x.shape
    t = bsz * seq
    assert w_in.shape[0] == 1, "single-layer trunk only"
    l = 0
    d_in = w_in.shape[-1]
    win = jnp.pad(w_in[l], ((0, 0), (0, D_IN_PAD - d_in))).astype(BF16)
    wgk = jnp.pad(w_gk_up[l], ((0, LANES - GLA_RANK), (0, 0))).astype(BF16)
    wr = jnp.pad(w_router[l], ((0, 0), (0, LANES - N_EXPERTS)))
    wrh = wr.astype(BF16)
    wrhl = jnp.concatenate([wrh, (wr - wrh.astype(F32)).astype(BF16)], axis=1)
    br = jnp.pad(b_router[l], (0, LANES - N_EXPERTS), constant_values=NEG_BIG).reshape(1, LANES)

    x1, h2, meta, route, seg, cnt = _mixer_call(
        x, norm_mix_g[l].reshape(1, d), win, conv_w[l], wgk, b_gk_up[l].reshape(1, GLA_QK),
        gla_norm_g[l].reshape(1, GLA_DV), w_out[l].astype(BF16), norm_ffn_g[l].reshape(1, d),
        wrhl, br)

    meta2 = meta.reshape(t, LANES)
    counts = cnt[0, :N_EXPERTS].astype(I32)
    (pad_start, bexp, first, nxt, nval, nreal, n_blocks, pad_lo,
     pad_n) = _routing_tables(counts, t)
    seg_src = ((pad_start[None, :] + seg[:, 0, :N_EXPERTS].astype(I32)) * ROW_TILES).reshape(-1)
    seg_len = (seg[:, 1, :N_EXPERTS].astype(I32) * ROW_TILES).reshape(-1)
    seg_dst = (seg[:, 2, :N_EXPERTS].astype(I32) * ROW_TILES).reshape(-1)
    xs = _dispatch_call(seg_src, seg_dst, seg_len, pad_lo, pad_n, route, h2.reshape(t, d),
                        n_blocks * MOE_BLOCK)
    ys = _experts_call(bexp, first, nxt, nval, nreal, xs, w_gate_up[l], b_gate_up[l], w_down[l],
                       b_down[l])
    out = _combine_call(seg_src, seg_dst, seg_len, x1.reshape(t, d), meta2,
                        norm_final_g.reshape(1, d), ys)
    return out.reshape(bsz, seq, d)
```

```python
import jax
import jax.numpy as jnp
from jax import lax
from jax.experimental import pallas as pl
from jax.experimental.pallas import tpu as pltpu

F32 = jnp.float32
BF16 = jnp.bfloat16
I32 = jnp.int32
U32 = jnp.uint32

D_MODEL = 1024
CONV_WIDTH = 512
CONV_K = 3
GLA_WIDTH = 512
GLA_HEADS = 4
GLA_DV = 128
GLA_DK = 64
GLA_QK = GLA_HEADS * GLA_DK
GLA_RANK = 16
GLA_NORMALIZER = 16.0
GLA_CHUNK = 64
N_EXPERTS = 32
TOP_K = 4
D_FF = 1024
SWIGLU_LIMIT = 7.0
SWIGLU_ALPHA = 1.702
RMS_EPS = 1e-5

LANES = 128
SUBLANES = 8
ROW_TILES = D_MODEL // LANES // 2
ROUTE_ROWS = 16

OFF_UH = 0
OFF_GB = OFF_UH + CONV_WIDTH
OFF_GC = OFF_GB + CONV_WIDTH
OFF_Q = OFF_GC + CONV_WIDTH
OFF_K = OFF_Q + GLA_QK
OFF_V = OFF_K + GLA_QK
OFF_GO = OFF_V + GLA_WIDTH
OFF_GKL = OFF_GO + GLA_WIDTH
D_IN_PAD = OFF_GKL + LANES

SEQ_TILE = 512
SORT_TILE = 256
MOE_BLOCK = 512
FF_CHUNK = 256
CAST_ROWS = 128
NEG_BIG = -1e30
VMEM_LIMIT = 56 * 1024 * 1024


def _rms(x, g):
    return x * lax.rsqrt(jnp.mean(x * x, axis=-1, keepdims=True) + RMS_EPS) * g


def _dot(a, b):
    return jnp.dot(a, b, preferred_element_type=F32)


def _dot_nt(a, b):
    return lax.dot_general(a, b, (((1,), (1,)), ((), ())), preferred_element_type=F32)


def _pack_rows(x):
    half = x.shape[1] // 2
    xr = x.astype(BF16).astype(F32)
    lo = lax.bitcast_convert_type(xr[:, :half], U32) >> 16
    hi = lax.bitcast_convert_type(xr[:, half:], U32) & jnp.uint32(0xFFFF0000)
    return hi | lo


def _unpack_words(w):
    lo = lax.bitcast_convert_type(w << 16, F32).astype(BF16)
    hi = lax.bitcast_convert_type(w & jnp.uint32(0xFFFF0000), F32).astype(BF16)
    return lo, hi


def _split_bf16(x):
    hi = x.astype(BF16)
    lo = (x - hi.astype(F32)).astype(BF16)
    return hi, lo


def _mixer_kernel(x_ref, g1_ref, win_ref, convw_ref, wgk_ref, bgk_ref, gng_ref, wout_ref,
                  g2_ref, wrhl_ref, br_ref,
                  x1_ref, h2_ref, meta_ref, route_ref, seg_ref, cnt_ref,
                  proj_ref, ubuf_ref, la_ref, ycat_ref, state_ref, carry_ref):
    ts = x_ref.shape[1]
    b_idx = pl.program_id(0)
    s_idx = pl.program_id(1)

    @pl.when(s_idx == 0)
    def _():
        state_ref[...] = jnp.zeros_like(state_ref)
        ubuf_ref[0:SUBLANES, :] = jnp.zeros((SUBLANES, CONV_WIDTH), F32)

    @pl.when((s_idx == 0) & (b_idx == 0))
    def _():
        carry_ref[...] = jnp.zeros_like(carry_ref)

    x = x_ref[0]
    h = _rms(x, g1_ref[...]).astype(BF16)
    proj_ref[...] = _dot(h, win_ref[...])

    u = proj_ref[:, OFF_GC:OFF_GC + CONV_WIDTH] * proj_ref[:, OFF_UH:OFF_UH + CONV_WIDTH]
    ubuf_ref[SUBLANES:SUBLANES + ts, :] = u
    u1 = ubuf_ref[pl.ds(SUBLANES - 1, ts), :]
    u2 = ubuf_ref[pl.ds(SUBLANES - 2, ts), :]
    conv = convw_ref[0:1, :] * u2 + convw_ref[1:2, :] * u1 + convw_ref[2:3, :] * u
    ycat_ref[:, 0:CONV_WIDTH] = (proj_ref[:, OFF_GB:OFF_GB + CONV_WIDTH] * conv).astype(BF16)
    ubuf_ref[0:SUBLANES, :] = ubuf_ref[ts:ts + SUBLANES, :]

    gk = _dot(proj_ref[:, OFF_GKL:OFF_GKL + LANES].astype(BF16), wgk_ref[...]) + bgk_ref[...]
    log_sig = jnp.minimum(gk, 0.0) - jnp.log1p(jnp.exp(-jnp.abs(gk)))
    la_ref[...] = log_sig / GLA_NORMALIZER

    ci = lax.broadcasted_iota(I32, (GLA_CHUNK, GLA_CHUNK), 0)
    cj = lax.broadcasted_iota(I32, (GLA_CHUNK, GLA_CHUNK), 1)
    tri_incl = (cj <= ci).astype(BF16)
    causal = cj <= ci
    causal4 = jnp.concatenate([causal] * GLA_HEADS, axis=0)
    lane_qk = lax.broadcasted_iota(I32, (1, GLA_QK), 1)
    head_masks = [((lane_qk >= hd * GLA_DK) & (lane_qk < (hd + 1) * GLA_DK)).astype(F32)
                  for hd in range(GLA_HEADS)]
    gng = gng_ref[...]

    n_chunks = ts // GLA_CHUNK
    chunk_rows = [pl.ds(c * GLA_CHUNK, GLA_CHUNK) for c in range(n_chunks)]
    lane_c = lax.broadcasted_iota(I32, (GLA_QK, 2 * GLA_CHUNK), 1)
    qd_all, kd_all, kr_all, bl_all, v_all = [], [], [], [], []
    for rows in chunk_rows:
        la_hi, la_lo = _split_bf16(la_ref[rows, :])
        bcum = _dot(tri_incl, la_hi) + _dot(tri_incl, la_lo)
        blast = bcum[GLA_CHUNK - 1:GLA_CHUNK, :]
        q = proj_ref[rows, OFF_Q:OFF_Q + GLA_QK] * (GLA_DK ** -0.5)
        k = proj_ref[rows, OFF_K:OFF_K + GLA_QK]
        qd_all.append(q * jnp.exp(bcum))
        kd_all.append((k * jnp.exp(-bcum)).astype(BF16))
        kr_all.append(k * jnp.exp(blast - bcum))
        bl_all.append(blast)
        v_all.append(proj_ref[rows, OFF_V:OFF_V + GLA_WIDTH].astype(BF16))

    scores_all = []
    for c in range(n_chunks):
        q_stack = jnp.concatenate([qd_all[c] * m for m in head_masks], axis=0).astype(BF16)
        scores_all.append(
            jnp.where(causal4, _dot_nt(q_stack, kd_all[c]), 0.0).astype(BF16))
    o_intra_all = []
    for c in range(n_chunks):
        o_intra_all.append(jnp.concatenate(
            [_dot(scores_all[c][hd * GLA_CHUNK:(hd + 1) * GLA_CHUNK, :],
                  v_all[c][:, hd * GLA_DV:(hd + 1) * GLA_DV]) for hd in range(GLA_HEADS)], axis=1))
    kv_all, dcol_all = [], []
    for c in range(n_chunks):
        kt = jnp.concatenate(
            [kr_all[c], jnp.broadcast_to(bl_all[c], (GLA_CHUNK, GLA_QK))], axis=0).T
        dcol_all.append(jnp.exp(kt[:, GLA_CHUNK:GLA_CHUNK + 1]))
        kt_b = jnp.where(lane_c < GLA_CHUNK, kt, 0.0).astype(BF16)
        v_pad = jnp.concatenate([v_all[c], jnp.zeros_like(v_all[c])], axis=0)
        kv_all.append([_dot(kt_b[hd * GLA_DK:(hd + 1) * GLA_DK, :],
                            v_pad[:, hd * GLA_DV:(hd + 1) * GLA_DV]) for hd in range(GLA_HEADS)])

    o_all = []
    for c in range(n_chunks):
        state = state_ref[...]
        o_all.append(_dot(qd_all[c].astype(BF16), state.astype(BF16)) + o_intra_all[c])
        for hd in range(GLA_HEADS):
            rs = slice(hd * GLA_DK, (hd + 1) * GLA_DK)
            cs = slice(hd * GLA_DV, (hd + 1) * GLA_DV)
            state_ref[rs, cs] = dcol_all[c][rs, :] * state[rs, cs] + kv_all[c][hd]

    for c, rows in enumerate(chunk_rows):
        o = o_all[c]
        g_out = proj_ref[rows, OFF_GO:OFF_GO + GLA_WIDTH]
        o_n = jnp.concatenate(
            [_rms(o[:, hd * GLA_DV:(hd + 1) * GLA_DV], gng) for hd in range(GLA_HEADS)], axis=1)
        y = o_n * (g_out * jax.nn.sigmoid(g_out))
        ycat_ref[rows, CONV_WIDTH:CONV_WIDTH + GLA_WIDTH] = y.astype(BF16)

    x1 = x + _dot(ycat_ref[...], wout_ref[...])
    x1_ref[0] = x1
    h2 = _rms(x1, g2_ref[...])
    h2_hi, h2_lo = _split_bf16(h2)
    h2_ref[0] = h2_hi
    hi_terms = _dot(h2_hi, wrhl_ref[...])
    logits = (hi_terms[:, 0:LANES] + hi_terms[:, LANES:2 * LANES]
              + _dot(h2_lo, wrhl_ref[:, 0:LANES]) + br_ref[...])

    lane = lax.broadcasted_iota(I32, (ts, LANES), 1).astype(F32)
    work = logits
    sel = jnp.zeros((ts, LANES), F32)
    top_v, top_i, top_oh = [], [], []
    for _ in range(TOP_K):
        m = jnp.max(work, axis=-1, keepdims=True)
        idx = jnp.min(jnp.where(work == m, lane, float(LANES)), axis=-1, keepdims=True)
        oh = lane == idx
        top_v.append(m)
        top_i.append(idx)
        top_oh.append(oh)
        sel = sel + oh.astype(F32)
        work = jnp.where(oh, -jnp.inf, work)
    exps = [jnp.exp(tv - top_v[0]) for tv in top_v]
    denom = exps[0] + exps[1] + exps[2] + exps[3]
    gates = [e / denom for e in exps]

    ti = lax.broadcasted_iota(I32, (ts, ts), 0)
    tj = lax.broadcasted_iota(I32, (ts, ts), 1)
    strict_lower = (tj < ti).astype(BF16)
    local = _dot(strict_lower, sel.astype(BF16))
    carry = carry_ref[0:1, :]
    ranks = [jnp.sum(jnp.where(oh, local + carry, 0.0), axis=-1, keepdims=True) for oh in top_oh]

    lane_row = lax.broadcasted_iota(I32, (1, LANES), 1)
    slot_base = []
    before = jnp.zeros((1, LANES), F32)
    for u in range(ts // SORT_TILE):
        sub_cnt = jnp.sum(sel[u * SORT_TILE:(u + 1) * SORT_TILE, :], axis=0, keepdims=True)
        seg_start = sub_cnt
        shift = 1
        while shift < LANES:
            seg_start = seg_start + jnp.where(lane_row >= shift,
                                              pltpu.roll(seg_start, shift, 1), 0.0)
            shift *= 2
        seg_start = seg_start - sub_cnt
        seg_ref[u, 0:1, :] = carry + before
        seg_ref[u, 1:2, :] = sub_cnt
        seg_ref[u, 2:3, :] = seg_start
        seg_ref[u, 3:SUBLANES, :] = jnp.zeros((SUBLANES - 3, LANES), F32)
        slot_base.append(jnp.broadcast_to(seg_start - before, (SORT_TILE, LANES)))
        before = before + sub_cnt
    slot_base = jnp.concatenate(slot_base, axis=0)
    slots = [jnp.sum(jnp.where(oh, local + slot_base, 0.0), axis=-1, keepdims=True)
             for oh in top_oh]
    new_carry = carry + before
    carry_ref[...] = jnp.broadcast_to(new_carry, carry_ref.shape)
    cnt_ref[...] = jnp.broadcast_to(new_carry, cnt_ref.shape)

    meta = jnp.zeros((ts, LANES), F32)
    for kk in range(TOP_K):
        meta = jnp.where(lane == float(kk), top_i[kk], meta)
        meta = jnp.where(lane == float(TOP_K + kk), gates[kk], meta)
        meta = jnp.where(lane == float(2 * TOP_K + kk), ranks[kk], meta)
        meta = jnp.where(lane == float(3 * TOP_K + kk), slots[kk], meta)
    meta_ref[0] = meta
    route_ref[...] = meta.T[0:ROUTE_ROWS, :]


def _mixer_call(x, g1, win, convw, wgk, bgk, gng, wout, g2, wrhl, br):
    bsz, seq, d = x.shape
    ts = SEQ_TILE
    grid = (bsz, seq // ts)

    def const(shape):
        return pl.BlockSpec(shape, lambda b, s: (0,) * len(shape))

    tile = lambda w: pl.BlockSpec((1, ts, w), lambda b, s: (b, s, 0))
    return pl.pallas_call(
        _mixer_kernel,
        grid=grid,
        in_specs=[tile(d), const(g1.shape), const(win.shape), const(convw.shape),
                  const(wgk.shape), const(bgk.shape), const(gng.shape), const(wout.shape),
                  const(g2.shape), const(wrhl.shape), const(br.shape)],
        out_specs=[tile(d),
                   tile(d),
                   tile(LANES),
                   pl.BlockSpec((ROUTE_ROWS, ts), lambda b, s: (0, b * (seq // ts) + s)),
                   pl.BlockSpec((ts // SORT_TILE, SUBLANES, LANES),
                                lambda b, s: (b * (seq // ts) + s, 0, 0)),
                   const((SUBLANES, LANES))],
        out_shape=[jax.ShapeDtypeStruct((bsz, seq, d), F32),
                   jax.ShapeDtypeStruct((bsz, seq, d), BF16),
                   jax.ShapeDtypeStruct((bsz, seq, LANES), F32),
                   jax.ShapeDtypeStruct((ROUTE_ROWS, bsz * seq), F32),
                   jax.ShapeDtypeStruct((bsz * seq // SORT_TILE, SUBLANES, LANES), F32),
                   jax.ShapeDtypeStruct((SUBLANES, LANES), F32)],
        scratch_shapes=[pltpu.VMEM((ts, D_IN_PAD), F32),
                        pltpu.VMEM((ts + SUBLANES, CONV_WIDTH), F32),
                        pltpu.VMEM((ts, GLA_QK), F32),
                        pltpu.VMEM((ts, D_MODEL), BF16),
                        pltpu.VMEM((GLA_QK, GLA_WIDTH), F32),
                        pltpu.VMEM((SUBLANES, LANES), F32)],
        compiler_params=pltpu.CompilerParams(
            dimension_semantics=("arbitrary", "arbitrary"), vmem_limit_bytes=VMEM_LIMIT),
        name="mixer",
    )(x, g1, win, convw, wgk, bgk, gng, wout, g2, wrhl, br)


def _dispatch_kernel(run_xs_ref, run_buf_ref, run_len_ref, pad_lo_ref, pad_n_ref,
                     route_ref, h2_ref, xs_hbm, sbuf, zbuf, sem, zsem):
    i = pl.program_id(0)
    n = pl.num_programs(0)
    td = h2_ref.shape[0]
    n_rows = TOP_K * td
    slot = i % 2

    def zero_fill(wait):
        def fill(g, carry):
            ln = pl.multiple_of(pad_n_ref[g], ROW_TILES)

            @pl.when(ln > 0)
            def _():
                lo = pl.multiple_of(pad_lo_ref[g], ROW_TILES)
                cp = pltpu.make_async_copy(zbuf.at[pl.ds(0, ln), :], xs_hbm.at[pl.ds(lo, ln), :],
                                           zsem.at[0])
                if wait:
                    cp.wait()
                else:
                    cp.start()
            return carry

        lax.fori_loop(0, pad_lo_ref.shape[0], fill, 0)

    def wait_runs(buf_slot):
        pltpu.make_async_copy(sbuf.at[buf_slot], xs_hbm.at[pl.ds(0, n_rows * ROW_TILES), :],
                              sem.at[buf_slot]).wait()

    @pl.when(i == 0)
    def _():
        zbuf[...] = jnp.zeros_like(zbuf)
        zero_fill(wait=False)

    @pl.when(i >= 2)
    def _():
        wait_runs(slot)

    route = route_ref[...]
    row = lax.broadcasted_iota(I32, (n_rows, td), 0).astype(F32)
    pick = jnp.zeros((n_rows, td), F32)
    for kk in range(TOP_K):
        pick = jnp.where(row == route[3 * TOP_K + kk:3 * TOP_K + kk + 1, :], 1.0, pick)
    rows_sorted = _dot(pick.astype(BF16), h2_ref[...])
    words = _pack_rows(rows_sorted)
    for j in range(ROW_TILES):
        sbuf[slot, pl.ds(j, n_rows, stride=ROW_TILES), :] = words[:, j * LANES:(j + 1) * LANES]

    def body(e, carry):
        g = i * N_EXPERTS + e
        ln = pl.multiple_of(run_len_ref[g], ROW_TILES)

        @pl.when(ln > 0)
        def _():
            src = pl.multiple_of(run_buf_ref[g], ROW_TILES)
            dst = pl.multiple_of(run_xs_ref[g], ROW_TILES)
            pltpu.make_async_copy(sbuf.at[slot, pl.ds(src, ln), :], xs_hbm.at[pl.ds(dst, ln), :],
                                  sem.at[slot]).start()
        return carry

    lax.fori_loop(0, N_EXPERTS, body, 0)

    @pl.when(i == n - 1)
    def _():
        @pl.when(i >= 1)
        def _():
            wait_runs(1 - slot)
        wait_runs(slot)
        zero_fill(wait=True)


def _dispatch_call(run_xs, run_buf, run_len, pad_lo, pad_n, route, h2, n_rows):
    t, d = h2.shape
    td = SORT_TILE
    grid_spec = pltpu.PrefetchScalarGridSpec(
        num_scalar_prefetch=5,
        grid=(t // td,),
        in_specs=[pl.BlockSpec((ROUTE_ROWS, td), lambda i, *_: (0, i)),
                  pl.BlockSpec((td, d), lambda i, *_: (i, 0))],
        out_specs=pl.BlockSpec(memory_space=pl.ANY),
        scratch_shapes=[pltpu.VMEM((2, TOP_K * td * ROW_TILES, LANES), U32),
                        pltpu.VMEM((MOE_BLOCK * ROW_TILES, LANES), U32),
                        pltpu.SemaphoreType.DMA((2,)),
                        pltpu.SemaphoreType.DMA((1,))],
    )
    return pl.pallas_call(
        _dispatch_kernel,
        grid_spec=grid_spec,
        out_shape=jax.ShapeDtypeStruct((n_rows * ROW_TILES, LANES), U32),
        compiler_params=pltpu.CompilerParams(
            dimension_semantics=("arbitrary",), vmem_limit_bytes=VMEM_LIMIT,
            has_side_effects=True),
        name="dispatch",
    )(run_xs, run_buf, run_len, pad_lo, pad_n, route, h2)


def _experts_kernel(bexp_ref, first_ref, nxt_ref, nval_ref, nreal_ref,
                    xs_ref, wgu_hbm, bgu_ref, wd_hbm, bd_ref,
                    ys_ref,
                    xb_ref, act_ref, wgu_stage, wd_stage, wgu_bf, wd_bf, wsem):
    i = pl.program_id(0)
    nreal = nreal_ref[0]
    bm = xb_ref.shape[0]

    def weight_copies(e):
        return (pltpu.make_async_copy(wgu_hbm.at[e], wgu_stage, wsem.at[0]),
                pltpu.make_async_copy(wd_hbm.at[e], wd_stage, wsem.at[1]))

    @pl.when(i >= nreal)
    def _():
        ys_ref[...] = jnp.zeros_like(ys_ref)

    @pl.when(i < nreal)
    def _():
        e = bexp_ref[i]

        @pl.when(i == 0)
        def _():
            for cp in weight_copies(e):
                cp.start(priority=1)

        @pl.when(first_ref[i] == 1)
        def _():
            for cp in weight_copies(e):
                cp.wait()

            def cast_gu(r, carry):
                rows = pl.ds(pl.multiple_of(r * CAST_ROWS, CAST_ROWS), CAST_ROWS)
                wgu_bf[rows, :] = wgu_stage[rows, :].astype(BF16)
                return carry

            def cast_d(r, carry):
                rows = pl.ds(pl.multiple_of(r * CAST_ROWS, CAST_ROWS), CAST_ROWS)
                wd_bf[rows, :] = wd_stage[rows, :].astype(BF16)
                return carry

            lax.fori_loop(0, D_MODEL // CAST_ROWS, cast_gu, 0)
            lax.fori_loop(0, D_FF // CAST_ROWS, cast_d, 0)

            @pl.when(nxt_ref[i] >= 0)
            def _():
                for cp in weight_copies(nxt_ref[i]):
                    cp.start(priority=1)

        def mlp(rows):
            for j in range(ROW_TILES):
                lo, hi = _unpack_words(xs_ref[pl.ds(j, rows, stride=ROW_TILES), :])
                xb_ref[0:rows, j * LANES:(j + 1) * LANES] = lo
                xb_ref[0:rows, D_MODEL // 2 + j * LANES:D_MODEL // 2 + (j + 1) * LANES] = hi
            for c in range(D_FF // FF_CHUNK):
                f0 = c * FF_CHUNK
                xb = xb_ref[0:rows, :]
                gate = _dot(xb, wgu_bf[:, f0:f0 + FF_CHUNK]) + bgu_ref[0, :, f0:f0 + FF_CHUNK]
                up = (_dot(xb, wgu_bf[:, D_FF + f0:D_FF + f0 + FF_CHUNK])
                      + bgu_ref[0, :, D_FF + f0:D_FF + f0 + FF_CHUNK])
                gate = jnp.minimum(gate, SWIGLU_LIMIT)
                up = jnp.clip(up, -SWIGLU_LIMIT, SWIGLU_LIMIT)
                glu = gate * jax.nn.sigmoid(gate * SWIGLU_ALPHA)
                act_ref[0:rows, f0:f0 + FF_CHUNK] = ((up + 1.0) * glu).astype(BF16)
            out = _pack_rows(_dot(act_ref[0:rows, :], wd_bf[...]) + bd_ref[0])
            for j in range(ROW_TILES):
                ys_ref[pl.ds(j, rows, stride=ROW_TILES), :] = out[:, j * LANES:(j + 1) * LANES]

        @pl.when(nval_ref[i] > bm // 2)
        def _():
            mlp(bm)

        @pl.when(nval_ref[i] <= bm // 2)
        def _():
            mlp(bm // 2)
            ys_ref[bm // 2 * ROW_TILES:bm * ROW_TILES, :] = jnp.zeros(
                (bm // 2 * ROW_TILES, LANES), U32)


def _experts_call(bexp, first, nxt, nval, nreal, xs, wgu, bgu, wd, bd):
    bm = MOE_BLOCK
    n_blocks = xs.shape[0] // (bm * ROW_TILES)
    bgu3 = bgu.reshape(N_EXPERTS, 1, 2 * D_FF)
    bd3 = bd.reshape(N_EXPERTS, 1, D_MODEL)
    grid_spec = pltpu.PrefetchScalarGridSpec(
        num_scalar_prefetch=5,
        grid=(n_blocks,),
        in_specs=[
            pl.BlockSpec((bm * ROW_TILES, LANES),
                         lambda i, be, fi, nx, nv, nr: (jnp.minimum(i, nr[0] - 1), 0)),
            pl.BlockSpec(memory_space=pl.ANY),
            pl.BlockSpec((1, 1, 2 * D_FF), lambda i, be, fi, nx, nv, nr: (be[i], 0, 0)),
            pl.BlockSpec(memory_space=pl.ANY),
            pl.BlockSpec((1, 1, D_MODEL), lambda i, be, fi, nx, nv, nr: (be[i], 0, 0)),
        ],
        out_specs=pl.BlockSpec((bm * ROW_TILES, LANES), lambda i, be, fi, nx, nv, nr: (i, 0)),
        scratch_shapes=[pltpu.VMEM((bm, D_MODEL), BF16),
                        pltpu.VMEM((bm, D_FF), BF16),
                        pltpu.VMEM((D_MODEL, 2 * D_FF), F32),
                        pltpu.VMEM((D_FF, D_MODEL), F32),
                        pltpu.VMEM((D_MODEL, 2 * D_FF), BF16),
                        pltpu.VMEM((D_FF, D_MODEL), BF16),
                        pltpu.SemaphoreType.DMA((2,))],
    )
    return pl.pallas_call(
        _experts_kernel,
        grid_spec=grid_spec,
        out_shape=jax.ShapeDtypeStruct(xs.shape, U32),
        compiler_params=pltpu.CompilerParams(
            dimension_semantics=("arbitrary",), vmem_limit_bytes=VMEM_LIMIT),
        name="experts",
    )(bexp, first, nxt, nval, nreal, xs, wgu, bgu3, wd, bd3)


def _combine_kernel(src_ref, dst_ref, len_ref, x1_ref, meta_ref, g_ref, ys_hbm, out_ref,
                    ybuf, ysort_ref, sem):
    i = pl.program_id(0)
    n = pl.num_programs(0)
    tb = x1_ref.shape[0]
    n_rows = TOP_K * tb
    slot = i % 2

    def start_runs(tile, buf_slot):
        def body(e, carry):
            g = tile * N_EXPERTS + e
            ln = pl.multiple_of(len_ref[g], ROW_TILES)

            @pl.when(ln > 0)
            def _():
                src = pl.multiple_of(src_ref[g], ROW_TILES)
                dst = pl.multiple_of(dst_ref[g], ROW_TILES)
                pltpu.make_async_copy(ys_hbm.at[pl.ds(src, ln), :],
                                      ybuf.at[buf_slot, pl.ds(dst, ln), :], sem.at[buf_slot]).start()
            return carry
        lax.fori_loop(0, N_EXPERTS, body, 0)

    @pl.when(i == 0)
    def _():
        start_runs(0, 0)

    @pl.when(i + 1 < n)
    def _():
        start_runs(i + 1, 1 - slot)

    pltpu.make_async_copy(ys_hbm.at[pl.ds(0, n_rows * ROW_TILES), :], ybuf.at[slot],
                          sem.at[slot]).wait()

    for j in range(ROW_TILES):
        lo, hi = _unpack_words(ybuf[slot, pl.ds(j, n_rows, stride=ROW_TILES), :])
        ysort_ref[:, j * LANES:(j + 1) * LANES] = lo
        ysort_ref[:, D_MODEL // 2 + j * LANES:D_MODEL // 2 + (j + 1) * LANES] = hi

    meta = meta_ref[...]
    col = lax.broadcasted_iota(I32, (tb, n_rows), 1).astype(F32)
    weights = jnp.zeros((tb, n_rows), F32)
    for kk in range(TOP_K):
        weights = jnp.where(col == meta[:, 3 * TOP_K + kk:3 * TOP_K + kk + 1],
                            meta[:, TOP_K + kk:TOP_K + kk + 1], weights)
    acc = x1_ref[...] + _dot(weights.astype(BF16), ysort_ref[...])
    out_ref[...] = _rms(acc, g_ref[...])


def _combine_call(seg_src, seg_dst, seg_len, x1, meta, g, ys):
    t, d = x1.shape
    tb = SORT_TILE
    nb = t // tb
    grid_spec = pltpu.PrefetchScalarGridSpec(
        num_scalar_prefetch=3,
        grid=(nb,),
        in_specs=[pl.BlockSpec((tb, d), lambda i, a, b, c: (i, 0)),
                  pl.BlockSpec((tb, LANES), lambda i, a, b, c: (i, 0)),
                  pl.BlockSpec((1, d), lambda i, a, b, c: (0, 0)),
                  pl.BlockSpec(memory_space=pl.ANY)],
        out_specs=pl.BlockSpec((tb, d), lambda i, a, b, c: (i, 0)),
        scratch_shapes=[pltpu.VMEM((2, TOP_K * tb * ROW_TILES, LANES), U32),
                        pltpu.VMEM((TOP_K * tb, D_MODEL), BF16),
                        pltpu.SemaphoreType.DMA((2,))],
    )
    return pl.pallas_call(
        _combine_kernel,
        grid_spec=grid_spec,
        out_shape=jax.ShapeDtypeStruct((t, d), F32),
        compiler_params=pltpu.CompilerParams(
            dimension_semantics=("arbitrary",), vmem_limit_bytes=VMEM_LIMIT),
        name="combine",
    )(seg_src, seg_dst, seg_len, x1, meta, g, ys)


def _routing_tables(counts, t):
    bm = MOE_BLOCK
    n_blocks = t * TOP_K // bm + N_EXPERTS
    eids = jnp.arange(N_EXPERTS, dtype=I32)
    nblk_e = (counts + bm - 1) // bm
    blk_end = jnp.sum(jnp.where(eids[None, :] <= eids[:, None], nblk_e[None, :], 0), axis=1)
    blk_start = blk_end - nblk_e
    nreal = blk_end[N_EXPERTS - 1]
    pad_start = blk_start * bm
    blk = jnp.arange(n_blocks, dtype=I32)
    bexp = jnp.minimum(jnp.sum((blk_end[None, :] <= blk[:, None]).astype(I32), axis=1),
                       N_EXPERTS - 1)
    blk_is_e = bexp[:, None] == eids[None, :]
    pick = lambda tab: jnp.sum(jnp.where(blk_is_e, tab[None, :], 0), axis=1)
    first = (blk == pick(blk_start)).astype(I32)
    nxt_e = jnp.sum((blk_end[None, :] <= blk_end[:, None]).astype(I32), axis=1)
    nxt_e = jnp.where(blk_end < nreal, jnp.minimum(nxt_e, N_EXPERTS - 1), -1)
    nxt = pick(nxt_e)
    nval = jnp.clip(pick(counts) - (blk - pick(blk_start)) * bm, 0, bm)
    tail_blk = jnp.arange(N_EXPERTS, dtype=I32) + nreal
    pad_lo = jnp.concatenate([pad_start + counts, jnp.minimum(tail_blk, n_blocks - 1) * bm])
    pad_n = jnp.concatenate([nblk_e * bm - counts, jnp.where(tail_blk < n_blocks, bm, 0)])
    return (pad_start, bexp, first, nxt, nval, nreal.reshape(1).astype(I32), n_blocks,
            pad_lo * ROW_TILES, pad_n * ROW_TILES)


def kernel(x, norm_mix_g, w_in, conv_w, w_gk_up, b_gk_up, gla_norm_g, w_out, norm_ffn_g,
           w_router, b_router, w_gate_up, b_gate_up, w_down, b_down, norm_final_g):
    bsz, seq, d = x.shape
    t = bsz * seq
    assert w_in.shape[0] == 1, "single-layer trunk only"
    l = 0
    d_in = w_in.shape[-1]
    win = jnp.pad(w_in[l], ((0, 0), (0, D_IN_PAD - d_in))).astype(BF16)
    wgk = jnp.pad(w_gk_up[l], ((0, LANES - GLA_RANK), (0, 0))).astype(BF16)
    wr = jnp.pad(w_router[l], ((0, 0), (0, LANES - N_EXPERTS)))
    wrh = wr.astype(BF16)
    wrhl = jnp.concatenate([wrh, (wr - wrh.astype(F32)).astype(BF16)], axis=1)
    br = jnp.pad(b_router[l], (0, LANES - N_EXPERTS), constant_values=NEG_BIG).reshape(1, LANES)

    x1, h2, meta, route, seg, cnt = _mixer_call(
        x, norm_mix_g[l].reshape(1, d), win, conv_w[l], wgk, b_gk_up[l].reshape(1, GLA_QK),
        gla_norm_g[l].reshape(1, GLA_DV), w_out[l].astype(BF16), norm_ffn_g[l].reshape(1, d),
        wrhl, br)

    meta2 = meta.reshape(t, LANES)
    counts = cnt[0, :N_EXPERTS].astype(I32)
    (pad_start, bexp, first, nxt, nval, nreal, n_blocks, pad_lo,
     pad_n) = _routing_tables(counts, t)
    seg_src = ((pad_start[None, :] + seg[:, 0, :N_EXPERTS].astype(I32)) * ROW_TILES).reshape(-1)
    seg_len = (seg[:, 1, :N_EXPERTS].astype(I32) * ROW_TILES).reshape(-1)
    seg_dst = (seg[:, 2, :N_EXPERTS].astype(I32) * ROW_TILES).reshape(-1)
    xs = _dispatch_call(seg_src, seg_dst, seg_len, pad_lo, pad_n, route, h2.reshape(t, d),
                        n_blocks * MOE_BLOCK)
    ys = _experts_call(bexp, first, nxt, nval, nreal, xs, w_gate_up[l], b_gate_up[l], w_down[l],
                       b_down[l])
    out = _combine_call(seg_src, seg_dst, seg_len, x1.reshape(t, d), meta2,
                        norm_final_g.reshape(1, d), ys)
    return out.reshape(bsz, seq, d)
```

```python
import jax
import jax.numpy as jnp
from jax import lax
from jax.experimental import pallas as pl
from jax.experimental.pallas import tpu as pltpu

F32 = jnp.float32
BF16 = jnp.bfloat16
I32 = jnp.int32
U32 = jnp.uint32

D_MODEL = 1024
CONV_WIDTH = 512
CONV_K = 3
GLA_WIDTH = 512
GLA_HEADS = 4
GLA_DV = 128
GLA_DK = 64
GLA_QK = GLA_HEADS * GLA_DK
GLA_RANK = 16
GLA_NORMALIZER = 16.0
GLA_CHUNK = 64
N_EXPERTS = 32
TOP_K = 4
D_FF = 1024
SWIGLU_LIMIT = 7.0
SWIGLU_ALPHA = 1.702
RMS_EPS = 1e-5

LANES = 128
SUBLANES = 8
ROW_TILES = D_MODEL // LANES // 2
ROUTE_ROWS = 16

OFF_UH = 0
OFF_GB = OFF_UH + CONV_WIDTH
OFF_GC = OFF_GB + CONV_WIDTH
OFF_Q = OFF_GC + CONV_WIDTH
OFF_K = OFF_Q + GLA_QK
OFF_V = OFF_K + GLA_QK
OFF_GO = OFF_V + GLA_WIDTH
OFF_GKL = OFF_GO + GLA_WIDTH
D_IN_PAD = OFF_GKL + LANES

SEQ_TILE = 512
SORT_TILE = 256
MOE_BLOCK = 512
FF_CHUNK = 256
TAIL_ROWS = 128
CAST_ROWS = 128
NEG_BIG = -1e30
VMEM_LIMIT = 56 * 1024 * 1024


def _rms(x, g):
    return x * lax.rsqrt(jnp.mean(x * x, axis=-1, keepdims=True) + RMS_EPS) * g


def _dot(a, b):
    return jnp.dot(a, b, preferred_element_type=F32)


def _dot_nt(a, b):
    return lax.dot_general(a, b, (((1,), (1,)), ((), ())), preferred_element_type=F32)


def _pack_rows(x):
    half = x.shape[1] // 2
    xr = x.astype(BF16).astype(F32)
    lo = lax.bitcast_convert_type(xr[:, :half], U32) >> 16
    hi = lax.bitcast_convert_type(xr[:, half:], U32) & jnp.uint32(0xFFFF0000)
    return hi | lo


def _unpack_words(w):
    lo = lax.bitcast_convert_type(w << 16, F32).astype(BF16)
    hi = lax.bitcast_convert_type(w & jnp.uint32(0xFFFF0000), F32).astype(BF16)
    return lo, hi


def _split_bf16(x):
    hi = x.astype(BF16)
    lo = (x - hi.astype(F32)).astype(BF16)
    return hi, lo


def _mixer_kernel(x_ref, g1_ref, win_ref, convw_ref, wgk_ref, bgk_ref, gng_ref, wout_ref,
                  g2_ref, wrhl_ref, br_ref,
                  x1_ref, h2_ref, meta_ref, route_ref, seg_ref, cnt_ref,
                  proj_ref, ubuf_ref, la_ref, ycat_ref, state_ref, carry_ref):
    ts = x_ref.shape[1]
    b_idx = pl.program_id(0)
    s_idx = pl.program_id(1)

    @pl.when(s_idx == 0)
    def _():
        state_ref[...] = jnp.zeros_like(state_ref)
        ubuf_ref[0:SUBLANES, :] = jnp.zeros((SUBLANES, CONV_WIDTH), F32)

    @pl.when((s_idx == 0) & (b_idx == 0))
    def _():
        carry_ref[...] = jnp.zeros_like(carry_ref)

    x = x_ref[0]
    h = _rms(x, g1_ref[...]).astype(BF16)
    proj_ref[...] = _dot(h, win_ref[...])

    u = proj_ref[:, OFF_GC:OFF_GC + CONV_WIDTH] * proj_ref[:, OFF_UH:OFF_UH + CONV_WIDTH]
    ubuf_ref[SUBLANES:SUBLANES + ts, :] = u
    u1 = ubuf_ref[pl.ds(SUBLANES - 1, ts), :]
    u2 = ubuf_ref[pl.ds(SUBLANES - 2, ts), :]
    conv = convw_ref[0:1, :] * u2 + convw_ref[1:2, :] * u1 + convw_ref[2:3, :] * u
    ycat_ref[:, 0:CONV_WIDTH] = (proj_ref[:, OFF_GB:OFF_GB + CONV_WIDTH] * conv).astype(BF16)
    ubuf_ref[0:SUBLANES, :] = ubuf_ref[ts:ts + SUBLANES, :]

    gk = _dot(proj_ref[:, OFF_GKL:OFF_GKL + LANES].astype(BF16), wgk_ref[...]) + bgk_ref[...]
    log_sig = jnp.minimum(gk, 0.0) - jnp.log1p(jnp.exp(-jnp.abs(gk)))
    la_ref[...] = log_sig / GLA_NORMALIZER

    ci = lax.broadcasted_iota(I32, (GLA_CHUNK, GLA_CHUNK), 0)
    cj = lax.broadcasted_iota(I32, (GLA_CHUNK, GLA_CHUNK), 1)
    tri_incl = (cj <= ci).astype(BF16)
    causal = cj <= ci
    causal4 = jnp.concatenate([causal] * GLA_HEADS, axis=0)
    lane_qk = lax.broadcasted_iota(I32, (1, GLA_QK), 1)
    head_masks = [((lane_qk >= hd * GLA_DK) & (lane_qk < (hd + 1) * GLA_DK)).astype(F32)
                  for hd in range(GLA_HEADS)]
    gng = gng_ref[...]

    n_chunks = ts // GLA_CHUNK
    chunk_rows = [pl.ds(c * GLA_CHUNK, GLA_CHUNK) for c in range(n_chunks)]
    lane_c = lax.broadcasted_iota(I32, (GLA_QK, 2 * GLA_CHUNK), 1)
    qd_all, kd_all, kr_all, bl_all, v_all = [], [], [], [], []
    for rows in chunk_rows:
        la_hi, la_lo = _split_bf16(la_ref[rows, :])
        bcum = _dot(tri_incl, la_hi) + _dot(tri_incl, la_lo)
        blast = bcum[GLA_CHUNK - 1:GLA_CHUNK, :]
        q = proj_ref[rows, OFF_Q:OFF_Q + GLA_QK] * (GLA_DK ** -0.5)
        k = proj_ref[rows, OFF_K:OFF_K + GLA_QK]
        qd_all.append(q * jnp.exp(bcum))
        kd_all.append((k * jnp.exp(-bcum)).astype(BF16))
        kr_all.append(k * jnp.exp(blast - bcum))
        bl_all.append(blast)
        v_all.append(proj_ref[rows, OFF_V:OFF_V + GLA_WIDTH].astype(BF16))

    scores_all = []
    for c in range(n_chunks):
        q_stack = jnp.concatenate([qd_all[c] * m for m in head_masks], axis=0).astype(BF16)
        scores_all.append(
            jnp.where(causal4, _dot_nt(q_stack, kd_all[c]), 0.0).astype(BF16))
    o_intra_all = []
    for c in range(n_chunks):
        o_intra_all.append(jnp.concatenate(
            [_dot(scores_all[c][hd * GLA_CHUNK:(hd + 1) * GLA_CHUNK, :],
                  v_all[c][:, hd * GLA_DV:(hd + 1) * GLA_DV]) for hd in range(GLA_HEADS)], axis=1))
    kv_all, dcol_all = [], []
    for c in range(n_chunks):
        kt = jnp.concatenate(
            [kr_all[c], jnp.broadcast_to(bl_all[c], (GLA_CHUNK, GLA_QK))], axis=0).T
        dcol_all.append(jnp.exp(kt[:, GLA_CHUNK:GLA_CHUNK + 1]))
        kt_b = jnp.where(lane_c < GLA_CHUNK, kt, 0.0).astype(BF16)
        v_pad = jnp.concatenate([v_all[c], jnp.zeros_like(v_all[c])], axis=0)
        kv_all.append([_dot(kt_b[hd * GLA_DK:(hd + 1) * GLA_DK, :],
                            v_pad[:, hd * GLA_DV:(hd + 1) * GLA_DV]) for hd in range(GLA_HEADS)])

    o_all = []
    for c in range(n_chunks):
        state = state_ref[...]
        o_all.append(_dot(qd_all[c].astype(BF16), state.astype(BF16)) + o_intra_all[c])
        for hd in range(GLA_HEADS):
            rs = slice(hd * GLA_DK, (hd + 1) * GLA_DK)
            cs = slice(hd * GLA_DV, (hd + 1) * GLA_DV)
            state_ref[rs, cs] = dcol_all[c][rs, :] * state[rs, cs] + kv_all[c][hd]

    for c, rows in enumerate(chunk_rows):
        o = o_all[c]
        g_out = proj_ref[rows, OFF_GO:OFF_GO + GLA_WIDTH]
        o_n = jnp.concatenate(
            [_rms(o[:, hd * GLA_DV:(hd + 1) * GLA_DV], gng) for hd in range(GLA_HEADS)], axis=1)
        y = o_n * (g_out * jax.nn.sigmoid(g_out))
        ycat_ref[rows, CONV_WIDTH:CONV_WIDTH + GLA_WIDTH] = y.astype(BF16)

    x1 = x + _dot(ycat_ref[...], wout_ref[...])
    x1_ref[0] = x1
    h2 = _rms(x1, g2_ref[...])
    h2_hi, h2_lo = _split_bf16(h2)
    h2_ref[0] = h2_hi
    hi_terms = _dot(h2_hi, wrhl_ref[...])
    logits = (hi_terms[:, 0:LANES] + hi_terms[:, LANES:2 * LANES]
              + _dot(h2_lo, wrhl_ref[:, 0:LANES]) + br_ref[...])

    lane = lax.broadcasted_iota(I32, (ts, LANES), 1).astype(F32)
    work = logits
    sel = jnp.zeros((ts, LANES), F32)
    top_v, top_i, top_oh = [], [], []
    for _ in range(TOP_K):
        m = jnp.max(work, axis=-1, keepdims=True)
        idx = jnp.min(jnp.where(work == m, lane, float(LANES)), axis=-1, keepdims=True)
        oh = lane == idx
        top_v.append(m)
        top_i.append(idx)
        top_oh.append(oh)
        sel = sel + oh.astype(F32)
        work = jnp.where(oh, -jnp.inf, work)
    exps = [jnp.exp(tv - top_v[0]) for tv in top_v]
    denom = exps[0] + exps[1] + exps[2] + exps[3]
    gates = [e / denom for e in exps]

    ti = lax.broadcasted_iota(I32, (ts, ts), 0)
    tj = lax.broadcasted_iota(I32, (ts, ts), 1)
    strict_lower = (tj < ti).astype(BF16)
    local = _dot(strict_lower, sel.astype(BF16))
    carry = carry_ref[0:1, :]
    ranks = [jnp.sum(jnp.where(oh, local + carry, 0.0), axis=-1, keepdims=True) for oh in top_oh]

    lane_row = lax.broadcasted_iota(I32, (1, LANES), 1)
    slot_base = []
    before = jnp.zeros((1, LANES), F32)
    for u in range(ts // SORT_TILE):
        sub_cnt = jnp.sum(sel[u * SORT_TILE:(u + 1) * SORT_TILE, :], axis=0, keepdims=True)
        seg_start = sub_cnt
        shift = 1
        while shift < LANES:
            seg_start = seg_start + jnp.where(lane_row >= shift,
                                              pltpu.roll(seg_start, shift, 1), 0.0)
            shift *= 2
        seg_start = seg_start - sub_cnt
        seg_ref[u, 0:1, :] = carry + before
        seg_ref[u, 1:2, :] = sub_cnt
        seg_ref[u, 2:3, :] = seg_start
        seg_ref[u, 3:SUBLANES, :] = jnp.zeros((SUBLANES - 3, LANES), F32)
        slot_base.append(jnp.broadcast_to(seg_start - before, (SORT_TILE, LANES)))
        before = before + sub_cnt
    slot_base = jnp.concatenate(slot_base, axis=0)
    slots = [jnp.sum(jnp.where(oh, local + slot_base, 0.0), axis=-1, keepdims=True)
             for oh in top_oh]
    new_carry = carry + before
    carry_ref[...] = jnp.broadcast_to(new_carry, carry_ref.shape)
    cnt_ref[...] = jnp.broadcast_to(new_carry, cnt_ref.shape)

    meta = jnp.zeros((ts, LANES), F32)
    for kk in range(TOP_K):
        meta = jnp.where(lane == float(kk), top_i[kk], meta)
        meta = jnp.where(lane == float(TOP_K + kk), gates[kk], meta)
        meta = jnp.where(lane == float(2 * TOP_K + kk), ranks[kk], meta)
        meta = jnp.where(lane == float(3 * TOP_K + kk), slots[kk], meta)
    meta_ref[0] = meta
    route_ref[...] = meta.T[0:ROUTE_ROWS, :]


def _mixer_call(x, g1, win, convw, wgk, bgk, gng, wout, g2, wrhl, br):
    bsz, seq, d = x.shape
    ts = SEQ_TILE
    grid = (bsz, seq // ts)

    def const(shape):
        return pl.BlockSpec(shape, lambda b, s: (0,) * len(shape))

    tile = lambda w: pl.BlockSpec((1, ts, w), lambda b, s: (b, s, 0))
    return pl.pallas_call(
        _mixer_kernel,
        grid=grid,
        in_specs=[tile(d), const(g1.shape), const(win.shape), const(convw.shape),
                  const(wgk.shape), const(bgk.shape), const(gng.shape), const(wout.shape),
                  const(g2.shape), const(wrhl.shape), const(br.shape)],
        out_specs=[tile(d),
                   tile(d),
                   tile(LANES),
                   pl.BlockSpec((ROUTE_ROWS, ts), lambda b, s: (0, b * (seq // ts) + s)),
                   pl.BlockSpec((ts // SORT_TILE, SUBLANES, LANES),
                                lambda b, s: (b * (seq // ts) + s, 0, 0)),
                   const((SUBLANES, LANES))],
        out_shape=[jax.ShapeDtypeStruct((bsz, seq, d), F32),
                   jax.ShapeDtypeStruct((bsz, seq, d), BF16),
                   jax.ShapeDtypeStruct((bsz, seq, LANES), F32),
                   jax.ShapeDtypeStruct((ROUTE_ROWS, bsz * seq), F32),
                   jax.ShapeDtypeStruct((bsz * seq // SORT_TILE, SUBLANES, LANES), F32),
                   jax.ShapeDtypeStruct((SUBLANES, LANES), F32)],
        scratch_shapes=[pltpu.VMEM((ts, D_IN_PAD), F32),
                        pltpu.VMEM((ts + SUBLANES, CONV_WIDTH), F32),
                        pltpu.VMEM((ts, GLA_QK), F32),
                        pltpu.VMEM((ts, D_MODEL), BF16),
                        pltpu.VMEM((GLA_QK, GLA_WIDTH), F32),
                        pltpu.VMEM((SUBLANES, LANES), F32)],
        compiler_params=pltpu.CompilerParams(
            dimension_semantics=("arbitrary", "arbitrary"), vmem_limit_bytes=VMEM_LIMIT),
        name="mixer",
    )(x, g1, win, convw, wgk, bgk, gng, wout, g2, wrhl, br)


def _dispatch_kernel(run_xs_ref, run_buf_ref, run_len_ref, pad_lo_ref, pad_n_ref,
                     route_ref, h2_ref, xs_hbm, sbuf, zbuf, sem, zsem):
    i = pl.program_id(0)
    n = pl.num_programs(0)
    td = h2_ref.shape[0]
    n_rows = TOP_K * td
    slot = i % 2

    def zero_fill(wait):
        def fill(g, carry):
            ln = pl.multiple_of(pad_n_ref[g], ROW_TILES)

            @pl.when(ln > 0)
            def _():
                lo = pl.multiple_of(pad_lo_ref[g], ROW_TILES)
                cp = pltpu.make_async_copy(zbuf.at[pl.ds(0, ln), :], xs_hbm.at[pl.ds(lo, ln), :],
                                           zsem.at[0])
                if wait:
                    cp.wait()
                else:
                    cp.start()
            return carry

        lax.fori_loop(0, pad_lo_ref.shape[0], fill, 0)

    def wait_runs(buf_slot):
        pltpu.make_async_copy(sbuf.at[buf_slot], xs_hbm.at[pl.ds(0, n_rows * ROW_TILES), :],
                              sem.at[buf_slot]).wait()

    @pl.when(i == 0)
    def _():
        zbuf[...] = jnp.zeros_like(zbuf)
        zero_fill(wait=False)

    @pl.when(i >= 2)
    def _():
        wait_runs(slot)

    route = route_ref[...]
    row = lax.broadcasted_iota(I32, (n_rows, td), 0).astype(F32)
    pick = jnp.zeros((n_rows, td), F32)
    for kk in range(TOP_K):
        pick = jnp.where(row == route[3 * TOP_K + kk:3 * TOP_K + kk + 1, :], 1.0, pick)
    rows_sorted = _dot(pick.astype(BF16), h2_ref[...])
    words = _pack_rows(rows_sorted)
    for j in range(ROW_TILES):
        sbuf[slot, pl.ds(j, n_rows, stride=ROW_TILES), :] = words[:, j * LANES:(j + 1) * LANES]

    def body(e, carry):
        g = i * N_EXPERTS + e
        ln = pl.multiple_of(run_len_ref[g], ROW_TILES)

        @pl.when(ln > 0)
        def _():
            src = pl.multiple_of(run_buf_ref[g], ROW_TILES)
            dst = pl.multiple_of(run_xs_ref[g], ROW_TILES)
            pltpu.make_async_copy(sbuf.at[slot, pl.ds(src, ln), :], xs_hbm.at[pl.ds(dst, ln), :],
                                  sem.at[slot]).start()
        return carry

    lax.fori_loop(0, N_EXPERTS, body, 0)

    @pl.when(i == n - 1)
    def _():
        @pl.when(i >= 1)
        def _():
            wait_runs(1 - slot)
        wait_runs(slot)
        zero_fill(wait=True)


def _dispatch_call(run_xs, run_buf, run_len, pad_lo, pad_n, route, h2, n_rows):
    t, d = h2.shape
    td = SORT_TILE
    grid_spec = pltpu.PrefetchScalarGridSpec(
        num_scalar_prefetch=5,
        grid=(t // td,),
        in_specs=[pl.BlockSpec((ROUTE_ROWS, td), lambda i, *_: (0, i)),
                  pl.BlockSpec((td, d), lambda i, *_: (i, 0))],
        out_specs=pl.BlockSpec(memory_space=pl.ANY),
        scratch_shapes=[pltpu.VMEM((2, TOP_K * td * ROW_TILES, LANES), U32),
                        pltpu.VMEM((MOE_BLOCK * ROW_TILES, LANES), U32),
                        pltpu.SemaphoreType.DMA((2,)),
                        pltpu.SemaphoreType.DMA((1,))],
    )
    return pl.pallas_call(
        _dispatch_kernel,
        grid_spec=grid_spec,
        out_shape=jax.ShapeDtypeStruct((n_rows * ROW_TILES, LANES), U32),
        compiler_params=pltpu.CompilerParams(
            dimension_semantics=("arbitrary",), vmem_limit_bytes=VMEM_LIMIT,
            has_side_effects=True),
        name="dispatch",
    )(run_xs, run_buf, run_len, pad_lo, pad_n, route, h2)


def _experts_kernel(bexp_ref, first_ref, nxt_ref, nval_ref, nreal_ref,
                    xs_ref, wgu_hbm, bgu_ref, wd_hbm, bd_ref,
                    ys_ref,
                    xb_ref, act_ref, wgu_stage, wd_stage, wgu_bf, wd_bf, wsem):
    i = pl.program_id(0)
    nreal = nreal_ref[0]
    bm = xb_ref.shape[0]

    def weight_copies(e):
        return (pltpu.make_async_copy(wgu_hbm.at[e], wgu_stage, wsem.at[0]),
                pltpu.make_async_copy(wd_hbm.at[e], wd_stage, wsem.at[1]))

    @pl.when(i >= nreal)
    def _():
        ys_ref[...] = jnp.zeros_like(ys_ref)

    @pl.when(i < nreal)
    def _():
        e = bexp_ref[i]

        @pl.when(i == 0)
        def _():
            for cp in weight_copies(e):
                cp.start(priority=1)

        @pl.when(first_ref[i] == 1)
        def _():
            for cp in weight_copies(e):
                cp.wait()

            def cast_gu(r, carry):
                rows = pl.ds(pl.multiple_of(r * CAST_ROWS, CAST_ROWS), CAST_ROWS)
                wgu_bf[rows, :] = wgu_stage[rows, :].astype(BF16)
                return carry

            def cast_d(r, carry):
                rows = pl.ds(pl.multiple_of(r * CAST_ROWS, CAST_ROWS), CAST_ROWS)
                wd_bf[rows, :] = wd_stage[rows, :].astype(BF16)
                return carry

            lax.fori_loop(0, D_MODEL // CAST_ROWS, cast_gu, 0)
            lax.fori_loop(0, D_FF // CAST_ROWS, cast_d, 0)

            @pl.when(nxt_ref[i] >= 0)
            def _():
                for cp in weight_copies(nxt_ref[i]):
                    cp.start(priority=1)

        def mlp(rows):
            for j in range(ROW_TILES):
                lo, hi = _unpack_words(xs_ref[pl.ds(j, rows, stride=ROW_TILES), :])
                xb_ref[0:rows, j * LANES:(j + 1) * LANES] = lo
                xb_ref[0:rows, D_MODEL // 2 + j * LANES:D_MODEL // 2 + (j + 1) * LANES] = hi
            for c in range(D_FF // FF_CHUNK):
                f0 = c * FF_CHUNK
                xb = xb_ref[0:rows, :]
                gate = _dot(xb, wgu_bf[:, f0:f0 + FF_CHUNK]) + bgu_ref[0, :, f0:f0 + FF_CHUNK]
                up = (_dot(xb, wgu_bf[:, D_FF + f0:D_FF + f0 + FF_CHUNK])
                      + bgu_ref[0, :, D_FF + f0:D_FF + f0 + FF_CHUNK])
                gate = jnp.minimum(gate, SWIGLU_LIMIT)
                up = jnp.clip(up, -SWIGLU_LIMIT, SWIGLU_LIMIT)
                glu = gate * jax.nn.sigmoid(gate * SWIGLU_ALPHA)
                act_ref[0:rows, f0:f0 + FF_CHUNK] = ((up + 1.0) * glu).astype(BF16)
            out = _pack_rows(_dot(act_ref[0:rows, :], wd_bf[...]) + bd_ref[0])
            for j in range(ROW_TILES):
                ys_ref[pl.ds(j, rows, stride=ROW_TILES), :] = out[:, j * LANES:(j + 1) * LANES]

        n_routed = nval_ref[i]
        for rows in range(TAIL_ROWS, bm + 1, TAIL_ROWS):
            @pl.when((n_routed > rows - TAIL_ROWS) & (n_routed <= rows))
            def _(rows=rows):
                mlp(rows)
                if rows < bm:
                    ys_ref[rows * ROW_TILES:bm * ROW_TILES, :] = jnp.zeros(
                        ((bm - rows) * ROW_TILES, LANES), U32)


def _experts_call(bexp, first, nxt, nval, nreal, xs, wgu, bgu, wd, bd):
    bm = MOE_BLOCK
    n_blocks = xs.shape[0] // (bm * ROW_TILES)
    bgu3 = bgu.reshape(N_EXPERTS, 1, 2 * D_FF)
    bd3 = bd.reshape(N_EXPERTS, 1, D_MODEL)
    grid_spec = pltpu.PrefetchScalarGridSpec(
        num_scalar_prefetch=5,
        grid=(n_blocks,),
        in_specs=[
            pl.BlockSpec((bm * ROW_TILES, LANES),
                         lambda i, be, fi, nx, nv, nr: (jnp.minimum(i, nr[0] - 1), 0)),
            pl.BlockSpec(memory_space=pl.ANY),
            pl.BlockSpec((1, 1, 2 * D_FF), lambda i, be, fi, nx, nv, nr: (be[i], 0, 0)),
            pl.BlockSpec(memory_space=pl.ANY),
            pl.BlockSpec((1, 1, D_MODEL), lambda i, be, fi, nx, nv, nr: (be[i], 0, 0)),
        ],
        out_specs=pl.BlockSpec((bm * ROW_TILES, LANES), lambda i, be, fi, nx, nv, nr: (i, 0)),
        scratch_shapes=[pltpu.VMEM((bm, D_MODEL), BF16),
                        pltpu.VMEM((bm, D_FF), BF16),
                        pltpu.VMEM((D_MODEL, 2 * D_FF), F32),
                        pltpu.VMEM((D_FF, D_MODEL), F32),
                        pltpu.VMEM((D_MODEL, 2 * D_FF), BF16),
                        pltpu.VMEM((D_FF, D_MODEL), BF16),
                        pltpu.SemaphoreType.DMA((2,))],
    )
    return pl.pallas_call(
        _experts_kernel,
        grid_spec=grid_spec,
        out_shape=jax.ShapeDtypeStruct(xs.shape, U32),
        compiler_params=pltpu.CompilerParams(
            dimension_semantics=("arbitrary",), vmem_limit_bytes=VMEM_LIMIT),
        name="experts",
    )(bexp, first, nxt, nval, nreal, xs, wgu, bgu3, wd, bd3)


def _combine_kernel(src_ref, dst_ref, len_ref, x1_ref, meta_ref, g_ref, ys_hbm, out_ref,
                    ybuf, ysort_ref, sem):
    i = pl.program_id(0)
    n = pl.num_programs(0)
    tb = x1_ref.shape[0]
    n_rows = TOP_K * tb
    slot = i % 2

    def start_runs(tile, buf_slot):
        def body(e, carry):
            g = tile * N_EXPERTS + e
            ln = pl.multiple_of(len_ref[g], ROW_TILES)

            @pl.when(ln > 0)
            def _():
                src = pl.multiple_of(src_ref[g], ROW_TILES)
                dst = pl.multiple_of(dst_ref[g], ROW_TILES)
                pltpu.make_async_copy(ys_hbm.at[pl.ds(src, ln), :],
                                      ybuf.at[buf_slot, pl.ds(dst, ln), :], sem.at[buf_slot]).start()
            return carry
        lax.fori_loop(0, N_EXPERTS, body, 0)

    @pl.when(i == 0)
    def _():
        start_runs(0, 0)

    @pl.when(i + 1 < n)
    def _():
        start_runs(i + 1, 1 - slot)

    pltpu.make_async_copy(ys_hbm.at[pl.ds(0, n_rows * ROW_TILES), :], ybuf.at[slot],
                          sem.at[slot]).wait()

    for j in range(ROW_TILES):
        lo, hi = _unpack_words(ybuf[slot, pl.ds(j, n_rows, stride=ROW_TILES), :])
        ysort_ref[:, j * LANES:(j + 1) * LANES] = lo
        ysort_ref[:, D_MODEL // 2 + j * LANES:D_MODEL // 2 + (j + 1) * LANES] = hi

    meta = meta_ref[...]
    col = lax.broadcasted_iota(I32, (tb, n_rows), 1).astype(F32)
    weights = jnp.zeros((tb, n_rows), F32)
    for kk in range(TOP_K):
        weights = jnp.where(col == meta[:, 3 * TOP_K + kk:3 * TOP_K + kk + 1],
                            meta[:, TOP_K + kk:TOP_K + kk + 1], weights)
    acc = x1_ref[...] + _dot(weights.astype(BF16), ysort_ref[...])
    out_ref[...] = _rms(acc, g_ref[...])


def _combine_call(seg_src, seg_dst, seg_len, x1, meta, g, ys):
    t, d = x1.shape
    tb = SORT_TILE
    nb = t // tb
    grid_spec = pltpu.PrefetchScalarGridSpec(
        num_scalar_prefetch=3,
        grid=(nb,),
        in_specs=[pl.BlockSpec((tb, d), lambda i, a, b, c: (i, 0)),
                  pl.BlockSpec((tb, LANES), lambda i, a, b, c: (i, 0)),
                  pl.BlockSpec((1, d), lambda i, a, b, c: (0, 0)),
                  pl.BlockSpec(memory_space=pl.ANY)],
        out_specs=pl.BlockSpec((tb, d), lambda i, a, b, c: (i, 0)),
        scratch_shapes=[pltpu.VMEM((2, TOP_K * tb * ROW_TILES, LANES), U32),
                        pltpu.VMEM((TOP_K * tb, D_MODEL), BF16),
                        pltpu.SemaphoreType.DMA((2,))],
    )
    return pl.pallas_call(
        _combine_kernel,
        grid_spec=grid_spec,
        out_shape=jax.ShapeDtypeStruct((t, d), F32),
        compiler_params=pltpu.CompilerParams(
            dimension_semantics=("arbitrary",), vmem_limit_bytes=VMEM_LIMIT),
        name="combine",
    )(seg_src, seg_dst, seg_len, x1, meta, g, ys)


def _routing_tables(counts, t):
    bm = MOE_BLOCK
    n_blocks = t * TOP_K // bm + N_EXPERTS
    eids = jnp.arange(N_EXPERTS, dtype=I32)
    nblk_e = (counts + bm - 1) // bm
    blk_end = jnp.sum(jnp.where(eids[None, :] <= eids[:, None], nblk_e[None, :], 0), axis=1)
    blk_start = blk_end - nblk_e
    nreal = blk_end[N_EXPERTS - 1]
    pad_start = blk_start * bm
    blk = jnp.arange(n_blocks, dtype=I32)
    bexp = jnp.minimum(jnp.sum((blk_end[None, :] <= blk[:, None]).astype(I32), axis=1),
                       N_EXPERTS - 1)
    blk_is_e = bexp[:, None] == eids[None, :]
    pick = lambda tab: jnp.sum(jnp.where(blk_is_e, tab[None, :], 0), axis=1)
    first = (blk == pick(blk_start)).astype(I32)
    nxt_e = jnp.sum((blk_end[None, :] <= blk_end[:, None]).astype(I32), axis=1)
    nxt_e = jnp.where(blk_end < nreal, jnp.minimum(nxt_e, N_EXPERTS - 1), -1)
    nxt = pick(nxt_e)
    nval = jnp.clip(pick(counts) - (blk - pick(blk_start)) * bm, 0, bm)
    tail_blk = jnp.arange(N_EXPERTS, dtype=I32) + nreal
    pad_lo = jnp.concatenate([pad_start + counts, jnp.minimum(tail_blk, n_blocks - 1) * bm])
    pad_n = jnp.concatenate([nblk_e * bm - counts, jnp.where(tail_blk < n_blocks, bm, 0)])
    return (pad_start, bexp, first, nxt, nval, nreal.reshape(1).astype(I32), n_blocks,
            pad_lo * ROW_TILES, pad_n * ROW_TILES)


def kernel(x, norm_mix_g, w_in, conv_w, w_gk_up, b_gk_up, gla_norm_g, w_out, norm_ffn_g,
           w_router, b_router, w_gate_up, b_gate_up, w_down, b_down, norm_final_g):
    bsz, seq, d = x.shape
    t = bsz * seq
    assert w_in.shape[0] == 1, "single-layer trunk only"
    l = 0
    d_in = w_in.shape[-1]
    win = jnp.pad(w_in[l], ((0, 0), (0, D_IN_PAD - d_in))).astype(BF16)
    wgk = jnp.pad(w_gk_up[l], ((0, LANES - GLA_RANK), (0, 0))).astype(BF16)
    wr = jnp.pad(w_router[l], ((0, 0), (0, LANES - N_EXPERTS)))
    wrh = wr.astype(BF16)
    wrhl = jnp.concatenate([wrh, (wr - wrh.astype(F32)).astype(BF16)], axis=1)
    br = jnp.pad(b_router[l], (0, LANES - N_EXPERTS), constant_values=NEG_BIG).reshape(1, LANES)

    x1, h2, meta, route, seg, cnt = _mixer_call(
        x, norm_mix_g[l].reshape(1, d), win, conv_w[l], wgk, b_gk_up[l].reshape(1, GLA_QK),
        gla_norm_g[l].reshape(1, GLA_DV), w_out[l].astype(BF16), norm_ffn_g[l].reshape(1, d),
        wrhl, br)

    meta2 = meta.reshape(t, LANES)
    counts = cnt[0, :N_EXPERTS].astype(I32)
    (pad_start, bexp, first, nxt, nval, nreal, n_blocks, pad_lo,
     pad_n) = _routing_tables(counts, t)
    seg_src = ((pad_start[None, :] + seg[:, 0, :N_EXPERTS].astype(I32)) * ROW_TILES).reshape(-1)
    seg_len = (seg[:, 1, :N_EXPERTS].astype(I32) * ROW_TILES).reshape(-1)
    seg_dst = (seg[:, 2, :N_EXPERTS].astype(I32) * ROW_TILES).reshape(-1)
    xs = _dispatch_call(seg_src, seg_dst, seg_len, pad_lo, pad_n, route, h2.reshape(t, d),
                        n_blocks * MOE_BLOCK)
    ys = _experts_call(bexp, first, nxt, nval, nreal, xs, w_gate_up[l], b_gate_up[l], w_down[l],
                       b_down[l])
    out = _combine_call(seg_src, seg_dst, seg_len, x1.reshape(t, d), meta2,
                        norm_final_g.reshape(1, d), ys)
    return out.reshape(bsz, seq, d)
```

```python
import jax
import jax.numpy as jnp
from jax import lax
from jax.experimental import pallas as pl
from jax.experimental.pallas import tpu as pltpu

F32 = jnp.float32
BF16 = jnp.bfloat16
I32 = jnp.int32
U32 = jnp.uint32

D_MODEL = 1024
CONV_WIDTH = 512
CONV_K = 3
GLA_WIDTH = 512
GLA_HEADS = 4
GLA_DV = 128
GLA_DK = 64
GLA_QK = GLA_HEADS * GLA_DK
GLA_RANK = 16
GLA_NORMALIZER = 16.0
GLA_CHUNK = 64
N_EXPERTS = 32
TOP_K = 4
D_FF = 1024
SWIGLU_LIMIT = 7.0
SWIGLU_ALPHA = 1.702
RMS_EPS = 1e-5

LANES = 128
SUBLANES = 8
ROW_TILES = D_MODEL // LANES // 2
ROUTE_ROWS = 16

OFF_UH = 0
OFF_GB = OFF_UH + CONV_WIDTH
OFF_GC = OFF_GB + CONV_WIDTH
OFF_Q = OFF_GC + CONV_WIDTH
OFF_K = OFF_Q + GLA_QK
OFF_V = OFF_K + GLA_QK
OFF_GO = OFF_V + GLA_WIDTH
OFF_GKL = OFF_GO + GLA_WIDTH
D_IN_PAD = OFF_GKL + LANES

SEQ_TILE = 512
SORT_TILE = 256
MOE_BLOCK = 512
FF_CHUNK = 256
TAIL_ROWS = 128
CAST_ROWS = 128
NEG_BIG = -1e30
VMEM_LIMIT = 56 * 1024 * 1024


def _rms(x, g):
    return x * lax.rsqrt(jnp.mean(x * x, axis=-1, keepdims=True) + RMS_EPS) * g


def _dot(a, b):
    return jnp.dot(a, b, preferred_element_type=F32)


def _dot_nt(a, b):
    return lax.dot_general(a, b, (((1,), (1,)), ((), ())), preferred_element_type=F32)


def _pack_rows(x):
    half = x.shape[1] // 2
    xr = x.astype(BF16).astype(F32)
    lo = lax.bitcast_convert_type(xr[:, :half], U32) >> 16
    hi = lax.bitcast_convert_type(xr[:, half:], U32) & jnp.uint32(0xFFFF0000)
    return hi | lo


def _unpack_words(w):
    lo = lax.bitcast_convert_type(w << 16, F32).astype(BF16)
    hi = lax.bitcast_convert_type(w & jnp.uint32(0xFFFF0000), F32).astype(BF16)
    return lo, hi


def _split_bf16(x):
    hi = x.astype(BF16)
    lo = (x - hi.astype(F32)).astype(BF16)
    return hi, lo


def _mixer_kernel(x_ref, g1_ref, win_ref, convw_ref, wgk_ref, bgk_ref, gng_ref, wout_ref,
                  g2_ref, wrhl_ref, br_ref,
                  x1_ref, h2_ref, meta_ref, route_ref, seg_ref, cnt_ref,
                  proj_ref, ubuf_ref, la_ref, ycat_ref, state_ref, carry_ref):
    ts = x_ref.shape[1]
    b_idx = pl.program_id(0)
    s_idx = pl.program_id(1)

    @pl.when(s_idx == 0)
    def _():
        state_ref[...] = jnp.zeros_like(state_ref)
        ubuf_ref[0:SUBLANES, :] = jnp.zeros((SUBLANES, CONV_WIDTH), F32)

    @pl.when((s_idx == 0) & (b_idx == 0))
    def _():
        carry_ref[...] = jnp.zeros_like(carry_ref)

    x = x_ref[0]
    h = _rms(x, g1_ref[...]).astype(BF16)
    proj_ref[...] = _dot(h, win_ref[...])

    u = proj_ref[:, OFF_GC:OFF_GC + CONV_WIDTH] * proj_ref[:, OFF_UH:OFF_UH + CONV_WIDTH]
    ubuf_ref[SUBLANES:SUBLANES + ts, :] = u
    u1 = ubuf_ref[pl.ds(SUBLANES - 1, ts), :]
    u2 = ubuf_ref[pl.ds(SUBLANES - 2, ts), :]
    conv = convw_ref[0:1, :] * u2 + convw_ref[1:2, :] * u1 + convw_ref[2:3, :] * u
    ycat_ref[:, 0:CONV_WIDTH] = (proj_ref[:, OFF_GB:OFF_GB + CONV_WIDTH] * conv).astype(BF16)
    ubuf_ref[0:SUBLANES, :] = ubuf_ref[ts:ts + SUBLANES, :]

    gk = _dot(proj_ref[:, OFF_GKL:OFF_GKL + LANES].astype(BF16), wgk_ref[...]) + bgk_ref[...]
    log_sig = jnp.minimum(gk, 0.0) - jnp.log1p(jnp.exp(-jnp.abs(gk)))
    la_ref[...] = log_sig / GLA_NORMALIZER

    ci = lax.broadcasted_iota(I32, (GLA_CHUNK, GLA_CHUNK), 0)
    cj = lax.broadcasted_iota(I32, (GLA_CHUNK, GLA_CHUNK), 1)
    tri_incl = (cj <= ci).astype(BF16)
    causal = cj <= ci
    causal4 = jnp.concatenate([causal] * GLA_HEADS, axis=0)
    lane_qk = lax.broadcasted_iota(I32, (1, GLA_QK), 1)
    head_masks = [((lane_qk >= hd * GLA_DK) & (lane_qk < (hd + 1) * GLA_DK)).astype(F32)
                  for hd in range(GLA_HEADS)]
    gng = gng_ref[...]

    n_chunks = ts // GLA_CHUNK
    chunk_rows = [pl.ds(c * GLA_CHUNK, GLA_CHUNK) for c in range(n_chunks)]
    lane_c = lax.broadcasted_iota(I32, (GLA_QK, 2 * GLA_CHUNK), 1)
    qd_all, kd_all, kr_all, bl_all, v_all = [], [], [], [], []
    for rows in chunk_rows:
        la_hi, la_lo = _split_bf16(la_ref[rows, :])
        bcum = _dot(tri_incl, la_hi) + _dot(tri_incl, la_lo)
        blast = bcum[GLA_CHUNK - 1:GLA_CHUNK, :]
        q = proj_ref[rows, OFF_Q:OFF_Q + GLA_QK] * (GLA_DK ** -0.5)
        k = proj_ref[rows, OFF_K:OFF_K + GLA_QK]
        qd_all.append(q * jnp.exp(bcum))
        kd_all.append((k * jnp.exp(-bcum)).astype(BF16))
        kr_all.append(k * jnp.exp(blast - bcum))
        bl_all.append(blast)
        v_all.append(proj_ref[rows, OFF_V:OFF_V + GLA_WIDTH].astype(BF16))

    scores_all = []
    for c in range(n_chunks):
        q_stack = jnp.concatenate([qd_all[c] * m for m in head_masks], axis=0).astype(BF16)
        scores_all.append(
            jnp.where(causal4, _dot_nt(q_stack, kd_all[c]), 0.0).astype(BF16))
    o_intra_all = []
    for c in range(n_chunks):
        o_intra_all.append(jnp.concatenate(
            [_dot(scores_all[c][hd * GLA_CHUNK:(hd + 1) * GLA_CHUNK, :],
                  v_all[c][:, hd * GLA_DV:(hd + 1) * GLA_DV]) for hd in range(GLA_HEADS)], axis=1))
    kv_all, dcol_all = [], []
    for c in range(n_chunks):
        kt = jnp.concatenate(
            [kr_all[c], jnp.broadcast_to(bl_all[c], (GLA_CHUNK, GLA_QK))], axis=0).T
        dcol_all.append(jnp.exp(kt[:, GLA_CHUNK:GLA_CHUNK + 1]))
        kt_b = jnp.where(lane_c < GLA_CHUNK, kt, 0.0).astype(BF16)
        v_pad = jnp.concatenate([v_all[c], jnp.zeros_like(v_all[c])], axis=0)
        kv_all.append([_dot(kt_b[hd * GLA_DK:(hd + 1) * GLA_DK, :],
                            v_pad[:, hd * GLA_DV:(hd + 1) * GLA_DV]) for hd in range(GLA_HEADS)])

    o_all = []
    for c in range(n_chunks):
        state = state_ref[...]
        o_all.append(_dot(qd_all[c].astype(BF16), state.astype(BF16)) + o_intra_all[c])
        for hd in range(GLA_HEADS):
            rs = slice(hd * GLA_DK, (hd + 1) * GLA_DK)
            cs = slice(hd * GLA_DV, (hd + 1) * GLA_DV)
            state_ref[rs, cs] = dcol_all[c][rs, :] * state[rs, cs] + kv_all[c][hd]

    for c, rows in enumerate(chunk_rows):
        o = o_all[c]
        g_out = proj_ref[rows, OFF_GO:OFF_GO + GLA_WIDTH]
        o_n = jnp.concatenate(
            [_rms(o[:, hd * GLA_DV:(hd + 1) * GLA_DV], gng) for hd in range(GLA_HEADS)], axis=1)
        y = o_n * (g_out * jax.nn.sigmoid(g_out))
        ycat_ref[rows, CONV_WIDTH:CONV_WIDTH + GLA_WIDTH] = y.astype(BF16)

    x1 = x + _dot(ycat_ref[...], wout_ref[...])
    x1_ref[0] = x1
    h2 = _rms(x1, g2_ref[...])
    h2_hi, h2_lo = _split_bf16(h2)
    h2_ref[0] = h2_hi
    hi_terms = _dot(h2_hi, wrhl_ref[...])
    logits = (hi_terms[:, 0:LANES] + hi_terms[:, LANES:2 * LANES]
              + _dot(h2_lo, wrhl_ref[:, 0:LANES]) + br_ref[...])

    lane = lax.broadcasted_iota(I32, (ts, LANES), 1).astype(F32)
    work = logits
    sel = jnp.zeros((ts, LANES), F32)
    top_v, top_i, top_oh = [], [], []
    for _ in range(TOP_K):
        m = jnp.max(work, axis=-1, keepdims=True)
        idx = jnp.min(jnp.where(work == m, lane, float(LANES)), axis=-1, keepdims=True)
        oh = lane == idx
        top_v.append(m)
        top_i.append(idx)
        top_oh.append(oh)
        sel = sel + oh.astype(F32)
        work = jnp.where(oh, -jnp.inf, work)
    exps = [jnp.exp(tv - top_v[0]) for tv in top_v]
    denom = exps[0] + exps[1] + exps[2] + exps[3]
    gates = [e / denom for e in exps]

    ti = lax.broadcasted_iota(I32, (ts, ts), 0)
    tj = lax.broadcasted_iota(I32, (ts, ts), 1)
    strict_lower = (tj < ti).astype(BF16)
    local = _dot(strict_lower, sel.astype(BF16))
    carry = carry_ref[0:1, :]
    ranks = [jnp.sum(jnp.where(oh, local + carry, 0.0), axis=-1, keepdims=True) for oh in top_oh]

    lane_row = lax.broadcasted_iota(I32, (1, LANES), 1)
    slot_base = []
    before = jnp.zeros((1, LANES), F32)
    for u in range(ts // SORT_TILE):
        sub_cnt = jnp.sum(sel[u * SORT_TILE:(u + 1) * SORT_TILE, :], axis=0, keepdims=True)
        seg_start = sub_cnt
        shift = 1
        while shift < LANES:
            seg_start = seg_start + jnp.where(lane_row >= shift,
                                              pltpu.roll(seg_start, shift, 1), 0.0)
            shift *= 2
        seg_start = seg_start - sub_cnt
        seg_ref[u, 0:1, :] = carry + before
        seg_ref[u, 1:2, :] = sub_cnt
        seg_ref[u, 2:3, :] = seg_start
        seg_ref[u, 3:SUBLANES, :] = jnp.zeros((SUBLANES - 3, LANES), F32)
        slot_base.append(jnp.broadcast_to(seg_start - before, (SORT_TILE, LANES)))
        before = before + sub_cnt
    slot_base = jnp.concatenate(slot_base, axis=0)
    slots = [jnp.sum(jnp.where(oh, local + slot_base, 0.0), axis=-1, keepdims=True)
             for oh in top_oh]
    new_carry = carry + before
    carry_ref[...] = jnp.broadcast_to(new_carry, carry_ref.shape)
    cnt_ref[...] = jnp.broadcast_to(new_carry, cnt_ref.shape)

    meta = jnp.zeros((ts, LANES), F32)
    for kk in range(TOP_K):
        meta = jnp.where(lane == float(kk), top_i[kk], meta)
        meta = jnp.where(lane == float(TOP_K + kk), gates[kk], meta)
        meta = jnp.where(lane == float(2 * TOP_K + kk), ranks[kk], meta)
        meta = jnp.where(lane == float(3 * TOP_K + kk), slots[kk], meta)
    meta_ref[0] = meta
    route_ref[...] = meta.T[0:ROUTE_ROWS, :]


def _mixer_call(x, g1, win, convw, wgk, bgk, gng, wout, g2, wrhl, br):
    bsz, seq, d = x.shape
    ts = SEQ_TILE
    grid = (bsz, seq // ts)

    def const(shape):
        return pl.BlockSpec(shape, lambda b, s: (0,) * len(shape))

    tile = lambda w: pl.BlockSpec((1, ts, w), lambda b, s: (b, s, 0))
    return pl.pallas_call(
        _mixer_kernel,
        grid=grid,
        in_specs=[tile(d), const(g1.shape), const(win.shape), const(convw.shape),
                  const(wgk.shape), const(bgk.shape), const(gng.shape), const(wout.shape),
                  const(g2.shape), const(wrhl.shape), const(br.shape)],
        out_specs=[tile(d),
                   tile(d),
                   tile(LANES),
                   pl.BlockSpec((ROUTE_ROWS, ts), lambda b, s: (0, b * (seq // ts) + s)),
                   pl.BlockSpec((ts // SORT_TILE, SUBLANES, LANES),
                                lambda b, s: (b * (seq // ts) + s, 0, 0)),
                   const((SUBLANES, LANES))],
        out_shape=[jax.ShapeDtypeStruct((bsz, seq, d), F32),
                   jax.ShapeDtypeStruct((bsz, seq, d), BF16),
                   jax.ShapeDtypeStruct((bsz, seq, LANES), F32),
                   jax.ShapeDtypeStruct((ROUTE_ROWS, bsz * seq), F32),
                   jax.ShapeDtypeStruct((bsz * seq // SORT_TILE, SUBLANES, LANES), F32),
                   jax.ShapeDtypeStruct((SUBLANES, LANES), F32)],
        scratch_shapes=[pltpu.VMEM((ts, D_IN_PAD), F32),
                        pltpu.VMEM((ts + SUBLANES, CONV_WIDTH), F32),
                        pltpu.VMEM((ts, GLA_QK), F32),
                        pltpu.VMEM((ts, D_MODEL), BF16),
                        pltpu.VMEM((GLA_QK, GLA_WIDTH), F32),
                        pltpu.VMEM((SUBLANES, LANES), F32)],
        compiler_params=pltpu.CompilerParams(
            dimension_semantics=("arbitrary", "arbitrary"), vmem_limit_bytes=VMEM_LIMIT),
        name="mixer",
    )(x, g1, win, convw, wgk, bgk, gng, wout, g2, wrhl, br)


def _dispatch_kernel(run_xs_ref, run_buf_ref, run_len_ref, pad_lo_ref, pad_n_ref,
                     route_ref, h2_ref, xs_hbm, sbuf, zbuf, sem, zsem):
    i = pl.program_id(0)
    n = pl.num_programs(0)
    td = h2_ref.shape[0]
    n_rows = TOP_K * td
    slot = i % 2

    def zero_fill(wait):
        def fill(g, carry):
            ln = pl.multiple_of(pad_n_ref[g], ROW_TILES)

            @pl.when(ln > 0)
            def _():
                lo = pl.multiple_of(pad_lo_ref[g], ROW_TILES)
                cp = pltpu.make_async_copy(zbuf.at[pl.ds(0, ln), :], xs_hbm.at[pl.ds(lo, ln), :],
                                           zsem.at[0])
                if wait:
                    cp.wait()
                else:
                    cp.start()
            return carry

        lax.fori_loop(0, pad_lo_ref.shape[0], fill, 0)

    def wait_runs(buf_slot):
        pltpu.make_async_copy(sbuf.at[buf_slot], xs_hbm.at[pl.ds(0, n_rows * ROW_TILES), :],
                              sem.at[buf_slot]).wait()

    @pl.when(i == 0)
    def _():
        zbuf[...] = jnp.zeros_like(zbuf)
        zero_fill(wait=False)

    @pl.when(i >= 2)
    def _():
        wait_runs(slot)

    route = route_ref[...]
    row = lax.broadcasted_iota(I32, (n_rows, td), 0).astype(F32)
    pick = jnp.zeros((n_rows, td), F32)
    for kk in range(TOP_K):
        pick = jnp.where(row == route[3 * TOP_K + kk:3 * TOP_K + kk + 1, :], 1.0, pick)
    rows_sorted = _dot(pick.astype(BF16), h2_ref[...])
    words = _pack_rows(rows_sorted)
    for j in range(ROW_TILES):
        sbuf[slot, pl.ds(j, n_rows, stride=ROW_TILES), :] = words[:, j * LANES:(j + 1) * LANES]

    def body(e, carry):
        g = i * N_EXPERTS + e
        ln = pl.multiple_of(run_len_ref[g], ROW_TILES)

        @pl.when(ln > 0)
        def _():
            src = pl.multiple_of(run_buf_ref[g], ROW_TILES)
            dst = pl.multiple_of(run_xs_ref[g], ROW_TILES)
            pltpu.make_async_copy(sbuf.at[slot, pl.ds(src, ln), :], xs_hbm.at[pl.ds(dst, ln), :],
                                  sem.at[slot]).start()
        return carry

    lax.fori_loop(0, N_EXPERTS, body, 0)

    @pl.when(i == n - 1)
    def _():
        @pl.when(i >= 1)
        def _():
            wait_runs(1 - slot)
        wait_runs(slot)
        zero_fill(wait=True)


def _dispatch_call(run_xs, run_buf, run_len, pad_lo, pad_n, route, h2, n_rows):
    t, d = h2.shape
    td = SORT_TILE
    grid_spec = pltpu.PrefetchScalarGridSpec(
        num_scalar_prefetch=5,
        grid=(t // td,),
        in_specs=[pl.BlockSpec((ROUTE_ROWS, td), lambda i, *_: (0, i)),
                  pl.BlockSpec((td, d), lambda i, *_: (i, 0))],
        out_specs=pl.BlockSpec(memory_space=pl.ANY),
        scratch_shapes=[pltpu.VMEM((2, TOP_K * td * ROW_TILES, LANES), U32),
                        pltpu.VMEM((MOE_BLOCK * ROW_TILES, LANES), U32),
                        pltpu.SemaphoreType.DMA((2,)),
                        pltpu.SemaphoreType.DMA((1,))],
    )
    return pl.pallas_call(
        _dispatch_kernel,
        grid_spec=grid_spec,
        out_shape=jax.ShapeDtypeStruct((n_rows * ROW_TILES, LANES), U32),
        compiler_params=pltpu.CompilerParams(
            dimension_semantics=("arbitrary",), vmem_limit_bytes=VMEM_LIMIT,
            has_side_effects=True),
        name="dispatch",
    )(run_xs, run_buf, run_len, pad_lo, pad_n, route, h2)


def _experts_kernel(bexp_ref, first_ref, nxt_ref, nval_ref, nreal_ref,
                    xs_ref, wgu_hbm, bgu_ref, wd_hbm, bd_ref,
                    ys_ref,
                    xb_ref, act_ref, wgu_stage, wd_stage, wgu_bf, wd_bf, wsem):
    i = pl.program_id(0)
    nreal = nreal_ref[0]
    bm = xb_ref.shape[0]

    def weight_copies(e):
        return (pltpu.make_async_copy(wgu_hbm.at[e], wgu_stage, wsem.at[0]),
                pltpu.make_async_copy(wd_hbm.at[e], wd_stage, wsem.at[1]))

    @pl.when(i >= nreal)
    def _():
        ys_ref[...] = jnp.zeros_like(ys_ref)

    @pl.when(i < nreal)
    def _():
        e = bexp_ref[i]

        @pl.when(i == 0)
        def _():
            for cp in weight_copies(e):
                cp.start(priority=1)

        @pl.when(first_ref[i] == 1)
        def _():
            for cp in weight_copies(e):
                cp.wait()

            def cast_gu(r, carry):
                rows = pl.ds(pl.multiple_of(r * CAST_ROWS, CAST_ROWS), CAST_ROWS)
                wgu_bf[rows, :] = wgu_stage[rows, :].astype(BF16)
                return carry

            def cast_d(r, carry):
                rows = pl.ds(pl.multiple_of(r * CAST_ROWS, CAST_ROWS), CAST_ROWS)
                wd_bf[rows, :] = wd_stage[rows, :].astype(BF16)
                return carry

            lax.fori_loop(0, D_MODEL // CAST_ROWS, cast_gu, 0)
            lax.fori_loop(0, D_FF // CAST_ROWS, cast_d, 0)

            @pl.when(nxt_ref[i] >= 0)
            def _():
                for cp in weight_copies(nxt_ref[i]):
                    cp.start(priority=1)

        def mlp(rows):
            for j in range(ROW_TILES):
                lo, hi = _unpack_words(xs_ref[pl.ds(j, rows, stride=ROW_TILES), :])
                xb_ref[0:rows, j * LANES:(j + 1) * LANES] = lo
                xb_ref[0:rows, D_MODEL // 2 + j * LANES:D_MODEL // 2 + (j + 1) * LANES] = hi
            for c in range(D_FF // FF_CHUNK):
                f0 = c * FF_CHUNK
                xb = xb_ref[0:rows, :]
                gate = _dot(xb, wgu_bf[:, f0:f0 + FF_CHUNK]) + bgu_ref[0, :, f0:f0 + FF_CHUNK]
                up = (_dot(xb, wgu_bf[:, D_FF + f0:D_FF + f0 + FF_CHUNK])
                      + bgu_ref[0, :, D_FF + f0:D_FF + f0 + FF_CHUNK])
                gate = jnp.minimum(gate, SWIGLU_LIMIT)
                up = jnp.clip(up, -SWIGLU_LIMIT, SWIGLU_LIMIT)
                glu = gate * jax.nn.sigmoid(gate * SWIGLU_ALPHA)
                act_ref[0:rows, f0:f0 + FF_CHUNK] = ((up + 1.0) * glu).astype(BF16)
            out = _pack_rows(_dot(act_ref[0:rows, :], wd_bf[...]) + bd_ref[0])
            for j in range(ROW_TILES):
                ys_ref[pl.ds(j, rows, stride=ROW_TILES), :] = out[:, j * LANES:(j + 1) * LANES]

        n_routed = nval_ref[i]
        for rows in range(TAIL_ROWS, bm + 1, TAIL_ROWS):
            @pl.when((n_routed > rows - TAIL_ROWS) & (n_routed <= rows))
            def _(rows=rows):
                mlp(rows)
                if rows < bm:
                    ys_ref[rows * ROW_TILES:bm * ROW_TILES, :] = jnp.zeros(
                        ((bm - rows) * ROW_TILES, LANES), U32)


def _experts_call(bexp, first, nxt, nval, nreal, xs, wgu, bgu, wd, bd):
    bm = MOE_BLOCK
    n_blocks = xs.shape[0] // (bm * ROW_TILES)
    bgu3 = bgu.reshape(N_EXPERTS, 1, 2 * D_FF)
    bd3 = bd.reshape(N_EXPERTS, 1, D_MODEL)
    grid_spec = pltpu.PrefetchScalarGridSpec(
        num_scalar_prefetch=5,
        grid=(n_blocks,),
        in_specs=[
            pl.BlockSpec((bm * ROW_TILES, LANES),
                         lambda i, be, fi, nx, nv, nr: (jnp.minimum(i, nr[0] - 1), 0)),
            pl.BlockSpec(memory_space=pl.ANY),
            pl.BlockSpec((1, 1, 2 * D_FF), lambda i, be, fi, nx, nv, nr: (be[i], 0, 0)),
            pl.BlockSpec(memory_space=pl.ANY),
            pl.BlockSpec((1, 1, D_MODEL), lambda i, be, fi, nx, nv, nr: (be[i], 0, 0)),
        ],
        out_specs=pl.BlockSpec((bm * ROW_TILES, LANES), lambda i, be, fi, nx, nv, nr: (i, 0)),
        scratch_shapes=[pltpu.VMEM((bm, D_MODEL), BF16),
                        pltpu.VMEM((bm, D_FF), BF16),
                        pltpu.VMEM((D_MODEL, 2 * D_FF), F32),
                        pltpu.VMEM((D_FF, D_MODEL), F32),
                        pltpu.VMEM((D_MODEL, 2 * D_FF), BF16),
                        pltpu.VMEM((D_FF, D_MODEL), BF16),
                        pltpu.SemaphoreType.DMA((2,))],
    )
    return pl.pallas_call(
        _experts_kernel,
        grid_spec=grid_spec,
        out_shape=jax.ShapeDtypeStruct(xs.shape, U32),
        compiler_params=pltpu.CompilerParams(
            dimension_semantics=("arbitrary",), vmem_limit_bytes=VMEM_LIMIT),
        name="experts",
    )(bexp, first, nxt, nval, nreal, xs, wgu, bgu3, wd, bd3)


def _combine_kernel(src_ref, dst_ref, len_ref, x1_ref, meta_ref, g_ref, ys_hbm, out_ref,
                    ybuf, ysort_ref, sem):
    i = pl.program_id(0)
    n = pl.num_programs(0)
    tb = x1_ref.shape[0]
    n_rows = TOP_K * tb
    slot = i % 2

    def start_runs(tile, buf_slot):
        def body(e, carry):
            g = tile * N_EXPERTS + e
            ln = pl.multiple_of(len_ref[g], ROW_TILES)

            @pl.when(ln > 0)
            def _():
                src = pl.multiple_of(src_ref[g], ROW_TILES)
                dst = pl.multiple_of(dst_ref[g], ROW_TILES)
                pltpu.make_async_copy(ys_hbm.at[pl.ds(src, ln), :],
                                      ybuf.at[buf_slot, pl.ds(dst, ln), :], sem.at[buf_slot]).start()
            return carry
        lax.fori_loop(0, N_EXPERTS, body, 0)

    @pl.when(i == 0)
    def _():
        start_runs(0, 0)

    @pl.when(i + 1 < n)
    def _():
        start_runs(i + 1, 1 - slot)

    pltpu.make_async_copy(ys_hbm.at[pl.ds(0, n_rows * ROW_TILES), :], ybuf.at[slot],
                          sem.at[slot]).wait()

    for j in range(ROW_TILES):
        lo, hi = _unpack_words(ybuf[slot, pl.ds(j, n_rows, stride=ROW_TILES), :])
        ysort_ref[:, j * LANES:(j + 1) * LANES] = lo
        ysort_ref[:, D_MODEL // 2 + j * LANES:D_MODEL // 2 + (j + 1) * LANES] = hi

    meta = meta_ref[...]
    col = lax.broadcasted_iota(I32, (tb, n_rows), 1).astype(F32)
    weights = jnp.zeros((tb, n_rows), F32)
    for kk in range(TOP_K):
        weights = jnp.where(col == meta[:, 3 * TOP_K + kk:3 * TOP_K + kk + 1],
                            meta[:, TOP_K + kk:TOP_K + kk + 1], weights)
    acc = x1_ref[...] + _dot(weights.astype(BF16), ysort_ref[...])
    out_ref[...] = _rms(acc, g_ref[...])


def _combine_call(seg_src, seg_dst, seg_len, x1, meta, g, ys):
    t, d = x1.shape
    tb = SORT_TILE
    nb = t // tb
    grid_spec = pltpu.PrefetchScalarGridSpec(
        num_scalar_prefetch=3,
        grid=(nb,),
        in_specs=[pl.BlockSpec((tb, d), lambda i, a, b, c: (i, 0)),
                  pl.BlockSpec((tb, LANES), lambda i, a, b, c: (i, 0)),
                  pl.BlockSpec((1, d), lambda i, a, b, c: (0, 0)),
                  pl.BlockSpec(memory_space=pl.ANY)],
        out_specs=pl.BlockSpec((tb, d), lambda i, a, b, c: (i, 0)),
        scratch_shapes=[pltpu.VMEM((2, TOP_K * tb * ROW_TILES, LANES), U32),
                        pltpu.VMEM((TOP_K * tb, D_MODEL), BF16),
                        pltpu.SemaphoreType.DMA((2,))],
    )
    return pl.pallas_call(
        _combine_kernel,
        grid_spec=grid_spec,
        out_shape=jax.ShapeDtypeStruct((t, d), F32),
        compiler_params=pltpu.CompilerParams(
            dimension_semantics=("arbitrary",), vmem_limit_bytes=VMEM_LIMIT),
        name="combine",
    )(seg_src, seg_dst, seg_len, x1, meta, g, ys)


def _routing_tables(counts, t):
    bm = MOE_BLOCK
    n_blocks = t * TOP_K // bm + N_EXPERTS
    eids = jnp.arange(N_EXPERTS, dtype=I32)
    nblk_e = (counts + bm - 1) // bm
    blk_end = jnp.sum(jnp.where(eids[None, :] <= eids[:, None], nblk_e[None, :], 0), axis=1)
    blk_start = blk_end - nblk_e
    nreal = blk_end[N_EXPERTS - 1]
    pad_start = blk_start * bm
    blk = jnp.arange(n_blocks, dtype=I32)
    bexp = jnp.minimum(jnp.sum((blk_end[None, :] <= blk[:, None]).astype(I32), axis=1),
                       N_EXPERTS - 1)
    blk_is_e = bexp[:, None] == eids[None, :]
    pick = lambda tab: jnp.sum(jnp.where(blk_is_e, tab[None, :], 0), axis=1)
    first = (blk == pick(blk_start)).astype(I32)
    nxt_e = jnp.sum((blk_end[None, :] <= blk_end[:, None]).astype(I32), axis=1)
    nxt_e = jnp.where(blk_end < nreal, jnp.minimum(nxt_e, N_EXPERTS - 1), -1)
    nxt = pick(nxt_e)
    nval = jnp.clip(pick(counts) - (blk - pick(blk_start)) * bm, 0, bm)
    tail_blk = jnp.arange(N_EXPERTS, dtype=I32) + nreal
    pad_lo = jnp.concatenate([pad_start + counts, jnp.minimum(tail_blk, n_blocks - 1) * bm])
    pad_n = jnp.concatenate([nblk_e * bm - counts, jnp.where(tail_blk < n_blocks, bm, 0)])
    return (pad_start, bexp, first, nxt, nval, nreal.reshape(1).astype(I32), n_blocks,
            pad_lo * ROW_TILES, pad_n * ROW_TILES)


def kernel(x, norm_mix_g, w_in, conv_w, w_gk_up, b_gk_up, gla_norm_g, w_out, norm_ffn_g,
           w_router, b_router, w_gate_up, b_gate_up, w_down, b_down, norm_final_g):
    bsz, seq, d = x.shape
    t = bsz * seq
    assert w_in.shape[0] == 1, "single-layer trunk only"
    l = 0
    d_in = w_in.shape[-1]
    win = jnp.pad(w_in[l].astype(BF16), ((0, 0), (0, D_IN_PAD - d_in)))
    wgk = jnp.pad(w_gk_up[l], ((0, LANES - GLA_RANK), (0, 0))).astype(BF16)
    wr = jnp.pad(w_router[l], ((0, 0), (0, LANES - N_EXPERTS)))
    wrh = wr.astype(BF16)
    wrhl = jnp.concatenate([wrh, (wr - wrh.astype(F32)).astype(BF16)], axis=1)
    br = jnp.pad(b_router[l], (0, LANES - N_EXPERTS), constant_values=NEG_BIG).reshape(1, LANES)

    x1, h2, meta, route, seg, cnt = _mixer_call(
        x, norm_mix_g[l].reshape(1, d), win, conv_w[l], wgk, b_gk_up[l].reshape(1, GLA_QK),
        gla_norm_g[l].reshape(1, GLA_DV), w_out[l].astype(BF16), norm_ffn_g[l].reshape(1, d),
        wrhl, br)

    meta2 = meta.reshape(t, LANES)
    counts = cnt[0, :N_EXPERTS].astype(I32)
    (pad_start, bexp, first, nxt, nval, nreal, n_blocks, pad_lo,
     pad_n) = _routing_tables(counts, t)
    seg_src = ((pad_start[None, :] + seg[:, 0, :N_EXPERTS].astype(I32)) * ROW_TILES).reshape(-1)
    seg_len = (seg[:, 1, :N_EXPERTS].astype(I32) * ROW_TILES).reshape(-1)
    seg_dst = (seg[:, 2, :N_EXPERTS].astype(I32) * ROW_TILES).reshape(-1)
    xs = _dispatch_call(seg_src, seg_dst, seg_len, pad_lo, pad_n, route, h2.reshape(t, d),
                        n_blocks * MOE_BLOCK)
    ys = _experts_call(bexp, first, nxt, nval, nreal, xs, w_gate_up[l], b_gate_up[l], w_down[l],
                       b_down[l])
    out = _combine_call(seg_src, seg_dst, seg_len, x1.reshape(t, d), meta2,
                        norm_final_g.reshape(1, d), ys)
    return out.reshape(bsz, seq, d)
```

```python
import jax
import jax.numpy as jnp
from jax import lax
from jax.experimental import pallas as pl
from jax.experimental.pallas import tpu as pltpu

F32 = jnp.float32
BF16 = jnp.bfloat16
I32 = jnp.int32
U32 = jnp.uint32

D_MODEL = 1024
CONV_WIDTH = 512
CONV_K = 3
GLA_WIDTH = 512
GLA_HEADS = 4
GLA_DV = 128
GLA_DK = 64
GLA_QK = GLA_HEADS * GLA_DK
GLA_RANK = 16
GLA_NORMALIZER = 16.0
GLA_CHUNK = 64
N_EXPERTS = 32
TOP_K = 4
D_FF = 1024
SWIGLU_LIMIT = 7.0
SWIGLU_ALPHA = 1.702
RMS_EPS = 1e-5

LANES = 128
SUBLANES = 8
ROW_TILES = D_MODEL // LANES // 2
ROUTE_ROWS = 16

OFF_UH = 0
OFF_GB = OFF_UH + CONV_WIDTH
OFF_GC = OFF_GB + CONV_WIDTH
OFF_Q = OFF_GC + CONV_WIDTH
OFF_K = OFF_Q + GLA_QK
OFF_V = OFF_K + GLA_QK
OFF_GO = OFF_V + GLA_WIDTH
OFF_GKL = OFF_GO + GLA_WIDTH
D_IN_PAD = OFF_GKL + LANES

SEQ_TILE = 512
SORT_TILE = 256
MOE_BLOCK = 512
FF_CHUNK = 256
TAIL_ROWS = 128
CAST_ROWS = 128
NEG_BIG = -1e30
VMEM_LIMIT = 56 * 1024 * 1024


def _rms(x, g):
    return x * lax.rsqrt(jnp.mean(x * x, axis=-1, keepdims=True) + RMS_EPS) * g


def _dot(a, b):
    return jnp.dot(a, b, preferred_element_type=F32)


def _dot_nt(a, b):
    return lax.dot_general(a, b, (((1,), (1,)), ((), ())), preferred_element_type=F32)


def _pack_rows(x):
    half = x.shape[1] // 2
    xr = x.astype(BF16).astype(F32)
    lo = lax.bitcast_convert_type(xr[:, :half], U32) >> 16
    hi = lax.bitcast_convert_type(xr[:, half:], U32) & jnp.uint32(0xFFFF0000)
    return hi | lo


def _unpack_words(w):
    lo = lax.bitcast_convert_type(w << 16, F32).astype(BF16)
    hi = lax.bitcast_convert_type(w & jnp.uint32(0xFFFF0000), F32).astype(BF16)
    return lo, hi


def _split_bf16(x):
    hi = x.astype(BF16)
    lo = (x - hi.astype(F32)).astype(BF16)
    return hi, lo


def _mixer_kernel(x_ref, g1_ref, win_ref, convw_ref, wgk_ref, bgk_ref, gng_ref, wout_ref,
                  g2_ref, wrhl_ref, br_ref,
                  x1_ref, h2_ref, meta_ref, route_ref, seg_ref, cnt_ref,
                  proj_ref, ubuf_ref, la_ref, ycat_ref, state_ref, carry_ref):
    ts = x_ref.shape[1]
    b_idx = pl.program_id(0)
    s_idx = pl.program_id(1)

    @pl.when(s_idx == 0)
    def _():
        state_ref[...] = jnp.zeros_like(state_ref)
        ubuf_ref[0:SUBLANES, :] = jnp.zeros((SUBLANES, CONV_WIDTH), F32)

    @pl.when((s_idx == 0) & (b_idx == 0))
    def _():
        carry_ref[...] = jnp.zeros_like(carry_ref)

    x = x_ref[0]
    h = _rms(x, g1_ref[...]).astype(BF16)
    proj_ref[...] = _dot(h, win_ref[...])

    u = proj_ref[:, OFF_GC:OFF_GC + CONV_WIDTH] * proj_ref[:, OFF_UH:OFF_UH + CONV_WIDTH]
    ubuf_ref[SUBLANES:SUBLANES + ts, :] = u
    u1 = ubuf_ref[pl.ds(SUBLANES - 1, ts), :]
    u2 = ubuf_ref[pl.ds(SUBLANES - 2, ts), :]
    conv = convw_ref[0:1, :] * u2 + convw_ref[1:2, :] * u1 + convw_ref[2:3, :] * u
    ycat_ref[:, 0:CONV_WIDTH] = (proj_ref[:, OFF_GB:OFF_GB + CONV_WIDTH] * conv).astype(BF16)
    ubuf_ref[0:SUBLANES, :] = ubuf_ref[ts:ts + SUBLANES, :]

    gk = _dot(proj_ref[:, OFF_GKL:OFF_GKL + LANES].astype(BF16), wgk_ref[...]) + bgk_ref[...]
    log_sig = jnp.minimum(gk, 0.0) - jnp.log1p(jnp.exp(-jnp.abs(gk)))
    la_ref[...] = log_sig / GLA_NORMALIZER

    ci = lax.broadcasted_iota(I32, (GLA_CHUNK, GLA_CHUNK), 0)
    cj = lax.broadcasted_iota(I32, (GLA_CHUNK, GLA_CHUNK), 1)
    tri_incl = (cj <= ci).astype(BF16)
    causal = cj <= ci
    causal4 = jnp.concatenate([causal] * GLA_HEADS, axis=0)
    lane_qk = lax.broadcasted_iota(I32, (1, GLA_QK), 1)
    head_masks = [((lane_qk >= hd * GLA_DK) & (lane_qk < (hd + 1) * GLA_DK)).astype(F32)
                  for hd in range(GLA_HEADS)]
    gng = gng_ref[...]

    n_chunks = ts // GLA_CHUNK
    chunk_rows = [pl.ds(c * GLA_CHUNK, GLA_CHUNK) for c in range(n_chunks)]
    lane_c = lax.broadcasted_iota(I32, (GLA_QK, 2 * GLA_CHUNK), 1)
    qd_all, kd_all, kr_all, bl_all, v_all = [], [], [], [], []
    for rows in chunk_rows:
        la_hi, la_lo = _split_bf16(la_ref[rows, :])
        bcum = _dot(tri_incl, la_hi) + _dot(tri_incl, la_lo)
        blast = bcum[GLA_CHUNK - 1:GLA_CHUNK, :]
        q = proj_ref[rows, OFF_Q:OFF_Q + GLA_QK] * (GLA_DK ** -0.5)
        k = proj_ref[rows, OFF_K:OFF_K + GLA_QK]
        qd_all.append(q * jnp.exp(bcum))
        kd_all.append((k * jnp.exp(-bcum)).astype(BF16))
        kr_all.append(k * jnp.exp(blast - bcum))
        bl_all.append(blast)
        v_all.append(proj_ref[rows, OFF_V:OFF_V + GLA_WIDTH].astype(BF16))

    scores_all = []
    for c in range(n_chunks):
        q_stack = jnp.concatenate([qd_all[c] * m for m in head_masks], axis=0).astype(BF16)
        scores_all.append(
            jnp.where(causal4, _dot_nt(q_stack, kd_all[c]), 0.0).astype(BF16))
    o_intra_all = []
    for c in range(n_chunks):
        o_intra_all.append(jnp.concatenate(
            [_dot(scores_all[c][hd * GLA_CHUNK:(hd + 1) * GLA_CHUNK, :],
                  v_all[c][:, hd * GLA_DV:(hd + 1) * GLA_DV]) for hd in range(GLA_HEADS)], axis=1))
    kv_all, dcol_all = [], []
    for c in range(n_chunks):
        kt = jnp.concatenate(
            [kr_all[c], jnp.broadcast_to(bl_all[c], (GLA_CHUNK, GLA_QK))], axis=0).T
        dcol_all.append(jnp.exp(kt[:, GLA_CHUNK:GLA_CHUNK + 1]))
        kt_b = jnp.where(lane_c < GLA_CHUNK, kt, 0.0).astype(BF16)
        v_pad = jnp.concatenate([v_all[c], jnp.zeros_like(v_all[c])], axis=0)
        kv_all.append([_dot(kt_b[hd * GLA_DK:(hd + 1) * GLA_DK, :],
                            v_pad[:, hd * GLA_DV:(hd + 1) * GLA_DV]) for hd in range(GLA_HEADS)])

    o_all = []
    for c in range(n_chunks):
        state = state_ref[...]
        o_all.append(_dot(qd_all[c].astype(BF16), state.astype(BF16)) + o_intra_all[c])
        for hd in range(GLA_HEADS):
            rs = slice(hd * GLA_DK, (hd + 1) * GLA_DK)
            cs = slice(hd * GLA_DV, (hd + 1) * GLA_DV)
            state_ref[rs, cs] = dcol_all[c][rs, :] * state[rs, cs] + kv_all[c][hd]

    for c, rows in enumerate(chunk_rows):
        o = o_all[c]
        g_out = proj_ref[rows, OFF_GO:OFF_GO + GLA_WIDTH]
        o_n = jnp.concatenate(
            [_rms(o[:, hd * GLA_DV:(hd + 1) * GLA_DV], gng) for hd in range(GLA_HEADS)], axis=1)
        y = o_n * (g_out * jax.nn.sigmoid(g_out))
        ycat_ref[rows, CONV_WIDTH:CONV_WIDTH + GLA_WIDTH] = y.astype(BF16)

    x1 = x + _dot(ycat_ref[...], wout_ref[...])
    x1_ref[0] = x1
    h2 = _rms(x1, g2_ref[...])
    h2_hi, h2_lo = _split_bf16(h2)
    h2_ref[0] = h2_hi
    hi_terms = _dot(h2_hi, wrhl_ref[...])
    logits = (hi_terms[:, 0:LANES] + hi_terms[:, LANES:2 * LANES]
              + _dot(h2_lo, wrhl_ref[:, 0:LANES]) + br_ref[...])

    lt = logits.T[0:N_EXPERTS, :]
    erow = lax.broadcasted_iota(I32, (N_EXPERTS, ts), 0).astype(F32)
    work = lt
    sel = jnp.zeros((N_EXPERTS, ts), F32)
    top_v, top_i, top_oh = [], [], []
    for _ in range(TOP_K):
        m = jnp.max(work, axis=0, keepdims=True)
        idx = jnp.min(jnp.where(work == m, erow, float(N_EXPERTS)), axis=0, keepdims=True)
        oh = erow == idx
        top_v.append(m)
        top_i.append(idx)
        top_oh.append(oh)
        sel = sel + oh.astype(F32)
        work = jnp.where(oh, -jnp.inf, work)
    exps = [jnp.exp(tv - top_v[0]) for tv in top_v]
    denom = exps[0] + exps[1] + exps[2] + exps[3]
    gates = [e / denom for e in exps]

    tsrc = lax.broadcasted_iota(I32, (ts, ts), 0)
    tdst = lax.broadcasted_iota(I32, (ts, ts), 1)
    earlier = (tsrc < tdst).astype(BF16)
    local = _dot(sel.astype(BF16), earlier)
    carry = carry_ref[:, 0:1]
    ranks = [jnp.sum(jnp.where(oh, local + carry, 0.0), axis=0, keepdims=True) for oh in top_oh]

    ei = lax.broadcasted_iota(I32, (N_EXPERTS, N_EXPERTS), 0)
    ej = lax.broadcasted_iota(I32, (N_EXPERTS, N_EXPERTS), 1)
    lower_experts = (ej < ei).astype(BF16)
    lane_f = lax.broadcasted_iota(I32, (N_EXPERTS, LANES), 1)
    slot_base = []
    before = jnp.zeros((N_EXPERTS, 1), F32)
    for u in range(ts // SORT_TILE):
        sub_cnt = jnp.sum(sel[:, u * SORT_TILE:(u + 1) * SORT_TILE], axis=1, keepdims=True)
        seg_start = _dot(lower_experts,
                         jnp.broadcast_to(sub_cnt, (N_EXPERTS, LANES)).astype(BF16))[:, 0:1]
        seg_ref[u] = jnp.where(lane_f == 0, carry + before,
                               jnp.where(lane_f == 1, sub_cnt,
                                         jnp.where(lane_f == 2, seg_start, 0.0)))
        slot_base.append(jnp.broadcast_to(seg_start - before, (N_EXPERTS, SORT_TILE)))
        before = before + sub_cnt
    slot_base = jnp.concatenate(slot_base, axis=1)
    slots = [jnp.sum(jnp.where(oh, local + slot_base, 0.0), axis=0, keepdims=True)
             for oh in top_oh]
    new_carry = carry + before
    carry_ref[...] = jnp.broadcast_to(new_carry, carry_ref.shape)
    cnt_ref[...] = jnp.broadcast_to(new_carry, cnt_ref.shape)

    route = jnp.concatenate(top_i + gates + ranks + slots, axis=0)
    route_ref[...] = route
    meta_ref[0] = jnp.concatenate(
        [route, jnp.zeros((LANES - ROUTE_ROWS, ts), F32)], axis=0).T


def _mixer_call(x, g1, win, convw, wgk, bgk, gng, wout, g2, wrhl, br):
    bsz, seq, d = x.shape
    ts = SEQ_TILE
    grid = (bsz, seq // ts)

    def const(shape):
        return pl.BlockSpec(shape, lambda b, s: (0,) * len(shape))

    tile = lambda w: pl.BlockSpec((1, ts, w), lambda b, s: (b, s, 0))
    return pl.pallas_call(
        _mixer_kernel,
        grid=grid,
        in_specs=[tile(d), const(g1.shape), const(win.shape), const(convw.shape),
                  const(wgk.shape), const(bgk.shape), const(gng.shape), const(wout.shape),
                  const(g2.shape), const(wrhl.shape), const(br.shape)],
        out_specs=[tile(d),
                   tile(d),
                   tile(LANES),
                   pl.BlockSpec((ROUTE_ROWS, ts), lambda b, s: (0, b * (seq // ts) + s)),
                   pl.BlockSpec((ts // SORT_TILE, N_EXPERTS, LANES),
                                lambda b, s: (b * (seq // ts) + s, 0, 0)),
                   const((N_EXPERTS, LANES))],
        out_shape=[jax.ShapeDtypeStruct((bsz, seq, d), F32),
                   jax.ShapeDtypeStruct((bsz, seq, d), BF16),
                   jax.ShapeDtypeStruct((bsz, seq, LANES), F32),
                   jax.ShapeDtypeStruct((ROUTE_ROWS, bsz * seq), F32),
                   jax.ShapeDtypeStruct((bsz * seq // SORT_TILE, N_EXPERTS, LANES), F32),
                   jax.ShapeDtypeStruct((N_EXPERTS, LANES), F32)],
        scratch_shapes=[pltpu.VMEM((ts, D_IN_PAD), F32),
                        pltpu.VMEM((ts + SUBLANES, CONV_WIDTH), F32),
                        pltpu.VMEM((ts, GLA_QK), F32),
                        pltpu.VMEM((ts, D_MODEL), BF16),
                        pltpu.VMEM((GLA_QK, GLA_WIDTH), F32),
                        pltpu.VMEM((N_EXPERTS, LANES), F32)],
        compiler_params=pltpu.CompilerParams(
            dimension_semantics=("arbitrary", "arbitrary"), vmem_limit_bytes=VMEM_LIMIT),
        name="mixer",
    )(x, g1, win, convw, wgk, bgk, gng, wout, g2, wrhl, br)


def _dispatch_kernel(run_xs_ref, run_buf_ref, run_len_ref, pad_lo_ref, pad_n_ref,
                     route_ref, h2_ref, xs_hbm, sbuf, zbuf, sem, zsem):
    i = pl.program_id(0)
    n = pl.num_programs(0)
    td = h2_ref.shape[0]
    n_rows = TOP_K * td
    slot = i % 2

    def zero_fill(wait):
        def fill(g, carry):
            ln = pl.multiple_of(pad_n_ref[g], ROW_TILES)

            @pl.when(ln > 0)
            def _():
                lo = pl.multiple_of(pad_lo_ref[g], ROW_TILES)
                cp = pltpu.make_async_copy(zbuf.at[pl.ds(0, ln), :], xs_hbm.at[pl.ds(lo, ln), :],
                                           zsem.at[0])
                if wait:
                    cp.wait()
                else:
                    cp.start()
            return carry

        lax.fori_loop(0, pad_lo_ref.shape[0], fill, 0)

    def wait_runs(buf_slot):
        pltpu.make_async_copy(sbuf.at[buf_slot], xs_hbm.at[pl.ds(0, n_rows * ROW_TILES), :],
                              sem.at[buf_slot]).wait()

    @pl.when(i == 0)
    def _():
        zbuf[...] = jnp.zeros_like(zbuf)
        zero_fill(wait=False)

    @pl.when(i >= 2)
    def _():
        wait_runs(slot)

    route = route_ref[...]
    row = lax.broadcasted_iota(I32, (n_rows, td), 0).astype(F32)
    pick = jnp.zeros((n_rows, td), F32)
    for kk in range(TOP_K):
        pick = jnp.where(row == route[3 * TOP_K + kk:3 * TOP_K + kk + 1, :], 1.0, pick)
    rows_sorted = _dot(pick.astype(BF16), h2_ref[...])
    words = _pack_rows(rows_sorted)
    for j in range(ROW_TILES):
        sbuf[slot, pl.ds(j, n_rows, stride=ROW_TILES), :] = words[:, j * LANES:(j + 1) * LANES]

    def body(e, carry):
        g = i * N_EXPERTS + e
        ln = pl.multiple_of(run_len_ref[g], ROW_TILES)

        @pl.when(ln > 0)
        def _():
            src = pl.multiple_of(run_buf_ref[g], ROW_TILES)
            dst = pl.multiple_of(run_xs_ref[g], ROW_TILES)
            pltpu.make_async_copy(sbuf.at[slot, pl.ds(src, ln), :], xs_hbm.at[pl.ds(dst, ln), :],
                                  sem.at[slot]).start()
        return carry

    lax.fori_loop(0, N_EXPERTS, body, 0)

    @pl.when(i == n - 1)
    def _():
        @pl.when(i >= 1)
        def _():
            wait_runs(1 - slot)
        wait_runs(slot)
        zero_fill(wait=True)


def _dispatch_call(run_xs, run_buf, run_len, pad_lo, pad_n, route, h2, n_rows):
    t, d = h2.shape
    td = SORT_TILE
    grid_spec = pltpu.PrefetchScalarGridSpec(
        num_scalar_prefetch=5,
        grid=(t // td,),
        in_specs=[pl.BlockSpec((ROUTE_ROWS, td), lambda i, *_: (0, i)),
                  pl.BlockSpec((td, d), lambda i, *_: (i, 0))],
        out_specs=pl.BlockSpec(memory_space=pl.ANY),
        scratch_shapes=[pltpu.VMEM((2, TOP_K * td * ROW_TILES, LANES), U32),
                        pltpu.VMEM((MOE_BLOCK * ROW_TILES, LANES), U32),
                        pltpu.SemaphoreType.DMA((2,)),
                        pltpu.SemaphoreType.DMA((1,))],
    )
    return pl.pallas_call(
        _dispatch_kernel,
        grid_spec=grid_spec,
        out_shape=jax.ShapeDtypeStruct((n_rows * ROW_TILES, LANES), U32),
        compiler_params=pltpu.CompilerParams(
            dimension_semantics=("arbitrary",), vmem_limit_bytes=VMEM_LIMIT,
            has_side_effects=True),
        name="dispatch",
    )(run_xs, run_buf, run_len, pad_lo, pad_n, route, h2)


def _experts_kernel(bexp_ref, first_ref, nxt_ref, nval_ref, nreal_ref,
                    xs_ref, wgu_hbm, bgu_ref, wd_hbm, bd_ref,
                    ys_ref,
                    xb_ref, act_ref, wgu_stage, wd_stage, wgu_bf, wd_bf, wsem):
    i = pl.program_id(0)
    nreal = nreal_ref[0]
    bm = xb_ref.shape[0]

    def weight_copies(e):
        return (pltpu.make_async_copy(wgu_hbm.at[e], wgu_stage, wsem.at[0]),
                pltpu.make_async_copy(wd_hbm.at[e], wd_stage, wsem.at[1]))

    @pl.when(i >= nreal)
    def _():
        ys_ref[...] = jnp.zeros_like(ys_ref)

    @pl.when(i < nreal)
    def _():
        e = bexp_ref[i]

        @pl.when(i == 0)
        def _():
            for cp in weight_copies(e):
                cp.start(priority=1)

        @pl.when(first_ref[i] == 1)
        def _():
            for cp in weight_copies(e):
                cp.wait()

            def cast_gu(r, carry):
                rows = pl.ds(pl.multiple_of(r * CAST_ROWS, CAST_ROWS), CAST_ROWS)
                wgu_bf[rows, :] = wgu_stage[rows, :].astype(BF16)
                return carry

            def cast_d(r, carry):
                rows = pl.ds(pl.multiple_of(r * CAST_ROWS, CAST_ROWS), CAST_ROWS)
                wd_bf[rows, :] = wd_stage[rows, :].astype(BF16)
                return carry

            lax.fori_loop(0, D_MODEL // CAST_ROWS, cast_gu, 0)
            lax.fori_loop(0, D_FF // CAST_ROWS, cast_d, 0)

            @pl.when(nxt_ref[i] >= 0)
            def _():
                for cp in weight_copies(nxt_ref[i]):
                    cp.start(priority=1)

        def mlp(rows):
            for j in range(ROW_TILES):
                lo, hi = _unpack_words(xs_ref[pl.ds(j, rows, stride=ROW_TILES), :])
                xb_ref[0:rows, j * LANES:(j + 1) * LANES] = lo
                xb_ref[0:rows, D_MODEL // 2 + j * LANES:D_MODEL // 2 + (j + 1) * LANES] = hi
            for c in range(D_FF // FF_CHUNK):
                f0 = c * FF_CHUNK
                xb = xb_ref[0:rows, :]
                gate = _dot(xb, wgu_bf[:, f0:f0 + FF_CHUNK]) + bgu_ref[0, :, f0:f0 + FF_CHUNK]
                up = (_dot(xb, wgu_bf[:, D_FF + f0:D_FF + f0 + FF_CHUNK])
                      + bgu_ref[0, :, D_FF + f0:D_FF + f0 + FF_CHUNK])
                gate = jnp.minimum(gate, SWIGLU_LIMIT)
                up = jnp.clip(up, -SWIGLU_LIMIT, SWIGLU_LIMIT)
                glu = gate * jax.nn.sigmoid(gate * SWIGLU_ALPHA)
                act_ref[0:rows, f0:f0 + FF_CHUNK] = ((up + 1.0) * glu).astype(BF16)
            out = _pack_rows(_dot(act_ref[0:rows, :], wd_bf[...]) + bd_ref[0])
            for j in range(ROW_TILES):
                ys_ref[pl.ds(j, rows, stride=ROW_TILES), :] = out[:, j * LANES:(j + 1) * LANES]

        n_routed = nval_ref[i]
        for rows in range(TAIL_ROWS, bm + 1, TAIL_ROWS):
            @pl.when((n_routed > rows - TAIL_ROWS) & (n_routed <= rows))
            def _(rows=rows):
                mlp(rows)
                if rows < bm:
                    ys_ref[rows * ROW_TILES:bm * ROW_TILES, :] = jnp.zeros(
                        ((bm - rows) * ROW_TILES, LANES), U32)


def _experts_call(bexp, first, nxt, nval, nreal, xs, wgu, bgu, wd, bd):
    bm = MOE_BLOCK
    n_blocks = xs.shape[0] // (bm * ROW_TILES)
    bgu3 = bgu.reshape(N_EXPERTS, 1, 2 * D_FF)
    bd3 = bd.reshape(N_EXPERTS, 1, D_MODEL)
    grid_spec = pltpu.PrefetchScalarGridSpec(
        num_scalar_prefetch=5,
        grid=(n_blocks,),
        in_specs=[
            pl.BlockSpec((bm * ROW_TILES, LANES),
                         lambda i, be, fi, nx, nv, nr: (jnp.minimum(i, nr[0] - 1), 0)),
            pl.BlockSpec(memory_space=pl.ANY),
            pl.BlockSpec((1, 1, 2 * D_FF), lambda i, be, fi, nx, nv, nr: (be[i], 0, 0)),
            pl.BlockSpec(memory_space=pl.ANY),
            pl.BlockSpec((1, 1, D_MODEL), lambda i, be, fi, nx, nv, nr: (be[i], 0, 0)),
        ],
        out_specs=pl.BlockSpec((bm * ROW_TILES, LANES), lambda i, be, fi, nx, nv, nr: (i, 0)),
        scratch_shapes=[pltpu.VMEM((bm, D_MODEL), BF16),
                        pltpu.VMEM((bm, D_FF), BF16),
                        pltpu.VMEM((D_MODEL, 2 * D_FF), F32),
                        pltpu.VMEM((D_FF, D_MODEL), F32),
                        pltpu.VMEM((D_MODEL, 2 * D_FF), BF16),
                        pltpu.VMEM((D_FF, D_MODEL), BF16),
                        pltpu.SemaphoreType.DMA((2,))],
    )
    return pl.pallas_call(
        _experts_kernel,
        grid_spec=grid_spec,
        out_shape=jax.ShapeDtypeStruct(xs.shape, U32),
        compiler_params=pltpu.CompilerParams(
            dimension_semantics=("arbitrary",), vmem_limit_bytes=VMEM_LIMIT),
        name="experts",
    )(bexp, first, nxt, nval, nreal, xs, wgu, bgu3, wd, bd3)


def _combine_kernel(src_ref, dst_ref, len_ref, x1_ref, meta_ref, g_ref, ys_hbm, out_ref,
                    ybuf, ysort_ref, sem):
    i = pl.program_id(0)
    n = pl.num_programs(0)
    tb = x1_ref.shape[0]
    n_rows = TOP_K * tb
    slot = i % 2

    def start_runs(tile, buf_slot):
        def body(e, carry):
            g = tile * N_EXPERTS + e
            ln = pl.multiple_of(len_ref[g], ROW_TILES)

            @pl.when(ln > 0)
            def _():
                src = pl.multiple_of(src_ref[g], ROW_TILES)
                dst = pl.multiple_of(dst_ref[g], ROW_TILES)
                pltpu.make_async_copy(ys_hbm.at[pl.ds(src, ln), :],
                                      ybuf.at[buf_slot, pl.ds(dst, ln), :], sem.at[buf_slot]).start()
            return carry
        lax.fori_loop(0, N_EXPERTS, body, 0)

    @pl.when(i == 0)
    def _():
        start_runs(0, 0)

    @pl.when(i + 1 < n)
    def _():
        start_runs(i + 1, 1 - slot)

    pltpu.make_async_copy(ys_hbm.at[pl.ds(0, n_rows * ROW_TILES), :], ybuf.at[slot],
                          sem.at[slot]).wait()

    for j in range(ROW_TILES):
        lo, hi = _unpack_words(ybuf[slot, pl.ds(j, n_rows, stride=ROW_TILES), :])
        ysort_ref[:, j * LANES:(j + 1) * LANES] = lo
        ysort_ref[:, D_MODEL // 2 + j * LANES:D_MODEL // 2 + (j + 1) * LANES] = hi

    meta = meta_ref[...]
    col = lax.broadcasted_iota(I32, (tb, n_rows), 1).astype(F32)
    weights = jnp.zeros((tb, n_rows), F32)
    for kk in range(TOP_K):
        weights = jnp.where(col == meta[:, 3 * TOP_K + kk:3 * TOP_K + kk + 1],
                            meta[:, TOP_K + kk:TOP_K + kk + 1], weights)
    acc = x1_ref[...] + _dot(weights.astype(BF16), ysort_ref[...])
    out_ref[...] = _rms(acc, g_ref[...])


def _combine_call(seg_src, seg_dst, seg_len, x1, meta, g, ys):
    t, d = x1.shape
    tb = SORT_TILE
    nb = t // tb
    grid_spec = pltpu.PrefetchScalarGridSpec(
        num_scalar_prefetch=3,
        grid=(nb,),
        in_specs=[pl.BlockSpec((tb, d), lambda i, a, b, c: (i, 0)),
                  pl.BlockSpec((tb, LANES), lambda i, a, b, c: (i, 0)),
                  pl.BlockSpec((1, d), lambda i, a, b, c: (0, 0)),
                  pl.BlockSpec(memory_space=pl.ANY)],
        out_specs=pl.BlockSpec((tb, d), lambda i, a, b, c: (i, 0)),
        scratch_shapes=[pltpu.VMEM((2, TOP_K * tb * ROW_TILES, LANES), U32),
                        pltpu.VMEM((TOP_K * tb, D_MODEL), BF16),
                        pltpu.SemaphoreType.DMA((2,))],
    )
    return pl.pallas_call(
        _combine_kernel,
        grid_spec=grid_spec,
        out_shape=jax.ShapeDtypeStruct((t, d), F32),
        compiler_params=pltpu.CompilerParams(
            dimension_semantics=("arbitrary",), vmem_limit_bytes=VMEM_LIMIT),
        name="combine",
    )(seg_src, seg_dst, seg_len, x1, meta, g, ys)


def _routing_tables(counts, t):
    bm = MOE_BLOCK
    n_blocks = t * TOP_K // bm + N_EXPERTS
    eids = jnp.arange(N_EXPERTS, dtype=I32)
    nblk_e = (counts + bm - 1) // bm
    blk_end = jnp.sum(jnp.where(eids[None, :] <= eids[:, None], nblk_e[None, :], 0), axis=1)
    blk_start = blk_end - nblk_e
    nreal = blk_end[N_EXPERTS - 1]
    pad_start = blk_start * bm
    blk = jnp.arange(n_blocks, dtype=I32)
    bexp = jnp.minimum(jnp.sum((blk_end[None, :] <= blk[:, None]).astype(I32), axis=1),
                       N_EXPERTS - 1)
    blk_is_e = bexp[:, None] == eids[None, :]
    pick = lambda tab: jnp.sum(jnp.where(blk_is_e, tab[None, :], 0), axis=1)
    first = (blk == pick(blk_start)).astype(I32)
    nxt_e = jnp.sum((blk_end[None, :] <= blk_end[:, None]).astype(I32), axis=1)
    nxt_e = jnp.where(blk_end < nreal, jnp.minimum(nxt_e, N_EXPERTS - 1), -1)
    nxt = pick(nxt_e)
    nval = jnp.clip(pick(counts) - (blk - pick(blk_start)) * bm, 0, bm)
    tail_blk = jnp.arange(N_EXPERTS, dtype=I32) + nreal
    pad_lo = jnp.concatenate([pad_start + counts, jnp.minimum(tail_blk, n_blocks - 1) * bm])
    pad_n = jnp.concatenate([nblk_e * bm - counts, jnp.where(tail_blk < n_blocks, bm, 0)])
    return (pad_start, bexp, first, nxt, nval, nreal.reshape(1).astype(I32), n_blocks,
            pad_lo * ROW_TILES, pad_n * ROW_TILES)


def kernel(x, norm_mix_g, w_in, conv_w, w_gk_up, b_gk_up, gla_norm_g, w_out, norm_ffn_g,
           w_router, b_router, w_gate_up, b_gate_up, w_down, b_down, norm_final_g):
    bsz, seq, d = x.shape
    t = bsz * seq
    assert w_in.shape[0] == 1, "single-layer trunk only"
    l = 0
    d_in = w_in.shape[-1]
    win = jnp.pad(w_in[l].astype(BF16), ((0, 0), (0, D_IN_PAD - d_in)))
    wgk = jnp.pad(w_gk_up[l], ((0, LANES - GLA_RANK), (0, 0))).astype(BF16)
    wr = jnp.pad(w_router[l], ((0, 0), (0, LANES - N_EXPERTS)))
    wrh = wr.astype(BF16)
    wrhl = jnp.concatenate([wrh, (wr - wrh.astype(F32)).astype(BF16)], axis=1)
    br = jnp.pad(b_router[l], (0, LANES - N_EXPERTS), constant_values=NEG_BIG).reshape(1, LANES)

    x1, h2, meta, route, seg, cnt = _mixer_call(
        x, norm_mix_g[l].reshape(1, d), win, conv_w[l], wgk, b_gk_up[l].reshape(1, GLA_QK),
        gla_norm_g[l].reshape(1, GLA_DV), w_out[l].astype(BF16), norm_ffn_g[l].reshape(1, d),
        wrhl, br)

    meta2 = meta.reshape(t, LANES)
    counts = cnt[:, 0].astype(I32)
    (pad_start, bexp, first, nxt, nval, nreal, n_blocks, pad_lo,
     pad_n) = _routing_tables(counts, t)
    seg_src = ((pad_start[None, :] + seg[:, :, 0].astype(I32)) * ROW_TILES).reshape(-1)
    seg_len = (seg[:, :, 1].astype(I32) * ROW_TILES).reshape(-1)
    seg_dst = (seg[:, :, 2].astype(I32) * ROW_TILES).reshape(-1)
    xs = _dispatch_call(seg_src, seg_dst, seg_len, pad_lo, pad_n, route, h2.reshape(t, d),
                        n_blocks * MOE_BLOCK)
    ys = _experts_call(bexp, first, nxt, nval, nreal, xs, w_gate_up[l], b_gate_up[l], w_down[l],
                       b_down[l])
    out = _combine_call(seg_src, seg_dst, seg_len, x1.reshape(t, d), meta2,
                        norm_final_g.reshape(1, d), ys)
    return out.reshape(bsz, seq, d)
```

```python
import jax
import jax.numpy as jnp
from jax import lax
from jax.experimental import pallas as pl
from jax.experimental.pallas import tpu as pltpu

F32 = jnp.float32
BF16 = jnp.bfloat16
I32 = jnp.int32
U32 = jnp.uint32

D_MODEL = 1024
CONV_WIDTH = 512
CONV_K = 3
GLA_WIDTH = 512
GLA_HEADS = 4
GLA_DV = 128
GLA_DK = 64
GLA_QK = GLA_HEADS * GLA_DK
GLA_RANK = 16
GLA_NORMALIZER = 16.0
GLA_CHUNK = 64
N_EXPERTS = 32
TOP_K = 4
D_FF = 1024
SWIGLU_LIMIT = 7.0
SWIGLU_ALPHA = 1.702
RMS_EPS = 1e-5

LANES = 128
SUBLANES = 8
ROW_TILES = D_MODEL // LANES // 2
ROUTE_ROWS = 4 * TOP_K

OFF_UH = 0
OFF_GB = OFF_UH + CONV_WIDTH
OFF_GC = OFF_GB + CONV_WIDTH
OFF_Q = OFF_GC + CONV_WIDTH
OFF_K = OFF_Q + GLA_QK
OFF_V = OFF_K + GLA_QK
OFF_GO = OFF_V + GLA_WIDTH
OFF_GKL = OFF_GO + GLA_WIDTH
D_IN_PAD = OFF_GKL + LANES

SEQ_TILE = 512
SORT_TILE = 256
MOE_BLOCK = 512
FF_CHUNK = 256
TAIL_ROWS = 128
CAST_ROWS = 128
NEG_BIG = -1e30
VMEM_LIMIT = 56 * 1024 * 1024


def _rms(x, g):
    return x * lax.rsqrt(jnp.mean(x * x, axis=-1, keepdims=True) + RMS_EPS) * g


def _dot(a, b):
    return jnp.dot(a, b, preferred_element_type=F32)


def _dot_nt(a, b):
    return lax.dot_general(a, b, (((1,), (1,)), ((), ())), preferred_element_type=F32)


def _pack_rows(x):
    half = x.shape[1] // 2
    xr = x.astype(BF16).astype(F32)
    lo = lax.bitcast_convert_type(xr[:, :half], U32) >> 16
    hi = lax.bitcast_convert_type(xr[:, half:], U32) & jnp.uint32(0xFFFF0000)
    return hi | lo


def _unpack_words(w):
    lo = lax.bitcast_convert_type(w << 16, F32).astype(BF16)
    hi = lax.bitcast_convert_type(w & jnp.uint32(0xFFFF0000), F32).astype(BF16)
    return lo, hi


def _split_bf16(x):
    hi = x.astype(BF16)
    lo = (x - hi.astype(F32)).astype(BF16)
    return hi, lo


def _mixer_kernel(x_ref, g1_ref, win_ref, convw_ref, wgk_ref, bgk_ref, gng_ref, wout_ref,
                  g2_ref, wrhl_ref, br_ref,
                  x1_ref, h2_ref, meta_ref, route_ref, seg_ref, cnt_ref,
                  proj_ref, ubuf_ref, la_ref, ycat_ref, state_ref, carry_ref):
    ts = x_ref.shape[1]
    b_idx = pl.program_id(0)
    s_idx = pl.program_id(1)

    @pl.when(s_idx == 0)
    def _():
        state_ref[...] = jnp.zeros_like(state_ref)
        ubuf_ref[0:SUBLANES, :] = jnp.zeros((SUBLANES, CONV_WIDTH), F32)

    @pl.when((s_idx == 0) & (b_idx == 0))
    def _():
        carry_ref[...] = jnp.zeros_like(carry_ref)

    x = x_ref[0]
    h = _rms(x, g1_ref[...]).astype(BF16)
    proj_ref[...] = _dot(h, win_ref[...])

    u = proj_ref[:, OFF_GC:OFF_GC + CONV_WIDTH] * proj_ref[:, OFF_UH:OFF_UH + CONV_WIDTH]
    ubuf_ref[SUBLANES:SUBLANES + ts, :] = u
    u1 = ubuf_ref[pl.ds(SUBLANES - 1, ts), :]
    u2 = ubuf_ref[pl.ds(SUBLANES - 2, ts), :]
    conv = convw_ref[0:1, :] * u2 + convw_ref[1:2, :] * u1 + convw_ref[2:3, :] * u
    ycat_ref[:, 0:CONV_WIDTH] = (proj_ref[:, OFF_GB:OFF_GB + CONV_WIDTH] * conv).astype(BF16)
    ubuf_ref[0:SUBLANES, :] = ubuf_ref[ts:ts + SUBLANES, :]

    gk = _dot(proj_ref[:, OFF_GKL:OFF_GKL + LANES].astype(BF16), wgk_ref[...]) + bgk_ref[...]
    log_sig = jnp.minimum(gk, 0.0) - jnp.log1p(jnp.exp(-jnp.abs(gk)))
    la_ref[...] = log_sig / GLA_NORMALIZER

    ci = lax.broadcasted_iota(I32, (GLA_CHUNK, GLA_CHUNK), 0)
    cj = lax.broadcasted_iota(I32, (GLA_CHUNK, GLA_CHUNK), 1)
    tri_incl = (cj <= ci).astype(BF16)
    causal = cj <= ci
    causal4 = jnp.concatenate([causal] * GLA_HEADS, axis=0)
    lane_qk = lax.broadcasted_iota(I32, (1, GLA_QK), 1)
    head_masks = [((lane_qk >= hd * GLA_DK) & (lane_qk < (hd + 1) * GLA_DK)).astype(F32)
                  for hd in range(GLA_HEADS)]
    gng = gng_ref[...]

    n_chunks = ts // GLA_CHUNK
    chunk_rows = [pl.ds(c * GLA_CHUNK, GLA_CHUNK) for c in range(n_chunks)]
    lane_c = lax.broadcasted_iota(I32, (GLA_QK, 2 * GLA_CHUNK), 1)
    qd_all, kd_all, kr_all, bl_all, v_all = [], [], [], [], []
    for rows in chunk_rows:
        la_hi, la_lo = _split_bf16(la_ref[rows, :])
        bcum = _dot(tri_incl, la_hi) + _dot(tri_incl, la_lo)
        blast = bcum[GLA_CHUNK - 1:GLA_CHUNK, :]
        q = proj_ref[rows, OFF_Q:OFF_Q + GLA_QK] * (GLA_DK ** -0.5)
        k = proj_ref[rows, OFF_K:OFF_K + GLA_QK]
        qd_all.append(q * jnp.exp(bcum))
        kd_all.append((k * jnp.exp(-bcum)).astype(BF16))
        kr_all.append(k * jnp.exp(blast - bcum))
        bl_all.append(blast)
        v_all.append(proj_ref[rows, OFF_V:OFF_V + GLA_WIDTH].astype(BF16))

    scores_all = []
    for c in range(n_chunks):
        q_stack = jnp.concatenate([qd_all[c] * m for m in head_masks], axis=0).astype(BF16)
        scores_all.append(
            jnp.where(causal4, _dot_nt(q_stack, kd_all[c]), 0.0).astype(BF16))
    o_intra_all = []
    for c in range(n_chunks):
        o_intra_all.append(jnp.concatenate(
            [_dot(scores_all[c][hd * GLA_CHUNK:(hd + 1) * GLA_CHUNK, :],
                  v_all[c][:, hd * GLA_DV:(hd + 1) * GLA_DV]) for hd in range(GLA_HEADS)], axis=1))
    kv_all, dcol_all = [], []
    for c in range(n_chunks):
        kt = jnp.concatenate(
            [kr_all[c], jnp.broadcast_to(bl_all[c], (GLA_CHUNK, GLA_QK))], axis=0).T
        dcol_all.append(jnp.exp(kt[:, GLA_CHUNK:GLA_CHUNK + 1]))
        kt_b = jnp.where(lane_c < GLA_CHUNK, kt, 0.0).astype(BF16)
        v_pad = jnp.concatenate([v_all[c], jnp.zeros_like(v_all[c])], axis=0)
        kv_all.append([_dot(kt_b[hd * GLA_DK:(hd + 1) * GLA_DK, :],
                            v_pad[:, hd * GLA_DV:(hd + 1) * GLA_DV]) for hd in range(GLA_HEADS)])

    o_all = []
    for c in range(n_chunks):
        state = state_ref[...]
        o_all.append(_dot(qd_all[c].astype(BF16), state.astype(BF16)) + o_intra_all[c])
        for hd in range(GLA_HEADS):
            rs = slice(hd * GLA_DK, (hd + 1) * GLA_DK)
            cs = slice(hd * GLA_DV, (hd + 1) * GLA_DV)
            state_ref[rs, cs] = dcol_all[c][rs, :] * state[rs, cs] + kv_all[c][hd]

    for c, rows in enumerate(chunk_rows):
        o = o_all[c]
        g_out = proj_ref[rows, OFF_GO:OFF_GO + GLA_WIDTH]
        o_n = jnp.concatenate(
            [_rms(o[:, hd * GLA_DV:(hd + 1) * GLA_DV], gng) for hd in range(GLA_HEADS)], axis=1)
        y = o_n * (g_out * jax.nn.sigmoid(g_out))
        ycat_ref[rows, CONV_WIDTH:CONV_WIDTH + GLA_WIDTH] = y.astype(BF16)

    x1 = x + _dot(ycat_ref[...], wout_ref[...])
    x1_ref[0] = x1
    h2 = _rms(x1, g2_ref[...])
    h2_hi, h2_lo = _split_bf16(h2)
    h2_ref[0] = h2_hi
    hi_terms = _dot(h2_hi, wrhl_ref[...])
    logits = (hi_terms[:, 0:LANES] + hi_terms[:, LANES:2 * LANES]
              + _dot(h2_lo, wrhl_ref[:, 0:LANES]) + br_ref[...])

    lt = logits.T[0:N_EXPERTS, :]
    erow = lax.broadcasted_iota(I32, (N_EXPERTS, ts), 0).astype(F32)
    work = lt
    sel = jnp.zeros((N_EXPERTS, ts), F32)
    top_v, top_i, top_oh = [], [], []
    for _ in range(TOP_K):
        m = jnp.max(work, axis=0, keepdims=True)
        idx = jnp.min(jnp.where(work == m, erow, float(N_EXPERTS)), axis=0, keepdims=True)
        oh = erow == idx
        top_v.append(m)
        top_i.append(idx)
        top_oh.append(oh)
        sel = sel + oh.astype(F32)
        work = jnp.where(oh, -jnp.inf, work)
    exps = [jnp.exp(tv - top_v[0]) for tv in top_v]
    denom = exps[0] + exps[1] + exps[2] + exps[3]
    gates = [e / denom for e in exps]

    tsrc = lax.broadcasted_iota(I32, (ts, ts), 0)
    tdst = lax.broadcasted_iota(I32, (ts, ts), 1)
    earlier = (tsrc < tdst).astype(BF16)
    local = _dot(sel.astype(BF16), earlier)
    carry = carry_ref[:, 0:1]
    ranks = [jnp.sum(jnp.where(oh, local + carry, 0.0), axis=0, keepdims=True) for oh in top_oh]

    ei = lax.broadcasted_iota(I32, (N_EXPERTS, N_EXPERTS), 0)
    ej = lax.broadcasted_iota(I32, (N_EXPERTS, N_EXPERTS), 1)
    lower_experts = (ej < ei).astype(BF16)
    lane_f = lax.broadcasted_iota(I32, (N_EXPERTS, LANES), 1)
    slot_base = []
    before = jnp.zeros((N_EXPERTS, 1), F32)
    for u in range(ts // SORT_TILE):
        sub_cnt = jnp.sum(sel[:, u * SORT_TILE:(u + 1) * SORT_TILE], axis=1, keepdims=True)
        seg_start = _dot(lower_experts,
                         jnp.broadcast_to(sub_cnt, (N_EXPERTS, LANES)).astype(BF16))[:, 0:1]
        seg_ref[u] = jnp.where(lane_f == 0, carry + before,
                               jnp.where(lane_f == 1, sub_cnt,
                                         jnp.where(lane_f == 2, seg_start, 0.0)))
        slot_base.append(jnp.broadcast_to(seg_start - before, (N_EXPERTS, SORT_TILE)))
        before = before + sub_cnt
    slot_base = jnp.concatenate(slot_base, axis=1)
    slots = [jnp.sum(jnp.where(oh, local + slot_base, 0.0), axis=0, keepdims=True)
             for oh in top_oh]
    new_carry = carry + before
    carry_ref[...] = jnp.broadcast_to(new_carry, carry_ref.shape)
    cnt_ref[...] = jnp.broadcast_to(new_carry, cnt_ref.shape)

    route = jnp.concatenate(top_i + gates + ranks + slots, axis=0)
    route_ref[...] = route
    meta_ref[0] = jnp.concatenate(
        [route, jnp.zeros((LANES - ROUTE_ROWS, ts), F32)], axis=0).T


def _mixer_call(x, g1, win, convw, wgk, bgk, gng, wout, g2, wrhl, br):
    bsz, seq, d = x.shape
    ts = SEQ_TILE
    grid = (bsz, seq // ts)

    def const(shape):
        return pl.BlockSpec(shape, lambda b, s: (0,) * len(shape))

    tile = lambda w: pl.BlockSpec((1, ts, w), lambda b, s: (b, s, 0))
    return pl.pallas_call(
        _mixer_kernel,
        grid=grid,
        in_specs=[tile(d), const(g1.shape), const(win.shape), const(convw.shape),
                  const(wgk.shape), const(bgk.shape), const(gng.shape), const(wout.shape),
                  const(g2.shape), const(wrhl.shape), const(br.shape)],
        out_specs=[tile(d),
                   tile(d),
                   tile(LANES),
                   pl.BlockSpec((ROUTE_ROWS, ts), lambda b, s: (0, b * (seq // ts) + s)),
                   pl.BlockSpec((ts // SORT_TILE, N_EXPERTS, LANES),
                                lambda b, s: (b * (seq // ts) + s, 0, 0)),
                   const((N_EXPERTS, LANES))],
        out_shape=[jax.ShapeDtypeStruct((bsz, seq, d), F32),
                   jax.ShapeDtypeStruct((bsz, seq, d), BF16),
                   jax.ShapeDtypeStruct((bsz, seq, LANES), F32),
                   jax.ShapeDtypeStruct((ROUTE_ROWS, bsz * seq), F32),
                   jax.ShapeDtypeStruct((bsz * seq // SORT_TILE, N_EXPERTS, LANES), F32),
                   jax.ShapeDtypeStruct((N_EXPERTS, LANES), F32)],
        scratch_shapes=[pltpu.VMEM((ts, D_IN_PAD), F32),
                        pltpu.VMEM((ts + SUBLANES, CONV_WIDTH), F32),
                        pltpu.VMEM((ts, GLA_QK), F32),
                        pltpu.VMEM((ts, D_MODEL), BF16),
                        pltpu.VMEM((GLA_QK, GLA_WIDTH), F32),
                        pltpu.VMEM((N_EXPERTS, LANES), F32)],
        compiler_params=pltpu.CompilerParams(
            dimension_semantics=("arbitrary", "arbitrary"), vmem_limit_bytes=VMEM_LIMIT),
        name="mixer",
    )(x, g1, win, convw, wgk, bgk, gng, wout, g2, wrhl, br)


def _dispatch_kernel(run_xs_ref, run_buf_ref, run_len_ref, pad_lo_ref, pad_n_ref,
                     route_ref, h2_ref, xs_hbm, sbuf, zbuf, sem, zsem):
    i = pl.program_id(0)
    n = pl.num_programs(0)
    td = h2_ref.shape[0]
    n_rows = TOP_K * td
    slot = i % 2

    def zero_fill(wait):
        def fill(g, carry):
            ln = pl.multiple_of(pad_n_ref[g], ROW_TILES)

            @pl.when(ln > 0)
            def _():
                lo = pl.multiple_of(pad_lo_ref[g], ROW_TILES)
                cp = pltpu.make_async_copy(zbuf.at[pl.ds(0, ln), :], xs_hbm.at[pl.ds(lo, ln), :],
                                           zsem.at[0])
                if wait:
                    cp.wait()
                else:
                    cp.start()
            return carry

        lax.fori_loop(0, pad_lo_ref.shape[0], fill, 0)

    def wait_runs(buf_slot):
        pltpu.make_async_copy(sbuf.at[buf_slot], xs_hbm.at[pl.ds(0, n_rows * ROW_TILES), :],
                              sem.at[buf_slot]).wait()

    @pl.when(i == 0)
    def _():
        zbuf[...] = jnp.zeros_like(zbuf)
        zero_fill(wait=False)

    @pl.when(i >= 2)
    def _():
        wait_runs(slot)

    route = route_ref[...]
    row = lax.broadcasted_iota(I32, (n_rows, td), 0).astype(F32)
    pick = jnp.zeros((n_rows, td), F32)
    for kk in range(TOP_K):
        pick = jnp.where(row == route[3 * TOP_K + kk:3 * TOP_K + kk + 1, :], 1.0, pick)
    rows_sorted = _dot(pick.astype(BF16), h2_ref[...])
    words = _pack_rows(rows_sorted)
    for j in range(ROW_TILES):
        sbuf[slot, pl.ds(j, n_rows, stride=ROW_TILES), :] = words[:, j * LANES:(j + 1) * LANES]

    def body(e, carry):
        g = i * N_EXPERTS + e
        ln = pl.multiple_of(run_len_ref[g], ROW_TILES)

        @pl.when(ln > 0)
        def _():
            src = pl.multiple_of(run_buf_ref[g], ROW_TILES)
            dst = pl.multiple_of(run_xs_ref[g], ROW_TILES)
            pltpu.make_async_copy(sbuf.at[slot, pl.ds(src, ln), :], xs_hbm.at[pl.ds(dst, ln), :],
                                  sem.at[slot]).start()
        return carry

    lax.fori_loop(0, N_EXPERTS, body, 0)

    @pl.when(i == n - 1)
    def _():
        @pl.when(i >= 1)
        def _():
            wait_runs(1 - slot)
        wait_runs(slot)
        zero_fill(wait=True)


def _dispatch_call(run_xs, run_buf, run_len, pad_lo, pad_n, route, h2, n_rows):
    t, d = h2.shape
    td = SORT_TILE
    grid_spec = pltpu.PrefetchScalarGridSpec(
        num_scalar_prefetch=5,
        grid=(t // td,),
        in_specs=[pl.BlockSpec((ROUTE_ROWS, td), lambda i, *_: (0, i)),
                  pl.BlockSpec((td, d), lambda i, *_: (i, 0))],
        out_specs=pl.BlockSpec(memory_space=pl.ANY),
        scratch_shapes=[pltpu.VMEM((2, TOP_K * td * ROW_TILES, LANES), U32),
                        pltpu.VMEM((MOE_BLOCK * ROW_TILES, LANES), U32),
                        pltpu.SemaphoreType.DMA((2,)),
                        pltpu.SemaphoreType.DMA((1,))],
    )
    return pl.pallas_call(
        _dispatch_kernel,
        grid_spec=grid_spec,
        out_shape=jax.ShapeDtypeStruct((n_rows * ROW_TILES, LANES), U32),
        compiler_params=pltpu.CompilerParams(
            dimension_semantics=("arbitrary",), vmem_limit_bytes=VMEM_LIMIT,
            has_side_effects=True),
        name="dispatch",
    )(run_xs, run_buf, run_len, pad_lo, pad_n, route, h2)


def _experts_kernel(bexp_ref, first_ref, nxt_ref, nval_ref, nreal_ref,
                    xs_ref, wgu_hbm, bgu_ref, wd_hbm, bd_ref,
                    ys_ref,
                    xb_ref, act_ref, wgu_stage, wd_stage, wgu_bf, wd_bf, wsem):
    i = pl.program_id(0)
    nreal = nreal_ref[0]
    bm = xb_ref.shape[0]

    def weight_copies(e):
        return (pltpu.make_async_copy(wgu_hbm.at[e], wgu_stage, wsem.at[0]),
                pltpu.make_async_copy(wd_hbm.at[e], wd_stage, wsem.at[1]))

    @pl.when(i >= nreal)
    def _():
        ys_ref[...] = jnp.zeros_like(ys_ref)

    @pl.when(i < nreal)
    def _():
        e = bexp_ref[i]

        @pl.when(i == 0)
        def _():
            for cp in weight_copies(e):
                cp.start(priority=1)

        @pl.when(first_ref[i] == 1)
        def _():
            for cp in weight_copies(e):
                cp.wait()

            def cast_gu(r, carry):
                rows = pl.ds(pl.multiple_of(r * CAST_ROWS, CAST_ROWS), CAST_ROWS)
                wgu_bf[rows, :] = wgu_stage[rows, :].astype(BF16)
                return carry

            def cast_d(r, carry):
                rows = pl.ds(pl.multiple_of(r * CAST_ROWS, CAST_ROWS), CAST_ROWS)
                wd_bf[rows, :] = wd_stage[rows, :].astype(BF16)
                return carry

            lax.fori_loop(0, D_MODEL // CAST_ROWS, cast_gu, 0)
            lax.fori_loop(0, D_FF // CAST_ROWS, cast_d, 0)

            @pl.when(nxt_ref[i] >= 0)
            def _():
                for cp in weight_copies(nxt_ref[i]):
                    cp.start(priority=1)

        def mlp(rows):
            for j in range(ROW_TILES):
                lo, hi = _unpack_words(xs_ref[pl.ds(j, rows, stride=ROW_TILES), :])
                xb_ref[0:rows, j * LANES:(j + 1) * LANES] = lo
                xb_ref[0:rows, D_MODEL // 2 + j * LANES:D_MODEL // 2 + (j + 1) * LANES] = hi
            for c in range(D_FF // FF_CHUNK):
                f0 = c * FF_CHUNK
                xb = xb_ref[0:rows, :]
                gate = _dot(xb, wgu_bf[:, f0:f0 + FF_CHUNK]) + bgu_ref[0, :, f0:f0 + FF_CHUNK]
                up = (_dot(xb, wgu_bf[:, D_FF + f0:D_FF + f0 + FF_CHUNK])
                      + bgu_ref[0, :, D_FF + f0:D_FF + f0 + FF_CHUNK])
                gate = jnp.minimum(gate, SWIGLU_LIMIT)
                up = jnp.clip(up, -SWIGLU_LIMIT, SWIGLU_LIMIT)
                glu = gate * jax.nn.sigmoid(gate * SWIGLU_ALPHA)
                act_ref[0:rows, f0:f0 + FF_CHUNK] = ((up + 1.0) * glu).astype(BF16)
            out = _pack_rows(_dot(act_ref[0:rows, :], wd_bf[...]) + bd_ref[0])
            for j in range(ROW_TILES):
                ys_ref[pl.ds(j, rows, stride=ROW_TILES), :] = out[:, j * LANES:(j + 1) * LANES]

        n_routed = nval_ref[i]
        for rows in range(TAIL_ROWS, bm + 1, TAIL_ROWS):
            @pl.when((n_routed > rows - TAIL_ROWS) & (n_routed <= rows))
            def _(rows=rows):
                mlp(rows)
                if rows < bm:
                    ys_ref[rows * ROW_TILES:bm * ROW_TILES, :] = jnp.zeros(
                        ((bm - rows) * ROW_TILES, LANES), U32)


def _experts_call(bexp, first, nxt, nval, nreal, xs, wgu, bgu, wd, bd):
    bm = MOE_BLOCK
    n_blocks = xs.shape[0] // (bm * ROW_TILES)
    bgu3 = bgu.reshape(N_EXPERTS, 1, 2 * D_FF)
    bd3 = bd.reshape(N_EXPERTS, 1, D_MODEL)
    grid_spec = pltpu.PrefetchScalarGridSpec(
        num_scalar_prefetch=5,
        grid=(n_blocks,),
        in_specs=[
            pl.BlockSpec((bm * ROW_TILES, LANES),
                         lambda i, be, fi, nx, nv, nr: (jnp.minimum(i, nr[0] - 1), 0)),
            pl.BlockSpec(memory_space=pl.ANY),
            pl.BlockSpec((1, 1, 2 * D_FF), lambda i, be, fi, nx, nv, nr: (be[i], 0, 0)),
            pl.BlockSpec(memory_space=pl.ANY),
            pl.BlockSpec((1, 1, D_MODEL), lambda i, be, fi, nx, nv, nr: (be[i], 0, 0)),
        ],
        out_specs=pl.BlockSpec((bm * ROW_TILES, LANES), lambda i, be, fi, nx, nv, nr: (i, 0)),
        scratch_shapes=[pltpu.VMEM((bm, D_MODEL), BF16),
                        pltpu.VMEM((bm, D_FF), BF16),
                        pltpu.VMEM((D_MODEL, 2 * D_FF), F32),
                        pltpu.VMEM((D_FF, D_MODEL), F32),
                        pltpu.VMEM((D_MODEL, 2 * D_FF), BF16),
                        pltpu.VMEM((D_FF, D_MODEL), BF16),
                        pltpu.SemaphoreType.DMA((2,))],
    )
    return pl.pallas_call(
        _experts_kernel,
        grid_spec=grid_spec,
        out_shape=jax.ShapeDtypeStruct(xs.shape, U32),
        compiler_params=pltpu.CompilerParams(
            dimension_semantics=("arbitrary",), vmem_limit_bytes=VMEM_LIMIT),
        name="experts",
    )(bexp, first, nxt, nval, nreal, xs, wgu, bgu3, wd, bd3)


def _combine_kernel(src_ref, dst_ref, len_ref, x1_ref, meta_ref, g_ref, ys_hbm, out_ref,
                    ybuf, ysort_ref, sem):
    i = pl.program_id(0)
    n = pl.num_programs(0)
    tb = x1_ref.shape[0]
    n_rows = TOP_K * tb
    slot = i % 2

    def start_runs(tile, buf_slot):
        def body(e, carry):
            g = tile * N_EXPERTS + e
            ln = pl.multiple_of(len_ref[g], ROW_TILES)

            @pl.when(ln > 0)
            def _():
                src = pl.multiple_of(src_ref[g], ROW_TILES)
                dst = pl.multiple_of(dst_ref[g], ROW_TILES)
                pltpu.make_async_copy(ys_hbm.at[pl.ds(src, ln), :],
                                      ybuf.at[buf_slot, pl.ds(dst, ln), :], sem.at[buf_slot]).start()
            return carry
        lax.fori_loop(0, N_EXPERTS, body, 0)

    @pl.when(i == 0)
    def _():
        start_runs(0, 0)

    @pl.when(i + 1 < n)
    def _():
        start_runs(i + 1, 1 - slot)

    pltpu.make_async_copy(ys_hbm.at[pl.ds(0, n_rows * ROW_TILES), :], ybuf.at[slot],
                          sem.at[slot]).wait()

    for j in range(ROW_TILES):
        lo, hi = _unpack_words(ybuf[slot, pl.ds(j, n_rows, stride=ROW_TILES), :])
        ysort_ref[:, j * LANES:(j + 1) * LANES] = lo
        ysort_ref[:, D_MODEL // 2 + j * LANES:D_MODEL // 2 + (j + 1) * LANES] = hi

    meta = meta_ref[...]
    col = lax.broadcasted_iota(I32, (tb, n_rows), 1).astype(F32)
    weights = jnp.zeros((tb, n_rows), F32)
    for kk in range(TOP_K):
        weights = jnp.where(col == meta[:, 3 * TOP_K + kk:3 * TOP_K + kk + 1],
                            meta[:, TOP_K + kk:TOP_K + kk + 1], weights)
    acc = x1_ref[...] + _dot(weights.astype(BF16), ysort_ref[...])
    out_ref[...] = _rms(acc, g_ref[...])


def _combine_call(seg_src, seg_dst, seg_len, x1, meta, g, ys):
    t, d = x1.shape
    tb = SORT_TILE
    nb = t // tb
    grid_spec = pltpu.PrefetchScalarGridSpec(
        num_scalar_prefetch=3,
        grid=(nb,),
        in_specs=[pl.BlockSpec((tb, d), lambda i, a, b, c: (i, 0)),
                  pl.BlockSpec((tb, LANES), lambda i, a, b, c: (i, 0)),
                  pl.BlockSpec((1, d), lambda i, a, b, c: (0, 0)),
                  pl.BlockSpec(memory_space=pl.ANY)],
        out_specs=pl.BlockSpec((tb, d), lambda i, a, b, c: (i, 0)),
        scratch_shapes=[pltpu.VMEM((2, TOP_K * tb * ROW_TILES, LANES), U32),
                        pltpu.VMEM((TOP_K * tb, D_MODEL), BF16),
                        pltpu.SemaphoreType.DMA((2,))],
    )
    return pl.pallas_call(
        _combine_kernel,
        grid_spec=grid_spec,
        out_shape=jax.ShapeDtypeStruct((t, d), F32),
        compiler_params=pltpu.CompilerParams(
            dimension_semantics=("arbitrary",), vmem_limit_bytes=VMEM_LIMIT),
        name="combine",
    )(seg_src, seg_dst, seg_len, x1, meta, g, ys)


def _routing_tables(counts, t):
    bm = MOE_BLOCK
    n_blocks = t * TOP_K // bm + N_EXPERTS
    eids = jnp.arange(N_EXPERTS, dtype=I32)
    nblk_e = (counts + bm - 1) // bm
    blk_end = jnp.sum(jnp.where(eids[None, :] <= eids[:, None], nblk_e[None, :], 0), axis=1)
    blk_start = blk_end - nblk_e
    nreal = blk_end[N_EXPERTS - 1]
    pad_start = blk_start * bm
    blk = jnp.arange(n_blocks, dtype=I32)
    bexp = jnp.minimum(jnp.sum((blk_end[None, :] <= blk[:, None]).astype(I32), axis=1),
                       N_EXPERTS - 1)
    blk_is_e = bexp[:, None] == eids[None, :]
    pick = lambda tab: jnp.sum(jnp.where(blk_is_e, tab[None, :], 0), axis=1)
    first = (blk == pick(blk_start)).astype(I32)
    nxt_e = jnp.sum((blk_end[None, :] <= blk_end[:, None]).astype(I32), axis=1)
    nxt_e = jnp.where(blk_end < nreal, jnp.minimum(nxt_e, N_EXPERTS - 1), -1)
    nxt = pick(nxt_e)
    nval = jnp.clip(pick(counts) - (blk - pick(blk_start)) * bm, 0, bm)
    tail_blk = jnp.arange(N_EXPERTS, dtype=I32) + nreal
    pad_lo = jnp.concatenate([pad_start + counts, jnp.minimum(tail_blk, n_blocks - 1) * bm])
    pad_n = jnp.concatenate([nblk_e * bm - counts, jnp.where(tail_blk < n_blocks, bm, 0)])
    return (pad_start, bexp, first, nxt, nval, nreal.reshape(1).astype(I32), n_blocks,
            pad_lo * ROW_TILES, pad_n * ROW_TILES)


def kernel(x, norm_mix_g, w_in, conv_w, w_gk_up, b_gk_up, gla_norm_g, w_out, norm_ffn_g,
           w_router, b_router, w_gate_up, b_gate_up, w_down, b_down, norm_final_g):
    bsz, seq, d = x.shape
    t = bsz * seq
    assert w_in.shape[0] == 1, "single-layer trunk only"
    assert conv_w.shape[1] == CONV_K and ROUTE_ROWS % SUBLANES == 0
    l = 0
    d_in = w_in.shape[-1]
    win = jnp.pad(w_in[l].astype(BF16), ((0, 0), (0, D_IN_PAD - d_in)))
    wgk = jnp.pad(w_gk_up[l], ((0, LANES - GLA_RANK), (0, 0))).astype(BF16)
    wr = jnp.pad(w_router[l], ((0, 0), (0, LANES - N_EXPERTS)))
    wrh = wr.astype(BF16)
    wrhl = jnp.concatenate([wrh, (wr - wrh.astype(F32)).astype(BF16)], axis=1)
    br = jnp.pad(b_router[l], (0, LANES - N_EXPERTS), constant_values=NEG_BIG).reshape(1, LANES)

    x1, h2, meta, route, seg, cnt = _mixer_call(
        x, norm_mix_g[l].reshape(1, d), win, conv_w[l], wgk, b_gk_up[l].reshape(1, GLA_QK),
        gla_norm_g[l].reshape(1, GLA_DV), w_out[l].astype(BF16), norm_ffn_g[l].reshape(1, d),
        wrhl, br)

    meta2 = meta.reshape(t, LANES)
    counts = cnt[:, 0].astype(I32)
    (pad_start, bexp, first, nxt, nval, nreal, n_blocks, pad_lo,
     pad_n) = _routing_tables(counts, t)
    seg_src = ((pad_start[None, :] + seg[:, :, 0].astype(I32)) * ROW_TILES).reshape(-1)
    seg_len = (seg[:, :, 1].astype(I32) * ROW_TILES).reshape(-1)
    seg_dst = (seg[:, :, 2].astype(I32) * ROW_TILES).reshape(-1)
    xs = _dispatch_call(seg_src, seg_dst, seg_len, pad_lo, pad_n, route, h2.reshape(t, d),
                        n_blocks * MOE_BLOCK)
    ys = _experts_call(bexp, first, nxt, nval, nreal, xs, w_gate_up[l], b_gate_up[l], w_down[l],
                       b_down[l])
    out = _combine_call(seg_src, seg_dst, seg_len, x1.reshape(t, d), meta2,
                        norm_final_g.reshape(1, d), ys)
    return out.reshape(bsz, seq, d)
```

```python
import jax
import jax.numpy as jnp
from jax import lax
from jax.experimental import pallas as pl
from jax.experimental.pallas import tpu as pltpu

F32 = jnp.float32
BF16 = jnp.bfloat16
I32 = jnp.int32
U32 = jnp.uint32

D_MODEL = 1024
CONV_WIDTH = 512
CONV_K = 3
GLA_WIDTH = 512
GLA_HEADS = 4
GLA_DV = 128
GLA_DK = 64
GLA_QK = GLA_HEADS * GLA_DK
GLA_RANK = 16
GLA_NORMALIZER = 16.0
GLA_CHUNK = 64
N_EXPERTS = 32
TOP_K = 4
D_FF = 1024
SWIGLU_LIMIT = 7.0
SWIGLU_ALPHA = 1.702
RMS_EPS = 1e-5

LANES = 128
SUBLANES = 8
ROW_TILES = D_MODEL // LANES // 2
ROUTE_ROWS = 4 * TOP_K

OFF_UH = 0
OFF_GB = OFF_UH + CONV_WIDTH
OFF_GC = OFF_GB + CONV_WIDTH
OFF_Q = OFF_GC + CONV_WIDTH
OFF_K = OFF_Q + GLA_QK
OFF_V = OFF_K + GLA_QK
OFF_GO = OFF_V + GLA_WIDTH
OFF_GKL = OFF_GO + GLA_WIDTH
D_IN_PAD = OFF_GKL + LANES

SEQ_TILE = 512
SORT_TILE = 256
STEP_TILES = 2
MOE_BLOCK = 512
FF_CHUNK = 256
TAIL_ROWS = 128
CAST_ROWS = 128
NEG_BIG = -1e30
VMEM_LIMIT = 56 * 1024 * 1024


def _rms(x, g):
    return x * lax.rsqrt(jnp.mean(x * x, axis=-1, keepdims=True) + RMS_EPS) * g


def _dot(a, b):
    return jnp.dot(a, b, preferred_element_type=F32)


def _dot_nt(a, b):
    return lax.dot_general(a, b, (((1,), (1,)), ((), ())), preferred_element_type=F32)


def _pack_rows(x):
    half = x.shape[1] // 2
    xr = x.astype(BF16).astype(F32)
    lo = lax.bitcast_convert_type(xr[:, :half], U32) >> 16
    hi = lax.bitcast_convert_type(xr[:, half:], U32) & jnp.uint32(0xFFFF0000)
    return hi | lo


def _unpack_words(w):
    lo = lax.bitcast_convert_type(w << 16, F32).astype(BF16)
    hi = lax.bitcast_convert_type(w & jnp.uint32(0xFFFF0000), F32).astype(BF16)
    return lo, hi


def _split_bf16(x):
    hi = x.astype(BF16)
    lo = (x - hi.astype(F32)).astype(BF16)
    return hi, lo


def _mixer_kernel(x_ref, g1_ref, win_ref, convw_ref, wgk_ref, bgk_ref, gng_ref, wout_ref,
                  g2_ref, wrhl_ref, br_ref,
                  x1_ref, h2_ref, meta_ref, route_ref, seg_ref, cnt_ref,
                  proj_ref, ubuf_ref, la_ref, ycat_ref, state_ref, carry_ref):
    ts = x_ref.shape[1]
    b_idx = pl.program_id(0)
    s_idx = pl.program_id(1)

    @pl.when(s_idx == 0)
    def _():
        state_ref[...] = jnp.zeros_like(state_ref)
        ubuf_ref[0:SUBLANES, :] = jnp.zeros((SUBLANES, CONV_WIDTH), F32)

    @pl.when((s_idx == 0) & (b_idx == 0))
    def _():
        carry_ref[...] = jnp.zeros_like(carry_ref)

    x = x_ref[0]
    h = _rms(x, g1_ref[...]).astype(BF16)
    proj_ref[...] = _dot(h, win_ref[...])

    u = proj_ref[:, OFF_GC:OFF_GC + CONV_WIDTH] * proj_ref[:, OFF_UH:OFF_UH + CONV_WIDTH]
    ubuf_ref[SUBLANES:SUBLANES + ts, :] = u
    u1 = ubuf_ref[pl.ds(SUBLANES - 1, ts), :]
    u2 = ubuf_ref[pl.ds(SUBLANES - 2, ts), :]
    conv = convw_ref[0:1, :] * u2 + convw_ref[1:2, :] * u1 + convw_ref[2:3, :] * u
    ycat_ref[:, 0:CONV_WIDTH] = (proj_ref[:, OFF_GB:OFF_GB + CONV_WIDTH] * conv).astype(BF16)
    ubuf_ref[0:SUBLANES, :] = ubuf_ref[ts:ts + SUBLANES, :]

    gk = _dot(proj_ref[:, OFF_GKL:OFF_GKL + LANES].astype(BF16), wgk_ref[...]) + bgk_ref[...]
    log_sig = jnp.minimum(gk, 0.0) - jnp.log1p(jnp.exp(-jnp.abs(gk)))
    la_ref[...] = log_sig / GLA_NORMALIZER

    ci = lax.broadcasted_iota(I32, (GLA_CHUNK, GLA_CHUNK), 0)
    cj = lax.broadcasted_iota(I32, (GLA_CHUNK, GLA_CHUNK), 1)
    tri_incl = (cj <= ci).astype(BF16)
    causal = cj <= ci
    causal4 = jnp.concatenate([causal] * GLA_HEADS, axis=0)
    lane_qk = lax.broadcasted_iota(I32, (1, GLA_QK), 1)
    head_masks = [((lane_qk >= hd * GLA_DK) & (lane_qk < (hd + 1) * GLA_DK)).astype(F32)
                  for hd in range(GLA_HEADS)]
    gng = gng_ref[...]

    n_chunks = ts // GLA_CHUNK
    chunk_rows = [pl.ds(c * GLA_CHUNK, GLA_CHUNK) for c in range(n_chunks)]
    lane_c = lax.broadcasted_iota(I32, (GLA_QK, 2 * GLA_CHUNK), 1)
    qd_all, kd_all, kr_all, bl_all, v_all = [], [], [], [], []
    for rows in chunk_rows:
        la_hi, la_lo = _split_bf16(la_ref[rows, :])
        bcum = _dot(tri_incl, la_hi) + _dot(tri_incl, la_lo)
        blast = bcum[GLA_CHUNK - 1:GLA_CHUNK, :]
        q = proj_ref[rows, OFF_Q:OFF_Q + GLA_QK] * (GLA_DK ** -0.5)
        k = proj_ref[rows, OFF_K:OFF_K + GLA_QK]
        qd_all.append(q * jnp.exp(bcum))
        kd_all.append((k * jnp.exp(-bcum)).astype(BF16))
        kr_all.append(k * jnp.exp(blast - bcum))
        bl_all.append(blast)
        v_all.append(proj_ref[rows, OFF_V:OFF_V + GLA_WIDTH].astype(BF16))

    scores_all = []
    for c in range(n_chunks):
        q_stack = jnp.concatenate([qd_all[c] * m for m in head_masks], axis=0).astype(BF16)
        scores_all.append(
            jnp.where(causal4, _dot_nt(q_stack, kd_all[c]), 0.0).astype(BF16))
    o_intra_all = []
    for c in range(n_chunks):
        o_intra_all.append(jnp.concatenate(
            [_dot(scores_all[c][hd * GLA_CHUNK:(hd + 1) * GLA_CHUNK, :],
                  v_all[c][:, hd * GLA_DV:(hd + 1) * GLA_DV]) for hd in range(GLA_HEADS)], axis=1))
    kv_all, dcol_all = [], []
    for c in range(n_chunks):
        kt = jnp.concatenate(
            [kr_all[c], jnp.broadcast_to(bl_all[c], (GLA_CHUNK, GLA_QK))], axis=0).T
        dcol_all.append(jnp.exp(kt[:, GLA_CHUNK:GLA_CHUNK + 1]))
        kt_b = jnp.where(lane_c < GLA_CHUNK, kt, 0.0).astype(BF16)
        v_pad = jnp.concatenate([v_all[c], jnp.zeros_like(v_all[c])], axis=0)
        kv_all.append([_dot(kt_b[hd * GLA_DK:(hd + 1) * GLA_DK, :],
                            v_pad[:, hd * GLA_DV:(hd + 1) * GLA_DV]) for hd in range(GLA_HEADS)])

    o_all = []
    for c in range(n_chunks):
        state = state_ref[...]
        o_all.append(_dot(qd_all[c].astype(BF16), state.astype(BF16)) + o_intra_all[c])
        for hd in range(GLA_HEADS):
            rs = slice(hd * GLA_DK, (hd + 1) * GLA_DK)
            cs = slice(hd * GLA_DV, (hd + 1) * GLA_DV)
            state_ref[rs, cs] = dcol_all[c][rs, :] * state[rs, cs] + kv_all[c][hd]

    for c, rows in enumerate(chunk_rows):
        o = o_all[c]
        g_out = proj_ref[rows, OFF_GO:OFF_GO + GLA_WIDTH]
        o_n = jnp.concatenate(
            [_rms(o[:, hd * GLA_DV:(hd + 1) * GLA_DV], gng) for hd in range(GLA_HEADS)], axis=1)
        y = o_n * (g_out * jax.nn.sigmoid(g_out))
        ycat_ref[rows, CONV_WIDTH:CONV_WIDTH + GLA_WIDTH] = y.astype(BF16)

    x1 = x + _dot(ycat_ref[...], wout_ref[...])
    x1_ref[0] = x1
    h2 = _rms(x1, g2_ref[...])
    h2_hi, h2_lo = _split_bf16(h2)
    h2_ref[0] = h2_hi
    hi_terms = _dot(h2_hi, wrhl_ref[...])
    logits = (hi_terms[:, 0:LANES] + hi_terms[:, LANES:2 * LANES]
              + _dot(h2_lo, wrhl_ref[:, 0:LANES]) + br_ref[...])

    lt = logits.T[0:N_EXPERTS, :]
    erow = lax.broadcasted_iota(I32, (N_EXPERTS, ts), 0).astype(F32)
    work = lt
    sel = jnp.zeros((N_EXPERTS, ts), F32)
    top_v, top_i, top_oh = [], [], []
    for _ in range(TOP_K):
        m = jnp.max(work, axis=0, keepdims=True)
        idx = jnp.min(jnp.where(work == m, erow, float(N_EXPERTS)), axis=0, keepdims=True)
        oh = erow == idx
        top_v.append(m)
        top_i.append(idx)
        top_oh.append(oh)
        sel = sel + oh.astype(F32)
        work = jnp.where(oh, -jnp.inf, work)
    exps = [jnp.exp(tv - top_v[0]) for tv in top_v]
    denom = exps[0] + exps[1] + exps[2] + exps[3]
    gates = [e / denom for e in exps]

    tsrc = lax.broadcasted_iota(I32, (ts, ts), 0)
    tdst = lax.broadcasted_iota(I32, (ts, ts), 1)
    earlier = (tsrc < tdst).astype(BF16)
    local = _dot(sel.astype(BF16), earlier)
    carry = carry_ref[:, 0:1]
    ranks = [jnp.sum(jnp.where(oh, local + carry, 0.0), axis=0, keepdims=True) for oh in top_oh]

    ei = lax.broadcasted_iota(I32, (N_EXPERTS, N_EXPERTS), 0)
    ej = lax.broadcasted_iota(I32, (N_EXPERTS, N_EXPERTS), 1)
    lower_experts = (ej < ei).astype(BF16)
    lane_f = lax.broadcasted_iota(I32, (N_EXPERTS, LANES), 1)
    slot_base = []
    before = jnp.zeros((N_EXPERTS, 1), F32)
    for u in range(ts // SORT_TILE):
        sub_cnt = jnp.sum(sel[:, u * SORT_TILE:(u + 1) * SORT_TILE], axis=1, keepdims=True)
        seg_start = _dot(lower_experts,
                         jnp.broadcast_to(sub_cnt, (N_EXPERTS, LANES)).astype(BF16))[:, 0:1]
        seg_ref[u] = jnp.where(lane_f == 0, carry + before,
                               jnp.where(lane_f == 1, sub_cnt,
                                         jnp.where(lane_f == 2, seg_start, 0.0)))
        slot_base.append(jnp.broadcast_to(seg_start - before, (N_EXPERTS, SORT_TILE)))
        before = before + sub_cnt
    slot_base = jnp.concatenate(slot_base, axis=1)
    slots = [jnp.sum(jnp.where(oh, local + slot_base, 0.0), axis=0, keepdims=True)
             for oh in top_oh]
    new_carry = carry + before
    carry_ref[...] = jnp.broadcast_to(new_carry, carry_ref.shape)
    cnt_ref[...] = jnp.broadcast_to(new_carry, cnt_ref.shape)

    route = jnp.concatenate(top_i + gates + ranks + slots, axis=0)
    route_ref[...] = route
    meta_ref[0] = jnp.concatenate(
        [route, jnp.zeros((LANES - ROUTE_ROWS, ts), F32)], axis=0).T


def _mixer_call(x, g1, win, convw, wgk, bgk, gng, wout, g2, wrhl, br):
    bsz, seq, d = x.shape
    ts = SEQ_TILE
    grid = (bsz, seq // ts)

    def const(shape):
        return pl.BlockSpec(shape, lambda b, s: (0,) * len(shape))

    tile = lambda w: pl.BlockSpec((1, ts, w), lambda b, s: (b, s, 0))
    return pl.pallas_call(
        _mixer_kernel,
        grid=grid,
        in_specs=[tile(d), const(g1.shape), const(win.shape), const(convw.shape),
                  const(wgk.shape), const(bgk.shape), const(gng.shape), const(wout.shape),
                  const(g2.shape), const(wrhl.shape), const(br.shape)],
        out_specs=[tile(d),
                   tile(d),
                   tile(LANES),
                   pl.BlockSpec((ROUTE_ROWS, ts), lambda b, s: (0, b * (seq // ts) + s)),
                   pl.BlockSpec((ts // SORT_TILE, N_EXPERTS, LANES),
                                lambda b, s: (b * (seq // ts) + s, 0, 0)),
                   const((N_EXPERTS, LANES))],
        out_shape=[jax.ShapeDtypeStruct((bsz, seq, d), F32),
                   jax.ShapeDtypeStruct((bsz, seq, d), BF16),
                   jax.ShapeDtypeStruct((bsz, seq, LANES), F32),
                   jax.ShapeDtypeStruct((ROUTE_ROWS, bsz * seq), F32),
                   jax.ShapeDtypeStruct((bsz * seq // SORT_TILE, N_EXPERTS, LANES), F32),
                   jax.ShapeDtypeStruct((N_EXPERTS, LANES), F32)],
        scratch_shapes=[pltpu.VMEM((ts, D_IN_PAD), F32),
                        pltpu.VMEM((ts + SUBLANES, CONV_WIDTH), F32),
                        pltpu.VMEM((ts, GLA_QK), F32),
                        pltpu.VMEM((ts, D_MODEL), BF16),
                        pltpu.VMEM((GLA_QK, GLA_WIDTH), F32),
                        pltpu.VMEM((N_EXPERTS, LANES), F32)],
        compiler_params=pltpu.CompilerParams(
            dimension_semantics=("arbitrary", "arbitrary"), vmem_limit_bytes=VMEM_LIMIT),
        name="mixer",
    )(x, g1, win, convw, wgk, bgk, gng, wout, g2, wrhl, br)


def _dispatch_kernel(run_xs_ref, run_buf_ref, run_len_ref, pad_lo_ref, pad_n_ref,
                     route_ref, h2_ref, xs_hbm, sbuf, zbuf, sem, zsem):
    i = pl.program_id(0)
    n = pl.num_programs(0)
    td = SORT_TILE
    n_rows = TOP_K * td

    def zero_fill(wait):
        def fill(g, carry):
            ln = pl.multiple_of(pad_n_ref[g], ROW_TILES)

            @pl.when(ln > 0)
            def _():
                lo = pl.multiple_of(pad_lo_ref[g], ROW_TILES)
                cp = pltpu.make_async_copy(zbuf.at[pl.ds(0, ln), :], xs_hbm.at[pl.ds(lo, ln), :],
                                           zsem.at[0])
                if wait:
                    cp.wait()
                else:
                    cp.start()
            return carry

        lax.fori_loop(0, pad_lo_ref.shape[0], fill, 0)

    def wait_runs(buf_slot):
        pltpu.make_async_copy(sbuf.at[buf_slot], xs_hbm.at[pl.ds(0, n_rows * ROW_TILES), :],
                              sem.at[buf_slot]).wait()

    @pl.when(i == 0)
    def _():
        zbuf[...] = jnp.zeros_like(zbuf)
        zero_fill(wait=False)

    for u in range(STEP_TILES):
        @pl.when(i >= 1)
        def _(u=u):
            wait_runs(u)

        route = route_ref[:, u * td:(u + 1) * td]
        row = lax.broadcasted_iota(I32, (n_rows, td), 0).astype(F32)
        pick = jnp.zeros((n_rows, td), F32)
        for kk in range(TOP_K):
            pick = jnp.where(row == route[3 * TOP_K + kk:3 * TOP_K + kk + 1, :], 1.0, pick)
        rows_sorted = _dot(pick.astype(BF16), h2_ref[u * td:(u + 1) * td, :])
        words = _pack_rows(rows_sorted)
        for j in range(ROW_TILES):
            sbuf[u, pl.ds(j, n_rows, stride=ROW_TILES), :] = words[:, j * LANES:(j + 1) * LANES]

        def body(e, carry, u=u):
            g = (i * STEP_TILES + u) * N_EXPERTS + e
            ln = pl.multiple_of(run_len_ref[g], ROW_TILES)

            @pl.when(ln > 0)
            def _():
                src = pl.multiple_of(run_buf_ref[g], ROW_TILES)
                dst = pl.multiple_of(run_xs_ref[g], ROW_TILES)
                pltpu.make_async_copy(sbuf.at[u, pl.ds(src, ln), :], xs_hbm.at[pl.ds(dst, ln), :],
                                      sem.at[u]).start()
            return carry

        lax.fori_loop(0, N_EXPERTS, body, 0)

    @pl.when(i == n - 1)
    def _():
        for u in range(STEP_TILES):
            wait_runs(u)
        zero_fill(wait=True)


def _dispatch_call(run_xs, run_buf, run_len, pad_lo, pad_n, route, h2, n_rows):
    t, d = h2.shape
    td = SORT_TILE * STEP_TILES
    grid_spec = pltpu.PrefetchScalarGridSpec(
        num_scalar_prefetch=5,
        grid=(t // td,),
        in_specs=[pl.BlockSpec((ROUTE_ROWS, td), lambda i, *_: (0, i)),
                  pl.BlockSpec((td, d), lambda i, *_: (i, 0))],
        out_specs=pl.BlockSpec(memory_space=pl.ANY),
        scratch_shapes=[pltpu.VMEM((STEP_TILES, TOP_K * SORT_TILE * ROW_TILES, LANES), U32),
                        pltpu.VMEM((MOE_BLOCK * ROW_TILES, LANES), U32),
                        pltpu.SemaphoreType.DMA((STEP_TILES,)),
                        pltpu.SemaphoreType.DMA((1,))],
    )
    return pl.pallas_call(
        _dispatch_kernel,
        grid_spec=grid_spec,
        out_shape=jax.ShapeDtypeStruct((n_rows * ROW_TILES, LANES), U32),
        compiler_params=pltpu.CompilerParams(
            dimension_semantics=("arbitrary",), vmem_limit_bytes=VMEM_LIMIT,
            has_side_effects=True),
        name="dispatch",
    )(run_xs, run_buf, run_len, pad_lo, pad_n, route, h2)


def _experts_kernel(bexp_ref, first_ref, nxt_ref, nval_ref, nreal_ref,
                    xs_ref, wgu_hbm, bgu_ref, wd_hbm, bd_ref,
                    ys_ref,
                    xb_ref, act_ref, wgu_stage, wd_stage, wgu_bf, wd_bf, wsem):
    i = pl.program_id(0)
    nreal = nreal_ref[0]
    bm = xb_ref.shape[0]

    def weight_copies(e):
        return (pltpu.make_async_copy(wgu_hbm.at[e], wgu_stage, wsem.at[0]),
                pltpu.make_async_copy(wd_hbm.at[e], wd_stage, wsem.at[1]))

    @pl.when(i >= nreal)
    def _():
        ys_ref[...] = jnp.zeros_like(ys_ref)

    @pl.when(i < nreal)
    def _():
        e = bexp_ref[i]

        @pl.when(i == 0)
        def _():
            for cp in weight_copies(e):
                cp.start(priority=1)

        @pl.when(first_ref[i] == 1)
        def _():
            for cp in weight_copies(e):
                cp.wait()

            def cast_gu(r, carry):
                rows = pl.ds(pl.multiple_of(r * CAST_ROWS, CAST_ROWS), CAST_ROWS)
                wgu_bf[rows, :] = wgu_stage[rows, :].astype(BF16)
                return carry

            def cast_d(r, carry):
                rows = pl.ds(pl.multiple_of(r * CAST_ROWS, CAST_ROWS), CAST_ROWS)
                wd_bf[rows, :] = wd_stage[rows, :].astype(BF16)
                return carry

            lax.fori_loop(0, D_MODEL // CAST_ROWS, cast_gu, 0)
            lax.fori_loop(0, D_FF // CAST_ROWS, cast_d, 0)

            @pl.when(nxt_ref[i] >= 0)
            def _():
                for cp in weight_copies(nxt_ref[i]):
                    cp.start(priority=1)

        def mlp(rows):
            for j in range(ROW_TILES):
                lo, hi = _unpack_words(xs_ref[pl.ds(j, rows, stride=ROW_TILES), :])
                xb_ref[0:rows, j * LANES:(j + 1) * LANES] = lo
                xb_ref[0:rows, D_MODEL // 2 + j * LANES:D_MODEL // 2 + (j + 1) * LANES] = hi
            for c in range(D_FF // FF_CHUNK):
                f0 = c * FF_CHUNK
                xb = xb_ref[0:rows, :]
                gate = _dot(xb, wgu_bf[:, f0:f0 + FF_CHUNK]) + bgu_ref[0, :, f0:f0 + FF_CHUNK]
                up = (_dot(xb, wgu_bf[:, D_FF + f0:D_FF + f0 + FF_CHUNK])
                      + bgu_ref[0, :, D_FF + f0:D_FF + f0 + FF_CHUNK])
                gate = jnp.minimum(gate, SWIGLU_LIMIT)
                up = jnp.clip(up, -SWIGLU_LIMIT, SWIGLU_LIMIT)
                glu = gate * jax.nn.sigmoid(gate * SWIGLU_ALPHA)
                act_ref[0:rows, f0:f0 + FF_CHUNK] = ((up + 1.0) * glu).astype(BF16)
            out = _pack_rows(_dot(act_ref[0:rows, :], wd_bf[...]) + bd_ref[0])
            for j in range(ROW_TILES):
                ys_ref[pl.ds(j, rows, stride=ROW_TILES), :] = out[:, j * LANES:(j + 1) * LANES]

        n_routed = nval_ref[i]
        for rows in range(TAIL_ROWS, bm + 1, TAIL_ROWS):
            @pl.when((n_routed > rows - TAIL_ROWS) & (n_routed <= rows))
            def _(rows=rows):
                mlp(rows)
                if rows < bm:
                    ys_ref[rows * ROW_TILES:bm * ROW_TILES, :] = jnp.zeros(
                        ((bm - rows) * ROW_TILES, LANES), U32)


def _experts_call(bexp, first, nxt, nval, nreal, xs, wgu, bgu, wd, bd):
    bm = MOE_BLOCK
    n_blocks = xs.shape[0] // (bm * ROW_TILES)
    bgu3 = bgu.reshape(N_EXPERTS, 1, 2 * D_FF)
    bd3 = bd.reshape(N_EXPERTS, 1, D_MODEL)
    grid_spec = pltpu.PrefetchScalarGridSpec(
        num_scalar_prefetch=5,
        grid=(n_blocks,),
        in_specs=[
            pl.BlockSpec((bm * ROW_TILES, LANES),
                         lambda i, be, fi, nx, nv, nr: (jnp.minimum(i, nr[0] - 1), 0)),
            pl.BlockSpec(memory_space=pl.ANY),
            pl.BlockSpec((1, 1, 2 * D_FF), lambda i, be, fi, nx, nv, nr: (be[i], 0, 0)),
            pl.BlockSpec(memory_space=pl.ANY),
            pl.BlockSpec((1, 1, D_MODEL), lambda i, be, fi, nx, nv, nr: (be[i], 0, 0)),
        ],
        out_specs=pl.BlockSpec((bm * ROW_TILES, LANES), lambda i, be, fi, nx, nv, nr: (i, 0)),
        scratch_shapes=[pltpu.VMEM((bm, D_MODEL), BF16),
                        pltpu.VMEM((bm, D_FF), BF16),
                        pltpu.VMEM((D_MODEL, 2 * D_FF), F32),
                        pltpu.VMEM((D_FF, D_MODEL), F32),
                        pltpu.VMEM((D_MODEL, 2 * D_FF), BF16),
                        pltpu.VMEM((D_FF, D_MODEL), BF16),
                        pltpu.SemaphoreType.DMA((2,))],
    )
    return pl.pallas_call(
        _experts_kernel,
        grid_spec=grid_spec,
        out_shape=jax.ShapeDtypeStruct(xs.shape, U32),
        compiler_params=pltpu.CompilerParams(
            dimension_semantics=("arbitrary",), vmem_limit_bytes=VMEM_LIMIT),
        name="experts",
    )(bexp, first, nxt, nval, nreal, xs, wgu, bgu3, wd, bd3)


def _combine_kernel(src_ref, dst_ref, len_ref, x1_ref, meta_ref, g_ref, ys_hbm, out_ref,
                    ybuf, ysort_ref, sem):
    i = pl.program_id(0)
    n = pl.num_programs(0)
    tb = SORT_TILE
    n_rows = TOP_K * tb

    def start_runs(tile, u):
        def body(e, carry):
            g = tile * N_EXPERTS + e
            ln = pl.multiple_of(len_ref[g], ROW_TILES)

            @pl.when(ln > 0)
            def _():
                src = pl.multiple_of(src_ref[g], ROW_TILES)
                dst = pl.multiple_of(dst_ref[g], ROW_TILES)
                pltpu.make_async_copy(ys_hbm.at[pl.ds(src, ln), :],
                                      ybuf.at[u, pl.ds(dst, ln), :], sem.at[u]).start()
            return carry
        lax.fori_loop(0, N_EXPERTS, body, 0)

    @pl.when(i == 0)
    def _():
        for u in range(STEP_TILES):
            start_runs(u, u)

    for u in range(STEP_TILES):
        rs = slice(u * tb, (u + 1) * tb)
        pltpu.make_async_copy(ys_hbm.at[pl.ds(0, n_rows * ROW_TILES), :], ybuf.at[u],
                              sem.at[u]).wait()
        for j in range(ROW_TILES):
            lo, hi = _unpack_words(ybuf[u, pl.ds(j, n_rows, stride=ROW_TILES), :])
            ysort_ref[:, j * LANES:(j + 1) * LANES] = lo
            ysort_ref[:, D_MODEL // 2 + j * LANES:D_MODEL // 2 + (j + 1) * LANES] = hi

        @pl.when(i + 1 < n)
        def _(u=u):
            start_runs((i + 1) * STEP_TILES + u, u)

        meta = meta_ref[rs, :]
        col = lax.broadcasted_iota(I32, (tb, n_rows), 1).astype(F32)
        weights = jnp.zeros((tb, n_rows), F32)
        for kk in range(TOP_K):
            weights = jnp.where(col == meta[:, 3 * TOP_K + kk:3 * TOP_K + kk + 1],
                                meta[:, TOP_K + kk:TOP_K + kk + 1], weights)
        acc = x1_ref[rs, :] + _dot(weights.astype(BF16), ysort_ref[...])
        out_ref[rs, :] = _rms(acc, g_ref[...])


def _combine_call(seg_src, seg_dst, seg_len, x1, meta, g, ys):
    t, d = x1.shape
    tb = SORT_TILE * STEP_TILES
    nb = t // tb
    grid_spec = pltpu.PrefetchScalarGridSpec(
        num_scalar_prefetch=3,
        grid=(nb,),
        in_specs=[pl.BlockSpec((tb, d), lambda i, a, b, c: (i, 0)),
                  pl.BlockSpec((tb, LANES), lambda i, a, b, c: (i, 0)),
                  pl.BlockSpec((1, d), lambda i, a, b, c: (0, 0)),
                  pl.BlockSpec(memory_space=pl.ANY)],
        out_specs=pl.BlockSpec((tb, d), lambda i, a, b, c: (i, 0)),
        scratch_shapes=[pltpu.VMEM((STEP_TILES, TOP_K * SORT_TILE * ROW_TILES, LANES), U32),
                        pltpu.VMEM((TOP_K * SORT_TILE, D_MODEL), BF16),
                        pltpu.SemaphoreType.DMA((STEP_TILES,))],
    )
    return pl.pallas_call(
        _combine_kernel,
        grid_spec=grid_spec,
        out_shape=jax.ShapeDtypeStruct((t, d), F32),
        compiler_params=pltpu.CompilerParams(
            dimension_semantics=("arbitrary",), vmem_limit_bytes=VMEM_LIMIT),
        name="combine",
    )(seg_src, seg_dst, seg_len, x1, meta, g, ys)


def _routing_tables(counts, t):
    bm = MOE_BLOCK
    n_blocks = t * TOP_K // bm + N_EXPERTS
    eids = jnp.arange(N_EXPERTS, dtype=I32)
    nblk_e = (counts + bm - 1) // bm
    blk_end = jnp.sum(jnp.where(eids[None, :] <= eids[:, None], nblk_e[None, :], 0), axis=1)
    blk_start = blk_end - nblk_e
    nreal = blk_end[N_EXPERTS - 1]
    pad_start = blk_start * bm
    blk = jnp.arange(n_blocks, dtype=I32)
    bexp = jnp.minimum(jnp.sum((blk_end[None, :] <= blk[:, None]).astype(I32), axis=1),
                       N_EXPERTS - 1)
    blk_is_e = bexp[:, None] == eids[None, :]
    pick = lambda tab: jnp.sum(jnp.where(blk_is_e, tab[None, :], 0), axis=1)
    first = (blk == pick(blk_start)).astype(I32)
    nxt_e = jnp.sum((blk_end[None, :] <= blk_end[:, None]).astype(I32), axis=1)
    nxt_e = jnp.where(blk_end < nreal, jnp.minimum(nxt_e, N_EXPERTS - 1), -1)
    nxt = pick(nxt_e)
    nval = jnp.clip(pick(counts) - (blk - pick(blk_start)) * bm, 0, bm)
    tail_blk = jnp.arange(N_EXPERTS, dtype=I32) + nreal
    pad_lo = jnp.concatenate([pad_start + counts, jnp.minimum(tail_blk, n_blocks - 1) * bm])
    pad_n = jnp.concatenate([nblk_e * bm - counts, jnp.where(tail_blk < n_blocks, bm, 0)])
    return (pad_start, bexp, first, nxt, nval, nreal.reshape(1).astype(I32), n_blocks,
            pad_lo * ROW_TILES, pad_n * ROW_TILES)


def kernel(x, norm_mix_g, w_in, conv_w, w_gk_up, b_gk_up, gla_norm_g, w_out, norm_ffn_g,
           w_router, b_router, w_gate_up, b_gate_up, w_down, b_down, norm_final_g):
    bsz, seq, d = x.shape
    t = bsz * seq
    assert w_in.shape[0] == 1, "single-layer trunk only"
    assert conv_w.shape[1] == CONV_K and ROUTE_ROWS % SUBLANES == 0
    l = 0
    d_in = w_in.shape[-1]
    win = jnp.pad(w_in[l].astype(BF16), ((0, 0), (0, D_IN_PAD - d_in)))
    wgk = jnp.pad(w_gk_up[l], ((0, LANES - GLA_RANK), (0, 0))).astype(BF16)
    wr = jnp.pad(w_router[l], ((0, 0), (0, LANES - N_EXPERTS)))
    wrh = wr.astype(BF16)
    wrhl = jnp.concatenate([wrh, (wr - wrh.astype(F32)).astype(BF16)], axis=1)
    br = jnp.pad(b_router[l], (0, LANES - N_EXPERTS), constant_values=NEG_BIG).reshape(1, LANES)

    x1, h2, meta, route, seg, cnt = _mixer_call(
        x, norm_mix_g[l].reshape(1, d), win, conv_w[l], wgk, b_gk_up[l].reshape(1, GLA_QK),
        gla_norm_g[l].reshape(1, GLA_DV), w_out[l].astype(BF16), norm_ffn_g[l].reshape(1, d),
        wrhl, br)

    meta2 = meta.reshape(t, LANES)
    counts = cnt[:, 0].astype(I32)
    (pad_start, bexp, first, nxt, nval, nreal, n_blocks, pad_lo,
     pad_n) = _routing_tables(counts, t)
    seg_src = ((pad_start[None, :] + seg[:, :, 0].astype(I32)) * ROW_TILES).reshape(-1)
    seg_len = (seg[:, :, 1].astype(I32) * ROW_TILES).reshape(-1)
    seg_dst = (seg[:, :, 2].astype(I32) * ROW_TILES).reshape(-1)
    xs = _dispatch_call(seg_src, seg_dst, seg_len, pad_lo, pad_n, route, h2.reshape(t, d),
                        n_blocks * MOE_BLOCK)
    ys = _experts_call(bexp, first, nxt, nval, nreal, xs, w_gate_up[l], b_gate_up[l], w_down[l],
                       b_down[l])
    out = _combine_call(seg_src, seg_dst, seg_len, x1.reshape(t, d), meta2,
                        norm_final_g.reshape(1, d), ys)
    return out.reshape(bsz, seq, d)
```

```python
import jax
import jax.numpy as jnp
from jax import lax
from jax.experimental import pallas as pl
from jax.experimental.pallas import tpu as pltpu

F32 = jnp.float32
BF16 = jnp.bfloat16
I32 = jnp.int32
U32 = jnp.uint32

D_MODEL = 1024
CONV_WIDTH = 512
CONV_K = 3
GLA_WIDTH = 512
GLA_HEADS = 4
GLA_DV = 128
GLA_DK = 64
GLA_QK = GLA_HEADS * GLA_DK
GLA_RANK = 16
GLA_NORMALIZER = 16.0
GLA_CHUNK = 64
N_EXPERTS = 32
TOP_K = 4
D_FF = 1024
SWIGLU_LIMIT = 7.0
SWIGLU_ALPHA = 1.702
RMS_EPS = 1e-5

LANES = 128
SUBLANES = 8
ROW_TILES = D_MODEL // LANES // 2
ROUTE_ROWS = 4 * TOP_K

OFF_UH = 0
OFF_GB = OFF_UH + CONV_WIDTH
OFF_GC = OFF_GB + CONV_WIDTH
OFF_Q = OFF_GC + CONV_WIDTH
OFF_K = OFF_Q + GLA_QK
OFF_V = OFF_K + GLA_QK
OFF_GO = OFF_V + GLA_WIDTH
OFF_GKL = OFF_GO + GLA_WIDTH
D_IN_PAD = OFF_GKL + LANES

SEQ_TILE = 512
SORT_TILE = 256
STEP_TILES = 2
MOE_BLOCK = 512
FF_CHUNK = 256
TAIL_ROWS = 128
CAST_ROWS = 128
NEG_BIG = -1e30
VMEM_LIMIT = 56 * 1024 * 1024


def _rms(x, g):
    return x * lax.rsqrt(jnp.mean(x * x, axis=-1, keepdims=True) + RMS_EPS) * g


def _dot(a, b):
    return jnp.dot(a, b, preferred_element_type=F32)


def _dot_nt(a, b):
    return lax.dot_general(a, b, (((1,), (1,)), ((), ())), preferred_element_type=F32)


def _pack_rows(x):
    half = x.shape[1] // 2
    xr = x.astype(BF16).astype(F32)
    lo = lax.bitcast_convert_type(xr[:, :half], U32) >> 16
    hi = lax.bitcast_convert_type(xr[:, half:], U32) & jnp.uint32(0xFFFF0000)
    return hi | lo


def _unpack_words(w):
    lo = lax.bitcast_convert_type(w << 16, F32).astype(BF16)
    hi = lax.bitcast_convert_type(w & jnp.uint32(0xFFFF0000), F32).astype(BF16)
    return lo, hi


def _split_bf16(x):
    hi = x.astype(BF16)
    lo = (x - hi.astype(F32)).astype(BF16)
    return hi, lo


def _mixer_kernel(x_ref, g1_ref, win_ref, convw_ref, wgk_ref, bgk_ref, gng_ref, wout_ref,
                  g2_ref, wrhl_ref, br_ref,
                  x1_ref, h2_ref, meta_ref, route_ref, seg_ref, cnt_ref,
                  proj_ref, ubuf_ref, la_ref, ycat_ref, state_ref, carry_ref):
    ts = x_ref.shape[1]
    b_idx = pl.program_id(0)
    s_idx = pl.program_id(1)

    @pl.when(s_idx == 0)
    def _():
        state_ref[...] = jnp.zeros_like(state_ref)
        ubuf_ref[0:SUBLANES, :] = jnp.zeros((SUBLANES, CONV_WIDTH), F32)

    @pl.when((s_idx == 0) & (b_idx == 0))
    def _():
        carry_ref[...] = jnp.zeros_like(carry_ref)

    x = x_ref[0]
    h = _rms(x, g1_ref[...]).astype(BF16)
    proj_ref[...] = _dot(h, win_ref[...])

    u = proj_ref[:, OFF_GC:OFF_GC + CONV_WIDTH] * proj_ref[:, OFF_UH:OFF_UH + CONV_WIDTH]
    ubuf_ref[SUBLANES:SUBLANES + ts, :] = u
    u1 = ubuf_ref[pl.ds(SUBLANES - 1, ts), :]
    u2 = ubuf_ref[pl.ds(SUBLANES - 2, ts), :]
    conv = convw_ref[0:1, :] * u2 + convw_ref[1:2, :] * u1 + convw_ref[2:3, :] * u
    ycat_ref[:, 0:CONV_WIDTH] = (proj_ref[:, OFF_GB:OFF_GB + CONV_WIDTH] * conv).astype(BF16)
    ubuf_ref[0:SUBLANES, :] = ubuf_ref[ts:ts + SUBLANES, :]

    gk = _dot(proj_ref[:, OFF_GKL:OFF_GKL + LANES].astype(BF16), wgk_ref[...]) + bgk_ref[...]
    log_sig = jnp.minimum(gk, 0.0) - jnp.log1p(jnp.exp(-jnp.abs(gk)))
    la_ref[...] = log_sig / GLA_NORMALIZER

    ci = lax.broadcasted_iota(I32, (GLA_CHUNK, GLA_CHUNK), 0)
    cj = lax.broadcasted_iota(I32, (GLA_CHUNK, GLA_CHUNK), 1)
    tri_incl = (cj <= ci).astype(BF16)
    causal = cj <= ci
    causal4 = jnp.concatenate([causal] * GLA_HEADS, axis=0)
    lane_qk = lax.broadcasted_iota(I32, (1, GLA_QK), 1)
    head_masks = [((lane_qk >= hd * GLA_DK) & (lane_qk < (hd + 1) * GLA_DK)).astype(F32)
                  for hd in range(GLA_HEADS)]
    gng = gng_ref[...]

    n_chunks = ts // GLA_CHUNK
    chunk_rows = [pl.ds(c * GLA_CHUNK, GLA_CHUNK) for c in range(n_chunks)]
    lane_c = lax.broadcasted_iota(I32, (GLA_QK, 2 * GLA_CHUNK), 1)
    qd_all, kd_all, kr_all, bl_all, v_all = [], [], [], [], []
    for rows in chunk_rows:
        la_hi, la_lo = _split_bf16(la_ref[rows, :])
        bcum = _dot(tri_incl, la_hi) + _dot(tri_incl, la_lo)
        blast = bcum[GLA_CHUNK - 1:GLA_CHUNK, :]
        q = proj_ref[rows, OFF_Q:OFF_Q + GLA_QK] * (GLA_DK ** -0.5)
        k = proj_ref[rows, OFF_K:OFF_K + GLA_QK]
        qd_all.append(q * jnp.exp(bcum))
        kd_all.append((k * jnp.exp(-bcum)).astype(BF16))
        kr_all.append(k * jnp.exp(blast - bcum))
        bl_all.append(blast)
        v_all.append(proj_ref[rows, OFF_V:OFF_V + GLA_WIDTH].astype(BF16))

    scores_all = []
    for c in range(n_chunks):
        q_stack = jnp.concatenate([qd_all[c] * m for m in head_masks], axis=0).astype(BF16)
        scores_all.append(
            jnp.where(causal4, _dot_nt(q_stack, kd_all[c]), 0.0).astype(BF16))
    o_intra_all = []
    for c in range(n_chunks):
        o_intra_all.append(jnp.concatenate(
            [_dot(scores_all[c][hd * GLA_CHUNK:(hd + 1) * GLA_CHUNK, :],
                  v_all[c][:, hd * GLA_DV:(hd + 1) * GLA_DV]) for hd in range(GLA_HEADS)], axis=1))
    kv_all, dcol_all = [], []
    for c in range(n_chunks):
        kt = jnp.concatenate(
            [kr_all[c], jnp.broadcast_to(bl_all[c], (GLA_CHUNK, GLA_QK))], axis=0).T
        dcol_all.append(jnp.exp(kt[:, GLA_CHUNK:GLA_CHUNK + 1]))
        kt_b = jnp.where(lane_c < GLA_CHUNK, kt, 0.0).astype(BF16)
        v_pad = jnp.concatenate([v_all[c], jnp.zeros_like(v_all[c])], axis=0)
        kv_all.append([_dot(kt_b[hd * GLA_DK:(hd + 1) * GLA_DK, :],
                            v_pad[:, hd * GLA_DV:(hd + 1) * GLA_DV]) for hd in range(GLA_HEADS)])

    o_all = []
    for c in range(n_chunks):
        state = state_ref[...]
        o_all.append(_dot(qd_all[c].astype(BF16), state.astype(BF16)) + o_intra_all[c])
        for hd in range(GLA_HEADS):
            rs = slice(hd * GLA_DK, (hd + 1) * GLA_DK)
            cs = slice(hd * GLA_DV, (hd + 1) * GLA_DV)
            state_ref[rs, cs] = dcol_all[c][rs, :] * state[rs, cs] + kv_all[c][hd]

    for c, rows in enumerate(chunk_rows):
        o = o_all[c]
        g_out = proj_ref[rows, OFF_GO:OFF_GO + GLA_WIDTH]
        o_n = jnp.concatenate(
            [_rms(o[:, hd * GLA_DV:(hd + 1) * GLA_DV], gng) for hd in range(GLA_HEADS)], axis=1)
        y = o_n * (g_out * jax.nn.sigmoid(g_out))
        ycat_ref[rows, CONV_WIDTH:CONV_WIDTH + GLA_WIDTH] = y.astype(BF16)

    x1 = x + _dot(ycat_ref[...], wout_ref[...])
    x1_ref[0] = x1
    h2 = _rms(x1, g2_ref[...])
    h2_hi, h2_lo = _split_bf16(h2)
    h2_ref[0] = h2_hi
    hi_terms = _dot(h2_hi, wrhl_ref[...])
    logits = (hi_terms[:, 0:LANES] + hi_terms[:, LANES:2 * LANES]
              + _dot(h2_lo, wrhl_ref[:, 0:LANES]) + br_ref[...])

    lt = logits.T[0:N_EXPERTS, :]
    erow = lax.broadcasted_iota(I32, (N_EXPERTS, ts), 0).astype(F32)
    work = lt
    sel = jnp.zeros((N_EXPERTS, ts), F32)
    top_v, top_i, top_oh = [], [], []
    for _ in range(TOP_K):
        m = jnp.max(work, axis=0, keepdims=True)
        idx = jnp.min(jnp.where(work == m, erow, float(N_EXPERTS)), axis=0, keepdims=True)
        oh = erow == idx
        top_v.append(m)
        top_i.append(idx)
        top_oh.append(oh)
        sel = sel + oh.astype(F32)
        work = jnp.where(oh, -jnp.inf, work)
    exps = [jnp.exp(tv - top_v[0]) for tv in top_v]
    denom = exps[0] + exps[1] + exps[2] + exps[3]
    gates = [e / denom for e in exps]

    tsrc = lax.broadcasted_iota(I32, (ts, ts), 0)
    tdst = lax.broadcasted_iota(I32, (ts, ts), 1)
    earlier = (tsrc < tdst).astype(BF16)
    local = _dot(sel.astype(BF16), earlier)
    carry = carry_ref[:, 0:1]
    ranks = [jnp.sum(jnp.where(oh, local + carry, 0.0), axis=0, keepdims=True) for oh in top_oh]

    ei = lax.broadcasted_iota(I32, (N_EXPERTS, N_EXPERTS), 0)
    ej = lax.broadcasted_iota(I32, (N_EXPERTS, N_EXPERTS), 1)
    lower_experts = (ej < ei).astype(BF16)
    lane_f = lax.broadcasted_iota(I32, (N_EXPERTS, LANES), 1)
    slot_base = []
    before = jnp.zeros((N_EXPERTS, 1), F32)
    for u in range(ts // SORT_TILE):
        sub_cnt = jnp.sum(sel[:, u * SORT_TILE:(u + 1) * SORT_TILE], axis=1, keepdims=True)
        seg_start = _dot(lower_experts,
                         jnp.broadcast_to(sub_cnt, (N_EXPERTS, LANES)).astype(BF16))[:, 0:1]
        seg_ref[u] = jnp.where(lane_f == 0, carry + before,
                               jnp.where(lane_f == 1, sub_cnt,
                                         jnp.where(lane_f == 2, seg_start, 0.0)))
        slot_base.append(jnp.broadcast_to(seg_start - before, (N_EXPERTS, SORT_TILE)))
        before = before + sub_cnt
    slot_base = jnp.concatenate(slot_base, axis=1)
    slots = [jnp.sum(jnp.where(oh, local + slot_base, 0.0), axis=0, keepdims=True)
             for oh in top_oh]
    new_carry = carry + before
    carry_ref[...] = jnp.broadcast_to(new_carry, carry_ref.shape)
    cnt_ref[...] = jnp.broadcast_to(new_carry, cnt_ref.shape)

    route = jnp.concatenate(top_i + gates + ranks + slots, axis=0)
    route_ref[...] = route
    meta_ref[0] = jnp.concatenate(
        [route, jnp.zeros((LANES - ROUTE_ROWS, ts), F32)], axis=0).T


def _mixer_call(x, g1, win, convw, wgk, bgk, gng, wout, g2, wrhl, br):
    bsz, seq, d = x.shape
    ts = SEQ_TILE
    grid = (bsz, seq // ts)

    def const(shape):
        return pl.BlockSpec(shape, lambda b, s: (0,) * len(shape))

    tile = lambda w: pl.BlockSpec((1, ts, w), lambda b, s: (b, s, 0))
    return pl.pallas_call(
        _mixer_kernel,
        grid=grid,
        in_specs=[tile(d), const(g1.shape), const(win.shape), const(convw.shape),
                  const(wgk.shape), const(bgk.shape), const(gng.shape), const(wout.shape),
                  const(g2.shape), const(wrhl.shape), const(br.shape)],
        out_specs=[tile(d),
                   tile(d),
                   tile(LANES),
                   pl.BlockSpec((ROUTE_ROWS, ts), lambda b, s: (0, b * (seq // ts) + s)),
                   pl.BlockSpec((ts // SORT_TILE, N_EXPERTS, LANES),
                                lambda b, s: (b * (seq // ts) + s, 0, 0)),
                   const((N_EXPERTS, LANES))],
        out_shape=[jax.ShapeDtypeStruct((bsz, seq, d), F32),
                   jax.ShapeDtypeStruct((bsz, seq, d), BF16),
                   jax.ShapeDtypeStruct((bsz, seq, LANES), F32),
                   jax.ShapeDtypeStruct((ROUTE_ROWS, bsz * seq), F32),
                   jax.ShapeDtypeStruct((bsz * seq // SORT_TILE, N_EXPERTS, LANES), F32),
                   jax.ShapeDtypeStruct((N_EXPERTS, LANES), F32)],
        scratch_shapes=[pltpu.VMEM((ts, D_IN_PAD), F32),
                        pltpu.VMEM((ts + SUBLANES, CONV_WIDTH), F32),
                        pltpu.VMEM((ts, GLA_QK), F32),
                        pltpu.VMEM((ts, D_MODEL), BF16),
                        pltpu.VMEM((GLA_QK, GLA_WIDTH), F32),
                        pltpu.VMEM((N_EXPERTS, LANES), F32)],
        compiler_params=pltpu.CompilerParams(
            dimension_semantics=("arbitrary", "arbitrary"), vmem_limit_bytes=VMEM_LIMIT),
        name="mixer",
    )(x, g1, win, convw, wgk, bgk, gng, wout, g2, wrhl, br)


def _dispatch_kernel(run_xs_ref, run_buf_ref, run_len_ref, pad_lo_ref, pad_n_ref,
                     route_ref, h2_ref, xs_hbm, sbuf, zbuf, sem, zsem):
    i = pl.program_id(0)
    n = pl.num_programs(0)
    td = SORT_TILE
    n_rows = TOP_K * td

    def zero_fill(wait):
        def fill(g, carry):
            ln = pl.multiple_of(pad_n_ref[g], ROW_TILES)

            @pl.when(ln > 0)
            def _():
                lo = pl.multiple_of(pad_lo_ref[g], ROW_TILES)
                cp = pltpu.make_async_copy(zbuf.at[pl.ds(0, ln), :], xs_hbm.at[pl.ds(lo, ln), :],
                                           zsem.at[0])
                if wait:
                    cp.wait()
                else:
                    cp.start()
            return carry

        lax.fori_loop(0, pad_lo_ref.shape[0], fill, 0)

    def wait_runs(buf_slot):
        pltpu.make_async_copy(sbuf.at[buf_slot], xs_hbm.at[pl.ds(0, n_rows * ROW_TILES), :],
                              sem.at[buf_slot]).wait()

    @pl.when(i == 0)
    def _():
        zbuf[...] = jnp.zeros_like(zbuf)
        zero_fill(wait=False)

    for u in range(STEP_TILES):
        @pl.when(i >= 1)
        def _(u=u):
            wait_runs(u)

        route = route_ref[:, u * td:(u + 1) * td]
        row = lax.broadcasted_iota(I32, (n_rows, td), 0).astype(F32)
        pick = jnp.zeros((n_rows, td), F32)
        for kk in range(TOP_K):
            pick = jnp.where(row == route[3 * TOP_K + kk:3 * TOP_K + kk + 1, :], 1.0, pick)
        rows_sorted = _dot(pick.astype(BF16), h2_ref[u * td:(u + 1) * td, :])
        words = _pack_rows(rows_sorted)
        for j in range(ROW_TILES):
            sbuf[u, pl.ds(j, n_rows, stride=ROW_TILES), :] = words[:, j * LANES:(j + 1) * LANES]

        def body(e, carry, u=u):
            g = (i * STEP_TILES + u) * N_EXPERTS + e
            ln = pl.multiple_of(run_len_ref[g], ROW_TILES)

            @pl.when(ln > 0)
            def _():
                src = pl.multiple_of(run_buf_ref[g], ROW_TILES)
                dst = pl.multiple_of(run_xs_ref[g], ROW_TILES)
                pltpu.make_async_copy(sbuf.at[u, pl.ds(src, ln), :], xs_hbm.at[pl.ds(dst, ln), :],
                                      sem.at[u]).start()
            return carry

        lax.fori_loop(0, N_EXPERTS, body, 0)

    @pl.when(i == n - 1)
    def _():
        for u in range(STEP_TILES):
            wait_runs(u)
        zero_fill(wait=True)


def _dispatch_call(run_xs, run_buf, run_len, pad_lo, pad_n, route, h2, n_rows):
    t, d = h2.shape
    td = SORT_TILE * STEP_TILES
    grid_spec = pltpu.PrefetchScalarGridSpec(
        num_scalar_prefetch=5,
        grid=(t // td,),
        in_specs=[pl.BlockSpec((ROUTE_ROWS, td), lambda i, *_: (0, i)),
                  pl.BlockSpec((td, d), lambda i, *_: (i, 0))],
        out_specs=pl.BlockSpec(memory_space=pl.ANY),
        scratch_shapes=[pltpu.VMEM((STEP_TILES, TOP_K * SORT_TILE * ROW_TILES, LANES), U32),
                        pltpu.VMEM((MOE_BLOCK * ROW_TILES, LANES), U32),
                        pltpu.SemaphoreType.DMA((STEP_TILES,)),
                        pltpu.SemaphoreType.DMA((1,))],
    )
    return pl.pallas_call(
        _dispatch_kernel,
        grid_spec=grid_spec,
        out_shape=jax.ShapeDtypeStruct((n_rows * ROW_TILES, LANES), U32),
        compiler_params=pltpu.CompilerParams(
            dimension_semantics=("arbitrary",), vmem_limit_bytes=VMEM_LIMIT,
            has_side_effects=True),
        name="dispatch",
    )(run_xs, run_buf, run_len, pad_lo, pad_n, route, h2)


def _experts_kernel(bexp_ref, first_ref, nxt_ref, nval_ref, nreal_ref,
                    xs_ref, wgu_hbm, bgu_ref, wd_hbm, bd_ref,
                    ys_ref,
                    xb_ref, act_ref, wgu_stage, wd_stage, wgu_bf, wd_bf, wsem):
    i = pl.program_id(0)
    nreal = nreal_ref[0]
    bm = xb_ref.shape[0]

    def weight_copies(e):
        return (pltpu.make_async_copy(wgu_hbm.at[e], wgu_stage, wsem.at[0]),
                pltpu.make_async_copy(wd_hbm.at[e], wd_stage, wsem.at[1]))

    @pl.when(i >= nreal)
    def _():
        ys_ref[...] = jnp.zeros_like(ys_ref)

    @pl.when(i < nreal)
    def _():
        e = bexp_ref[i]

        @pl.when(i == 0)
        def _():
            for cp in weight_copies(e):
                cp.start(priority=1)

        @pl.when(first_ref[i] == 1)
        def _():
            for cp in weight_copies(e):
                cp.wait()

            def cast_gu(r, carry):
                rows = pl.ds(pl.multiple_of(r * CAST_ROWS, CAST_ROWS), CAST_ROWS)
                wgu_bf[rows, :] = wgu_stage[rows, :].astype(BF16)
                return carry

            def cast_d(r, carry):
                rows = pl.ds(pl.multiple_of(r * CAST_ROWS, CAST_ROWS), CAST_ROWS)
                wd_bf[rows, :] = wd_stage[rows, :].astype(BF16)
                return carry

            lax.fori_loop(0, D_MODEL // CAST_ROWS, cast_gu, 0)
            lax.fori_loop(0, D_FF // CAST_ROWS, cast_d, 0)

            @pl.when(nxt_ref[i] >= 0)
            def _():
                for cp in weight_copies(nxt_ref[i]):
                    cp.start(priority=1)

        def mlp(rows):
            for j in range(ROW_TILES):
                lo, hi = _unpack_words(xs_ref[pl.ds(j, rows, stride=ROW_TILES), :])
                xb_ref[0:rows, j * LANES:(j + 1) * LANES] = lo
                xb_ref[0:rows, D_MODEL // 2 + j * LANES:D_MODEL // 2 + (j + 1) * LANES] = hi
            for c in range(D_FF // FF_CHUNK):
                f0 = c * FF_CHUNK
                xb = xb_ref[0:rows, :]
                gate = _dot(xb, wgu_bf[:, f0:f0 + FF_CHUNK]) + bgu_ref[0, :, f0:f0 + FF_CHUNK]
                up = (_dot(xb, wgu_bf[:, D_FF + f0:D_FF + f0 + FF_CHUNK])
                      + bgu_ref[0, :, D_FF + f0:D_FF + f0 + FF_CHUNK])
                gate = jnp.minimum(gate, SWIGLU_LIMIT)
                up = jnp.clip(up, -SWIGLU_LIMIT, SWIGLU_LIMIT)
                glu = gate * jax.nn.sigmoid(gate * SWIGLU_ALPHA)
                act_ref[0:rows, f0:f0 + FF_CHUNK] = ((up + 1.0) * glu).astype(BF16)
            out = _pack_rows(_dot(act_ref[0:rows, :], wd_bf[...]) + bd_ref[0])
            for j in range(ROW_TILES):
                ys_ref[pl.ds(j, rows, stride=ROW_TILES), :] = out[:, j * LANES:(j + 1) * LANES]

        n_routed = nval_ref[i]
        for rows in range(TAIL_ROWS, bm + 1, TAIL_ROWS):
            @pl.when((n_routed > rows - TAIL_ROWS) & (n_routed <= rows))
            def _(rows=rows):
                mlp(rows)
                if rows < bm:
                    ys_ref[rows * ROW_TILES:bm * ROW_TILES, :] = jnp.zeros(
                        ((bm - rows) * ROW_TILES, LANES), U32)


def _experts_call(bexp, first, nxt, nval, nreal, xs, wgu, bgu, wd, bd):
    bm = MOE_BLOCK
    n_blocks = xs.shape[0] // (bm * ROW_TILES)
    bgu3 = bgu.reshape(N_EXPERTS, 1, 2 * D_FF)
    bd3 = bd.reshape(N_EXPERTS, 1, D_MODEL)
    grid_spec = pltpu.PrefetchScalarGridSpec(
        num_scalar_prefetch=5,
        grid=(n_blocks,),
        in_specs=[
            pl.BlockSpec((bm * ROW_TILES, LANES),
                         lambda i, be, fi, nx, nv, nr: (jnp.minimum(i, nr[0] - 1), 0)),
            pl.BlockSpec(memory_space=pl.ANY),
            pl.BlockSpec((1, 1, 2 * D_FF), lambda i, be, fi, nx, nv, nr: (be[i], 0, 0)),
            pl.BlockSpec(memory_space=pl.ANY),
            pl.BlockSpec((1, 1, D_MODEL), lambda i, be, fi, nx, nv, nr: (be[i], 0, 0)),
        ],
        out_specs=pl.BlockSpec((bm * ROW_TILES, LANES), lambda i, be, fi, nx, nv, nr: (i, 0)),
        scratch_shapes=[pltpu.VMEM((bm, D_MODEL), BF16),
                        pltpu.VMEM((bm, D_FF), BF16),
                        pltpu.VMEM((D_MODEL, 2 * D_FF), F32),
                        pltpu.VMEM((D_FF, D_MODEL), F32),
                        pltpu.VMEM((D_MODEL, 2 * D_FF), BF16),
                        pltpu.VMEM((D_FF, D_MODEL), BF16),
                        pltpu.SemaphoreType.DMA((2,))],
    )
    return pl.pallas_call(
        _experts_kernel,
        grid_spec=grid_spec,
        out_shape=jax.ShapeDtypeStruct(xs.shape, U32),
        compiler_params=pltpu.CompilerParams(
            dimension_semantics=("arbitrary",), vmem_limit_bytes=VMEM_LIMIT),
        name="experts",
    )(bexp, first, nxt, nval, nreal, xs, wgu, bgu3, wd, bd3)


def _combine_kernel(src_ref, dst_ref, len_ref, x1_ref, meta_ref, g_ref, ys_hbm, out_ref,
                    ybuf, ysort_ref, sem):
    i = pl.program_id(0)
    n = pl.num_programs(0)
    tb = x1_ref.shape[0]
    n_rows = TOP_K * tb
    slot = i % 2

    def start_runs(tile, buf_slot):
        def body(e, carry):
            g = tile * N_EXPERTS + e
            ln = pl.multiple_of(len_ref[g], ROW_TILES)

            @pl.when(ln > 0)
            def _():
                src = pl.multiple_of(src_ref[g], ROW_TILES)
                dst = pl.multiple_of(dst_ref[g], ROW_TILES)
                pltpu.make_async_copy(ys_hbm.at[pl.ds(src, ln), :],
                                      ybuf.at[buf_slot, pl.ds(dst, ln), :], sem.at[buf_slot]).start()
            return carry
        lax.fori_loop(0, N_EXPERTS, body, 0)

    @pl.when(i == 0)
    def _():
        start_runs(0, 0)

    @pl.when(i + 1 < n)
    def _():
        start_runs(i + 1, 1 - slot)

    pltpu.make_async_copy(ys_hbm.at[pl.ds(0, n_rows * ROW_TILES), :], ybuf.at[slot],
                          sem.at[slot]).wait()

    for j in range(ROW_TILES):
        lo, hi = _unpack_words(ybuf[slot, pl.ds(j, n_rows, stride=ROW_TILES), :])
        ysort_ref[:, j * LANES:(j + 1) * LANES] = lo
        ysort_ref[:, D_MODEL // 2 + j * LANES:D_MODEL // 2 + (j + 1) * LANES] = hi

    meta = meta_ref[...]
    col = lax.broadcasted_iota(I32, (tb, n_rows), 1).astype(F32)
    weights = jnp.zeros((tb, n_rows), F32)
    for kk in range(TOP_K):
        weights = jnp.where(col == meta[:, 3 * TOP_K + kk:3 * TOP_K + kk + 1],
                            meta[:, TOP_K + kk:TOP_K + kk + 1], weights)
    acc = x1_ref[...] + _dot(weights.astype(BF16), ysort_ref[...])
    out_ref[...] = _rms(acc, g_ref[...])


def _combine_call(seg_src, seg_dst, seg_len, x1, meta, g, ys):
    t, d = x1.shape
    tb = SORT_TILE
    nb = t // tb
    grid_spec = pltpu.PrefetchScalarGridSpec(
        num_scalar_prefetch=3,
        grid=(nb,),
        in_specs=[pl.BlockSpec((tb, d), lambda i, a, b, c: (i, 0)),
                  pl.BlockSpec((tb, LANES), lambda i, a, b, c: (i, 0)),
                  pl.BlockSpec((1, d), lambda i, a, b, c: (0, 0)),
                  pl.BlockSpec(memory_space=pl.ANY)],
        out_specs=pl.BlockSpec((tb, d), lambda i, a, b, c: (i, 0)),
        scratch_shapes=[pltpu.VMEM((2, TOP_K * tb * ROW_TILES, LANES), U32),
                        pltpu.VMEM((TOP_K * tb, D_MODEL), BF16),
                        pltpu.SemaphoreType.DMA((2,))],
    )
    return pl.pallas_call(
        _combine_kernel,
        grid_spec=grid_spec,
        out_shape=jax.ShapeDtypeStruct((t, d), F32),
        compiler_params=pltpu.CompilerParams(
            dimension_semantics=("arbitrary",), vmem_limit_bytes=VMEM_LIMIT),
        name="combine",
    )(seg_src, seg_dst, seg_len, x1, meta, g, ys)


def _routing_tables(counts, t):
    bm = MOE_BLOCK
    n_blocks = t * TOP_K // bm + N_EXPERTS
    eids = jnp.arange(N_EXPERTS, dtype=I32)
    nblk_e = (counts + bm - 1) // bm
    blk_end = jnp.sum(jnp.where(eids[None, :] <= eids[:, None], nblk_e[None, :], 0), axis=1)
    blk_start = blk_end - nblk_e
    nreal = blk_end[N_EXPERTS - 1]
    pad_start = blk_start * bm
    blk = jnp.arange(n_blocks, dtype=I32)
    bexp = jnp.minimum(jnp.sum((blk_end[None, :] <= blk[:, None]).astype(I32), axis=1),
                       N_EXPERTS - 1)
    blk_is_e = bexp[:, None] == eids[None, :]
    pick = lambda tab: jnp.sum(jnp.where(blk_is_e, tab[None, :], 0), axis=1)
    first = (blk == pick(blk_start)).astype(I32)
    nxt_e = jnp.sum((blk_end[None, :] <= blk_end[:, None]).astype(I32), axis=1)
    nxt_e = jnp.where(blk_end < nreal, jnp.minimum(nxt_e, N_EXPERTS - 1), -1)
    nxt = pick(nxt_e)
    nval = jnp.clip(pick(counts) - (blk - pick(blk_start)) * bm, 0, bm)
    tail_blk = jnp.arange(N_EXPERTS, dtype=I32) + nreal
    pad_lo = jnp.concatenate([pad_start + counts, jnp.minimum(tail_blk, n_blocks - 1) * bm])
    pad_n = jnp.concatenate([nblk_e * bm - counts, jnp.where(tail_blk < n_blocks, bm, 0)])
    return (pad_start, bexp, first, nxt, nval, nreal.reshape(1).astype(I32), n_blocks,
            pad_lo * ROW_TILES, pad_n * ROW_TILES)


def kernel(x, norm_mix_g, w_in, conv_w, w_gk_up, b_gk_up, gla_norm_g, w_out, norm_ffn_g,
           w_router, b_router, w_gate_up, b_gate_up, w_down, b_down, norm_final_g):
    bsz, seq, d = x.shape
    t = bsz * seq
    assert w_in.shape[0] == 1, "single-layer trunk only"
    assert conv_w.shape[1] == CONV_K and ROUTE_ROWS % SUBLANES == 0
    l = 0
    d_in = w_in.shape[-1]
    win = jnp.pad(w_in[l].astype(BF16), ((0, 0), (0, D_IN_PAD - d_in)))
    wgk = jnp.pad(w_gk_up[l], ((0, LANES - GLA_RANK), (0, 0))).astype(BF16)
    wr = jnp.pad(w_router[l], ((0, 0), (0, LANES - N_EXPERTS)))
    wrh = wr.astype(BF16)
    wrhl = jnp.concatenate([wrh, (wr - wrh.astype(F32)).astype(BF16)], axis=1)
    br = jnp.pad(b_router[l], (0, LANES - N_EXPERTS), constant_values=NEG_BIG).reshape(1, LANES)

    x1, h2, meta, route, seg, cnt = _mixer_call(
        x, norm_mix_g[l].reshape(1, d), win, conv_w[l], wgk, b_gk_up[l].reshape(1, GLA_QK),
        gla_norm_g[l].reshape(1, GLA_DV), w_out[l].astype(BF16), norm_ffn_g[l].reshape(1, d),
        wrhl, br)

    meta2 = meta.reshape(t, LANES)
    counts = cnt[:, 0].astype(I32)
    (pad_start, bexp, first, nxt, nval, nreal, n_blocks, pad_lo,
     pad_n) = _routing_tables(counts, t)
    seg_src = ((pad_start[None, :] + seg[:, :, 0].astype(I32)) * ROW_TILES).reshape(-1)
    seg_len = (seg[:, :, 1].astype(I32) * ROW_TILES).reshape(-1)
    seg_dst = (seg[:, :, 2].astype(I32) * ROW_TILES).reshape(-1)
    xs = _dispatch_call(seg_src, seg_dst, seg_len, pad_lo, pad_n, route, h2.reshape(t, d),
                        n_blocks * MOE_BLOCK)
    ys = _experts_call(bexp, first, nxt, nval, nreal, xs, w_gate_up[l], b_gate_up[l], w_down[l],
                       b_down[l])
    out = _combine_call(seg_src, seg_dst, seg_len, x1.reshape(t, d), meta2,
                        norm_final_g.reshape(1, d), ys)
    return out.reshape(bsz, seq, d)
```

```python
import jax
import jax.numpy as jnp
from jax import lax
from jax.experimental import pallas as pl
from jax.experimental.pallas import tpu as pltpu

F32 = jnp.float32
BF16 = jnp.bfloat16
I32 = jnp.int32
U32 = jnp.uint32

D_MODEL = 1024
CONV_WIDTH = 512
CONV_K = 3
GLA_WIDTH = 512
GLA_HEADS = 4
GLA_DV = 128
GLA_DK = 64
GLA_QK = GLA_HEADS * GLA_DK
GLA_RANK = 16
GLA_NORMALIZER = 16.0
GLA_CHUNK = 64
N_EXPERTS = 32
TOP_K = 4
D_FF = 1024
SWIGLU_LIMIT = 7.0
SWIGLU_ALPHA = 1.702
RMS_EPS = 1e-5

LANES = 128
SUBLANES = 8
ROW_TILES = D_MODEL // LANES // 2
ROUTE_ROWS = 4 * TOP_K

OFF_UH = 0
OFF_GB = OFF_UH + CONV_WIDTH
OFF_GC = OFF_GB + CONV_WIDTH
OFF_Q = OFF_GC + CONV_WIDTH
OFF_K = OFF_Q + GLA_QK
OFF_V = OFF_K + GLA_QK
OFF_GO = OFF_V + GLA_WIDTH
OFF_GKL = OFF_GO + GLA_WIDTH
D_IN_PAD = OFF_GKL + LANES

SEQ_TILE = 512
SORT_TILE = 256
STEP_TILES = 4
MOE_BLOCK = 512
FF_CHUNK = 256
TAIL_ROWS = 128
CAST_ROWS = 128
NEG_BIG = -1e30
VMEM_LIMIT = 56 * 1024 * 1024


def _rms(x, g):
    return x * lax.rsqrt(jnp.mean(x * x, axis=-1, keepdims=True) + RMS_EPS) * g


def _dot(a, b):
    return jnp.dot(a, b, preferred_element_type=F32)


def _dot_nt(a, b):
    return lax.dot_general(a, b, (((1,), (1,)), ((), ())), preferred_element_type=F32)


def _pack_rows(x):
    half = x.shape[1] // 2
    xr = x.astype(BF16).astype(F32)
    lo = lax.bitcast_convert_type(xr[:, :half], U32) >> 16
    hi = lax.bitcast_convert_type(xr[:, half:], U32) & jnp.uint32(0xFFFF0000)
    return hi | lo


def _unpack_words(w):
    lo = lax.bitcast_convert_type(w << 16, F32).astype(BF16)
    hi = lax.bitcast_convert_type(w & jnp.uint32(0xFFFF0000), F32).astype(BF16)
    return lo, hi


def _split_bf16(x):
    hi = x.astype(BF16)
    lo = (x - hi.astype(F32)).astype(BF16)
    return hi, lo


def _mixer_kernel(x_ref, g1_ref, win_ref, convw_ref, wgk_ref, bgk_ref, gng_ref, wout_ref,
                  g2_ref, wrhl_ref, br_ref,
                  x1_ref, h2_ref, meta_ref, route_ref, seg_ref, cnt_ref,
                  proj_ref, ubuf_ref, la_ref, ycat_ref, state_ref, carry_ref):
    ts = x_ref.shape[1]
    b_idx = pl.program_id(0)
    s_idx = pl.program_id(1)

    @pl.when(s_idx == 0)
    def _():
        state_ref[...] = jnp.zeros_like(state_ref)
        ubuf_ref[0:SUBLANES, :] = jnp.zeros((SUBLANES, CONV_WIDTH), F32)

    @pl.when((s_idx == 0) & (b_idx == 0))
    def _():
        carry_ref[...] = jnp.zeros_like(carry_ref)

    x = x_ref[0]
    h = _rms(x, g1_ref[...]).astype(BF16)
    proj_ref[...] = _dot(h, win_ref[...])

    u = proj_ref[:, OFF_GC:OFF_GC + CONV_WIDTH] * proj_ref[:, OFF_UH:OFF_UH + CONV_WIDTH]
    ubuf_ref[SUBLANES:SUBLANES + ts, :] = u
    u1 = ubuf_ref[pl.ds(SUBLANES - 1, ts), :]
    u2 = ubuf_ref[pl.ds(SUBLANES - 2, ts), :]
    conv = convw_ref[0:1, :] * u2 + convw_ref[1:2, :] * u1 + convw_ref[2:3, :] * u
    ycat_ref[:, 0:CONV_WIDTH] = (proj_ref[:, OFF_GB:OFF_GB + CONV_WIDTH] * conv).astype(BF16)
    ubuf_ref[0:SUBLANES, :] = ubuf_ref[ts:ts + SUBLANES, :]

    gk = _dot(proj_ref[:, OFF_GKL:OFF_GKL + LANES].astype(BF16), wgk_ref[...]) + bgk_ref[...]
    log_sig = jnp.minimum(gk, 0.0) - jnp.log1p(jnp.exp(-jnp.abs(gk)))
    la_ref[...] = log_sig / GLA_NORMALIZER

    ci = lax.broadcasted_iota(I32, (GLA_CHUNK, GLA_CHUNK), 0)
    cj = lax.broadcasted_iota(I32, (GLA_CHUNK, GLA_CHUNK), 1)
    tri_incl = (cj <= ci).astype(BF16)
    causal = cj <= ci
    causal4 = jnp.concatenate([causal] * GLA_HEADS, axis=0)
    lane_qk = lax.broadcasted_iota(I32, (1, GLA_QK), 1)
    head_masks = [((lane_qk >= hd * GLA_DK) & (lane_qk < (hd + 1) * GLA_DK)).astype(F32)
                  for hd in range(GLA_HEADS)]
    gng = gng_ref[...]

    n_chunks = ts // GLA_CHUNK
    chunk_rows = [pl.ds(c * GLA_CHUNK, GLA_CHUNK) for c in range(n_chunks)]
    lane_c = lax.broadcasted_iota(I32, (GLA_QK, 2 * GLA_CHUNK), 1)
    qd_all, kd_all, kr_all, bl_all, v_all = [], [], [], [], []
    for rows in chunk_rows:
        la_hi, la_lo = _split_bf16(la_ref[rows, :])
        bcum = _dot(tri_incl, la_hi) + _dot(tri_incl, la_lo)
        blast = bcum[GLA_CHUNK - 1:GLA_CHUNK, :]
        q = proj_ref[rows, OFF_Q:OFF_Q + GLA_QK] * (GLA_DK ** -0.5)
        k = proj_ref[rows, OFF_K:OFF_K + GLA_QK]
        qd_all.append(q * jnp.exp(bcum))
        kd_all.append((k * jnp.exp(-bcum)).astype(BF16))
        kr_all.append(k * jnp.exp(blast - bcum))
        bl_all.append(blast)
        v_all.append(proj_ref[rows, OFF_V:OFF_V + GLA_WIDTH].astype(BF16))

    scores_all = []
    for c in range(n_chunks):
        q_stack = jnp.concatenate([qd_all[c] * m for m in head_masks], axis=0).astype(BF16)
        scores_all.append(
            jnp.where(causal4, _dot_nt(q_stack, kd_all[c]), 0.0).astype(BF16))
    o_intra_all = []
    for c in range(n_chunks):
        o_intra_all.append(jnp.concatenate(
            [_dot(scores_all[c][hd * GLA_CHUNK:(hd + 1) * GLA_CHUNK, :],
                  v_all[c][:, hd * GLA_DV:(hd + 1) * GLA_DV]) for hd in range(GLA_HEADS)], axis=1))
    kv_all, dcol_all = [], []
    for c in range(n_chunks):
        kt = jnp.concatenate(
            [kr_all[c], jnp.broadcast_to(bl_all[c], (GLA_CHUNK, GLA_QK))], axis=0).T
        dcol_all.append(jnp.exp(kt[:, GLA_CHUNK:GLA_CHUNK + 1]))
        kt_b = jnp.where(lane_c < GLA_CHUNK, kt, 0.0).astype(BF16)
        v_pad = jnp.concatenate([v_all[c], jnp.zeros_like(v_all[c])], axis=0)
        kv_all.append([_dot(kt_b[hd * GLA_DK:(hd + 1) * GLA_DK, :],
                            v_pad[:, hd * GLA_DV:(hd + 1) * GLA_DV]) for hd in range(GLA_HEADS)])

    o_all = []
    for c in range(n_chunks):
        state = state_ref[...]
        o_all.append(_dot(qd_all[c].astype(BF16), state.astype(BF16)) + o_intra_all[c])
        for hd in range(GLA_HEADS):
            rs = slice(hd * GLA_DK, (hd + 1) * GLA_DK)
            cs = slice(hd * GLA_DV, (hd + 1) * GLA_DV)
            state_ref[rs, cs] = dcol_all[c][rs, :] * state[rs, cs] + kv_all[c][hd]

    for c, rows in enumerate(chunk_rows):
        o = o_all[c]
        g_out = proj_ref[rows, OFF_GO:OFF_GO + GLA_WIDTH]
        o_n = jnp.concatenate(
            [_rms(o[:, hd * GLA_DV:(hd + 1) * GLA_DV], gng) for hd in range(GLA_HEADS)], axis=1)
        y = o_n * (g_out * jax.nn.sigmoid(g_out))
        ycat_ref[rows, CONV_WIDTH:CONV_WIDTH + GLA_WIDTH] = y.astype(BF16)

    x1 = x + _dot(ycat_ref[...], wout_ref[...])
    x1_ref[0] = x1
    h2 = _rms(x1, g2_ref[...])
    h2_hi, h2_lo = _split_bf16(h2)
    h2_ref[0] = h2_hi
    hi_terms = _dot(h2_hi, wrhl_ref[...])
    logits = (hi_terms[:, 0:LANES] + hi_terms[:, LANES:2 * LANES]
              + _dot(h2_lo, wrhl_ref[:, 0:LANES]) + br_ref[...])

    lt = logits.T[0:N_EXPERTS, :]
    erow = lax.broadcasted_iota(I32, (N_EXPERTS, ts), 0).astype(F32)
    work = lt
    sel = jnp.zeros((N_EXPERTS, ts), F32)
    top_v, top_i, top_oh = [], [], []
    for _ in range(TOP_K):
        m = jnp.max(work, axis=0, keepdims=True)
        idx = jnp.min(jnp.where(work == m, erow, float(N_EXPERTS)), axis=0, keepdims=True)
        oh = erow == idx
        top_v.append(m)
        top_i.append(idx)
        top_oh.append(oh)
        sel = sel + oh.astype(F32)
        work = jnp.where(oh, -jnp.inf, work)
    exps = [jnp.exp(tv - top_v[0]) for tv in top_v]
    denom = exps[0] + exps[1] + exps[2] + exps[3]
    gates = [e / denom for e in exps]

    tsrc = lax.broadcasted_iota(I32, (ts, ts), 0)
    tdst = lax.broadcasted_iota(I32, (ts, ts), 1)
    earlier = (tsrc < tdst).astype(BF16)
    local = _dot(sel.astype(BF16), earlier)
    carry = carry_ref[:, 0:1]
    ranks = [jnp.sum(jnp.where(oh, local + carry, 0.0), axis=0, keepdims=True) for oh in top_oh]

    ei = lax.broadcasted_iota(I32, (N_EXPERTS, N_EXPERTS), 0)
    ej = lax.broadcasted_iota(I32, (N_EXPERTS, N_EXPERTS), 1)
    lower_experts = (ej < ei).astype(BF16)
    lane_f = lax.broadcasted_iota(I32, (N_EXPERTS, LANES), 1)
    slot_base = []
    before = jnp.zeros((N_EXPERTS, 1), F32)
    for u in range(ts // SORT_TILE):
        sub_cnt = jnp.sum(sel[:, u * SORT_TILE:(u + 1) * SORT_TILE], axis=1, keepdims=True)
        seg_start = _dot(lower_experts,
                         jnp.broadcast_to(sub_cnt, (N_EXPERTS, LANES)).astype(BF16))[:, 0:1]
        seg_ref[u] = jnp.where(lane_f == 0, carry + before,
                               jnp.where(lane_f == 1, sub_cnt,
                                         jnp.where(lane_f == 2, seg_start, 0.0)))
        slot_base.append(jnp.broadcast_to(seg_start - before, (N_EXPERTS, SORT_TILE)))
        before = before + sub_cnt
    slot_base = jnp.concatenate(slot_base, axis=1)
    slots = [jnp.sum(jnp.where(oh, local + slot_base, 0.0), axis=0, keepdims=True)
             for oh in top_oh]
    new_carry = carry + before
    carry_ref[...] = jnp.broadcast_to(new_carry, carry_ref.shape)
    cnt_ref[...] = jnp.broadcast_to(new_carry, cnt_ref.shape)

    route = jnp.concatenate(top_i + gates + ranks + slots, axis=0)
    route_ref[...] = route
    meta_ref[0] = jnp.concatenate(
        [route, jnp.zeros((LANES - ROUTE_ROWS, ts), F32)], axis=0).T


def _mixer_call(x, g1, win, convw, wgk, bgk, gng, wout, g2, wrhl, br):
    bsz, seq, d = x.shape
    ts = SEQ_TILE
    grid = (bsz, seq // ts)

    def const(shape):
        return pl.BlockSpec(shape, lambda b, s: (0,) * len(shape))

    tile = lambda w: pl.BlockSpec((1, ts, w), lambda b, s: (b, s, 0))
    return pl.pallas_call(
        _mixer_kernel,
        grid=grid,
        in_specs=[tile(d), const(g1.shape), const(win.shape), const(convw.shape),
                  const(wgk.shape), const(bgk.shape), const(gng.shape), const(wout.shape),
                  const(g2.shape), const(wrhl.shape), const(br.shape)],
        out_specs=[tile(d),
                   tile(d),
                   tile(LANES),
                   pl.BlockSpec((ROUTE_ROWS, ts), lambda b, s: (0, b * (seq // ts) + s)),
                   pl.BlockSpec((ts // SORT_TILE, N_EXPERTS, LANES),
                                lambda b, s: (b * (seq // ts) + s, 0, 0)),
                   const((N_EXPERTS, LANES))],
        out_shape=[jax.ShapeDtypeStruct((bsz, seq, d), F32),
                   jax.ShapeDtypeStruct((bsz, seq, d), BF16),
                   jax.ShapeDtypeStruct((bsz, seq, LANES), F32),
                   jax.ShapeDtypeStruct((ROUTE_ROWS, bsz * seq), F32),
                   jax.ShapeDtypeStruct((bsz * seq // SORT_TILE, N_EXPERTS, LANES), F32),
                   jax.ShapeDtypeStruct((N_EXPERTS, LANES), F32)],
        scratch_shapes=[pltpu.VMEM((ts, D_IN_PAD), F32),
                        pltpu.VMEM((ts + SUBLANES, CONV_WIDTH), F32),
                        pltpu.VMEM((ts, GLA_QK), F32),
                        pltpu.VMEM((ts, D_MODEL), BF16),
                        pltpu.VMEM((GLA_QK, GLA_WIDTH), F32),
                        pltpu.VMEM((N_EXPERTS, LANES), F32)],
        compiler_params=pltpu.CompilerParams(
            dimension_semantics=("arbitrary", "arbitrary"), vmem_limit_bytes=VMEM_LIMIT),
        name="mixer",
    )(x, g1, win, convw, wgk, bgk, gng, wout, g2, wrhl, br)


def _dispatch_kernel(run_xs_ref, run_buf_ref, run_len_ref, pad_lo_ref, pad_n_ref,
                     route_ref, h2_ref, xs_hbm, sbuf, zbuf, sem, zsem):
    i = pl.program_id(0)
    n = pl.num_programs(0)
    td = SORT_TILE
    n_rows = TOP_K * td

    def zero_fill(wait):
        def fill(g, carry):
            ln = pl.multiple_of(pad_n_ref[g], ROW_TILES)

            @pl.when(ln > 0)
            def _():
                lo = pl.multiple_of(pad_lo_ref[g], ROW_TILES)
                cp = pltpu.make_async_copy(zbuf.at[pl.ds(0, ln), :], xs_hbm.at[pl.ds(lo, ln), :],
                                           zsem.at[0])
                if wait:
                    cp.wait()
                else:
                    cp.start()
            return carry

        lax.fori_loop(0, pad_lo_ref.shape[0], fill, 0)

    def wait_runs(buf_slot):
        pltpu.make_async_copy(sbuf.at[buf_slot], xs_hbm.at[pl.ds(0, n_rows * ROW_TILES), :],
                              sem.at[buf_slot]).wait()

    @pl.when(i == 0)
    def _():
        zbuf[...] = jnp.zeros_like(zbuf)
        zero_fill(wait=False)

    for u in range(STEP_TILES):
        @pl.when(i >= 1)
        def _(u=u):
            wait_runs(u)

        route = route_ref[:, u * td:(u + 1) * td]
        row = lax.broadcasted_iota(I32, (n_rows, td), 0).astype(F32)
        pick = jnp.zeros((n_rows, td), F32)
        for kk in range(TOP_K):
            pick = jnp.where(row == route[3 * TOP_K + kk:3 * TOP_K + kk + 1, :], 1.0, pick)
        rows_sorted = _dot(pick.astype(BF16), h2_ref[u * td:(u + 1) * td, :])
        words = _pack_rows(rows_sorted)
        for j in range(ROW_TILES):
            sbuf[u, pl.ds(j, n_rows, stride=ROW_TILES), :] = words[:, j * LANES:(j + 1) * LANES]

        def body(e, carry, u=u):
            g = (i * STEP_TILES + u) * N_EXPERTS + e
            ln = pl.multiple_of(run_len_ref[g], ROW_TILES)

            @pl.when(ln > 0)
            def _():
                src = pl.multiple_of(run_buf_ref[g], ROW_TILES)
                dst = pl.multiple_of(run_xs_ref[g], ROW_TILES)
                pltpu.make_async_copy(sbuf.at[u, pl.ds(src, ln), :], xs_hbm.at[pl.ds(dst, ln), :],
                                      sem.at[u]).start()
            return carry

        lax.fori_loop(0, N_EXPERTS, body, 0)

    @pl.when(i == n - 1)
    def _():
        for u in range(STEP_TILES):
            wait_runs(u)
        zero_fill(wait=True)


def _dispatch_call(run_xs, run_buf, run_len, pad_lo, pad_n, route, h2, n_rows):
    t, d = h2.shape
    td = SORT_TILE * STEP_TILES
    grid_spec = pltpu.PrefetchScalarGridSpec(
        num_scalar_prefetch=5,
        grid=(t // td,),
        in_specs=[pl.BlockSpec((ROUTE_ROWS, td), lambda i, *_: (0, i)),
                  pl.BlockSpec((td, d), lambda i, *_: (i, 0))],
        out_specs=pl.BlockSpec(memory_space=pl.ANY),
        scratch_shapes=[pltpu.VMEM((STEP_TILES, TOP_K * SORT_TILE * ROW_TILES, LANES), U32),
                        pltpu.VMEM((MOE_BLOCK * ROW_TILES, LANES), U32),
                        pltpu.SemaphoreType.DMA((STEP_TILES,)),
                        pltpu.SemaphoreType.DMA((1,))],
    )
    return pl.pallas_call(
        _dispatch_kernel,
        grid_spec=grid_spec,
        out_shape=jax.ShapeDtypeStruct((n_rows * ROW_TILES, LANES), U32),
        compiler_params=pltpu.CompilerParams(
            dimension_semantics=("arbitrary",), vmem_limit_bytes=VMEM_LIMIT,
            has_side_effects=True),
        name="dispatch",
    )(run_xs, run_buf, run_len, pad_lo, pad_n, route, h2)


def _experts_kernel(bexp_ref, first_ref, nxt_ref, nval_ref, nreal_ref,
                    xs_ref, wgu_hbm, bgu_ref, wd_hbm, bd_ref,
                    ys_ref,
                    xb_ref, act_ref, wgu_stage, wd_stage, wgu_bf, wd_bf, wsem):
    i = pl.program_id(0)
    nreal = nreal_ref[0]
    bm = xb_ref.shape[0]

    def weight_copies(e):
        return (pltpu.make_async_copy(wgu_hbm.at[e], wgu_stage, wsem.at[0]),
                pltpu.make_async_copy(wd_hbm.at[e], wd_stage, wsem.at[1]))

    @pl.when(i >= nreal)
    def _():
        ys_ref[...] = jnp.zeros_like(ys_ref)

    @pl.when(i < nreal)
    def _():
        e = bexp_ref[i]

        @pl.when(i == 0)
        def _():
            for cp in weight_copies(e):
                cp.start(priority=1)

        @pl.when(first_ref[i] == 1)
        def _():
            for cp in weight_copies(e):
                cp.wait()

            def cast_gu(r, carry):
                rows = pl.ds(pl.multiple_of(r * CAST_ROWS, CAST_ROWS), CAST_ROWS)
                wgu_bf[rows, :] = wgu_stage[rows, :].astype(BF16)
                return carry

            def cast_d(r, carry):
                rows = pl.ds(pl.multiple_of(r * CAST_ROWS, CAST_ROWS), CAST_ROWS)
                wd_bf[rows, :] = wd_stage[rows, :].astype(BF16)
                return carry

            lax.fori_loop(0, D_MODEL // CAST_ROWS, cast_gu, 0)
            lax.fori_loop(0, D_FF // CAST_ROWS, cast_d, 0)

            @pl.when(nxt_ref[i] >= 0)
            def _():
                for cp in weight_copies(nxt_ref[i]):
                    cp.start(priority=1)

        def mlp(rows):
            for j in range(ROW_TILES):
                lo, hi = _unpack_words(xs_ref[pl.ds(j, rows, stride=ROW_TILES), :])
                xb_ref[0:rows, j * LANES:(j + 1) * LANES] = lo
                xb_ref[0:rows, D_MODEL // 2 + j * LANES:D_MODEL // 2 + (j + 1) * LANES] = hi
            for c in range(D_FF // FF_CHUNK):
                f0 = c * FF_CHUNK
                xb = xb_ref[0:rows, :]
                gate = _dot(xb, wgu_bf[:, f0:f0 + FF_CHUNK]) + bgu_ref[0, :, f0:f0 + FF_CHUNK]
                up = (_dot(xb, wgu_bf[:, D_FF + f0:D_FF + f0 + FF_CHUNK])
                      + bgu_ref[0, :, D_FF + f0:D_FF + f0 + FF_CHUNK])
                gate = jnp.minimum(gate, SWIGLU_LIMIT)
                up = jnp.clip(up, -SWIGLU_LIMIT, SWIGLU_LIMIT)
                glu = gate * jax.nn.sigmoid(gate * SWIGLU_ALPHA)
                act_ref[0:rows, f0:f0 + FF_CHUNK] = ((up + 1.0) * glu).astype(BF16)
            out = _pack_rows(_dot(act_ref[0:rows, :], wd_bf[...]) + bd_ref[0])
            for j in range(ROW_TILES):
                ys_ref[pl.ds(j, rows, stride=ROW_TILES), :] = out[:, j * LANES:(j + 1) * LANES]

        n_routed = nval_ref[i]
        for rows in range(TAIL_ROWS, bm + 1, TAIL_ROWS):
            @pl.when((n_routed > rows - TAIL_ROWS) & (n_routed <= rows))
            def _(rows=rows):
                mlp(rows)
                if rows < bm:
                    ys_ref[rows * ROW_TILES:bm * ROW_TILES, :] = jnp.zeros(
                        ((bm - rows) * ROW_TILES, LANES), U32)


def _experts_call(bexp, first, nxt, nval, nreal, xs, wgu, bgu, wd, bd):
    bm = MOE_BLOCK
    n_blocks = xs.shape[0] // (bm * ROW_TILES)
    bgu3 = bgu.reshape(N_EXPERTS, 1, 2 * D_FF)
    bd3 = bd.reshape(N_EXPERTS, 1, D_MODEL)
    grid_spec = pltpu.PrefetchScalarGridSpec(
        num_scalar_prefetch=5,
        grid=(n_blocks,),
        in_specs=[
            pl.BlockSpec((bm * ROW_TILES, LANES),
                         lambda i, be, fi, nx, nv, nr: (jnp.minimum(i, nr[0] - 1), 0)),
            pl.BlockSpec(memory_space=pl.ANY),
            pl.BlockSpec((1, 1, 2 * D_FF), lambda i, be, fi, nx, nv, nr: (be[i], 0, 0)),
            pl.BlockSpec(memory_space=pl.ANY),
            pl.BlockSpec((1, 1, D_MODEL), lambda i, be, fi, nx, nv, nr: (be[i], 0, 0)),
        ],
        out_specs=pl.BlockSpec((bm * ROW_TILES, LANES), lambda i, be, fi, nx, nv, nr: (i, 0)),
        scratch_shapes=[pltpu.VMEM((bm, D_MODEL), BF16),
                        pltpu.VMEM((bm, D_FF), BF16),
                        pltpu.VMEM((D_MODEL, 2 * D_FF), F32),
                        pltpu.VMEM((D_FF, D_MODEL), F32),
                        pltpu.VMEM((D_MODEL, 2 * D_FF), BF16),
                        pltpu.VMEM((D_FF, D_MODEL), BF16),
                        pltpu.SemaphoreType.DMA((2,))],
    )
    return pl.pallas_call(
        _experts_kernel,
        grid_spec=grid_spec,
        out_shape=jax.ShapeDtypeStruct(xs.shape, U32),
        compiler_params=pltpu.CompilerParams(
            dimension_semantics=("arbitrary",), vmem_limit_bytes=VMEM_LIMIT),
        name="experts",
    )(bexp, first, nxt, nval, nreal, xs, wgu, bgu3, wd, bd3)


def _combine_kernel(src_ref, dst_ref, len_ref, x1_ref, meta_ref, g_ref, ys_hbm, out_ref,
                    ybuf, ysort_ref, sem):
    i = pl.program_id(0)
    n = pl.num_programs(0)
    tb = x1_ref.shape[0]
    n_rows = TOP_K * tb
    slot = i % 2

    def start_runs(tile, buf_slot):
        def body(e, carry):
            g = tile * N_EXPERTS + e
            ln = pl.multiple_of(len_ref[g], ROW_TILES)

            @pl.when(ln > 0)
            def _():
                src = pl.multiple_of(src_ref[g], ROW_TILES)
                dst = pl.multiple_of(dst_ref[g], ROW_TILES)
                pltpu.make_async_copy(ys_hbm.at[pl.ds(src, ln), :],
                                      ybuf.at[buf_slot, pl.ds(dst, ln), :], sem.at[buf_slot]).start()
            return carry
        lax.fori_loop(0, N_EXPERTS, body, 0)

    @pl.when(i == 0)
    def _():
        start_runs(0, 0)

    @pl.when(i + 1 < n)
    def _():
        start_runs(i + 1, 1 - slot)

    pltpu.make_async_copy(ys_hbm.at[pl.ds(0, n_rows * ROW_TILES), :], ybuf.at[slot],
                          sem.at[slot]).wait()

    for j in range(ROW_TILES):
        lo, hi = _unpack_words(ybuf[slot, pl.ds(j, n_rows, stride=ROW_TILES), :])
        ysort_ref[:, j * LANES:(j + 1) * LANES] = lo
        ysort_ref[:, D_MODEL // 2 + j * LANES:D_MODEL // 2 + (j + 1) * LANES] = hi

    meta = meta_ref[...]
    col = lax.broadcasted_iota(I32, (tb, n_rows), 1).astype(F32)
    weights = jnp.zeros((tb, n_rows), F32)
    for kk in range(TOP_K):
        weights = jnp.where(col == meta[:, 3 * TOP_K + kk:3 * TOP_K + kk + 1],
                            meta[:, TOP_K + kk:TOP_K + kk + 1], weights)
    acc = x1_ref[...] + _dot(weights.astype(BF16), ysort_ref[...])
    out_ref[...] = _rms(acc, g_ref[...])


def _combine_call(seg_src, seg_dst, seg_len, x1, meta, g, ys):
    t, d = x1.shape
    tb = SORT_TILE
    nb = t // tb
    grid_spec = pltpu.PrefetchScalarGridSpec(
        num_scalar_prefetch=3,
        grid=(nb,),
        in_specs=[pl.BlockSpec((tb, d), lambda i, a, b, c: (i, 0)),
                  pl.BlockSpec((tb, LANES), lambda i, a, b, c: (i, 0)),
                  pl.BlockSpec((1, d), lambda i, a, b, c: (0, 0)),
                  pl.BlockSpec(memory_space=pl.ANY)],
        out_specs=pl.BlockSpec((tb, d), lambda i, a, b, c: (i, 0)),
        scratch_shapes=[pltpu.VMEM((2, TOP_K * tb * ROW_TILES, LANES), U32),
                        pltpu.VMEM((TOP_K * tb, D_MODEL), BF16),
                        pltpu.SemaphoreType.DMA((2,))],
    )
    return pl.pallas_call(
        _combine_kernel,
        grid_spec=grid_spec,
        out_shape=jax.ShapeDtypeStruct((t, d), F32),
        compiler_params=pltpu.CompilerParams(
            dimension_semantics=("arbitrary",), vmem_limit_bytes=VMEM_LIMIT),
        name="combine",
    )(seg_src, seg_dst, seg_len, x1, meta, g, ys)


def _routing_tables(counts, t):
    bm = MOE_BLOCK
    n_blocks = t * TOP_K // bm + N_EXPERTS
    eids = jnp.arange(N_EXPERTS, dtype=I32)
    nblk_e = (counts + bm - 1) // bm
    blk_end = jnp.sum(jnp.where(eids[None, :] <= eids[:, None], nblk_e[None, :], 0), axis=1)
    blk_start = blk_end - nblk_e
    nreal = blk_end[N_EXPERTS - 1]
    pad_start = blk_start * bm
    blk = jnp.arange(n_blocks, dtype=I32)
    bexp = jnp.minimum(jnp.sum((blk_end[None, :] <= blk[:, None]).astype(I32), axis=1),
                       N_EXPERTS - 1)
    blk_is_e = bexp[:, None] == eids[None, :]
    pick = lambda tab: jnp.sum(jnp.where(blk_is_e, tab[None, :], 0), axis=1)
    first = (blk == pick(blk_start)).astype(I32)
    nxt_e = jnp.sum((blk_end[None, :] <= blk_end[:, None]).astype(I32), axis=1)
    nxt_e = jnp.where(blk_end < nreal, jnp.minimum(nxt_e, N_EXPERTS - 1), -1)
    nxt = pick(nxt_e)
    nval = jnp.clip(pick(counts) - (blk - pick(blk_start)) * bm, 0, bm)
    tail_blk = jnp.arange(N_EXPERTS, dtype=I32) + nreal
    pad_lo = jnp.concatenate([pad_start + counts, jnp.minimum(tail_blk, n_blocks - 1) * bm])
    pad_n = jnp.concatenate([nblk_e * bm - counts, jnp.where(tail_blk < n_blocks, bm, 0)])
    return (pad_start, bexp, first, nxt, nval, nreal.reshape(1).astype(I32), n_blocks,
            pad_lo * ROW_TILES, pad_n * ROW_TILES)


def kernel(x, norm_mix_g, w_in, conv_w, w_gk_up, b_gk_up, gla_norm_g, w_out, norm_ffn_g,
           w_router, b_router, w_gate_up, b_gate_up, w_down, b_down, norm_final_g):
    bsz, seq, d = x.shape
    t = bsz * seq
    assert w_in.shape[0] == 1, "single-layer trunk only"
    assert conv_w.shape[1] == CONV_K and ROUTE_ROWS % SUBLANES == 0
    l = 0
    d_in = w_in.shape[-1]
    win = jnp.pad(w_in[l].astype(BF16), ((0, 0), (0, D_IN_PAD - d_in)))
    wgk = jnp.pad(w_gk_up[l], ((0, LANES - GLA_RANK), (0, 0))).astype(BF16)
    wr = jnp.pad(w_router[l], ((0, 0), (0, LANES - N_EXPERTS)))
    wrh = wr.astype(BF16)
    wrhl = jnp.concatenate([wrh, (wr - wrh.astype(F32)).astype(BF16)], axis=1)
    br = jnp.pad(b_router[l], (0, LANES - N_EXPERTS), constant_values=NEG_BIG).reshape(1, LANES)

    x1, h2, meta, route, seg, cnt = _mixer_call(
        x, norm_mix_g[l].reshape(1, d), win, conv_w[l], wgk, b_gk_up[l].reshape(1, GLA_QK),
        gla_norm_g[l].reshape(1, GLA_DV), w_out[l].astype(BF16), norm_ffn_g[l].reshape(1, d),
        wrhl, br)

    meta2 = meta.reshape(t, LANES)
    counts = cnt[:, 0].astype(I32)
    (pad_start, bexp, first, nxt, nval, nreal, n_blocks, pad_lo,
     pad_n) = _routing_tables(counts, t)
    seg_src = ((pad_start[None, :] + seg[:, :, 0].astype(I32)) * ROW_TILES).reshape(-1)
    seg_len = (seg[:, :, 1].astype(I32) * ROW_TILES).reshape(-1)
    seg_dst = (seg[:, :, 2].astype(I32) * ROW_TILES).reshape(-1)
    xs = _dispatch_call(seg_src, seg_dst, seg_len, pad_lo, pad_n, route, h2.reshape(t, d),
                        n_blocks * MOE_BLOCK)
    ys = _experts_call(bexp, first, nxt, nval, nreal, xs, w_gate_up[l], b_gate_up[l], w_down[l],
                       b_down[l])
    out = _combine_call(seg_src, seg_dst, seg_len, x1.reshape(t, d), meta2,
                        norm_final_g.reshape(1, d), ys)
    return out.reshape(bsz, seq, d)
```

```python
import jax
import jax.numpy as jnp
from jax import lax
from jax.experimental import pallas as pl
from jax.experimental.pallas import tpu as pltpu

F32 = jnp.float32
BF16 = jnp.bfloat16
I32 = jnp.int32
U32 = jnp.uint32

D_MODEL = 1024
CONV_WIDTH = 512
CONV_K = 3
GLA_WIDTH = 512
GLA_HEADS = 4
GLA_DV = 128
GLA_DK = 64
GLA_QK = GLA_HEADS * GLA_DK
GLA_RANK = 16
GLA_NORMALIZER = 16.0
GLA_CHUNK = 64
N_EXPERTS = 32
TOP_K = 4
D_FF = 1024
SWIGLU_LIMIT = 7.0
SWIGLU_ALPHA = 1.702
RMS_EPS = 1e-5

LANES = 128
SUBLANES = 8
ROW_TILES = D_MODEL // LANES // 2
ROUTE_ROWS = 4 * TOP_K

OFF_UH = 0
OFF_GB = OFF_UH + CONV_WIDTH
OFF_GC = OFF_GB + CONV_WIDTH
OFF_Q = OFF_GC + CONV_WIDTH
OFF_K = OFF_Q + GLA_QK
OFF_V = OFF_K + GLA_QK
OFF_GO = OFF_V + GLA_WIDTH
OFF_GKL = OFF_GO + GLA_WIDTH
D_IN_PAD = OFF_GKL + LANES

SEQ_TILE = 512
SORT_TILE = 256
STEP_TILES = 4
COMBINE_TILES = 2
MOE_BLOCK = 512
FF_CHUNK = 256
TAIL_ROWS = 128
CAST_ROWS = 128
NEG_BIG = -1e30
VMEM_LIMIT = 56 * 1024 * 1024


def _rms(x, g):
    return x * lax.rsqrt(jnp.mean(x * x, axis=-1, keepdims=True) + RMS_EPS) * g


def _dot(a, b):
    return jnp.dot(a, b, preferred_element_type=F32)


def _dot_nt(a, b):
    return lax.dot_general(a, b, (((1,), (1,)), ((), ())), preferred_element_type=F32)


def _pack_rows(x):
    half = x.shape[1] // 2
    xr = x.astype(BF16).astype(F32)
    lo = lax.bitcast_convert_type(xr[:, :half], U32) >> 16
    hi = lax.bitcast_convert_type(xr[:, half:], U32) & jnp.uint32(0xFFFF0000)
    return hi | lo


def _unpack_words(w):
    lo = lax.bitcast_convert_type(w << 16, F32).astype(BF16)
    hi = lax.bitcast_convert_type(w & jnp.uint32(0xFFFF0000), F32).astype(BF16)
    return lo, hi


def _split_bf16(x):
    hi = x.astype(BF16)
    lo = (x - hi.astype(F32)).astype(BF16)
    return hi, lo


def _mixer_kernel(x_ref, g1_ref, win_ref, convw_ref, wgk_ref, bgk_ref, gng_ref, wout_ref,
                  g2_ref, wrhl_ref, br_ref,
                  x1_ref, h2_ref, meta_ref, route_ref, seg_ref, cnt_ref,
                  proj_ref, ubuf_ref, la_ref, ycat_ref, state_ref, carry_ref):
    ts = x_ref.shape[1]
    b_idx = pl.program_id(0)
    s_idx = pl.program_id(1)

    @pl.when(s_idx == 0)
    def _():
        state_ref[...] = jnp.zeros_like(state_ref)
        ubuf_ref[0:SUBLANES, :] = jnp.zeros((SUBLANES, CONV_WIDTH), F32)

    @pl.when((s_idx == 0) & (b_idx == 0))
    def _():
        carry_ref[...] = jnp.zeros_like(carry_ref)

    x = x_ref[0]
    h = _rms(x, g1_ref[...]).astype(BF16)
    proj_ref[...] = _dot(h, win_ref[...])

    u = proj_ref[:, OFF_GC:OFF_GC + CONV_WIDTH] * proj_ref[:, OFF_UH:OFF_UH + CONV_WIDTH]
    ubuf_ref[SUBLANES:SUBLANES + ts, :] = u
    u1 = ubuf_ref[pl.ds(SUBLANES - 1, ts), :]
    u2 = ubuf_ref[pl.ds(SUBLANES - 2, ts), :]
    conv = convw_ref[0:1, :] * u2 + convw_ref[1:2, :] * u1 + convw_ref[2:3, :] * u
    ycat_ref[:, 0:CONV_WIDTH] = (proj_ref[:, OFF_GB:OFF_GB + CONV_WIDTH] * conv).astype(BF16)
    ubuf_ref[0:SUBLANES, :] = ubuf_ref[ts:ts + SUBLANES, :]

    gk = _dot(proj_ref[:, OFF_GKL:OFF_GKL + LANES].astype(BF16), wgk_ref[...]) + bgk_ref[...]
    log_sig = jnp.minimum(gk, 0.0) - jnp.log1p(jnp.exp(-jnp.abs(gk)))
    la_ref[...] = log_sig / GLA_NORMALIZER

    ci = lax.broadcasted_iota(I32, (GLA_CHUNK, GLA_CHUNK), 0)
    cj = lax.broadcasted_iota(I32, (GLA_CHUNK, GLA_CHUNK), 1)
    tri_incl = (cj <= ci).astype(BF16)
    causal = cj <= ci
    causal4 = jnp.concatenate([causal] * GLA_HEADS, axis=0)
    lane_qk = lax.broadcasted_iota(I32, (1, GLA_QK), 1)
    head_masks = [((lane_qk >= hd * GLA_DK) & (lane_qk < (hd + 1) * GLA_DK)).astype(F32)
                  for hd in range(GLA_HEADS)]
    gng = gng_ref[...]

    n_chunks = ts // GLA_CHUNK
    chunk_rows = [pl.ds(c * GLA_CHUNK, GLA_CHUNK) for c in range(n_chunks)]
    lane_c = lax.broadcasted_iota(I32, (GLA_QK, 2 * GLA_CHUNK), 1)
    qd_all, kd_all, kr_all, bl_all, v_all = [], [], [], [], []
    for rows in chunk_rows:
        la_hi, la_lo = _split_bf16(la_ref[rows, :])
        bcum = _dot(tri_incl, la_hi) + _dot(tri_incl, la_lo)
        blast = bcum[GLA_CHUNK - 1:GLA_CHUNK, :]
        q = proj_ref[rows, OFF_Q:OFF_Q + GLA_QK] * (GLA_DK ** -0.5)
        k = proj_ref[rows, OFF_K:OFF_K + GLA_QK]
        qd_all.append(q * jnp.exp(bcum))
        kd_all.append((k * jnp.exp(-bcum)).astype(BF16))
        kr_all.append(k * jnp.exp(blast - bcum))
        bl_all.append(blast)
        v_all.append(proj_ref[rows, OFF_V:OFF_V + GLA_WIDTH].astype(BF16))

    scores_all = []
    for c in range(n_chunks):
        q_stack = jnp.concatenate([qd_all[c] * m for m in head_masks], axis=0).astype(BF16)
        scores_all.append(
            jnp.where(causal4, _dot_nt(q_stack, kd_all[c]), 0.0).astype(BF16))
    o_intra_all = []
    for c in range(n_chunks):
        o_intra_all.append(jnp.concatenate(
            [_dot(scores_all[c][hd * GLA_CHUNK:(hd + 1) * GLA_CHUNK, :],
                  v_all[c][:, hd * GLA_DV:(hd + 1) * GLA_DV]) for hd in range(GLA_HEADS)], axis=1))
    kv_all, dcol_all = [], []
    for c in range(n_chunks):
        kt = jnp.concatenate(
            [kr_all[c], jnp.broadcast_to(bl_all[c], (GLA_CHUNK, GLA_QK))], axis=0).T
        dcol_all.append(jnp.exp(kt[:, GLA_CHUNK:GLA_CHUNK + 1]))
        kt_b = jnp.where(lane_c < GLA_CHUNK, kt, 0.0).astype(BF16)
        v_pad = jnp.concatenate([v_all[c], jnp.zeros_like(v_all[c])], axis=0)
        kv_all.append([_dot(kt_b[hd * GLA_DK:(hd + 1) * GLA_DK, :],
                            v_pad[:, hd * GLA_DV:(hd + 1) * GLA_DV]) for hd in range(GLA_HEADS)])

    o_all = []
    for c in range(n_chunks):
        state = state_ref[...]
        o_all.append(_dot(qd_all[c].astype(BF16), state.astype(BF16)) + o_intra_all[c])
        for hd in range(GLA_HEADS):
            rs = slice(hd * GLA_DK, (hd + 1) * GLA_DK)
            cs = slice(hd * GLA_DV, (hd + 1) * GLA_DV)
            state_ref[rs, cs] = dcol_all[c][rs, :] * state[rs, cs] + kv_all[c][hd]

    for c, rows in enumerate(chunk_rows):
        o = o_all[c]
        g_out = proj_ref[rows, OFF_GO:OFF_GO + GLA_WIDTH]
        o_n = jnp.concatenate(
            [_rms(o[:, hd * GLA_DV:(hd + 1) * GLA_DV], gng) for hd in range(GLA_HEADS)], axis=1)
        y = o_n * (g_out * jax.nn.sigmoid(g_out))
        ycat_ref[rows, CONV_WIDTH:CONV_WIDTH + GLA_WIDTH] = y.astype(BF16)

    x1 = x + _dot(ycat_ref[...], wout_ref[...])
    x1_ref[0] = x1
    h2 = _rms(x1, g2_ref[...])
    h2_hi, h2_lo = _split_bf16(h2)
    h2_ref[0] = h2_hi
    hi_terms = _dot(h2_hi, wrhl_ref[...])
    logits = (hi_terms[:, 0:LANES] + hi_terms[:, LANES:2 * LANES]
              + _dot(h2_lo, wrhl_ref[:, 0:LANES]) + br_ref[...])

    lt = logits.T[0:N_EXPERTS, :]
    erow = lax.broadcasted_iota(I32, (N_EXPERTS, ts), 0).astype(F32)
    work = lt
    sel = jnp.zeros((N_EXPERTS, ts), F32)
    top_v, top_i, top_oh = [], [], []
    for _ in range(TOP_K):
        m = jnp.max(work, axis=0, keepdims=True)
        idx = jnp.min(jnp.where(work == m, erow, float(N_EXPERTS)), axis=0, keepdims=True)
        oh = erow == idx
        top_v.append(m)
        top_i.append(idx)
        top_oh.append(oh)
        sel = sel + oh.astype(F32)
        work = jnp.where(oh, -jnp.inf, work)
    exps = [jnp.exp(tv - top_v[0]) for tv in top_v]
    denom = exps[0] + exps[1] + exps[2] + exps[3]
    gates = [e / denom for e in exps]

    tsrc = lax.broadcasted_iota(I32, (ts, ts), 0)
    tdst = lax.broadcasted_iota(I32, (ts, ts), 1)
    earlier = (tsrc < tdst).astype(BF16)
    local = _dot(sel.astype(BF16), earlier)
    carry = carry_ref[:, 0:1]
    ranks = [jnp.sum(jnp.where(oh, local + carry, 0.0), axis=0, keepdims=True) for oh in top_oh]

    ei = lax.broadcasted_iota(I32, (N_EXPERTS, N_EXPERTS), 0)
    ej = lax.broadcasted_iota(I32, (N_EXPERTS, N_EXPERTS), 1)
    lower_experts = (ej < ei).astype(BF16)
    lane_f = lax.broadcasted_iota(I32, (N_EXPERTS, LANES), 1)
    slot_base = []
    before = jnp.zeros((N_EXPERTS, 1), F32)
    for u in range(ts // SORT_TILE):
        sub_cnt = jnp.sum(sel[:, u * SORT_TILE:(u + 1) * SORT_TILE], axis=1, keepdims=True)
        seg_start = _dot(lower_experts,
                         jnp.broadcast_to(sub_cnt, (N_EXPERTS, LANES)).astype(BF16))[:, 0:1]
        seg_ref[u] = jnp.where(lane_f == 0, carry + before,
                               jnp.where(lane_f == 1, sub_cnt,
                                         jnp.where(lane_f == 2, seg_start, 0.0)))
        slot_base.append(jnp.broadcast_to(seg_start - before, (N_EXPERTS, SORT_TILE)))
        before = before + sub_cnt
    slot_base = jnp.concatenate(slot_base, axis=1)
    slots = [jnp.sum(jnp.where(oh, local + slot_base, 0.0), axis=0, keepdims=True)
             for oh in top_oh]
    new_carry = carry + before
    carry_ref[...] = jnp.broadcast_to(new_carry, carry_ref.shape)
    cnt_ref[...] = jnp.broadcast_to(new_carry, cnt_ref.shape)

    route = jnp.concatenate(top_i + gates + ranks + slots, axis=0)
    route_ref[...] = route
    meta_ref[0] = jnp.concatenate(
        [route, jnp.zeros((LANES - ROUTE_ROWS, ts), F32)], axis=0).T


def _mixer_call(x, g1, win, convw, wgk, bgk, gng, wout, g2, wrhl, br):
    bsz, seq, d = x.shape
    ts = SEQ_TILE
    grid = (bsz, seq // ts)

    def const(shape):
        return pl.BlockSpec(shape, lambda b, s: (0,) * len(shape))

    tile = lambda w: pl.BlockSpec((1, ts, w), lambda b, s: (b, s, 0))
    return pl.pallas_call(
        _mixer_kernel,
        grid=grid,
        in_specs=[tile(d), const(g1.shape), const(win.shape), const(convw.shape),
                  const(wgk.shape), const(bgk.shape), const(gng.shape), const(wout.shape),
                  const(g2.shape), const(wrhl.shape), const(br.shape)],
        out_specs=[tile(d),
                   tile(d),
                   tile(LANES),
                   pl.BlockSpec((ROUTE_ROWS, ts), lambda b, s: (0, b * (seq // ts) + s)),
                   pl.BlockSpec((ts // SORT_TILE, N_EXPERTS, LANES),
                                lambda b, s: (b * (seq // ts) + s, 0, 0)),
                   const((N_EXPERTS, LANES))],
        out_shape=[jax.ShapeDtypeStruct((bsz, seq, d), F32),
                   jax.ShapeDtypeStruct((bsz, seq, d), BF16),
                   jax.ShapeDtypeStruct((bsz, seq, LANES), F32),
                   jax.ShapeDtypeStruct((ROUTE_ROWS, bsz * seq), F32),
                   jax.ShapeDtypeStruct((bsz * seq // SORT_TILE, N_EXPERTS, LANES), F32),
                   jax.ShapeDtypeStruct((N_EXPERTS, LANES), F32)],
        scratch_shapes=[pltpu.VMEM((ts, D_IN_PAD), F32),
                        pltpu.VMEM((ts + SUBLANES, CONV_WIDTH), F32),
                        pltpu.VMEM((ts, GLA_QK), F32),
                        pltpu.VMEM((ts, D_MODEL), BF16),
                        pltpu.VMEM((GLA_QK, GLA_WIDTH), F32),
                        pltpu.VMEM((N_EXPERTS, LANES), F32)],
        compiler_params=pltpu.CompilerParams(
            dimension_semantics=("arbitrary", "arbitrary"), vmem_limit_bytes=VMEM_LIMIT),
        name="mixer",
    )(x, g1, win, convw, wgk, bgk, gng, wout, g2, wrhl, br)


def _dispatch_kernel(run_xs_ref, run_buf_ref, run_len_ref, pad_lo_ref, pad_n_ref,
                     route_ref, h2_ref, xs_hbm, sbuf, zbuf, sem, zsem):
    i = pl.program_id(0)
    n = pl.num_programs(0)
    td = SORT_TILE
    n_rows = TOP_K * td

    def zero_fill(wait):
        def fill(g, carry):
            ln = pl.multiple_of(pad_n_ref[g], ROW_TILES)

            @pl.when(ln > 0)
            def _():
                lo = pl.multiple_of(pad_lo_ref[g], ROW_TILES)
                cp = pltpu.make_async_copy(zbuf.at[pl.ds(0, ln), :], xs_hbm.at[pl.ds(lo, ln), :],
                                           zsem.at[0])
                if wait:
                    cp.wait()
                else:
                    cp.start()
            return carry

        lax.fori_loop(0, pad_lo_ref.shape[0], fill, 0)

    def wait_runs(buf_slot):
        pltpu.make_async_copy(sbuf.at[buf_slot], xs_hbm.at[pl.ds(0, n_rows * ROW_TILES), :],
                              sem.at[buf_slot]).wait()

    @pl.when(i == 0)
    def _():
        zbuf[...] = jnp.zeros_like(zbuf)
        zero_fill(wait=False)

    for u in range(STEP_TILES):
        @pl.when(i >= 1)
        def _(u=u):
            wait_runs(u)

        route = route_ref[:, u * td:(u + 1) * td]
        row = lax.broadcasted_iota(I32, (n_rows, td), 0).astype(F32)
        pick = jnp.zeros((n_rows, td), F32)
        for kk in range(TOP_K):
            pick = jnp.where(row == route[3 * TOP_K + kk:3 * TOP_K + kk + 1, :], 1.0, pick)
        rows_sorted = _dot(pick.astype(BF16), h2_ref[u * td:(u + 1) * td, :])
        words = _pack_rows(rows_sorted)
        for j in range(ROW_TILES):
            sbuf[u, pl.ds(j, n_rows, stride=ROW_TILES), :] = words[:, j * LANES:(j + 1) * LANES]

        def body(e, carry, u=u):
            g = (i * STEP_TILES + u) * N_EXPERTS + e
            ln = pl.multiple_of(run_len_ref[g], ROW_TILES)

            @pl.when(ln > 0)
            def _():
                src = pl.multiple_of(run_buf_ref[g], ROW_TILES)
                dst = pl.multiple_of(run_xs_ref[g], ROW_TILES)
                pltpu.make_async_copy(sbuf.at[u, pl.ds(src, ln), :], xs_hbm.at[pl.ds(dst, ln), :],
                                      sem.at[u]).start()
            return carry

        lax.fori_loop(0, N_EXPERTS, body, 0)

    @pl.when(i == n - 1)
    def _():
        for u in range(STEP_TILES):
            wait_runs(u)
        zero_fill(wait=True)


def _dispatch_call(run_xs, run_buf, run_len, pad_lo, pad_n, route, h2, n_rows):
    t, d = h2.shape
    td = SORT_TILE * STEP_TILES
    grid_spec = pltpu.PrefetchScalarGridSpec(
        num_scalar_prefetch=5,
        grid=(t // td,),
        in_specs=[pl.BlockSpec((ROUTE_ROWS, td), lambda i, *_: (0, i)),
                  pl.BlockSpec((td, d), lambda i, *_: (i, 0))],
        out_specs=pl.BlockSpec(memory_space=pl.ANY),
        scratch_shapes=[pltpu.VMEM((STEP_TILES, TOP_K * SORT_TILE * ROW_TILES, LANES), U32),
                        pltpu.VMEM((MOE_BLOCK * ROW_TILES, LANES), U32),
                        pltpu.SemaphoreType.DMA((STEP_TILES,)),
                        pltpu.SemaphoreType.DMA((1,))],
    )
    return pl.pallas_call(
        _dispatch_kernel,
        grid_spec=grid_spec,
        out_shape=jax.ShapeDtypeStruct((n_rows * ROW_TILES, LANES), U32),
        compiler_params=pltpu.CompilerParams(
            dimension_semantics=("arbitrary",), vmem_limit_bytes=VMEM_LIMIT,
            has_side_effects=True),
        name="dispatch",
    )(run_xs, run_buf, run_len, pad_lo, pad_n, route, h2)


def _experts_kernel(bexp_ref, first_ref, nxt_ref, nval_ref, nreal_ref,
                    xs_ref, wgu_hbm, bgu_ref, wd_hbm, bd_ref,
                    ys_ref,
                    xb_ref, act_ref, wgu_stage, wd_stage, wgu_bf, wd_bf, wsem):
    i = pl.program_id(0)
    nreal = nreal_ref[0]
    bm = xb_ref.shape[0]

    def weight_copies(e):
        return (pltpu.make_async_copy(wgu_hbm.at[e], wgu_stage, wsem.at[0]),
                pltpu.make_async_copy(wd_hbm.at[e], wd_stage, wsem.at[1]))

    @pl.when(i >= nreal)
    def _():
        ys_ref[...] = jnp.zeros_like(ys_ref)

    @pl.when(i < nreal)
    def _():
        e = bexp_ref[i]

        @pl.when(i == 0)
        def _():
            for cp in weight_copies(e):
                cp.start(priority=1)

        @pl.when(first_ref[i] == 1)
        def _():
            for cp in weight_copies(e):
                cp.wait()

            def cast_gu(r, carry):
                rows = pl.ds(pl.multiple_of(r * CAST_ROWS, CAST_ROWS), CAST_ROWS)
                wgu_bf[rows, :] = wgu_stage[rows, :].astype(BF16)
                return carry

            def cast_d(r, carry):
                rows = pl.ds(pl.multiple_of(r * CAST_ROWS, CAST_ROWS), CAST_ROWS)
                wd_bf[rows, :] = wd_stage[rows, :].astype(BF16)
                return carry

            lax.fori_loop(0, D_MODEL // CAST_ROWS, cast_gu, 0)
            lax.fori_loop(0, D_FF // CAST_ROWS, cast_d, 0)

            @pl.when(nxt_ref[i] >= 0)
            def _():
                for cp in weight_copies(nxt_ref[i]):
                    cp.start(priority=1)

        def mlp(rows):
            for j in range(ROW_TILES):
                lo, hi = _unpack_words(xs_ref[pl.ds(j, rows, stride=ROW_TILES), :])
                xb_ref[0:rows, j * LANES:(j + 1) * LANES] = lo
                xb_ref[0:rows, D_MODEL // 2 + j * LANES:D_MODEL // 2 + (j + 1) * LANES] = hi
            for c in range(D_FF // FF_CHUNK):
                f0 = c * FF_CHUNK
                xb = xb_ref[0:rows, :]
                gate = _dot(xb, wgu_bf[:, f0:f0 + FF_CHUNK]) + bgu_ref[0, :, f0:f0 + FF_CHUNK]
                up = (_dot(xb, wgu_bf[:, D_FF + f0:D_FF + f0 + FF_CHUNK])
                      + bgu_ref[0, :, D_FF + f0:D_FF + f0 + FF_CHUNK])
                gate = jnp.minimum(gate, SWIGLU_LIMIT)
                up = jnp.clip(up, -SWIGLU_LIMIT, SWIGLU_LIMIT)
                glu = gate * jax.nn.sigmoid(gate * SWIGLU_ALPHA)
                act_ref[0:rows, f0:f0 + FF_CHUNK] = ((up + 1.0) * glu).astype(BF16)
            out = _pack_rows(_dot(act_ref[0:rows, :], wd_bf[...]) + bd_ref[0])
            for j in range(ROW_TILES):
                ys_ref[pl.ds(j, rows, stride=ROW_TILES), :] = out[:, j * LANES:(j + 1) * LANES]

        n_routed = nval_ref[i]
        for rows in range(TAIL_ROWS, bm + 1, TAIL_ROWS):
            @pl.when((n_routed > rows - TAIL_ROWS) & (n_routed <= rows))
            def _(rows=rows):
                mlp(rows)
                if rows < bm:
                    ys_ref[rows * ROW_TILES:bm * ROW_TILES, :] = jnp.zeros(
                        ((bm - rows) * ROW_TILES, LANES), U32)


def _experts_call(bexp, first, nxt, nval, nreal, xs, wgu, bgu, wd, bd):
    bm = MOE_BLOCK
    n_blocks = xs.shape[0] // (bm * ROW_TILES)
    bgu3 = bgu.reshape(N_EXPERTS, 1, 2 * D_FF)
    bd3 = bd.reshape(N_EXPERTS, 1, D_MODEL)
    grid_spec = pltpu.PrefetchScalarGridSpec(
        num_scalar_prefetch=5,
        grid=(n_blocks,),
        in_specs=[
            pl.BlockSpec((bm * ROW_TILES, LANES),
                         lambda i, be, fi, nx, nv, nr: (jnp.minimum(i, nr[0] - 1), 0)),
            pl.BlockSpec(memory_space=pl.ANY),
            pl.BlockSpec((1, 1, 2 * D_FF), lambda i, be, fi, nx, nv, nr: (be[i], 0, 0)),
            pl.BlockSpec(memory_space=pl.ANY),
            pl.BlockSpec((1, 1, D_MODEL), lambda i, be, fi, nx, nv, nr: (be[i], 0, 0)),
        ],
        out_specs=pl.BlockSpec((bm * ROW_TILES, LANES), lambda i, be, fi, nx, nv, nr: (i, 0)),
        scratch_shapes=[pltpu.VMEM((bm, D_MODEL), BF16),
                        pltpu.VMEM((bm, D_FF), BF16),
                        pltpu.VMEM((D_MODEL, 2 * D_FF), F32),
                        pltpu.VMEM((D_FF, D_MODEL), F32),
                        pltpu.VMEM((D_MODEL, 2 * D_FF), BF16),
                        pltpu.VMEM((D_FF, D_MODEL), BF16),
                        pltpu.SemaphoreType.DMA((2,))],
    )
    return pl.pallas_call(
        _experts_kernel,
        grid_spec=grid_spec,
        out_shape=jax.ShapeDtypeStruct(xs.shape, U32),
        compiler_params=pltpu.CompilerParams(
            dimension_semantics=("arbitrary",), vmem_limit_bytes=VMEM_LIMIT),
        name="experts",
    )(bexp, first, nxt, nval, nreal, xs, wgu, bgu3, wd, bd3)


def _combine_kernel(src_ref, dst_ref, len_ref, x1_ref, meta_ref, g_ref, ys_hbm, out_ref,
                    ybuf, ysort_ref, sem):
    i = pl.program_id(0)
    n = pl.num_programs(0)
    tb = SORT_TILE
    n_rows = TOP_K * tb
    slot = i % 2

    def start_runs(step, buf_slot):
        for u in range(COMBINE_TILES):
            def body(e, carry, u=u):
                g = (step * COMBINE_TILES + u) * N_EXPERTS + e
                ln = pl.multiple_of(len_ref[g], ROW_TILES)

                @pl.when(ln > 0)
                def _():
                    src = pl.multiple_of(src_ref[g], ROW_TILES)
                    dst = pl.multiple_of(dst_ref[g], ROW_TILES)
                    pltpu.make_async_copy(ys_hbm.at[pl.ds(src, ln), :],
                                          ybuf.at[buf_slot, u, pl.ds(dst, ln), :],
                                          sem.at[buf_slot]).start()
                return carry
            lax.fori_loop(0, N_EXPERTS, body, 0)

    @pl.when(i == 0)
    def _():
        start_runs(0, 0)

    @pl.when(i + 1 < n)
    def _():
        start_runs(i + 1, 1 - slot)

    for u in range(COMBINE_TILES):
        pltpu.make_async_copy(ys_hbm.at[pl.ds(0, n_rows * ROW_TILES), :], ybuf.at[slot, u],
                              sem.at[slot]).wait()

    for u in range(COMBINE_TILES):
        rs = slice(u * tb, (u + 1) * tb)
        for j in range(ROW_TILES):
            lo, hi = _unpack_words(ybuf[slot, u, pl.ds(j, n_rows, stride=ROW_TILES), :])
            ysort_ref[u, :, j * LANES:(j + 1) * LANES] = lo
            ysort_ref[u, :, D_MODEL // 2 + j * LANES:D_MODEL // 2 + (j + 1) * LANES] = hi
        meta = meta_ref[rs, :]
        col = lax.broadcasted_iota(I32, (tb, n_rows), 1).astype(F32)
        weights = jnp.zeros((tb, n_rows), F32)
        for kk in range(TOP_K):
            weights = jnp.where(col == meta[:, 3 * TOP_K + kk:3 * TOP_K + kk + 1],
                                meta[:, TOP_K + kk:TOP_K + kk + 1], weights)
        acc = x1_ref[rs, :] + _dot(weights.astype(BF16), ysort_ref[u])
        out_ref[rs, :] = _rms(acc, g_ref[...])


def _combine_call(seg_src, seg_dst, seg_len, x1, meta, g, ys):
    t, d = x1.shape
    tb = SORT_TILE * COMBINE_TILES
    nb = t // tb
    grid_spec = pltpu.PrefetchScalarGridSpec(
        num_scalar_prefetch=3,
        grid=(nb,),
        in_specs=[pl.BlockSpec((tb, d), lambda i, a, b, c: (i, 0)),
                  pl.BlockSpec((tb, LANES), lambda i, a, b, c: (i, 0)),
                  pl.BlockSpec((1, d), lambda i, a, b, c: (0, 0)),
                  pl.BlockSpec(memory_space=pl.ANY)],
        out_specs=pl.BlockSpec((tb, d), lambda i, a, b, c: (i, 0)),
        scratch_shapes=[pltpu.VMEM((2, COMBINE_TILES, TOP_K * SORT_TILE * ROW_TILES, LANES), U32),
                        pltpu.VMEM((COMBINE_TILES, TOP_K * SORT_TILE, D_MODEL), BF16),
                        pltpu.SemaphoreType.DMA((2,))],
    )
    return pl.pallas_call(
        _combine_kernel,
        grid_spec=grid_spec,
        out_shape=jax.ShapeDtypeStruct((t, d), F32),
        compiler_params=pltpu.CompilerParams(
            dimension_semantics=("arbitrary",), vmem_limit_bytes=VMEM_LIMIT),
        name="combine",
    )(seg_src, seg_dst, seg_len, x1, meta, g, ys)


def _routing_tables(counts, t):
    bm = MOE_BLOCK
    n_blocks = t * TOP_K // bm + N_EXPERTS
    eids = jnp.arange(N_EXPERTS, dtype=I32)
    nblk_e = (counts + bm - 1) // bm
    blk_end = jnp.sum(jnp.where(eids[None, :] <= eids[:, None], nblk_e[None, :], 0), axis=1)
    blk_start = blk_end - nblk_e
    nreal = blk_end[N_EXPERTS - 1]
    pad_start = blk_start * bm
    blk = jnp.arange(n_blocks, dtype=I32)
    bexp = jnp.minimum(jnp.sum((blk_end[None, :] <= blk[:, None]).astype(I32), axis=1),
                       N_EXPERTS - 1)
    blk_is_e = bexp[:, None] == eids[None, :]
    pick = lambda tab: jnp.sum(jnp.where(blk_is_e, tab[None, :], 0), axis=1)
    first = (blk == pick(blk_start)).astype(I32)
    nxt_e = jnp.sum((blk_end[None, :] <= blk_end[:, None]).astype(I32), axis=1)
    nxt_e = jnp.where(blk_end < nreal, jnp.minimum(nxt_e, N_EXPERTS - 1), -1)
    nxt = pick(nxt_e)
    nval = jnp.clip(pick(counts) - (blk - pick(blk_start)) * bm, 0, bm)
    tail_blk = jnp.arange(N_EXPERTS, dtype=I32) + nreal
    pad_lo = jnp.concatenate([pad_start + counts, jnp.minimum(tail_blk, n_blocks - 1) * bm])
    pad_n = jnp.concatenate([nblk_e * bm - counts, jnp.where(tail_blk < n_blocks, bm, 0)])
    return (pad_start, bexp, first, nxt, nval, nreal.reshape(1).astype(I32), n_blocks,
            pad_lo * ROW_TILES, pad_n * ROW_TILES)


def kernel(x, norm_mix_g, w_in, conv_w, w_gk_up, b_gk_up, gla_norm_g, w_out, norm_ffn_g,
           w_router, b_router, w_gate_up, b_gate_up, w_down, b_down, norm_final_g):
    bsz, seq, d = x.shape
    t = bsz * seq
    assert w_in.shape[0] == 1, "single-layer trunk only"
    assert conv_w.shape[1] == CONV_K and ROUTE_ROWS % SUBLANES == 0
    l = 0
    d_in = w_in.shape[-1]
    win = jnp.pad(w_in[l].astype(BF16), ((0, 0), (0, D_IN_PAD - d_in)))
    wgk = jnp.pad(w_gk_up[l], ((0, LANES - GLA_RANK), (0, 0))).astype(BF16)
    wr = jnp.pad(w_router[l], ((0, 0), (0, LANES - N_EXPERTS)))
    wrh = wr.astype(BF16)
    wrhl = jnp.concatenate([wrh, (wr - wrh.astype(F32)).astype(BF16)], axis=1)
    br = jnp.pad(b_router[l], (0, LANES - N_EXPERTS), constant_values=NEG_BIG).reshape(1, LANES)

    x1, h2, meta, route, seg, cnt = _mixer_call(
        x, norm_mix_g[l].reshape(1, d), win, conv_w[l], wgk, b_gk_up[l].reshape(1, GLA_QK),
        gla_norm_g[l].reshape(1, GLA_DV), w_out[l].astype(BF16), norm_ffn_g[l].reshape(1, d),
        wrhl, br)

    meta2 = meta.reshape(t, LANES)
    counts = cnt[:, 0].astype(I32)
    (pad_start, bexp, first, nxt, nval, nreal, n_blocks, pad_lo,
     pad_n) = _routing_tables(counts, t)
    seg_src = ((pad_start[None, :] + seg[:, :, 0].astype(I32)) * ROW_TILES).reshape(-1)
    seg_len = (seg[:, :, 1].astype(I32) * ROW_TILES).reshape(-1)
    seg_dst = (seg[:, :, 2].astype(I32) * ROW_TILES).reshape(-1)
    xs = _dispatch_call(seg_src, seg_dst, seg_len, pad_lo, pad_n, route, h2.reshape(t, d),
                        n_blocks * MOE_BLOCK)
    ys = _experts_call(bexp, first, nxt, nval, nreal, xs, w_gate_up[l], b_gate_up[l], w_down[l],
                       b_down[l])
    out = _combine_call(seg_src, seg_dst, seg_len, x1.reshape(t, d), meta2,
                        norm_final_g.reshape(1, d), ys)
    return out.reshape(bsz, seq, d)
```

```python
import jax
import jax.numpy as jnp
from jax import lax
from jax.experimental import pallas as pl
from jax.experimental.pallas import tpu as pltpu

F32 = jnp.float32
BF16 = jnp.bfloat16
I32 = jnp.int32
U32 = jnp.uint32

D_MODEL = 1024
CONV_WIDTH = 512
CONV_K = 3
GLA_WIDTH = 512
GLA_HEADS = 4
GLA_DV = 128
GLA_DK = 64
GLA_QK = GLA_HEADS * GLA_DK
GLA_RANK = 16
GLA_NORMALIZER = 16.0
GLA_CHUNK = 64
N_EXPERTS = 32
TOP_K = 4
D_FF = 1024
SWIGLU_LIMIT = 7.0
SWIGLU_ALPHA = 1.702
RMS_EPS = 1e-5

LANES = 128
SUBLANES = 8
ROW_TILES = D_MODEL // LANES // 2
ROUTE_ROWS = 4 * TOP_K

OFF_UH = 0
OFF_GB = OFF_UH + CONV_WIDTH
OFF_GC = OFF_GB + CONV_WIDTH
OFF_Q = OFF_GC + CONV_WIDTH
OFF_K = OFF_Q + GLA_QK
OFF_V = OFF_K + GLA_QK
OFF_GO = OFF_V + GLA_WIDTH
OFF_GKL = OFF_GO + GLA_WIDTH
D_IN_PAD = OFF_GKL + LANES

SEQ_TILE = 512
SORT_TILE = 256
STEP_TILES = 4
COMBINE_TILES = 4
MOE_BLOCK = 512
FF_CHUNK = 256
TAIL_ROWS = 128
CAST_ROWS = 128
NEG_BIG = -1e30
VMEM_LIMIT = 56 * 1024 * 1024


def _rms(x, g):
    return x * lax.rsqrt(jnp.mean(x * x, axis=-1, keepdims=True) + RMS_EPS) * g


def _dot(a, b):
    return jnp.dot(a, b, preferred_element_type=F32)


def _dot_nt(a, b):
    return lax.dot_general(a, b, (((1,), (1,)), ((), ())), preferred_element_type=F32)


def _pack_rows(x):
    half = x.shape[1] // 2
    xr = x.astype(BF16).astype(F32)
    lo = lax.bitcast_convert_type(xr[:, :half], U32) >> 16
    hi = lax.bitcast_convert_type(xr[:, half:], U32) & jnp.uint32(0xFFFF0000)
    return hi | lo


def _unpack_words(w):
    lo = lax.bitcast_convert_type(w << 16, F32).astype(BF16)
    hi = lax.bitcast_convert_type(w & jnp.uint32(0xFFFF0000), F32).astype(BF16)
    return lo, hi


def _split_bf16(x):
    hi = x.astype(BF16)
    lo = (x - hi.astype(F32)).astype(BF16)
    return hi, lo


def _mixer_kernel(x_ref, g1_ref, win_ref, convw_ref, wgk_ref, bgk_ref, gng_ref, wout_ref,
                  g2_ref, wrhl_ref, br_ref,
                  x1_ref, h2_ref, meta_ref, route_ref, seg_ref, cnt_ref,
                  proj_ref, ubuf_ref, la_ref, ycat_ref, state_ref, carry_ref):
    ts = x_ref.shape[1]
    b_idx = pl.program_id(0)
    s_idx = pl.program_id(1)

    @pl.when(s_idx == 0)
    def _():
        state_ref[...] = jnp.zeros_like(state_ref)
        ubuf_ref[0:SUBLANES, :] = jnp.zeros((SUBLANES, CONV_WIDTH), F32)

    @pl.when((s_idx == 0) & (b_idx == 0))
    def _():
        carry_ref[...] = jnp.zeros_like(carry_ref)

    x = x_ref[0]
    h = _rms(x, g1_ref[...]).astype(BF16)
    proj_ref[...] = _dot(h, win_ref[...])

    u = proj_ref[:, OFF_GC:OFF_GC + CONV_WIDTH] * proj_ref[:, OFF_UH:OFF_UH + CONV_WIDTH]
    ubuf_ref[SUBLANES:SUBLANES + ts, :] = u
    u1 = ubuf_ref[pl.ds(SUBLANES - 1, ts), :]
    u2 = ubuf_ref[pl.ds(SUBLANES - 2, ts), :]
    conv = convw_ref[0:1, :] * u2 + convw_ref[1:2, :] * u1 + convw_ref[2:3, :] * u
    ycat_ref[:, 0:CONV_WIDTH] = (proj_ref[:, OFF_GB:OFF_GB + CONV_WIDTH] * conv).astype(BF16)
    ubuf_ref[0:SUBLANES, :] = ubuf_ref[ts:ts + SUBLANES, :]

    gk = _dot(proj_ref[:, OFF_GKL:OFF_GKL + LANES].astype(BF16), wgk_ref[...]) + bgk_ref[...]
    log_sig = jnp.minimum(gk, 0.0) - jnp.log1p(jnp.exp(-jnp.abs(gk)))
    la_ref[...] = log_sig / GLA_NORMALIZER

    ci = lax.broadcasted_iota(I32, (GLA_CHUNK, GLA_CHUNK), 0)
    cj = lax.broadcasted_iota(I32, (GLA_CHUNK, GLA_CHUNK), 1)
    tri_incl = (cj <= ci).astype(BF16)
    causal = cj <= ci
    causal4 = jnp.concatenate([causal] * GLA_HEADS, axis=0)
    lane_qk = lax.broadcasted_iota(I32, (1, GLA_QK), 1)
    head_masks = [((lane_qk >= hd * GLA_DK) & (lane_qk < (hd + 1) * GLA_DK)).astype(F32)
                  for hd in range(GLA_HEADS)]
    gng = gng_ref[...]

    n_chunks = ts // GLA_CHUNK
    chunk_rows = [pl.ds(c * GLA_CHUNK, GLA_CHUNK) for c in range(n_chunks)]
    lane_c = lax.broadcasted_iota(I32, (GLA_QK, 2 * GLA_CHUNK), 1)
    qd_all, kd_all, kr_all, bl_all, v_all = [], [], [], [], []
    for rows in chunk_rows:
        la_hi, la_lo = _split_bf16(la_ref[rows, :])
        bcum = _dot(tri_incl, la_hi) + _dot(tri_incl, la_lo)
        blast = bcum[GLA_CHUNK - 1:GLA_CHUNK, :]
        q = proj_ref[rows, OFF_Q:OFF_Q + GLA_QK] * (GLA_DK ** -0.5)
        k = proj_ref[rows, OFF_K:OFF_K + GLA_QK]
        qd_all.append(q * jnp.exp(bcum))
        kd_all.append((k * jnp.exp(-bcum)).astype(BF16))
        kr_all.append(k * jnp.exp(blast - bcum))
        bl_all.append(blast)
        v_all.append(proj_ref[rows, OFF_V:OFF_V + GLA_WIDTH].astype(BF16))

    scores_all = []
    for c in range(n_chunks):
        q_stack = jnp.concatenate([qd_all[c] * m for m in head_masks], axis=0).astype(BF16)
        scores_all.append(
            jnp.where(causal4, _dot_nt(q_stack, kd_all[c]), 0.0).astype(BF16))
    o_intra_all = []
    for c in range(n_chunks):
        o_intra_all.append(jnp.concatenate(
            [_dot(scores_all[c][hd * GLA_CHUNK:(hd + 1) * GLA_CHUNK, :],
                  v_all[c][:, hd * GLA_DV:(hd + 1) * GLA_DV]) for hd in range(GLA_HEADS)], axis=1))
    kv_all, dcol_all = [], []
    for c in range(n_chunks):
        kt = jnp.concatenate(
            [kr_all[c], jnp.broadcast_to(bl_all[c], (GLA_CHUNK, GLA_QK))], axis=0).T
        dcol_all.append(jnp.exp(kt[:, GLA_CHUNK:GLA_CHUNK + 1]))
        kt_b = jnp.where(lane_c < GLA_CHUNK, kt, 0.0).astype(BF16)
        v_pad = jnp.concatenate([v_all[c], jnp.zeros_like(v_all[c])], axis=0)
        kv_all.append([_dot(kt_b[hd * GLA_DK:(hd + 1) * GLA_DK, :],
                            v_pad[:, hd * GLA_DV:(hd + 1) * GLA_DV]) for hd in range(GLA_HEADS)])

    o_all = []
    for c in range(n_chunks):
        state = state_ref[...]
        o_all.append(_dot(qd_all[c].astype(BF16), state.astype(BF16)) + o_intra_all[c])
        for hd in range(GLA_HEADS):
            rs = slice(hd * GLA_DK, (hd + 1) * GLA_DK)
            cs = slice(hd * GLA_DV, (hd + 1) * GLA_DV)
            state_ref[rs, cs] = dcol_all[c][rs, :] * state[rs, cs] + kv_all[c][hd]

    for c, rows in enumerate(chunk_rows):
        o = o_all[c]
        g_out = proj_ref[rows, OFF_GO:OFF_GO + GLA_WIDTH]
        o_n = jnp.concatenate(
            [_rms(o[:, hd * GLA_DV:(hd + 1) * GLA_DV], gng) for hd in range(GLA_HEADS)], axis=1)
        y = o_n * (g_out * jax.nn.sigmoid(g_out))
        ycat_ref[rows, CONV_WIDTH:CONV_WIDTH + GLA_WIDTH] = y.astype(BF16)

    x1 = x + _dot(ycat_ref[...], wout_ref[...])
    x1_ref[0] = x1
    h2 = _rms(x1, g2_ref[...])
    h2_hi, h2_lo = _split_bf16(h2)
    h2_ref[0] = h2_hi
    hi_terms = _dot(h2_hi, wrhl_ref[...])
    logits = (hi_terms[:, 0:LANES] + hi_terms[:, LANES:2 * LANES]
              + _dot(h2_lo, wrhl_ref[:, 0:LANES]) + br_ref[...])

    lt = logits.T[0:N_EXPERTS, :]
    erow = lax.broadcasted_iota(I32, (N_EXPERTS, ts), 0).astype(F32)
    work = lt
    sel = jnp.zeros((N_EXPERTS, ts), F32)
    top_v, top_i, top_oh = [], [], []
    for _ in range(TOP_K):
        m = jnp.max(work, axis=0, keepdims=True)
        idx = jnp.min(jnp.where(work == m, erow, float(N_EXPERTS)), axis=0, keepdims=True)
        oh = erow == idx
        top_v.append(m)
        top_i.append(idx)
        top_oh.append(oh)
        sel = sel + oh.astype(F32)
        work = jnp.where(oh, -jnp.inf, work)
    exps = [jnp.exp(tv - top_v[0]) for tv in top_v]
    denom = exps[0] + exps[1] + exps[2] + exps[3]
    gates = [e / denom for e in exps]

    tsrc = lax.broadcasted_iota(I32, (ts, ts), 0)
    tdst = lax.broadcasted_iota(I32, (ts, ts), 1)
    earlier = (tsrc < tdst).astype(BF16)
    local = _dot(sel.astype(BF16), earlier)
    carry = carry_ref[:, 0:1]
    ranks = [jnp.sum(jnp.where(oh, local + carry, 0.0), axis=0, keepdims=True) for oh in top_oh]

    ei = lax.broadcasted_iota(I32, (N_EXPERTS, N_EXPERTS), 0)
    ej = lax.broadcasted_iota(I32, (N_EXPERTS, N_EXPERTS), 1)
    lower_experts = (ej < ei).astype(BF16)
    lane_f = lax.broadcasted_iota(I32, (N_EXPERTS, LANES), 1)
    slot_base = []
    before = jnp.zeros((N_EXPERTS, 1), F32)
    for u in range(ts // SORT_TILE):
        sub_cnt = jnp.sum(sel[:, u * SORT_TILE:(u + 1) * SORT_TILE], axis=1, keepdims=True)
        seg_start = _dot(lower_experts,
                         jnp.broadcast_to(sub_cnt, (N_EXPERTS, LANES)).astype(BF16))[:, 0:1]
        seg_ref[u] = jnp.where(lane_f == 0, carry + before,
                               jnp.where(lane_f == 1, sub_cnt,
                                         jnp.where(lane_f == 2, seg_start, 0.0)))
        slot_base.append(jnp.broadcast_to(seg_start - before, (N_EXPERTS, SORT_TILE)))
        before = before + sub_cnt
    slot_base = jnp.concatenate(slot_base, axis=1)
    slots = [jnp.sum(jnp.where(oh, local + slot_base, 0.0), axis=0, keepdims=True)
             for oh in top_oh]
    new_carry = carry + before
    carry_ref[...] = jnp.broadcast_to(new_carry, carry_ref.shape)
    cnt_ref[...] = jnp.broadcast_to(new_carry, cnt_ref.shape)

    route = jnp.concatenate(top_i + gates + ranks + slots, axis=0)
    route_ref[...] = route
    meta_ref[0] = jnp.concatenate(
        [route, jnp.zeros((LANES - ROUTE_ROWS, ts), F32)], axis=0).T


def _mixer_call(x, g1, win, convw, wgk, bgk, gng, wout, g2, wrhl, br):
    bsz, seq, d = x.shape
    ts = SEQ_TILE
    grid = (bsz, seq // ts)

    def const(shape):
        return pl.BlockSpec(shape, lambda b, s: (0,) * len(shape))

    tile = lambda w: pl.BlockSpec((1, ts, w), lambda b, s: (b, s, 0))
    return pl.pallas_call(
        _mixer_kernel,
        grid=grid,
        in_specs=[tile(d), const(g1.shape), const(win.shape), const(convw.shape),
                  const(wgk.shape), const(bgk.shape), const(gng.shape), const(wout.shape),
                  const(g2.shape), const(wrhl.shape), const(br.shape)],
        out_specs=[tile(d),
                   tile(d),
                   tile(LANES),
                   pl.BlockSpec((ROUTE_ROWS, ts), lambda b, s: (0, b * (seq // ts) + s)),
                   pl.BlockSpec((ts // SORT_TILE, N_EXPERTS, LANES),
                                lambda b, s: (b * (seq // ts) + s, 0, 0)),
                   const((N_EXPERTS, LANES))],
        out_shape=[jax.ShapeDtypeStruct((bsz, seq, d), F32),
                   jax.ShapeDtypeStruct((bsz, seq, d), BF16),
                   jax.ShapeDtypeStruct((bsz, seq, LANES), F32),
                   jax.ShapeDtypeStruct((ROUTE_ROWS, bsz * seq), F32),
                   jax.ShapeDtypeStruct((bsz * seq // SORT_TILE, N_EXPERTS, LANES), F32),
                   jax.ShapeDtypeStruct((N_EXPERTS, LANES), F32)],
        scratch_shapes=[pltpu.VMEM((ts, D_IN_PAD), F32),
                        pltpu.VMEM((ts + SUBLANES, CONV_WIDTH), F32),
                        pltpu.VMEM((ts, GLA_QK), F32),
                        pltpu.VMEM((ts, D_MODEL), BF16),
                        pltpu.VMEM((GLA_QK, GLA_WIDTH), F32),
                        pltpu.VMEM((N_EXPERTS, LANES), F32)],
        compiler_params=pltpu.CompilerParams(
            dimension_semantics=("arbitrary", "arbitrary"), vmem_limit_bytes=VMEM_LIMIT),
        name="mixer",
    )(x, g1, win, convw, wgk, bgk, gng, wout, g2, wrhl, br)


def _dispatch_kernel(run_xs_ref, run_buf_ref, run_len_ref, pad_lo_ref, pad_n_ref,
                     route_ref, h2_ref, xs_hbm, sbuf, zbuf, sem, zsem):
    i = pl.program_id(0)
    n = pl.num_programs(0)
    td = SORT_TILE
    n_rows = TOP_K * td

    def zero_fill(wait):
        def fill(g, carry):
            ln = pl.multiple_of(pad_n_ref[g], ROW_TILES)

            @pl.when(ln > 0)
            def _():
                lo = pl.multiple_of(pad_lo_ref[g], ROW_TILES)
                cp = pltpu.make_async_copy(zbuf.at[pl.ds(0, ln), :], xs_hbm.at[pl.ds(lo, ln), :],
                                           zsem.at[0])
                if wait:
                    cp.wait()
                else:
                    cp.start()
            return carry

        lax.fori_loop(0, pad_lo_ref.shape[0], fill, 0)

    def wait_runs(buf_slot):
        pltpu.make_async_copy(sbuf.at[buf_slot], xs_hbm.at[pl.ds(0, n_rows * ROW_TILES), :],
                              sem.at[buf_slot]).wait()

    @pl.when(i == 0)
    def _():
        zbuf[...] = jnp.zeros_like(zbuf)
        zero_fill(wait=False)

    for u in range(STEP_TILES):
        @pl.when(i >= 1)
        def _(u=u):
            wait_runs(u)

        route = route_ref[:, u * td:(u + 1) * td]
        row = lax.broadcasted_iota(I32, (n_rows, td), 0).astype(F32)
        pick = jnp.zeros((n_rows, td), F32)
        for kk in range(TOP_K):
            pick = jnp.where(row == route[3 * TOP_K + kk:3 * TOP_K + kk + 1, :], 1.0, pick)
        rows_sorted = _dot(pick.astype(BF16), h2_ref[u * td:(u + 1) * td, :])
        words = _pack_rows(rows_sorted)
        for j in range(ROW_TILES):
            sbuf[u, pl.ds(j, n_rows, stride=ROW_TILES), :] = words[:, j * LANES:(j + 1) * LANES]

        def body(e, carry, u=u):
            g = (i * STEP_TILES + u) * N_EXPERTS + e
            ln = pl.multiple_of(run_len_ref[g], ROW_TILES)

            @pl.when(ln > 0)
            def _():
                src = pl.multiple_of(run_buf_ref[g], ROW_TILES)
                dst = pl.multiple_of(run_xs_ref[g], ROW_TILES)
                pltpu.make_async_copy(sbuf.at[u, pl.ds(src, ln), :], xs_hbm.at[pl.ds(dst, ln), :],
                                      sem.at[u]).start()
            return carry

        lax.fori_loop(0, N_EXPERTS, body, 0)

    @pl.when(i == n - 1)
    def _():
        for u in range(STEP_TILES):
            wait_runs(u)
        zero_fill(wait=True)


def _dispatch_call(run_xs, run_buf, run_len, pad_lo, pad_n, route, h2, n_rows):
    t, d = h2.shape
    td = SORT_TILE * STEP_TILES
    grid_spec = pltpu.PrefetchScalarGridSpec(
        num_scalar_prefetch=5,
        grid=(t // td,),
        in_specs=[pl.BlockSpec((ROUTE_ROWS, td), lambda i, *_: (0, i)),
                  pl.BlockSpec((td, d), lambda i, *_: (i, 0))],
        out_specs=pl.BlockSpec(memory_space=pl.ANY),
        scratch_shapes=[pltpu.VMEM((STEP_TILES, TOP_K * SORT_TILE * ROW_TILES, LANES), U32),
                        pltpu.VMEM((MOE_BLOCK * ROW_TILES, LANES), U32),
                        pltpu.SemaphoreType.DMA((STEP_TILES,)),
                        pltpu.SemaphoreType.DMA((1,))],
    )
    return pl.pallas_call(
        _dispatch_kernel,
        grid_spec=grid_spec,
        out_shape=jax.ShapeDtypeStruct((n_rows * ROW_TILES, LANES), U32),
        compiler_params=pltpu.CompilerParams(
            dimension_semantics=("arbitrary",), vmem_limit_bytes=VMEM_LIMIT,
            has_side_effects=True),
        name="dispatch",
    )(run_xs, run_buf, run_len, pad_lo, pad_n, route, h2)


def _experts_kernel(bexp_ref, first_ref, nxt_ref, nval_ref, nreal_ref,
                    xs_ref, wgu_hbm, bgu_ref, wd_hbm, bd_ref,
                    ys_ref,
                    xb_ref, act_ref, wgu_stage, wd_stage, wgu_bf, wd_bf, wsem):
    i = pl.program_id(0)
    nreal = nreal_ref[0]
    bm = xb_ref.shape[0]

    def weight_copies(e):
        return (pltpu.make_async_copy(wgu_hbm.at[e], wgu_stage, wsem.at[0]),
                pltpu.make_async_copy(wd_hbm.at[e], wd_stage, wsem.at[1]))

    @pl.when(i >= nreal)
    def _():
        ys_ref[...] = jnp.zeros_like(ys_ref)

    @pl.when(i < nreal)
    def _():
        e = bexp_ref[i]

        @pl.when(i == 0)
        def _():
            for cp in weight_copies(e):
                cp.start(priority=1)

        @pl.when(first_ref[i] == 1)
        def _():
            for cp in weight_copies(e):
                cp.wait()

            def cast_gu(r, carry):
                rows = pl.ds(pl.multiple_of(r * CAST_ROWS, CAST_ROWS), CAST_ROWS)
                wgu_bf[rows, :] = wgu_stage[rows, :].astype(BF16)
                return carry

            def cast_d(r, carry):
                rows = pl.ds(pl.multiple_of(r * CAST_ROWS, CAST_ROWS), CAST_ROWS)
                wd_bf[rows, :] = wd_stage[rows, :].astype(BF16)
                return carry

            lax.fori_loop(0, D_MODEL // CAST_ROWS, cast_gu, 0)
            lax.fori_loop(0, D_FF // CAST_ROWS, cast_d, 0)

            @pl.when(nxt_ref[i] >= 0)
            def _():
                for cp in weight_copies(nxt_ref[i]):
                    cp.start(priority=1)

        def mlp(rows):
            for j in range(ROW_TILES):
                lo, hi = _unpack_words(xs_ref[pl.ds(j, rows, stride=ROW_TILES), :])
                xb_ref[0:rows, j * LANES:(j + 1) * LANES] = lo
                xb_ref[0:rows, D_MODEL // 2 + j * LANES:D_MODEL // 2 + (j + 1) * LANES] = hi
            for c in range(D_FF // FF_CHUNK):
                f0 = c * FF_CHUNK
                xb = xb_ref[0:rows, :]
                gate = _dot(xb, wgu_bf[:, f0:f0 + FF_CHUNK]) + bgu_ref[0, :, f0:f0 + FF_CHUNK]
                up = (_dot(xb, wgu_bf[:, D_FF + f0:D_FF + f0 + FF_CHUNK])
                      + bgu_ref[0, :, D_FF + f0:D_FF + f0 + FF_CHUNK])
                gate = jnp.minimum(gate, SWIGLU_LIMIT)
                up = jnp.clip(up, -SWIGLU_LIMIT, SWIGLU_LIMIT)
                glu = gate * jax.nn.sigmoid(gate * SWIGLU_ALPHA)
                act_ref[0:rows, f0:f0 + FF_CHUNK] = ((up + 1.0) * glu).astype(BF16)
            out = _pack_rows(_dot(act_ref[0:rows, :], wd_bf[...]) + bd_ref[0])
            for j in range(ROW_TILES):
                ys_ref[pl.ds(j, rows, stride=ROW_TILES), :] = out[:, j * LANES:(j + 1) * LANES]

        n_routed = nval_ref[i]
        for rows in range(TAIL_ROWS, bm + 1, TAIL_ROWS):
            @pl.when((n_routed > rows - TAIL_ROWS) & (n_routed <= rows))
            def _(rows=rows):
                mlp(rows)
                if rows < bm:
                    ys_ref[rows * ROW_TILES:bm * ROW_TILES, :] = jnp.zeros(
                        ((bm - rows) * ROW_TILES, LANES), U32)


def _experts_call(bexp, first, nxt, nval, nreal, xs, wgu, bgu, wd, bd):
    bm = MOE_BLOCK
    n_blocks = xs.shape[0] // (bm * ROW_TILES)
    bgu3 = bgu.reshape(N_EXPERTS, 1, 2 * D_FF)
    bd3 = bd.reshape(N_EXPERTS, 1, D_MODEL)
    grid_spec = pltpu.PrefetchScalarGridSpec(
        num_scalar_prefetch=5,
        grid=(n_blocks,),
        in_specs=[
            pl.BlockSpec((bm * ROW_TILES, LANES),
                         lambda i, be, fi, nx, nv, nr: (jnp.minimum(i, nr[0] - 1), 0)),
            pl.BlockSpec(memory_space=pl.ANY),
            pl.BlockSpec((1, 1, 2 * D_FF), lambda i, be, fi, nx, nv, nr: (be[i], 0, 0)),
            pl.BlockSpec(memory_space=pl.ANY),
            pl.BlockSpec((1, 1, D_MODEL), lambda i, be, fi, nx, nv, nr: (be[i], 0, 0)),
        ],
        out_specs=pl.BlockSpec((bm * ROW_TILES, LANES), lambda i, be, fi, nx, nv, nr: (i, 0)),
        scratch_shapes=[pltpu.VMEM((bm, D_MODEL), BF16),
                        pltpu.VMEM((bm, D_FF), BF16),
                        pltpu.VMEM((D_MODEL, 2 * D_FF), F32),
                        pltpu.VMEM((D_FF, D_MODEL), F32),
                        pltpu.VMEM((D_MODEL, 2 * D_FF), BF16),
                        pltpu.VMEM((D_FF, D_MODEL), BF16),
                        pltpu.SemaphoreType.DMA((2,))],
    )
    return pl.pallas_call(
        _experts_kernel,
        grid_spec=grid_spec,
        out_shape=jax.ShapeDtypeStruct(xs.shape, U32),
        compiler_params=pltpu.CompilerParams(
            dimension_semantics=("arbitrary",), vmem_limit_bytes=VMEM_LIMIT),
        name="experts",
    )(bexp, first, nxt, nval, nreal, xs, wgu, bgu3, wd, bd3)


def _combine_kernel(src_ref, dst_ref, len_ref, x1_ref, meta_ref, g_ref, ys_hbm, out_ref,
                    ybuf, ysort_ref, sem):
    i = pl.program_id(0)
    n = pl.num_programs(0)
    tb = SORT_TILE
    n_rows = TOP_K * tb
    slot = i % 2

    def start_runs(step, buf_slot):
        for u in range(COMBINE_TILES):
            def body(e, carry, u=u):
                g = (step * COMBINE_TILES + u) * N_EXPERTS + e
                ln = pl.multiple_of(len_ref[g], ROW_TILES)

                @pl.when(ln > 0)
                def _():
                    src = pl.multiple_of(src_ref[g], ROW_TILES)
                    dst = pl.multiple_of(dst_ref[g], ROW_TILES)
                    pltpu.make_async_copy(ys_hbm.at[pl.ds(src, ln), :],
                                          ybuf.at[buf_slot, u, pl.ds(dst, ln), :],
                                          sem.at[buf_slot]).start()
                return carry
            lax.fori_loop(0, N_EXPERTS, body, 0)

    @pl.when(i == 0)
    def _():
        start_runs(0, 0)

    @pl.when(i + 1 < n)
    def _():
        start_runs(i + 1, 1 - slot)

    for u in range(COMBINE_TILES):
        pltpu.make_async_copy(ys_hbm.at[pl.ds(0, n_rows * ROW_TILES), :], ybuf.at[slot, u],
                              sem.at[slot]).wait()

    for u in range(COMBINE_TILES):
        rs = slice(u * tb, (u + 1) * tb)
        for j in range(ROW_TILES):
            lo, hi = _unpack_words(ybuf[slot, u, pl.ds(j, n_rows, stride=ROW_TILES), :])
            ysort_ref[u, :, j * LANES:(j + 1) * LANES] = lo
            ysort_ref[u, :, D_MODEL // 2 + j * LANES:D_MODEL // 2 + (j + 1) * LANES] = hi
        meta = meta_ref[rs, :]
        col = lax.broadcasted_iota(I32, (tb, n_rows), 1).astype(F32)
        weights = jnp.zeros((tb, n_rows), F32)
        for kk in range(TOP_K):
            weights = jnp.where(col == meta[:, 3 * TOP_K + kk:3 * TOP_K + kk + 1],
                                meta[:, TOP_K + kk:TOP_K + kk + 1], weights)
        acc = x1_ref[rs, :] + _dot(weights.astype(BF16), ysort_ref[u])
        out_ref[rs, :] = _rms(acc, g_ref[...])


def _combine_call(seg_src, seg_dst, seg_len, x1, meta, g, ys):
    t, d = x1.shape
    tb = SORT_TILE * COMBINE_TILES
    nb = t // tb
    grid_spec = pltpu.PrefetchScalarGridSpec(
        num_scalar_prefetch=3,
        grid=(nb,),
        in_specs=[pl.BlockSpec((tb, d), lambda i, a, b, c: (i, 0)),
                  pl.BlockSpec((tb, LANES), lambda i, a, b, c: (i, 0)),
                  pl.BlockSpec((1, d), lambda i, a, b, c: (0, 0)),
                  pl.BlockSpec(memory_space=pl.ANY)],
        out_specs=pl.BlockSpec((tb, d), lambda i, a, b, c: (i, 0)),
        scratch_shapes=[pltpu.VMEM((2, COMBINE_TILES, TOP_K * SORT_TILE * ROW_TILES, LANES), U32),
                        pltpu.VMEM((COMBINE_TILES, TOP_K * SORT_TILE, D_MODEL), BF16),
                        pltpu.SemaphoreType.DMA((2,))],
    )
    return pl.pallas_call(
        _combine_kernel,
        grid_spec=grid_spec,
        out_shape=jax.ShapeDtypeStruct((t, d), F32),
        compiler_params=pltpu.CompilerParams(
            dimension_semantics=("arbitrary",), vmem_limit_bytes=VMEM_LIMIT),
        name="combine",
    )(seg_src, seg_dst, seg_len, x1, meta, g, ys)


def _routing_tables(counts, t):
    bm = MOE_BLOCK
    n_blocks = t * TOP_K // bm + N_EXPERTS
    eids = jnp.arange(N_EXPERTS, dtype=I32)
    nblk_e = (counts + bm - 1) // bm
    blk_end = jnp.sum(jnp.where(eids[None, :] <= eids[:, None], nblk_e[None, :], 0), axis=1)
    blk_start = blk_end - nblk_e
    nreal = blk_end[N_EXPERTS - 1]
    pad_start = blk_start * bm
    blk = jnp.arange(n_blocks, dtype=I32)
    bexp = jnp.minimum(jnp.sum((blk_end[None, :] <= blk[:, None]).astype(I32), axis=1),
                       N_EXPERTS - 1)
    blk_is_e = bexp[:, None] == eids[None, :]
    pick = lambda tab: jnp.sum(jnp.where(blk_is_e, tab[None, :], 0), axis=1)
    first = (blk == pick(blk_start)).astype(I32)
    nxt_e = jnp.sum((blk_end[None, :] <= blk_end[:, None]).astype(I32), axis=1)
    nxt_e = jnp.where(blk_end < nreal, jnp.minimum(nxt_e, N_EXPERTS - 1), -1)
    nxt = pick(nxt_e)
    nval = jnp.clip(pick(counts) - (blk - pick(blk_start)) * bm, 0, bm)
    tail_blk = jnp.arange(N_EXPERTS, dtype=I32) + nreal
    pad_lo = jnp.concatenate([pad_start + counts, jnp.minimum(tail_blk, n_blocks - 1) * bm])
    pad_n = jnp.concatenate([nblk_e * bm - counts, jnp.where(tail_blk < n_blocks, bm, 0)])
    return (pad_start, bexp, first, nxt, nval, nreal.reshape(1).astype(I32), n_blocks,
            pad_lo * ROW_TILES, pad_n * ROW_TILES)


def kernel(x, norm_mix_g, w_in, conv_w, w_gk_up, b_gk_up, gla_norm_g, w_out, norm_ffn_g,
           w_router, b_router, w_gate_up, b_gate_up, w_down, b_down, norm_final_g):
    bsz, seq, d = x.shape
    t = bsz * seq
    assert w_in.shape[0] == 1, "single-layer trunk only"
    assert conv_w.shape[1] == CONV_K and ROUTE_ROWS % SUBLANES == 0
    l = 0
    d_in = w_in.shape[-1]
    win = jnp.pad(w_in[l].astype(BF16), ((0, 0), (0, D_IN_PAD - d_in)))
    wgk = jnp.pad(w_gk_up[l], ((0, LANES - GLA_RANK), (0, 0))).astype(BF16)
    wr = jnp.pad(w_router[l], ((0, 0), (0, LANES - N_EXPERTS)))
    wrh = wr.astype(BF16)
    wrhl = jnp.concatenate([wrh, (wr - wrh.astype(F32)).astype(BF16)], axis=1)
    br = jnp.pad(b_router[l], (0, LANES - N_EXPERTS), constant_values=NEG_BIG).reshape(1, LANES)

    x1, h2, meta, route, seg, cnt = _mixer_call(
        x, norm_mix_g[l].reshape(1, d), win, conv_w[l], wgk, b_gk_up[l].reshape(1, GLA_QK),
        gla_norm_g[l].reshape(1, GLA_DV), w_out[l].astype(BF16), norm_ffn_g[l].reshape(1, d),
        wrhl, br)

    meta2 = meta.reshape(t, LANES)
    counts = cnt[:, 0].astype(I32)
    (pad_start, bexp, first, nxt, nval, nreal, n_blocks, pad_lo,
     pad_n) = _routing_tables(counts, t)
    seg_src = ((pad_start[None, :] + seg[:, :, 0].astype(I32)) * ROW_TILES).reshape(-1)
    seg_len = (seg[:, :, 1].astype(I32) * ROW_TILES).reshape(-1)
    seg_dst = (seg[:, :, 2].astype(I32) * ROW_TILES).reshape(-1)
    xs = _dispatch_call(seg_src, seg_dst, seg_len, pad_lo, pad_n, route, h2.reshape(t, d),
                        n_blocks * MOE_BLOCK)
    ys = _experts_call(bexp, first, nxt, nval, nreal, xs, w_gate_up[l], b_gate_up[l], w_down[l],
                       b_down[l])
    out = _combine_call(seg_src, seg_dst, seg_len, x1.reshape(t, d), meta2,
                        norm_final_g.reshape(1, d), ys)
    return out.reshape(bsz, seq, d)
```

```python
import jax
import jax.numpy as jnp
from jax import lax
from jax.experimental import pallas as pl
from jax.experimental.pallas import tpu as pltpu

F32 = jnp.float32
BF16 = jnp.bfloat16
I32 = jnp.int32
U32 = jnp.uint32

D_MODEL = 1024
CONV_WIDTH = 512
CONV_K = 3
GLA_WIDTH = 512
GLA_HEADS = 4
GLA_DV = 128
GLA_DK = 64
GLA_QK = GLA_HEADS * GLA_DK
GLA_RANK = 16
GLA_NORMALIZER = 16.0
GLA_CHUNK = 64
N_EXPERTS = 32
TOP_K = 4
D_FF = 1024
SWIGLU_LIMIT = 7.0
SWIGLU_ALPHA = 1.702
RMS_EPS = 1e-5

LANES = 128
SUBLANES = 8
ROW_TILES = D_MODEL // LANES // 2
ROUTE_ROWS = 4 * TOP_K

OFF_UH = 0
OFF_GB = OFF_UH + CONV_WIDTH
OFF_GC = OFF_GB + CONV_WIDTH
OFF_Q = OFF_GC + CONV_WIDTH
OFF_K = OFF_Q + GLA_QK
OFF_V = OFF_K + GLA_QK
OFF_GO = OFF_V + GLA_WIDTH
OFF_GKL = OFF_GO + GLA_WIDTH
D_IN_PAD = OFF_GKL + LANES

SEQ_TILE = 512
SORT_TILE = 256
STEP_TILES = 4
COMBINE_TILES = 4
MOE_BLOCK = 1024
FF_CHUNK = 256
TAIL_ROWS = 128
CAST_ROWS = 128
NEG_BIG = -1e30
VMEM_LIMIT = 56 * 1024 * 1024


def _rms(x, g):
    return x * lax.rsqrt(jnp.mean(x * x, axis=-1, keepdims=True) + RMS_EPS) * g


def _dot(a, b):
    return jnp.dot(a, b, preferred_element_type=F32)


def _dot_nt(a, b):
    return lax.dot_general(a, b, (((1,), (1,)), ((), ())), preferred_element_type=F32)


def _pack_rows(x):
    half = x.shape[1] // 2
    xr = x.astype(BF16).astype(F32)
    lo = lax.bitcast_convert_type(xr[:, :half], U32) >> 16
    hi = lax.bitcast_convert_type(xr[:, half:], U32) & jnp.uint32(0xFFFF0000)
    return hi | lo


def _unpack_words(w):
    lo = lax.bitcast_convert_type(w << 16, F32).astype(BF16)
    hi = lax.bitcast_convert_type(w & jnp.uint32(0xFFFF0000), F32).astype(BF16)
    return lo, hi


def _split_bf16(x):
    hi = x.astype(BF16)
    lo = (x - hi.astype(F32)).astype(BF16)
    return hi, lo


def _mixer_kernel(x_ref, g1_ref, win_ref, convw_ref, wgk_ref, bgk_ref, gng_ref, wout_ref,
                  g2_ref, wrhl_ref, br_ref,
                  x1_ref, h2_ref, meta_ref, route_ref, seg_ref, cnt_ref,
                  proj_ref, ubuf_ref, la_ref, ycat_ref, state_ref, carry_ref):
    ts = x_ref.shape[1]
    b_idx = pl.program_id(0)
    s_idx = pl.program_id(1)

    @pl.when(s_idx == 0)
    def _():
        state_ref[...] = jnp.zeros_like(state_ref)
        ubuf_ref[0:SUBLANES, :] = jnp.zeros((SUBLANES, CONV_WIDTH), F32)

    @pl.when((s_idx == 0) & (b_idx == 0))
    def _():
        carry_ref[...] = jnp.zeros_like(carry_ref)

    x = x_ref[0]
    h = _rms(x, g1_ref[...]).astype(BF16)
    proj_ref[...] = _dot(h, win_ref[...])

    u = proj_ref[:, OFF_GC:OFF_GC + CONV_WIDTH] * proj_ref[:, OFF_UH:OFF_UH + CONV_WIDTH]
    ubuf_ref[SUBLANES:SUBLANES + ts, :] = u
    u1 = ubuf_ref[pl.ds(SUBLANES - 1, ts), :]
    u2 = ubuf_ref[pl.ds(SUBLANES - 2, ts), :]
    conv = convw_ref[0:1, :] * u2 + convw_ref[1:2, :] * u1 + convw_ref[2:3, :] * u
    ycat_ref[:, 0:CONV_WIDTH] = (proj_ref[:, OFF_GB:OFF_GB + CONV_WIDTH] * conv).astype(BF16)
    ubuf_ref[0:SUBLANES, :] = ubuf_ref[ts:ts + SUBLANES, :]

    gk = _dot(proj_ref[:, OFF_GKL:OFF_GKL + LANES].astype(BF16), wgk_ref[...]) + bgk_ref[...]
    log_sig = jnp.minimum(gk, 0.0) - jnp.log1p(jnp.exp(-jnp.abs(gk)))
    la_ref[...] = log_sig / GLA_NORMALIZER

    ci = lax.broadcasted_iota(I32, (GLA_CHUNK, GLA_CHUNK), 0)
    cj = lax.broadcasted_iota(I32, (GLA_CHUNK, GLA_CHUNK), 1)
    tri_incl = (cj <= ci).astype(BF16)
    causal = cj <= ci
    causal4 = jnp.concatenate([causal] * GLA_HEADS, axis=0)
    lane_qk = lax.broadcasted_iota(I32, (1, GLA_QK), 1)
    head_masks = [((lane_qk >= hd * GLA_DK) & (lane_qk < (hd + 1) * GLA_DK)).astype(F32)
                  for hd in range(GLA_HEADS)]
    gng = gng_ref[...]

    n_chunks = ts // GLA_CHUNK
    chunk_rows = [pl.ds(c * GLA_CHUNK, GLA_CHUNK) for c in range(n_chunks)]
    lane_c = lax.broadcasted_iota(I32, (GLA_QK, 2 * GLA_CHUNK), 1)
    qd_all, kd_all, kr_all, bl_all, v_all = [], [], [], [], []
    for rows in chunk_rows:
        la_hi, la_lo = _split_bf16(la_ref[rows, :])
        bcum = _dot(tri_incl, la_hi) + _dot(tri_incl, la_lo)
        blast = bcum[GLA_CHUNK - 1:GLA_CHUNK, :]
        q = proj_ref[rows, OFF_Q:OFF_Q + GLA_QK] * (GLA_DK ** -0.5)
        k = proj_ref[rows, OFF_K:OFF_K + GLA_QK]
        qd_all.append(q * jnp.exp(bcum))
        kd_all.append((k * jnp.exp(-bcum)).astype(BF16))
        kr_all.append(k * jnp.exp(blast - bcum))
        bl_all.append(blast)
        v_all.append(proj_ref[rows, OFF_V:OFF_V + GLA_WIDTH].astype(BF16))

    scores_all = []
    for c in range(n_chunks):
        q_stack = jnp.concatenate([qd_all[c] * m for m in head_masks], axis=0).astype(BF16)
        scores_all.append(
            jnp.where(causal4, _dot_nt(q_stack, kd_all[c]), 0.0).astype(BF16))
    o_intra_all = []
    for c in range(n_chunks):
        o_intra_all.append(jnp.concatenate(
            [_dot(scores_all[c][hd * GLA_CHUNK:(hd + 1) * GLA_CHUNK, :],
                  v_all[c][:, hd * GLA_DV:(hd + 1) * GLA_DV]) for hd in range(GLA_HEADS)], axis=1))
    kv_all, dcol_all = [], []
    for c in range(n_chunks):
        kt = jnp.concatenate(
            [kr_all[c], jnp.broadcast_to(bl_all[c], (GLA_CHUNK, GLA_QK))], axis=0).T
        dcol_all.append(jnp.exp(kt[:, GLA_CHUNK:GLA_CHUNK + 1]))
        kt_b = jnp.where(lane_c < GLA_CHUNK, kt, 0.0).astype(BF16)
        v_pad = jnp.concatenate([v_all[c], jnp.zeros_like(v_all[c])], axis=0)
        kv_all.append([_dot(kt_b[hd * GLA_DK:(hd + 1) * GLA_DK, :],
                            v_pad[:, hd * GLA_DV:(hd + 1) * GLA_DV]) for hd in range(GLA_HEADS)])

    o_all = []
    for c in range(n_chunks):
        state = state_ref[...]
        o_all.append(_dot(qd_all[c].astype(BF16), state.astype(BF16)) + o_intra_all[c])
        for hd in range(GLA_HEADS):
            rs = slice(hd * GLA_DK, (hd + 1) * GLA_DK)
            cs = slice(hd * GLA_DV, (hd + 1) * GLA_DV)
            state_ref[rs, cs] = dcol_all[c][rs, :] * state[rs, cs] + kv_all[c][hd]

    for c, rows in enumerate(chunk_rows):
        o = o_all[c]
        g_out = proj_ref[rows, OFF_GO:OFF_GO + GLA_WIDTH]
        o_n = jnp.concatenate(
            [_rms(o[:, hd * GLA_DV:(hd + 1) * GLA_DV], gng) for hd in range(GLA_HEADS)], axis=1)
        y = o_n * (g_out * jax.nn.sigmoid(g_out))
        ycat_ref[rows, CONV_WIDTH:CONV_WIDTH + GLA_WIDTH] = y.astype(BF16)

    x1 = x + _dot(ycat_ref[...], wout_ref[...])
    x1_ref[0] = x1
    h2 = _rms(x1, g2_ref[...])
    h2_hi, h2_lo = _split_bf16(h2)
    h2_ref[0] = h2_hi
    hi_terms = _dot(h2_hi, wrhl_ref[...])
    logits = (hi_terms[:, 0:LANES] + hi_terms[:, LANES:2 * LANES]
              + _dot(h2_lo, wrhl_ref[:, 0:LANES]) + br_ref[...])

    lt = logits.T[0:N_EXPERTS, :]
    erow = lax.broadcasted_iota(I32, (N_EXPERTS, ts), 0).astype(F32)
    work = lt
    sel = jnp.zeros((N_EXPERTS, ts), F32)
    top_v, top_i, top_oh = [], [], []
    for _ in range(TOP_K):
        m = jnp.max(work, axis=0, keepdims=True)
        idx = jnp.min(jnp.where(work == m, erow, float(N_EXPERTS)), axis=0, keepdims=True)
        oh = erow == idx
        top_v.append(m)
        top_i.append(idx)
        top_oh.append(oh)
        sel = sel + oh.astype(F32)
        work = jnp.where(oh, -jnp.inf, work)
    exps = [jnp.exp(tv - top_v[0]) for tv in top_v]
    denom = exps[0] + exps[1] + exps[2] + exps[3]
    gates = [e / denom for e in exps]

    tsrc = lax.broadcasted_iota(I32, (ts, ts), 0)
    tdst = lax.broadcasted_iota(I32, (ts, ts), 1)
    earlier = (tsrc < tdst).astype(BF16)
    local = _dot(sel.astype(BF16), earlier)
    carry = carry_ref[:, 0:1]
    ranks = [jnp.sum(jnp.where(oh, local + carry, 0.0), axis=0, keepdims=True) for oh in top_oh]

    ei = lax.broadcasted_iota(I32, (N_EXPERTS, N_EXPERTS), 0)
    ej = lax.broadcasted_iota(I32, (N_EXPERTS, N_EXPERTS), 1)
    lower_experts = (ej < ei).astype(BF16)
    lane_f = lax.broadcasted_iota(I32, (N_EXPERTS, LANES), 1)
    slot_base = []
    before = jnp.zeros((N_EXPERTS, 1), F32)
    for u in range(ts // SORT_TILE):
        sub_cnt = jnp.sum(sel[:, u * SORT_TILE:(u + 1) * SORT_TILE], axis=1, keepdims=True)
        seg_start = _dot(lower_experts,
                         jnp.broadcast_to(sub_cnt, (N_EXPERTS, LANES)).astype(BF16))[:, 0:1]
        seg_ref[u] = jnp.where(lane_f == 0, carry + before,
                               jnp.where(lane_f == 1, sub_cnt,
                                         jnp.where(lane_f == 2, seg_start, 0.0)))
        slot_base.append(jnp.broadcast_to(seg_start - before, (N_EXPERTS, SORT_TILE)))
        before = before + sub_cnt
    slot_base = jnp.concatenate(slot_base, axis=1)
    slots = [jnp.sum(jnp.where(oh, local + slot_base, 0.0), axis=0, keepdims=True)
             for oh in top_oh]
    new_carry = carry + before
    carry_ref[...] = jnp.broadcast_to(new_carry, carry_ref.shape)
    cnt_ref[...] = jnp.broadcast_to(new_carry, cnt_ref.shape)

    route = jnp.concatenate(top_i + gates + ranks + slots, axis=0)
    route_ref[...] = route
    meta_ref[0] = jnp.concatenate(
        [route, jnp.zeros((LANES - ROUTE_ROWS, ts), F32)], axis=0).T


def _mixer_call(x, g1, win, convw, wgk, bgk, gng, wout, g2, wrhl, br):
    bsz, seq, d = x.shape
    ts = SEQ_TILE
    grid = (bsz, seq // ts)

    def const(shape):
        return pl.BlockSpec(shape, lambda b, s: (0,) * len(shape))

    tile = lambda w: pl.BlockSpec((1, ts, w), lambda b, s: (b, s, 0))
    return pl.pallas_call(
        _mixer_kernel,
        grid=grid,
        in_specs=[tile(d), const(g1.shape), const(win.shape), const(convw.shape),
                  const(wgk.shape), const(bgk.shape), const(gng.shape), const(wout.shape),
                  const(g2.shape), const(wrhl.shape), const(br.shape)],
        out_specs=[tile(d),
                   tile(d),
                   tile(LANES),
                   pl.BlockSpec((ROUTE_ROWS, ts), lambda b, s: (0, b * (seq // ts) + s)),
                   pl.BlockSpec((ts // SORT_TILE, N_EXPERTS, LANES),
                                lambda b, s: (b * (seq // ts) + s, 0, 0)),
                   const((N_EXPERTS, LANES))],
        out_shape=[jax.ShapeDtypeStruct((bsz, seq, d), F32),
                   jax.ShapeDtypeStruct((bsz, seq, d), BF16),
                   jax.ShapeDtypeStruct((bsz, seq, LANES), F32),
                   jax.ShapeDtypeStruct((ROUTE_ROWS, bsz * seq), F32),
                   jax.ShapeDtypeStruct((bsz * seq // SORT_TILE, N_EXPERTS, LANES), F32),
                   jax.ShapeDtypeStruct((N_EXPERTS, LANES), F32)],
        scratch_shapes=[pltpu.VMEM((ts, D_IN_PAD), F32),
                        pltpu.VMEM((ts + SUBLANES, CONV_WIDTH), F32),
                        pltpu.VMEM((ts, GLA_QK), F32),
                        pltpu.VMEM((ts, D_MODEL), BF16),
                        pltpu.VMEM((GLA_QK, GLA_WIDTH), F32),
                        pltpu.VMEM((N_EXPERTS, LANES), F32)],
        compiler_params=pltpu.CompilerParams(
            dimension_semantics=("arbitrary", "arbitrary"), vmem_limit_bytes=VMEM_LIMIT),
        name="mixer",
    )(x, g1, win, convw, wgk, bgk, gng, wout, g2, wrhl, br)


def _dispatch_kernel(run_xs_ref, run_buf_ref, run_len_ref, pad_lo_ref, pad_n_ref,
                     route_ref, h2_ref, xs_hbm, sbuf, zbuf, sem, zsem):
    i = pl.program_id(0)
    n = pl.num_programs(0)
    td = SORT_TILE
    n_rows = TOP_K * td

    def zero_fill(wait):
        def fill(g, carry):
            ln = pl.multiple_of(pad_n_ref[g], ROW_TILES)

            @pl.when(ln > 0)
            def _():
                lo = pl.multiple_of(pad_lo_ref[g], ROW_TILES)
                cp = pltpu.make_async_copy(zbuf.at[pl.ds(0, ln), :], xs_hbm.at[pl.ds(lo, ln), :],
                                           zsem.at[0])
                if wait:
                    cp.wait()
                else:
                    cp.start()
            return carry

        lax.fori_loop(0, pad_lo_ref.shape[0], fill, 0)

    def wait_runs(buf_slot):
        pltpu.make_async_copy(sbuf.at[buf_slot], xs_hbm.at[pl.ds(0, n_rows * ROW_TILES), :],
                              sem.at[buf_slot]).wait()

    @pl.when(i == 0)
    def _():
        zbuf[...] = jnp.zeros_like(zbuf)
        zero_fill(wait=False)

    for u in range(STEP_TILES):
        @pl.when(i >= 1)
        def _(u=u):
            wait_runs(u)

        route = route_ref[:, u * td:(u + 1) * td]
        row = lax.broadcasted_iota(I32, (n_rows, td), 0).astype(F32)
        pick = jnp.zeros((n_rows, td), F32)
        for kk in range(TOP_K):
            pick = jnp.where(row == route[3 * TOP_K + kk:3 * TOP_K + kk + 1, :], 1.0, pick)
        rows_sorted = _dot(pick.astype(BF16), h2_ref[u * td:(u + 1) * td, :])
        words = _pack_rows(rows_sorted)
        for j in range(ROW_TILES):
            sbuf[u, pl.ds(j, n_rows, stride=ROW_TILES), :] = words[:, j * LANES:(j + 1) * LANES]

        def body(e, carry, u=u):
            g = (i * STEP_TILES + u) * N_EXPERTS + e
            ln = pl.multiple_of(run_len_ref[g], ROW_TILES)

            @pl.when(ln > 0)
            def _():
                src = pl.multiple_of(run_buf_ref[g], ROW_TILES)
                dst = pl.multiple_of(run_xs_ref[g], ROW_TILES)
                pltpu.make_async_copy(sbuf.at[u, pl.ds(src, ln), :], xs_hbm.at[pl.ds(dst, ln), :],
                                      sem.at[u]).start()
            return carry

        lax.fori_loop(0, N_EXPERTS, body, 0)

    @pl.when(i == n - 1)
    def _():
        for u in range(STEP_TILES):
            wait_runs(u)
        zero_fill(wait=True)


def _dispatch_call(run_xs, run_buf, run_len, pad_lo, pad_n, route, h2, n_rows):
    t, d = h2.shape
    td = SORT_TILE * STEP_TILES
    grid_spec = pltpu.PrefetchScalarGridSpec(
        num_scalar_prefetch=5,
        grid=(t // td,),
        in_specs=[pl.BlockSpec((ROUTE_ROWS, td), lambda i, *_: (0, i)),
                  pl.BlockSpec((td, d), lambda i, *_: (i, 0))],
        out_specs=pl.BlockSpec(memory_space=pl.ANY),
        scratch_shapes=[pltpu.VMEM((STEP_TILES, TOP_K * SORT_TILE * ROW_TILES, LANES), U32),
                        pltpu.VMEM((MOE_BLOCK * ROW_TILES, LANES), U32),
                        pltpu.SemaphoreType.DMA((STEP_TILES,)),
                        pltpu.SemaphoreType.DMA((1,))],
    )
    return pl.pallas_call(
        _dispatch_kernel,
        grid_spec=grid_spec,
        out_shape=jax.ShapeDtypeStruct((n_rows * ROW_TILES, LANES), U32),
        compiler_params=pltpu.CompilerParams(
            dimension_semantics=("arbitrary",), vmem_limit_bytes=VMEM_LIMIT,
            has_side_effects=True),
        name="dispatch",
    )(run_xs, run_buf, run_len, pad_lo, pad_n, route, h2)


def _experts_kernel(bexp_ref, first_ref, nxt_ref, nval_ref, nreal_ref,
                    xs_ref, wgu_hbm, bgu_ref, wd_hbm, bd_ref,
                    ys_ref,
                    xb_ref, act_ref, wgu_stage, wd_stage, wgu_bf, wd_bf, wsem):
    i = pl.program_id(0)
    nreal = nreal_ref[0]
    bm = xb_ref.shape[0]

    def weight_copies(e):
        return (pltpu.make_async_copy(wgu_hbm.at[e], wgu_stage, wsem.at[0]),
                pltpu.make_async_copy(wd_hbm.at[e], wd_stage, wsem.at[1]))

    @pl.when(i >= nreal)
    def _():
        ys_ref[...] = jnp.zeros_like(ys_ref)

    @pl.when(i < nreal)
    def _():
        e = bexp_ref[i]

        @pl.when(i == 0)
        def _():
            for cp in weight_copies(e):
                cp.start(priority=1)

        @pl.when(first_ref[i] == 1)
        def _():
            for cp in weight_copies(e):
                cp.wait()

            def cast_gu(r, carry):
                rows = pl.ds(pl.multiple_of(r * CAST_ROWS, CAST_ROWS), CAST_ROWS)
                wgu_bf[rows, :] = wgu_stage[rows, :].astype(BF16)
                return carry

            def cast_d(r, carry):
                rows = pl.ds(pl.multiple_of(r * CAST_ROWS, CAST_ROWS), CAST_ROWS)
                wd_bf[rows, :] = wd_stage[rows, :].astype(BF16)
                return carry

            lax.fori_loop(0, D_MODEL // CAST_ROWS, cast_gu, 0)
            lax.fori_loop(0, D_FF // CAST_ROWS, cast_d, 0)

            @pl.when(nxt_ref[i] >= 0)
            def _():
                for cp in weight_copies(nxt_ref[i]):
                    cp.start(priority=1)

        def mlp(rows):
            for j in range(ROW_TILES):
                lo, hi = _unpack_words(xs_ref[pl.ds(j, rows, stride=ROW_TILES), :])
                xb_ref[0:rows, j * LANES:(j + 1) * LANES] = lo
                xb_ref[0:rows, D_MODEL // 2 + j * LANES:D_MODEL // 2 + (j + 1) * LANES] = hi
            for c in range(D_FF // FF_CHUNK):
                f0 = c * FF_CHUNK
                xb = xb_ref[0:rows, :]
                gate = _dot(xb, wgu_bf[:, f0:f0 + FF_CHUNK]) + bgu_ref[0, :, f0:f0 + FF_CHUNK]
                up = (_dot(xb, wgu_bf[:, D_FF + f0:D_FF + f0 + FF_CHUNK])
                      + bgu_ref[0, :, D_FF + f0:D_FF + f0 + FF_CHUNK])
                gate = jnp.minimum(gate, SWIGLU_LIMIT)
                up = jnp.clip(up, -SWIGLU_LIMIT, SWIGLU_LIMIT)
                glu = gate * jax.nn.sigmoid(gate * SWIGLU_ALPHA)
                act_ref[0:rows, f0:f0 + FF_CHUNK] = ((up + 1.0) * glu).astype(BF16)
            out = _pack_rows(_dot(act_ref[0:rows, :], wd_bf[...]) + bd_ref[0])
            for j in range(ROW_TILES):
                ys_ref[pl.ds(j, rows, stride=ROW_TILES), :] = out[:, j * LANES:(j + 1) * LANES]

        n_routed = nval_ref[i]
        for rows in range(TAIL_ROWS, bm + 1, TAIL_ROWS):
            @pl.when((n_routed > rows - TAIL_ROWS) & (n_routed <= rows))
            def _(rows=rows):
                mlp(rows)
                if rows < bm:
                    ys_ref[rows * ROW_TILES:bm * ROW_TILES, :] = jnp.zeros(
                        ((bm - rows) * ROW_TILES, LANES), U32)


def _experts_call(bexp, first, nxt, nval, nreal, xs, wgu, bgu, wd, bd):
    bm = MOE_BLOCK
    n_blocks = xs.shape[0] // (bm * ROW_TILES)
    bgu3 = bgu.reshape(N_EXPERTS, 1, 2 * D_FF)
    bd3 = bd.reshape(N_EXPERTS, 1, D_MODEL)
    grid_spec = pltpu.PrefetchScalarGridSpec(
        num_scalar_prefetch=5,
        grid=(n_blocks,),
        in_specs=[
            pl.BlockSpec((bm * ROW_TILES, LANES),
                         lambda i, be, fi, nx, nv, nr: (jnp.minimum(i, nr[0] - 1), 0)),
            pl.BlockSpec(memory_space=pl.ANY),
            pl.BlockSpec((1, 1, 2 * D_FF), lambda i, be, fi, nx, nv, nr: (be[i], 0, 0)),
            pl.BlockSpec(memory_space=pl.ANY),
            pl.BlockSpec((1, 1, D_MODEL), lambda i, be, fi, nx, nv, nr: (be[i], 0, 0)),
        ],
        out_specs=pl.BlockSpec((bm * ROW_TILES, LANES), lambda i, be, fi, nx, nv, nr: (i, 0)),
        scratch_shapes=[pltpu.VMEM((bm, D_MODEL), BF16),
                        pltpu.VMEM((bm, D_FF), BF16),
                        pltpu.VMEM((D_MODEL, 2 * D_FF), F32),
                        pltpu.VMEM((D_FF, D_MODEL), F32),
                        pltpu.VMEM((D_MODEL, 2 * D_FF), BF16),
                        pltpu.VMEM((D_FF, D_MODEL), BF16),
                        pltpu.SemaphoreType.DMA((2,))],
    )
    return pl.pallas_call(
        _experts_kernel,
        grid_spec=grid_spec,
        out_shape=jax.ShapeDtypeStruct(xs.shape, U32),
        compiler_params=pltpu.CompilerParams(
            dimension_semantics=("arbitrary",), vmem_limit_bytes=VMEM_LIMIT),
        name="experts",
    )(bexp, first, nxt, nval, nreal, xs, wgu, bgu3, wd, bd3)


def _combine_kernel(src_ref, dst_ref, len_ref, x1_ref, meta_ref, g_ref, ys_hbm, out_ref,
                    ybuf, ysort_ref, sem):
    i = pl.program_id(0)
    n = pl.num_programs(0)
    tb = SORT_TILE
    n_rows = TOP_K * tb
    slot = i % 2

    def start_runs(step, buf_slot):
        for u in range(COMBINE_TILES):
            def body(e, carry, u=u):
                g = (step * COMBINE_TILES + u) * N_EXPERTS + e
                ln = pl.multiple_of(len_ref[g], ROW_TILES)

                @pl.when(ln > 0)
                def _():
                    src = pl.multiple_of(src_ref[g], ROW_TILES)
                    dst = pl.multiple_of(dst_ref[g], ROW_TILES)
                    pltpu.make_async_copy(ys_hbm.at[pl.ds(src, ln), :],
                                          ybuf.at[buf_slot, u, pl.ds(dst, ln), :],
                                          sem.at[buf_slot]).start()
                return carry
            lax.fori_loop(0, N_EXPERTS, body, 0)

    @pl.when(i == 0)
    def _():
        start_runs(0, 0)

    @pl.when(i + 1 < n)
    def _():
        start_runs(i + 1, 1 - slot)

    for u in range(COMBINE_TILES):
        pltpu.make_async_copy(ys_hbm.at[pl.ds(0, n_rows * ROW_TILES), :], ybuf.at[slot, u],
                              sem.at[slot]).wait()

    for u in range(COMBINE_TILES):
        rs = slice(u * tb, (u + 1) * tb)
        for j in range(ROW_TILES):
            lo, hi = _unpack_words(ybuf[slot, u, pl.ds(j, n_rows, stride=ROW_TILES), :])
            ysort_ref[u, :, j * LANES:(j + 1) * LANES] = lo
            ysort_ref[u, :, D_MODEL // 2 + j * LANES:D_MODEL // 2 + (j + 1) * LANES] = hi
        meta = meta_ref[rs, :]
        col = lax.broadcasted_iota(I32, (tb, n_rows), 1).astype(F32)
        weights = jnp.zeros((tb, n_rows), F32)
        for kk in range(TOP_K):
            weights = jnp.where(col == meta[:, 3 * TOP_K + kk:3 * TOP_K + kk + 1],
                                meta[:, TOP_K + kk:TOP_K + kk + 1], weights)
        acc = x1_ref[rs, :] + _dot(weights.astype(BF16), ysort_ref[u])
        out_ref[rs, :] = _rms(acc, g_ref[...])


def _combine_call(seg_src, seg_dst, seg_len, x1, meta, g, ys):
    t, d = x1.shape
    tb = SORT_TILE * COMBINE_TILES
    nb = t // tb
    grid_spec = pltpu.PrefetchScalarGridSpec(
        num_scalar_prefetch=3,
        grid=(nb,),
        in_specs=[pl.BlockSpec((tb, d), lambda i, a, b, c: (i, 0)),
                  pl.BlockSpec((tb, LANES), lambda i, a, b, c: (i, 0)),
                  pl.BlockSpec((1, d), lambda i, a, b, c: (0, 0)),
                  pl.BlockSpec(memory_space=pl.ANY)],
        out_specs=pl.BlockSpec((tb, d), lambda i, a, b, c: (i, 0)),
        scratch_shapes=[pltpu.VMEM((2, COMBINE_TILES, TOP_K * SORT_TILE * ROW_TILES, LANES), U32),
                        pltpu.VMEM((COMBINE_TILES, TOP_K * SORT_TILE, D_MODEL), BF16),
                        pltpu.SemaphoreType.DMA((2,))],
    )
    return pl.pallas_call(
        _combine_kernel,
        grid_spec=grid_spec,
        out_shape=jax.ShapeDtypeStruct((t, d), F32),
        compiler_params=pltpu.CompilerParams(
            dimension_semantics=("arbitrary",), vmem_limit_bytes=VMEM_LIMIT),
        name="combine",
    )(seg_src, seg_dst, seg_len, x1, meta, g, ys)


def _routing_tables(counts, t):
    bm = MOE_BLOCK
    n_blocks = t * TOP_K // bm + N_EXPERTS
    eids = jnp.arange(N_EXPERTS, dtype=I32)
    nblk_e = (counts + bm - 1) // bm
    blk_end = jnp.sum(jnp.where(eids[None, :] <= eids[:, None], nblk_e[None, :], 0), axis=1)
    blk_start = blk_end - nblk_e
    nreal = blk_end[N_EXPERTS - 1]
    pad_start = blk_start * bm
    blk = jnp.arange(n_blocks, dtype=I32)
    bexp = jnp.minimum(jnp.sum((blk_end[None, :] <= blk[:, None]).astype(I32), axis=1),
                       N_EXPERTS - 1)
    blk_is_e = bexp[:, None] == eids[None, :]
    pick = lambda tab: jnp.sum(jnp.where(blk_is_e, tab[None, :], 0), axis=1)
    first = (blk == pick(blk_start)).astype(I32)
    nxt_e = jnp.sum((blk_end[None, :] <= blk_end[:, None]).astype(I32), axis=1)
    nxt_e = jnp.where(blk_end < nreal, jnp.minimum(nxt_e, N_EXPERTS - 1), -1)
    nxt = pick(nxt_e)
    nval = jnp.clip(pick(counts) - (blk - pick(blk_start)) * bm, 0, bm)
    tail_blk = jnp.arange(N_EXPERTS, dtype=I32) + nreal
    pad_lo = jnp.concatenate([pad_start + counts, jnp.minimum(tail_blk, n_blocks - 1) * bm])
    pad_n = jnp.concatenate([nblk_e * bm - counts, jnp.where(tail_blk < n_blocks, bm, 0)])
    return (pad_start, bexp, first, nxt, nval, nreal.reshape(1).astype(I32), n_blocks,
            pad_lo * ROW_TILES, pad_n * ROW_TILES)


def kernel(x, norm_mix_g, w_in, conv_w, w_gk_up, b_gk_up, gla_norm_g, w_out, norm_ffn_g,
           w_router, b_router, w_gate_up, b_gate_up, w_down, b_down, norm_final_g):
    bsz, seq, d = x.shape
    t = bsz * seq
    assert w_in.shape[0] == 1, "single-layer trunk only"
    assert conv_w.shape[1] == CONV_K and ROUTE_ROWS % SUBLANES == 0
    l = 0
    d_in = w_in.shape[-1]
    win = jnp.pad(w_in[l].astype(BF16), ((0, 0), (0, D_IN_PAD - d_in)))
    wgk = jnp.pad(w_gk_up[l], ((0, LANES - GLA_RANK), (0, 0))).astype(BF16)
    wr = jnp.pad(w_router[l], ((0, 0), (0, LANES - N_EXPERTS)))
    wrh = wr.astype(BF16)
    wrhl = jnp.concatenate([wrh, (wr - wrh.astype(F32)).astype(BF16)], axis=1)
    br = jnp.pad(b_router[l], (0, LANES - N_EXPERTS), constant_values=NEG_BIG).reshape(1, LANES)

    x1, h2, meta, route, seg, cnt = _mixer_call(
        x, norm_mix_g[l].reshape(1, d), win, conv_w[l], wgk, b_gk_up[l].reshape(1, GLA_QK),
        gla_norm_g[l].reshape(1, GLA_DV), w_out[l].astype(BF16), norm_ffn_g[l].reshape(1, d),
        wrhl, br)

    meta2 = meta.reshape(t, LANES)
    counts = cnt[:, 0].astype(I32)
    (pad_start, bexp, first, nxt, nval, nreal, n_blocks, pad_lo,
     pad_n) = _routing_tables(counts, t)
    seg_src = ((pad_start[None, :] + seg[:, :, 0].astype(I32)) * ROW_TILES).reshape(-1)
    seg_len = (seg[:, :, 1].astype(I32) * ROW_TILES).reshape(-1)
    seg_dst = (seg[:, :, 2].astype(I32) * ROW_TILES).reshape(-1)
    xs = _dispatch_call(seg_src, seg_dst, seg_len, pad_lo, pad_n, route, h2.reshape(t, d),
                        n_blocks * MOE_BLOCK)
    ys = _experts_call(bexp, first, nxt, nval, nreal, xs, w_gate_up[l], b_gate_up[l], w_down[l],
                       b_down[l])
    out = _combine_call(seg_src, seg_dst, seg_len, x1.reshape(t, d), meta2,
                        norm_final_g.reshape(1, d), ys)
    return out.reshape(bsz, seq, d)
```

```python
import jax
import jax.numpy as jnp
from jax import lax
from jax.experimental import pallas as pl
from jax.experimental.pallas import tpu as pltpu

F32 = jnp.float32
BF16 = jnp.bfloat16
I32 = jnp.int32
U32 = jnp.uint32

D_MODEL = 1024
CONV_WIDTH = 512
CONV_K = 3
GLA_WIDTH = 512
GLA_HEADS = 4
GLA_DV = 128
GLA_DK = 64
GLA_QK = GLA_HEADS * GLA_DK
GLA_RANK = 16
GLA_NORMALIZER = 16.0
GLA_CHUNK = 64
N_EXPERTS = 32
TOP_K = 4
D_FF = 1024
SWIGLU_LIMIT = 7.0
SWIGLU_ALPHA = 1.702
RMS_EPS = 1e-5

LANES = 128
SUBLANES = 8
ROW_TILES = D_MODEL // LANES // 2
ROUTE_ROWS = 4 * TOP_K

OFF_UH = 0
OFF_GB = OFF_UH + CONV_WIDTH
OFF_GC = OFF_GB + CONV_WIDTH
OFF_Q = OFF_GC + CONV_WIDTH
OFF_K = OFF_Q + GLA_QK
OFF_V = OFF_K + GLA_QK
OFF_GO = OFF_V + GLA_WIDTH
OFF_GKL = OFF_GO + GLA_WIDTH
D_IN_PAD = OFF_GKL + LANES

SEQ_TILE = 512
SORT_TILE = 256
STEP_TILES = 4
COMBINE_TILES = 4
DMA_QUEUES = 2
MOE_BLOCK = 512
FF_CHUNK = 256
TAIL_ROWS = 128
CAST_ROWS = 128
NEG_BIG = -1e30
VMEM_LIMIT = 56 * 1024 * 1024


def _rms(x, g):
    return x * lax.rsqrt(jnp.mean(x * x, axis=-1, keepdims=True) + RMS_EPS) * g


def _dot(a, b):
    return jnp.dot(a, b, preferred_element_type=F32)


def _dot_nt(a, b):
    return lax.dot_general(a, b, (((1,), (1,)), ((), ())), preferred_element_type=F32)


def _pack_rows(x):
    half = x.shape[1] // 2
    xr = x.astype(BF16).astype(F32)
    lo = lax.bitcast_convert_type(xr[:, :half], U32) >> 16
    hi = lax.bitcast_convert_type(xr[:, half:], U32) & jnp.uint32(0xFFFF0000)
    return hi | lo


def _unpack_words(w):
    lo = lax.bitcast_convert_type(w << 16, F32).astype(BF16)
    hi = lax.bitcast_convert_type(w & jnp.uint32(0xFFFF0000), F32).astype(BF16)
    return lo, hi


def _split_bf16(x):
    hi = x.astype(BF16)
    lo = (x - hi.astype(F32)).astype(BF16)
    return hi, lo


def _mixer_kernel(x_ref, g1_ref, win_ref, convw_ref, wgk_ref, bgk_ref, gng_ref, wout_ref,
                  g2_ref, wrhl_ref, br_ref,
                  x1_ref, h2_ref, meta_ref, route_ref, seg_ref, cnt_ref,
                  proj_ref, ubuf_ref, la_ref, ycat_ref, state_ref, carry_ref):
    ts = x_ref.shape[1]
    b_idx = pl.program_id(0)
    s_idx = pl.program_id(1)

    @pl.when(s_idx == 0)
    def _():
        state_ref[...] = jnp.zeros_like(state_ref)
        ubuf_ref[0:SUBLANES, :] = jnp.zeros((SUBLANES, CONV_WIDTH), F32)

    @pl.when((s_idx == 0) & (b_idx == 0))
    def _():
        carry_ref[...] = jnp.zeros_like(carry_ref)

    x = x_ref[0]
    h = _rms(x, g1_ref[...]).astype(BF16)
    proj_ref[...] = _dot(h, win_ref[...])

    u = proj_ref[:, OFF_GC:OFF_GC + CONV_WIDTH] * proj_ref[:, OFF_UH:OFF_UH + CONV_WIDTH]
    ubuf_ref[SUBLANES:SUBLANES + ts, :] = u
    u1 = ubuf_ref[pl.ds(SUBLANES - 1, ts), :]
    u2 = ubuf_ref[pl.ds(SUBLANES - 2, ts), :]
    conv = convw_ref[0:1, :] * u2 + convw_ref[1:2, :] * u1 + convw_ref[2:3, :] * u
    ycat_ref[:, 0:CONV_WIDTH] = (proj_ref[:, OFF_GB:OFF_GB + CONV_WIDTH] * conv).astype(BF16)
    ubuf_ref[0:SUBLANES, :] = ubuf_ref[ts:ts + SUBLANES, :]

    gk = _dot(proj_ref[:, OFF_GKL:OFF_GKL + LANES].astype(BF16), wgk_ref[...]) + bgk_ref[...]
    log_sig = jnp.minimum(gk, 0.0) - jnp.log1p(jnp.exp(-jnp.abs(gk)))
    la_ref[...] = log_sig / GLA_NORMALIZER

    ci = lax.broadcasted_iota(I32, (GLA_CHUNK, GLA_CHUNK), 0)
    cj = lax.broadcasted_iota(I32, (GLA_CHUNK, GLA_CHUNK), 1)
    tri_incl = (cj <= ci).astype(BF16)
    causal = cj <= ci
    causal4 = jnp.concatenate([causal] * GLA_HEADS, axis=0)
    lane_qk = lax.broadcasted_iota(I32, (1, GLA_QK), 1)
    head_masks = [((lane_qk >= hd * GLA_DK) & (lane_qk < (hd + 1) * GLA_DK)).astype(F32)
                  for hd in range(GLA_HEADS)]
    gng = gng_ref[...]

    n_chunks = ts // GLA_CHUNK
    chunk_rows = [pl.ds(c * GLA_CHUNK, GLA_CHUNK) for c in range(n_chunks)]
    lane_c = lax.broadcasted_iota(I32, (GLA_QK, 2 * GLA_CHUNK), 1)
    qd_all, kd_all, kr_all, bl_all, v_all = [], [], [], [], []
    for rows in chunk_rows:
        la_hi, la_lo = _split_bf16(la_ref[rows, :])
        bcum = _dot(tri_incl, la_hi) + _dot(tri_incl, la_lo)
        blast = bcum[GLA_CHUNK - 1:GLA_CHUNK, :]
        q = proj_ref[rows, OFF_Q:OFF_Q + GLA_QK] * (GLA_DK ** -0.5)
        k = proj_ref[rows, OFF_K:OFF_K + GLA_QK]
        qd_all.append(q * jnp.exp(bcum))
        kd_all.append((k * jnp.exp(-bcum)).astype(BF16))
        kr_all.append(k * jnp.exp(blast - bcum))
        bl_all.append(blast)
        v_all.append(proj_ref[rows, OFF_V:OFF_V + GLA_WIDTH].astype(BF16))

    scores_all = []
    for c in range(n_chunks):
        q_stack = jnp.concatenate([qd_all[c] * m for m in head_masks], axis=0).astype(BF16)
        scores_all.append(
            jnp.where(causal4, _dot_nt(q_stack, kd_all[c]), 0.0).astype(BF16))
    o_intra_all = []
    for c in range(n_chunks):
        o_intra_all.append(jnp.concatenate(
            [_dot(scores_all[c][hd * GLA_CHUNK:(hd + 1) * GLA_CHUNK, :],
                  v_all[c][:, hd * GLA_DV:(hd + 1) * GLA_DV]) for hd in range(GLA_HEADS)], axis=1))
    kv_all, dcol_all = [], []
    for c in range(n_chunks):
        kt = jnp.concatenate(
            [kr_all[c], jnp.broadcast_to(bl_all[c], (GLA_CHUNK, GLA_QK))], axis=0).T
        dcol_all.append(jnp.exp(kt[:, GLA_CHUNK:GLA_CHUNK + 1]))
        kt_b = jnp.where(lane_c < GLA_CHUNK, kt, 0.0).astype(BF16)
        v_pad = jnp.concatenate([v_all[c], jnp.zeros_like(v_all[c])], axis=0)
        kv_all.append([_dot(kt_b[hd * GLA_DK:(hd + 1) * GLA_DK, :],
                            v_pad[:, hd * GLA_DV:(hd + 1) * GLA_DV]) for hd in range(GLA_HEADS)])

    o_all = []
    for c in range(n_chunks):
        state = state_ref[...]
        o_all.append(_dot(qd_all[c].astype(BF16), state.astype(BF16)) + o_intra_all[c])
        for hd in range(GLA_HEADS):
            rs = slice(hd * GLA_DK, (hd + 1) * GLA_DK)
            cs = slice(hd * GLA_DV, (hd + 1) * GLA_DV)
            state_ref[rs, cs] = dcol_all[c][rs, :] * state[rs, cs] + kv_all[c][hd]

    for c, rows in enumerate(chunk_rows):
        o = o_all[c]
        g_out = proj_ref[rows, OFF_GO:OFF_GO + GLA_WIDTH]
        o_n = jnp.concatenate(
            [_rms(o[:, hd * GLA_DV:(hd + 1) * GLA_DV], gng) for hd in range(GLA_HEADS)], axis=1)
        y = o_n * (g_out * jax.nn.sigmoid(g_out))
        ycat_ref[rows, CONV_WIDTH:CONV_WIDTH + GLA_WIDTH] = y.astype(BF16)

    x1 = x + _dot(ycat_ref[...], wout_ref[...])
    x1_ref[0] = x1
    h2 = _rms(x1, g2_ref[...])
    h2_hi, h2_lo = _split_bf16(h2)
    h2_ref[0] = h2_hi
    hi_terms = _dot(h2_hi, wrhl_ref[...])
    logits = (hi_terms[:, 0:LANES] + hi_terms[:, LANES:2 * LANES]
              + _dot(h2_lo, wrhl_ref[:, 0:LANES]) + br_ref[...])

    lt = logits.T[0:N_EXPERTS, :]
    erow = lax.broadcasted_iota(I32, (N_EXPERTS, ts), 0).astype(F32)
    work = lt
    sel = jnp.zeros((N_EXPERTS, ts), F32)
    top_v, top_i, top_oh = [], [], []
    for _ in range(TOP_K):
        m = jnp.max(work, axis=0, keepdims=True)
        idx = jnp.min(jnp.where(work == m, erow, float(N_EXPERTS)), axis=0, keepdims=True)
        oh = erow == idx
        top_v.append(m)
        top_i.append(idx)
        top_oh.append(oh)
        sel = sel + oh.astype(F32)
        work = jnp.where(oh, -jnp.inf, work)
    exps = [jnp.exp(tv - top_v[0]) for tv in top_v]
    denom = exps[0] + exps[1] + exps[2] + exps[3]
    gates = [e / denom for e in exps]

    tsrc = lax.broadcasted_iota(I32, (ts, ts), 0)
    tdst = lax.broadcasted_iota(I32, (ts, ts), 1)
    earlier = (tsrc < tdst).astype(BF16)
    local = _dot(sel.astype(BF16), earlier)
    carry = carry_ref[:, 0:1]
    ranks = [jnp.sum(jnp.where(oh, local + carry, 0.0), axis=0, keepdims=True) for oh in top_oh]

    ei = lax.broadcasted_iota(I32, (N_EXPERTS, N_EXPERTS), 0)
    ej = lax.broadcasted_iota(I32, (N_EXPERTS, N_EXPERTS), 1)
    lower_experts = (ej < ei).astype(BF16)
    lane_f = lax.broadcasted_iota(I32, (N_EXPERTS, LANES), 1)
    slot_base = []
    before = jnp.zeros((N_EXPERTS, 1), F32)
    for u in range(ts // SORT_TILE):
        sub_cnt = jnp.sum(sel[:, u * SORT_TILE:(u + 1) * SORT_TILE], axis=1, keepdims=True)
        seg_start = _dot(lower_experts,
                         jnp.broadcast_to(sub_cnt, (N_EXPERTS, LANES)).astype(BF16))[:, 0:1]
        seg_ref[u] = jnp.where(lane_f == 0, carry + before,
                               jnp.where(lane_f == 1, sub_cnt,
                                         jnp.where(lane_f == 2, seg_start, 0.0)))
        slot_base.append(jnp.broadcast_to(seg_start - before, (N_EXPERTS, SORT_TILE)))
        before = before + sub_cnt
    slot_base = jnp.concatenate(slot_base, axis=1)
    slots = [jnp.sum(jnp.where(oh, local + slot_base, 0.0), axis=0, keepdims=True)
             for oh in top_oh]
    new_carry = carry + before
    carry_ref[...] = jnp.broadcast_to(new_carry, carry_ref.shape)
    cnt_ref[...] = jnp.broadcast_to(new_carry, cnt_ref.shape)

    route = jnp.concatenate(top_i + gates + ranks + slots, axis=0)
    route_ref[...] = route
    meta_ref[0] = jnp.concatenate(
        [route, jnp.zeros((LANES - ROUTE_ROWS, ts), F32)], axis=0).T


def _mixer_call(x, g1, win, convw, wgk, bgk, gng, wout, g2, wrhl, br):
    bsz, seq, d = x.shape
    ts = SEQ_TILE
    grid = (bsz, seq // ts)

    def const(shape):
        return pl.BlockSpec(shape, lambda b, s: (0,) * len(shape))

    tile = lambda w: pl.BlockSpec((1, ts, w), lambda b, s: (b, s, 0))
    return pl.pallas_call(
        _mixer_kernel,
        grid=grid,
        in_specs=[tile(d), const(g1.shape), const(win.shape), const(convw.shape),
                  const(wgk.shape), const(bgk.shape), const(gng.shape), const(wout.shape),
                  const(g2.shape), const(wrhl.shape), const(br.shape)],
        out_specs=[tile(d),
                   tile(d),
                   tile(LANES),
                   pl.BlockSpec((ROUTE_ROWS, ts), lambda b, s: (0, b * (seq // ts) + s)),
                   pl.BlockSpec((ts // SORT_TILE, N_EXPERTS, LANES),
                                lambda b, s: (b * (seq // ts) + s, 0, 0)),
                   const((N_EXPERTS, LANES))],
        out_shape=[jax.ShapeDtypeStruct((bsz, seq, d), F32),
                   jax.ShapeDtypeStruct((bsz, seq, d), BF16),
                   jax.ShapeDtypeStruct((bsz, seq, LANES), F32),
                   jax.ShapeDtypeStruct((ROUTE_ROWS, bsz * seq), F32),
                   jax.ShapeDtypeStruct((bsz * seq // SORT_TILE, N_EXPERTS, LANES), F32),
                   jax.ShapeDtypeStruct((N_EXPERTS, LANES), F32)],
        scratch_shapes=[pltpu.VMEM((ts, D_IN_PAD), F32),
                        pltpu.VMEM((ts + SUBLANES, CONV_WIDTH), F32),
                        pltpu.VMEM((ts, GLA_QK), F32),
                        pltpu.VMEM((ts, D_MODEL), BF16),
                        pltpu.VMEM((GLA_QK, GLA_WIDTH), F32),
                        pltpu.VMEM((N_EXPERTS, LANES), F32)],
        compiler_params=pltpu.CompilerParams(
            dimension_semantics=("arbitrary", "arbitrary"), vmem_limit_bytes=VMEM_LIMIT),
        name="mixer",
    )(x, g1, win, convw, wgk, bgk, gng, wout, g2, wrhl, br)


def _dispatch_kernel(run_xs_ref, run_buf_ref, run_len_ref, pad_lo_ref, pad_n_ref,
                     route_ref, h2_ref, xs_hbm, sbuf, zbuf, sem, zsem):
    i = pl.program_id(0)
    n = pl.num_programs(0)
    td = SORT_TILE
    n_rows = TOP_K * td

    def zero_fill(wait):
        def fill(g, carry):
            ln = pl.multiple_of(pad_n_ref[g], ROW_TILES)

            @pl.when(ln > 0)
            def _():
                lo = pl.multiple_of(pad_lo_ref[g], ROW_TILES)
                cp = pltpu.make_async_copy(zbuf.at[pl.ds(0, ln), :], xs_hbm.at[pl.ds(lo, ln), :],
                                           zsem.at[0])
                if wait:
                    cp.wait()
                else:
                    cp.start()
            return carry

        lax.fori_loop(0, pad_lo_ref.shape[0], fill, 0)

    def wait_runs(buf_slot):
        pltpu.make_async_copy(sbuf.at[buf_slot], xs_hbm.at[pl.ds(0, n_rows * ROW_TILES), :],
                              sem.at[buf_slot]).wait()

    @pl.when(i == 0)
    def _():
        zbuf[...] = jnp.zeros_like(zbuf)
        zero_fill(wait=False)

    for u in range(STEP_TILES):
        @pl.when(i >= 1)
        def _(u=u):
            wait_runs(u)

        route = route_ref[:, u * td:(u + 1) * td]
        row = lax.broadcasted_iota(I32, (n_rows, td), 0).astype(F32)
        pick = jnp.zeros((n_rows, td), F32)
        for kk in range(TOP_K):
            pick = jnp.where(row == route[3 * TOP_K + kk:3 * TOP_K + kk + 1, :], 1.0, pick)
        rows_sorted = _dot(pick.astype(BF16), h2_ref[u * td:(u + 1) * td, :])
        words = _pack_rows(rows_sorted)
        for j in range(ROW_TILES):
            sbuf[u, pl.ds(j, n_rows, stride=ROW_TILES), :] = words[:, j * LANES:(j + 1) * LANES]

        def body(pair, carry, u=u):
            for queue in range(DMA_QUEUES):
                g = (i * STEP_TILES + u) * N_EXPERTS + pair * DMA_QUEUES + queue
                ln = pl.multiple_of(run_len_ref[g], ROW_TILES)

                @pl.when(ln > 0)
                def _(g=g, ln=ln, queue=queue):
                    src = pl.multiple_of(run_buf_ref[g], ROW_TILES)
                    dst = pl.multiple_of(run_xs_ref[g], ROW_TILES)
                    pltpu.make_async_copy(sbuf.at[u, pl.ds(src, ln), :],
                                          xs_hbm.at[pl.ds(dst, ln), :],
                                          sem.at[u]).start(priority=queue)
            return carry

        lax.fori_loop(0, N_EXPERTS // DMA_QUEUES, body, 0)

    @pl.when(i == n - 1)
    def _():
        for u in range(STEP_TILES):
            wait_runs(u)
        zero_fill(wait=True)


def _dispatch_call(run_xs, run_buf, run_len, pad_lo, pad_n, route, h2, n_rows):
    t, d = h2.shape
    td = SORT_TILE * STEP_TILES
    grid_spec = pltpu.PrefetchScalarGridSpec(
        num_scalar_prefetch=5,
        grid=(t // td,),
        in_specs=[pl.BlockSpec((ROUTE_ROWS, td), lambda i, *_: (0, i)),
                  pl.BlockSpec((td, d), lambda i, *_: (i, 0))],
        out_specs=pl.BlockSpec(memory_space=pl.ANY),
        scratch_shapes=[pltpu.VMEM((STEP_TILES, TOP_K * SORT_TILE * ROW_TILES, LANES), U32),
                        pltpu.VMEM((MOE_BLOCK * ROW_TILES, LANES), U32),
                        pltpu.SemaphoreType.DMA((STEP_TILES,)),
                        pltpu.SemaphoreType.DMA((1,))],
    )
    return pl.pallas_call(
        _dispatch_kernel,
        grid_spec=grid_spec,
        out_shape=jax.ShapeDtypeStruct((n_rows * ROW_TILES, LANES), U32),
        compiler_params=pltpu.CompilerParams(
            dimension_semantics=("arbitrary",), vmem_limit_bytes=VMEM_LIMIT,
            has_side_effects=True),
        name="dispatch",
    )(run_xs, run_buf, run_len, pad_lo, pad_n, route, h2)


def _experts_kernel(bexp_ref, first_ref, nxt_ref, nval_ref, nreal_ref,
                    xs_ref, wgu_hbm, bgu_ref, wd_hbm, bd_ref,
                    ys_ref,
                    xb_ref, act_ref, wgu_stage, wd_stage, wgu_bf, wd_bf, wsem):
    i = pl.program_id(0)
    nreal = nreal_ref[0]
    bm = xb_ref.shape[0]

    def weight_copies(e):
        return (pltpu.make_async_copy(wgu_hbm.at[e], wgu_stage, wsem.at[0]),
                pltpu.make_async_copy(wd_hbm.at[e], wd_stage, wsem.at[1]))

    @pl.when(i >= nreal)
    def _():
        ys_ref[...] = jnp.zeros_like(ys_ref)

    @pl.when(i < nreal)
    def _():
        e = bexp_ref[i]

        @pl.when(i == 0)
        def _():
            for cp in weight_copies(e):
                cp.start(priority=1)

        @pl.when(first_ref[i] == 1)
        def _():
            for cp in weight_copies(e):
                cp.wait()

            def cast_gu(r, carry):
                rows = pl.ds(pl.multiple_of(r * CAST_ROWS, CAST_ROWS), CAST_ROWS)
                wgu_bf[rows, :] = wgu_stage[rows, :].astype(BF16)
                return carry

            def cast_d(r, carry):
                rows = pl.ds(pl.multiple_of(r * CAST_ROWS, CAST_ROWS), CAST_ROWS)
                wd_bf[rows, :] = wd_stage[rows, :].astype(BF16)
                return carry

            lax.fori_loop(0, D_MODEL // CAST_ROWS, cast_gu, 0)
            lax.fori_loop(0, D_FF // CAST_ROWS, cast_d, 0)

            @pl.when(nxt_ref[i] >= 0)
            def _():
                for cp in weight_copies(nxt_ref[i]):
                    cp.start(priority=1)

        def mlp(rows):
            for j in range(ROW_TILES):
                lo, hi = _unpack_words(xs_ref[pl.ds(j, rows, stride=ROW_TILES), :])
                xb_ref[0:rows, j * LANES:(j + 1) * LANES] = lo
                xb_ref[0:rows, D_MODEL // 2 + j * LANES:D_MODEL // 2 + (j + 1) * LANES] = hi
            for c in range(D_FF // FF_CHUNK):
                f0 = c * FF_CHUNK
                xb = xb_ref[0:rows, :]
                gate = _dot(xb, wgu_bf[:, f0:f0 + FF_CHUNK]) + bgu_ref[0, :, f0:f0 + FF_CHUNK]
                up = (_dot(xb, wgu_bf[:, D_FF + f0:D_FF + f0 + FF_CHUNK])
                      + bgu_ref[0, :, D_FF + f0:D_FF + f0 + FF_CHUNK])
                gate = jnp.minimum(gate, SWIGLU_LIMIT)
                up = jnp.clip(up, -SWIGLU_LIMIT, SWIGLU_LIMIT)
                glu = gate * jax.nn.sigmoid(gate * SWIGLU_ALPHA)
                act_ref[0:rows, f0:f0 + FF_CHUNK] = ((up + 1.0) * glu).astype(BF16)
            out = _pack_rows(_dot(act_ref[0:rows, :], wd_bf[...]) + bd_ref[0])
            for j in range(ROW_TILES):
                ys_ref[pl.ds(j, rows, stride=ROW_TILES), :] = out[:, j * LANES:(j + 1) * LANES]

        n_routed = nval_ref[i]
        for rows in range(TAIL_ROWS, bm + 1, TAIL_ROWS):
            @pl.when((n_routed > rows - TAIL_ROWS) & (n_routed <= rows))
            def _(rows=rows):
                mlp(rows)
                if rows < bm:
                    ys_ref[rows * ROW_TILES:bm * ROW_TILES, :] = jnp.zeros(
                        ((bm - rows) * ROW_TILES, LANES), U32)


def _experts_call(bexp, first, nxt, nval, nreal, xs, wgu, bgu, wd, bd):
    bm = MOE_BLOCK
    n_blocks = xs.shape[0] // (bm * ROW_TILES)
    bgu3 = bgu.reshape(N_EXPERTS, 1, 2 * D_FF)
    bd3 = bd.reshape(N_EXPERTS, 1, D_MODEL)
    grid_spec = pltpu.PrefetchScalarGridSpec(
        num_scalar_prefetch=5,
        grid=(n_blocks,),
        in_specs=[
            pl.BlockSpec((bm * ROW_TILES, LANES),
                         lambda i, be, fi, nx, nv, nr: (jnp.minimum(i, nr[0] - 1), 0)),
            pl.BlockSpec(memory_space=pl.ANY),
            pl.BlockSpec((1, 1, 2 * D_FF), lambda i, be, fi, nx, nv, nr: (be[i], 0, 0)),
            pl.BlockSpec(memory_space=pl.ANY),
            pl.BlockSpec((1, 1, D_MODEL), lambda i, be, fi, nx, nv, nr: (be[i], 0, 0)),
        ],
        out_specs=pl.BlockSpec((bm * ROW_TILES, LANES), lambda i, be, fi, nx, nv, nr: (i, 0)),
        scratch_shapes=[pltpu.VMEM((bm, D_MODEL), BF16),
                        pltpu.VMEM((bm, D_FF), BF16),
                        pltpu.VMEM((D_MODEL, 2 * D_FF), F32),
                        pltpu.VMEM((D_FF, D_MODEL), F32),
                        pltpu.VMEM((D_MODEL, 2 * D_FF), BF16),
                        pltpu.VMEM((D_FF, D_MODEL), BF16),
                        pltpu.SemaphoreType.DMA((2,))],
    )
    return pl.pallas_call(
        _experts_kernel,
        grid_spec=grid_spec,
        out_shape=jax.ShapeDtypeStruct(xs.shape, U32),
        compiler_params=pltpu.CompilerParams(
            dimension_semantics=("arbitrary",), vmem_limit_bytes=VMEM_LIMIT),
        name="experts",
    )(bexp, first, nxt, nval, nreal, xs, wgu, bgu3, wd, bd3)


def _combine_kernel(src_ref, dst_ref, len_ref, x1_ref, meta_ref, g_ref, ys_hbm, out_ref,
                    ybuf, ysort_ref, sem):
    i = pl.program_id(0)
    n = pl.num_programs(0)
    tb = SORT_TILE
    n_rows = TOP_K * tb
    slot = i % 2

    def start_runs(step, buf_slot):
        for u in range(COMBINE_TILES):
            def body(pair, carry, u=u):
                for queue in range(DMA_QUEUES):
                    g = (step * COMBINE_TILES + u) * N_EXPERTS + pair * DMA_QUEUES + queue
                    ln = pl.multiple_of(len_ref[g], ROW_TILES)

                    @pl.when(ln > 0)
                    def _(g=g, ln=ln, queue=queue):
                        src = pl.multiple_of(src_ref[g], ROW_TILES)
                        dst = pl.multiple_of(dst_ref[g], ROW_TILES)
                        pltpu.make_async_copy(ys_hbm.at[pl.ds(src, ln), :],
                                              ybuf.at[buf_slot, u, pl.ds(dst, ln), :],
                                              sem.at[buf_slot]).start(priority=queue)
                return carry
            lax.fori_loop(0, N_EXPERTS // DMA_QUEUES, body, 0)

    @pl.when(i == 0)
    def _():
        start_runs(0, 0)

    @pl.when(i + 1 < n)
    def _():
        start_runs(i + 1, 1 - slot)

    for u in range(COMBINE_TILES):
        pltpu.make_async_copy(ys_hbm.at[pl.ds(0, n_rows * ROW_TILES), :], ybuf.at[slot, u],
                              sem.at[slot]).wait()

    for u in range(COMBINE_TILES):
        rs = slice(u * tb, (u + 1) * tb)
        for j in range(ROW_TILES):
            lo, hi = _unpack_words(ybuf[slot, u, pl.ds(j, n_rows, stride=ROW_TILES), :])
            ysort_ref[u, :, j * LANES:(j + 1) * LANES] = lo
            ysort_ref[u, :, D_MODEL // 2 + j * LANES:D_MODEL // 2 + (j + 1) * LANES] = hi
        meta = meta_ref[rs, :]
        col = lax.broadcasted_iota(I32, (tb, n_rows), 1).astype(F32)
        weights = jnp.zeros((tb, n_rows), F32)
        for kk in range(TOP_K):
            weights = jnp.where(col == meta[:, 3 * TOP_K + kk:3 * TOP_K + kk + 1],
                                meta[:, TOP_K + kk:TOP_K + kk + 1], weights)
        acc = x1_ref[rs, :] + _dot(weights.astype(BF16), ysort_ref[u])
        out_ref[rs, :] = _rms(acc, g_ref[...])


def _combine_call(seg_src, seg_dst, seg_len, x1, meta, g, ys):
    t, d = x1.shape
    tb = SORT_TILE * COMBINE_TILES
    nb = t // tb
    grid_spec = pltpu.PrefetchScalarGridSpec(
        num_scalar_prefetch=3,
        grid=(nb,),
        in_specs=[pl.BlockSpec((tb, d), lambda i, a, b, c: (i, 0)),
                  pl.BlockSpec((tb, LANES), lambda i, a, b, c: (i, 0)),
                  pl.BlockSpec((1, d), lambda i, a, b, c: (0, 0)),
                  pl.BlockSpec(memory_space=pl.ANY)],
        out_specs=pl.BlockSpec((tb, d), lambda i, a, b, c: (i, 0)),
        scratch_shapes=[pltpu.VMEM((2, COMBINE_TILES, TOP_K * SORT_TILE * ROW_TILES, LANES), U32),
                        pltpu.VMEM((COMBINE_TILES, TOP_K * SORT_TILE, D_MODEL), BF16),
                        pltpu.SemaphoreType.DMA((2,))],
    )
    return pl.pallas_call(
        _combine_kernel,
        grid_spec=grid_spec,
        out_shape=jax.ShapeDtypeStruct((t, d), F32),
        compiler_params=pltpu.CompilerParams(
            dimension_semantics=("arbitrary",), vmem_limit_bytes=VMEM_LIMIT),
        name="combine",
    )(seg_src, seg_dst, seg_len, x1, meta, g, ys)


def _routing_tables(counts, t):
    bm = MOE_BLOCK
    n_blocks = t * TOP_K // bm + N_EXPERTS
    eids = jnp.arange(N_EXPERTS, dtype=I32)
    nblk_e = (counts + bm - 1) // bm
    blk_end = jnp.sum(jnp.where(eids[None, :] <= eids[:, None], nblk_e[None, :], 0), axis=1)
    blk_start = blk_end - nblk_e
    nreal = blk_end[N_EXPERTS - 1]
    pad_start = blk_start * bm
    blk = jnp.arange(n_blocks, dtype=I32)
    bexp = jnp.minimum(jnp.sum((blk_end[None, :] <= blk[:, None]).astype(I32), axis=1),
                       N_EXPERTS - 1)
    blk_is_e = bexp[:, None] == eids[None, :]
    pick = lambda tab: jnp.sum(jnp.where(blk_is_e, tab[None, :], 0), axis=1)
    first = (blk == pick(blk_start)).astype(I32)
    nxt_e = jnp.sum((blk_end[None, :] <= blk_end[:, None]).astype(I32), axis=1)
    nxt_e = jnp.where(blk_end < nreal, jnp.minimum(nxt_e, N_EXPERTS - 1), -1)
    nxt = pick(nxt_e)
    nval = jnp.clip(pick(counts) - (blk - pick(blk_start)) * bm, 0, bm)
    tail_blk = jnp.arange(N_EXPERTS, dtype=I32) + nreal
    pad_lo = jnp.concatenate([pad_start + counts, jnp.minimum(tail_blk, n_blocks - 1) * bm])
    pad_n = jnp.concatenate([nblk_e * bm - counts, jnp.where(tail_blk < n_blocks, bm, 0)])
    return (pad_start, bexp, first, nxt, nval, nreal.reshape(1).astype(I32), n_blocks,
            pad_lo * ROW_TILES, pad_n * ROW_TILES)


def kernel(x, norm_mix_g, w_in, conv_w, w_gk_up, b_gk_up, gla_norm_g, w_out, norm_ffn_g,
           w_router, b_router, w_gate_up, b_gate_up, w_down, b_down, norm_final_g):
    bsz, seq, d = x.shape
    t = bsz * seq
    assert w_in.shape[0] == 1, "single-layer trunk only"
    assert conv_w.shape[1] == CONV_K and ROUTE_ROWS % SUBLANES == 0
    l = 0
    d_in = w_in.shape[-1]
    win = jnp.pad(w_in[l].astype(BF16), ((0, 0), (0, D_IN_PAD - d_in)))
    wgk = jnp.pad(w_gk_up[l], ((0, LANES - GLA_RANK), (0, 0))).astype(BF16)
    wr = jnp.pad(w_router[l], ((0, 0), (0, LANES - N_EXPERTS)))
    wrh = wr.astype(BF16)
    wrhl = jnp.concatenate([wrh, (wr - wrh.astype(F32)).astype(BF16)], axis=1)
    br = jnp.pad(b_router[l], (0, LANES - N_EXPERTS), constant_values=NEG_BIG).reshape(1, LANES)

    x1, h2, meta, route, seg, cnt = _mixer_call(
        x, norm_mix_g[l].reshape(1, d), win, conv_w[l], wgk, b_gk_up[l].reshape(1, GLA_QK),
        gla_norm_g[l].reshape(1, GLA_DV), w_out[l].astype(BF16), norm_ffn_g[l].reshape(1, d),
        wrhl, br)

    meta2 = meta.reshape(t, LANES)
    counts = cnt[:, 0].astype(I32)
    (pad_start, bexp, first, nxt, nval, nreal, n_blocks, pad_lo,
     pad_n) = _routing_tables(counts, t)
    seg_src = ((pad_start[None, :] + seg[:, :, 0].astype(I32)) * ROW_TILES).reshape(-1)
    seg_len = (seg[:, :, 1].astype(I32) * ROW_TILES).reshape(-1)
    seg_dst = (seg[:, :, 2].astype(I32) * ROW_TILES).reshape(-1)
    xs = _dispatch_call(seg_src, seg_dst, seg_len, pad_lo, pad_n, route, h2.reshape(t, d),
                        n_blocks * MOE_BLOCK)
    ys = _experts_call(bexp, first, nxt, nval, nreal, xs, w_gate_up[l], b_gate_up[l], w_down[l],
                       b_down[l])
    out = _combine_call(seg_src, seg_dst, seg_len, x1.reshape(t, d), meta2,
                        norm_final_g.reshape(1, d), ys)
    return out.reshape(bsz, seq, d)
```

```python
import jax
import jax.numpy as jnp
from jax import lax
from jax.experimental import pallas as pl
from jax.experimental.pallas import tpu as pltpu

F32 = jnp.float32
BF16 = jnp.bfloat16
I32 = jnp.int32
U32 = jnp.uint32

D_MODEL = 1024
CONV_WIDTH = 512
CONV_K = 3
GLA_WIDTH = 512
GLA_HEADS = 4
GLA_DV = 128
GLA_DK = 64
GLA_QK = GLA_HEADS * GLA_DK
GLA_RANK = 16
GLA_NORMALIZER = 16.0
GLA_CHUNK = 64
N_EXPERTS = 32
TOP_K = 4
D_FF = 1024
SWIGLU_LIMIT = 7.0
SWIGLU_ALPHA = 1.702
RMS_EPS = 1e-5

LANES = 128
SUBLANES = 8
ROW_TILES = D_MODEL // LANES // 2
ROUTE_ROWS = 4 * TOP_K

OFF_UH = 0
OFF_GB = OFF_UH + CONV_WIDTH
OFF_GC = OFF_GB + CONV_WIDTH
OFF_Q = OFF_GC + CONV_WIDTH
OFF_K = OFF_Q + GLA_QK
OFF_V = OFF_K + GLA_QK
OFF_GO = OFF_V + GLA_WIDTH
OFF_GKL = OFF_GO + GLA_WIDTH
D_IN_PAD = OFF_GKL + LANES

SEQ_TILE = 512
SORT_TILE = 256
STEP_TILES = 4
COMBINE_TILES = 4
DMA_QUEUES = 2
MOE_BLOCK = 512
FF_CHUNK = 256
TAIL_ROWS = 128
CAST_ROWS = 128
NEG_BIG = -1e30
VMEM_LIMIT = 56 * 1024 * 1024


def _rms(x, g):
    return x * lax.rsqrt(jnp.mean(x * x, axis=-1, keepdims=True) + RMS_EPS) * g


def _dot(a, b):
    return jnp.dot(a, b, preferred_element_type=F32)


def _dot_nt(a, b):
    return lax.dot_general(a, b, (((1,), (1,)), ((), ())), preferred_element_type=F32)


def _pack_rows(x):
    half = x.shape[1] // 2
    xr = x.astype(BF16).astype(F32)
    lo = lax.bitcast_convert_type(xr[:, :half], U32) >> 16
    hi = lax.bitcast_convert_type(xr[:, half:], U32) & jnp.uint32(0xFFFF0000)
    return hi | lo


def _unpack_words(w):
    lo = lax.bitcast_convert_type(w << 16, F32).astype(BF16)
    hi = lax.bitcast_convert_type(w & jnp.uint32(0xFFFF0000), F32).astype(BF16)
    return lo, hi


def _split_bf16(x):
    hi = x.astype(BF16)
    lo = (x - hi.astype(F32)).astype(BF16)
    return hi, lo


def _mixer_kernel(x_ref, g1_ref, win_ref, convw_ref, wgk_ref, bgk_ref, gng_ref, wout_ref,
                  g2_ref, wrhl_ref, br_ref,
                  x1_ref, h2_ref, meta_ref, route_ref, seg_ref, cnt_ref,
                  proj_ref, ubuf_ref, la_ref, ycat_ref, state_ref, carry_ref):
    ts = x_ref.shape[1]
    b_idx = pl.program_id(0)
    s_idx = pl.program_id(1)

    @pl.when(s_idx == 0)
    def _():
        state_ref[...] = jnp.zeros_like(state_ref)
        ubuf_ref[0:SUBLANES, :] = jnp.zeros((SUBLANES, CONV_WIDTH), F32)

    @pl.when((s_idx == 0) & (b_idx == 0))
    def _():
        carry_ref[...] = jnp.zeros_like(carry_ref)

    x = x_ref[0]
    h = _rms(x, g1_ref[...]).astype(BF16)
    proj_ref[...] = _dot(h, win_ref[...])

    u = proj_ref[:, OFF_GC:OFF_GC + CONV_WIDTH] * proj_ref[:, OFF_UH:OFF_UH + CONV_WIDTH]
    ubuf_ref[SUBLANES:SUBLANES + ts, :] = u
    u1 = ubuf_ref[pl.ds(SUBLANES - 1, ts), :]
    u2 = ubuf_ref[pl.ds(SUBLANES - 2, ts), :]
    conv = convw_ref[0:1, :] * u2 + convw_ref[1:2, :] * u1 + convw_ref[2:3, :] * u
    ycat_ref[:, 0:CONV_WIDTH] = (proj_ref[:, OFF_GB:OFF_GB + CONV_WIDTH] * conv).astype(BF16)
    ubuf_ref[0:SUBLANES, :] = ubuf_ref[ts:ts + SUBLANES, :]

    gk = _dot(proj_ref[:, OFF_GKL:OFF_GKL + LANES].astype(BF16), wgk_ref[...]) + bgk_ref[...]
    log_sig = jnp.minimum(gk, 0.0) - jnp.log1p(jnp.exp(-jnp.abs(gk)))
    la_ref[...] = log_sig / GLA_NORMALIZER

    ci = lax.broadcasted_iota(I32, (GLA_CHUNK, GLA_CHUNK), 0)
    cj = lax.broadcasted_iota(I32, (GLA_CHUNK, GLA_CHUNK), 1)
    tri_incl = (cj <= ci).astype(BF16)
    causal = cj <= ci
    causal4 = jnp.concatenate([causal] * GLA_HEADS, axis=0)
    lane_qk = lax.broadcasted_iota(I32, (1, GLA_QK), 1)
    head_masks = [((lane_qk >= hd * GLA_DK) & (lane_qk < (hd + 1) * GLA_DK)).astype(F32)
                  for hd in range(GLA_HEADS)]
    gng = gng_ref[...]

    n_chunks = ts // GLA_CHUNK
    chunk_rows = [pl.ds(c * GLA_CHUNK, GLA_CHUNK) for c in range(n_chunks)]
    lane_c = lax.broadcasted_iota(I32, (GLA_QK, 2 * GLA_CHUNK), 1)
    qd_all, kd_all, kr_all, bl_all, v_all = [], [], [], [], []
    for rows in chunk_rows:
        la_hi, la_lo = _split_bf16(la_ref[rows, :])
        bcum = _dot(tri_incl, la_hi) + _dot(tri_incl, la_lo)
        blast = bcum[GLA_CHUNK - 1:GLA_CHUNK, :]
        q = proj_ref[rows, OFF_Q:OFF_Q + GLA_QK] * (GLA_DK ** -0.5)
        k = proj_ref[rows, OFF_K:OFF_K + GLA_QK]
        qd_all.append(q * jnp.exp(bcum))
        kd_all.append((k * jnp.exp(-bcum)).astype(BF16))
        kr_all.append(k * jnp.exp(blast - bcum))
        bl_all.append(blast)
        v_all.append(proj_ref[rows, OFF_V:OFF_V + GLA_WIDTH].astype(BF16))

    scores_all = []
    for c in range(n_chunks):
        q_stack = jnp.concatenate([qd_all[c] * m for m in head_masks], axis=0).astype(BF16)
        scores_all.append(
            jnp.where(causal4, _dot_nt(q_stack, kd_all[c]), 0.0).astype(BF16))
    o_intra_all = []
    for c in range(n_chunks):
        o_intra_all.append(jnp.concatenate(
            [_dot(scores_all[c][hd * GLA_CHUNK:(hd + 1) * GLA_CHUNK, :],
                  v_all[c][:, hd * GLA_DV:(hd + 1) * GLA_DV]) for hd in range(GLA_HEADS)], axis=1))
    kv_all, dcol_all = [], []
    for c in range(n_chunks):
        kt = jnp.concatenate(
            [kr_all[c], jnp.broadcast_to(bl_all[c], (GLA_CHUNK, GLA_QK))], axis=0).T
        dcol_all.append(jnp.exp(kt[:, GLA_CHUNK:GLA_CHUNK + 1]))
        kt_b = jnp.where(lane_c < GLA_CHUNK, kt, 0.0).astype(BF16)
        v_pad = jnp.concatenate([v_all[c], jnp.zeros_like(v_all[c])], axis=0)
        kv_all.append([_dot(kt_b[hd * GLA_DK:(hd + 1) * GLA_DK, :],
                            v_pad[:, hd * GLA_DV:(hd + 1) * GLA_DV]) for hd in range(GLA_HEADS)])

    o_all = []
    for c in range(n_chunks):
        state = state_ref[...]
        o_all.append(_dot(qd_all[c].astype(BF16), state.astype(BF16)) + o_intra_all[c])
        for hd in range(GLA_HEADS):
            rs = slice(hd * GLA_DK, (hd + 1) * GLA_DK)
            cs = slice(hd * GLA_DV, (hd + 1) * GLA_DV)
            state_ref[rs, cs] = dcol_all[c][rs, :] * state[rs, cs] + kv_all[c][hd]

    for c, rows in enumerate(chunk_rows):
        o = o_all[c]
        g_out = proj_ref[rows, OFF_GO:OFF_GO + GLA_WIDTH]
        o_n = jnp.concatenate(
            [_rms(o[:, hd * GLA_DV:(hd + 1) * GLA_DV], gng) for hd in range(GLA_HEADS)], axis=1)
        y = o_n * (g_out * jax.nn.sigmoid(g_out))
        ycat_ref[rows, CONV_WIDTH:CONV_WIDTH + GLA_WIDTH] = y.astype(BF16)

    x1 = x + _dot(ycat_ref[...], wout_ref[...])
    x1_ref[0] = x1
    h2 = _rms(x1, g2_ref[...])
    h2_hi, h2_lo = _split_bf16(h2)
    h2_ref[0] = h2_hi
    hi_terms = _dot(h2_hi, wrhl_ref[...])
    logits = (hi_terms[:, 0:LANES] + hi_terms[:, LANES:2 * LANES]
              + _dot(h2_lo, wrhl_ref[:, 0:LANES]) + br_ref[...])

    lt = logits.T[0:N_EXPERTS, :]
    erow = lax.broadcasted_iota(I32, (N_EXPERTS, ts), 0).astype(F32)
    work = lt
    sel = jnp.zeros((N_EXPERTS, ts), F32)
    top_v, top_i, top_oh = [], [], []
    for _ in range(TOP_K):
        m = jnp.max(work, axis=0, keepdims=True)
        idx = jnp.min(jnp.where(work == m, erow, float(N_EXPERTS)), axis=0, keepdims=True)
        oh = erow == idx
        top_v.append(m)
        top_i.append(idx)
        top_oh.append(oh)
        sel = sel + oh.astype(F32)
        work = jnp.where(oh, -jnp.inf, work)
    exps = [jnp.exp(tv - top_v[0]) for tv in top_v]
    denom = exps[0] + exps[1] + exps[2] + exps[3]
    gates = [e / denom for e in exps]

    tsrc = lax.broadcasted_iota(I32, (ts, ts), 0)
    tdst = lax.broadcasted_iota(I32, (ts, ts), 1)
    earlier = (tsrc < tdst).astype(BF16)
    local = _dot(sel.astype(BF16), earlier)
    carry = carry_ref[:, 0:1]
    ranks = [jnp.sum(jnp.where(oh, local + carry, 0.0), axis=0, keepdims=True) for oh in top_oh]

    ei = lax.broadcasted_iota(I32, (N_EXPERTS, N_EXPERTS), 0)
    ej = lax.broadcasted_iota(I32, (N_EXPERTS, N_EXPERTS), 1)
    lower_experts = (ej < ei).astype(BF16)
    lane_f = lax.broadcasted_iota(I32, (N_EXPERTS, LANES), 1)
    slot_base = []
    before = jnp.zeros((N_EXPERTS, 1), F32)
    for u in range(ts // SORT_TILE):
        sub_cnt = jnp.sum(sel[:, u * SORT_TILE:(u + 1) * SORT_TILE], axis=1, keepdims=True)
        seg_start = _dot(lower_experts,
                         jnp.broadcast_to(sub_cnt, (N_EXPERTS, LANES)).astype(BF16))[:, 0:1]
        seg_ref[u] = jnp.where(lane_f == 0, carry + before,
                               jnp.where(lane_f == 1, sub_cnt,
                                         jnp.where(lane_f == 2, seg_start, 0.0)))
        slot_base.append(jnp.broadcast_to(seg_start - before, (N_EXPERTS, SORT_TILE)))
        before = before + sub_cnt
    slot_base = jnp.concatenate(slot_base, axis=1)
    slots = [jnp.sum(jnp.where(oh, local + slot_base, 0.0), axis=0, keepdims=True)
             for oh in top_oh]
    new_carry = carry + before
    carry_ref[...] = jnp.broadcast_to(new_carry, carry_ref.shape)
    cnt_ref[...] = jnp.broadcast_to(new_carry, cnt_ref.shape)

    route = jnp.concatenate(top_i + gates + ranks + slots, axis=0)
    route_ref[...] = route
    meta_ref[0] = jnp.concatenate(
        [route, jnp.zeros((LANES - ROUTE_ROWS, ts), F32)], axis=0).T


def _mixer_call(x, g1, win, convw, wgk, bgk, gng, wout, g2, wrhl, br):
    bsz, seq, d = x.shape
    ts = SEQ_TILE
    grid = (bsz, seq // ts)

    def const(shape):
        return pl.BlockSpec(shape, lambda b, s: (0,) * len(shape))

    tile = lambda w: pl.BlockSpec((1, ts, w), lambda b, s: (b, s, 0))
    return pl.pallas_call(
        _mixer_kernel,
        grid=grid,
        in_specs=[tile(d), const(g1.shape), const(win.shape), const(convw.shape),
                  const(wgk.shape), const(bgk.shape), const(gng.shape), const(wout.shape),
                  const(g2.shape), const(wrhl.shape), const(br.shape)],
        out_specs=[tile(d),
                   tile(d),
                   tile(LANES),
                   pl.BlockSpec((ROUTE_ROWS, ts), lambda b, s: (0, b * (seq // ts) + s)),
                   pl.BlockSpec((ts // SORT_TILE, N_EXPERTS, LANES),
                                lambda b, s: (b * (seq // ts) + s, 0, 0)),
                   const((N_EXPERTS, LANES))],
        out_shape=[jax.ShapeDtypeStruct((bsz, seq, d), F32),
                   jax.ShapeDtypeStruct((bsz, seq, d), BF16),
                   jax.ShapeDtypeStruct((bsz, seq, LANES), F32),
                   jax.ShapeDtypeStruct((ROUTE_ROWS, bsz * seq), F32),
                   jax.ShapeDtypeStruct((bsz * seq // SORT_TILE, N_EXPERTS, LANES), F32),
                   jax.ShapeDtypeStruct((N_EXPERTS, LANES), F32)],
        scratch_shapes=[pltpu.VMEM((ts, D_IN_PAD), F32),
                        pltpu.VMEM((ts + SUBLANES, CONV_WIDTH), F32),
                        pltpu.VMEM((ts, GLA_QK), F32),
                        pltpu.VMEM((ts, D_MODEL), BF16),
                        pltpu.VMEM((GLA_QK, GLA_WIDTH), F32),
                        pltpu.VMEM((N_EXPERTS, LANES), F32)],
        compiler_params=pltpu.CompilerParams(
            dimension_semantics=("arbitrary", "arbitrary"), vmem_limit_bytes=VMEM_LIMIT),
        name="mixer",
    )(x, g1, win, convw, wgk, bgk, gng, wout, g2, wrhl, br)


def _dispatch_kernel(run_xs_ref, run_buf_ref, run_len_ref, pad_lo_ref, pad_n_ref,
                     route_ref, h2_ref, xs_hbm, sbuf, zbuf, sem, zsem):
    i = pl.program_id(0)
    n = pl.num_programs(0)
    td = SORT_TILE
    n_rows = TOP_K * td

    def zero_fill(wait):
        def fill(g, carry):
            ln = pl.multiple_of(pad_n_ref[g], ROW_TILES)

            @pl.when(ln > 0)
            def _():
                lo = pl.multiple_of(pad_lo_ref[g], ROW_TILES)
                cp = pltpu.make_async_copy(zbuf.at[pl.ds(0, ln), :], xs_hbm.at[pl.ds(lo, ln), :],
                                           zsem.at[0])
                if wait:
                    cp.wait()
                else:
                    cp.start()
            return carry

        lax.fori_loop(0, pad_lo_ref.shape[0], fill, 0)

    def wait_runs(buf_slot):
        pltpu.make_async_copy(sbuf.at[buf_slot], xs_hbm.at[pl.ds(0, n_rows * ROW_TILES), :],
                              sem.at[buf_slot]).wait()

    @pl.when(i == 0)
    def _():
        zbuf[...] = jnp.zeros_like(zbuf)
        zero_fill(wait=False)

    base = (i % 2) * STEP_TILES

    @pl.when(i >= 2)
    def _():
        for u in range(STEP_TILES):
            wait_runs(base + u)

    for u in range(STEP_TILES):
        route = route_ref[:, u * td:(u + 1) * td]
        row = lax.broadcasted_iota(I32, (n_rows, td), 0).astype(F32)
        pick = jnp.zeros((n_rows, td), F32)
        for kk in range(TOP_K):
            pick = jnp.where(row == route[3 * TOP_K + kk:3 * TOP_K + kk + 1, :], 1.0, pick)
        rows_sorted = _dot(pick.astype(BF16), h2_ref[u * td:(u + 1) * td, :])
        words = _pack_rows(rows_sorted)
        for j in range(ROW_TILES):
            sbuf[base + u, pl.ds(j, n_rows, stride=ROW_TILES), :] = words[:, j * LANES:(j + 1) * LANES]

    for u in range(STEP_TILES):
        def body(pair, carry, u=u):
            for queue in range(DMA_QUEUES):
                g = (i * STEP_TILES + u) * N_EXPERTS + pair * DMA_QUEUES + queue
                ln = pl.multiple_of(run_len_ref[g], ROW_TILES)

                @pl.when(ln > 0)
                def _(g=g, ln=ln, queue=queue):
                    src = pl.multiple_of(run_buf_ref[g], ROW_TILES)
                    dst = pl.multiple_of(run_xs_ref[g], ROW_TILES)
                    pltpu.make_async_copy(sbuf.at[base + u, pl.ds(src, ln), :],
                                          xs_hbm.at[pl.ds(dst, ln), :],
                                          sem.at[base + u]).start(priority=queue)
            return carry

        lax.fori_loop(0, N_EXPERTS // DMA_QUEUES, body, 0)

    @pl.when(i == n - 1)
    def _():
        for u in range(STEP_TILES):
            wait_runs(base + u)

        @pl.when(i >= 1)
        def _():
            for u in range(STEP_TILES):
                wait_runs(STEP_TILES - base + u)
        zero_fill(wait=True)


def _dispatch_call(run_xs, run_buf, run_len, pad_lo, pad_n, route, h2, n_rows):
    t, d = h2.shape
    td = SORT_TILE * STEP_TILES
    grid_spec = pltpu.PrefetchScalarGridSpec(
        num_scalar_prefetch=5,
        grid=(t // td,),
        in_specs=[pl.BlockSpec((ROUTE_ROWS, td), lambda i, *_: (0, i)),
                  pl.BlockSpec((td, d), lambda i, *_: (i, 0))],
        out_specs=pl.BlockSpec(memory_space=pl.ANY),
        scratch_shapes=[pltpu.VMEM((2 * STEP_TILES, TOP_K * SORT_TILE * ROW_TILES, LANES), U32),
                        pltpu.VMEM((MOE_BLOCK * ROW_TILES, LANES), U32),
                        pltpu.SemaphoreType.DMA((2 * STEP_TILES,)),
                        pltpu.SemaphoreType.DMA((1,))],
    )
    return pl.pallas_call(
        _dispatch_kernel,
        grid_spec=grid_spec,
        out_shape=jax.ShapeDtypeStruct((n_rows * ROW_TILES, LANES), U32),
        compiler_params=pltpu.CompilerParams(
            dimension_semantics=("arbitrary",), vmem_limit_bytes=VMEM_LIMIT,
            has_side_effects=True),
        name="dispatch",
    )(run_xs, run_buf, run_len, pad_lo, pad_n, route, h2)


def _experts_kernel(bexp_ref, first_ref, nxt_ref, nval_ref, nreal_ref,
                    xs_ref, wgu_hbm, bgu_ref, wd_hbm, bd_ref,
                    ys_ref,
                    xb_ref, act_ref, wgu_stage, wd_stage, wgu_bf, wd_bf, wsem):
    i = pl.program_id(0)
    nreal = nreal_ref[0]
    bm = xb_ref.shape[0]

    def weight_copies(e):
        return (pltpu.make_async_copy(wgu_hbm.at[e], wgu_stage, wsem.at[0]),
                pltpu.make_async_copy(wd_hbm.at[e], wd_stage, wsem.at[1]))

    @pl.when(i >= nreal)
    def _():
        ys_ref[...] = jnp.zeros_like(ys_ref)

    @pl.when(i < nreal)
    def _():
        e = bexp_ref[i]

        @pl.when(i == 0)
        def _():
            for cp in weight_copies(e):
                cp.start(priority=1)

        @pl.when(first_ref[i] == 1)
        def _():
            for cp in weight_copies(e):
                cp.wait()

            def cast_gu(r, carry):
                rows = pl.ds(pl.multiple_of(r * CAST_ROWS, CAST_ROWS), CAST_ROWS)
                wgu_bf[rows, :] = wgu_stage[rows, :].astype(BF16)
                return carry

            def cast_d(r, carry):
                rows = pl.ds(pl.multiple_of(r * CAST_ROWS, CAST_ROWS), CAST_ROWS)
                wd_bf[rows, :] = wd_stage[rows, :].astype(BF16)
                return carry

            lax.fori_loop(0, D_MODEL // CAST_ROWS, cast_gu, 0)
            lax.fori_loop(0, D_FF // CAST_ROWS, cast_d, 0)

            @pl.when(nxt_ref[i] >= 0)
            def _():
                for cp in weight_copies(nxt_ref[i]):
                    cp.start(priority=1)

        def mlp(rows):
            for j in range(ROW_TILES):
                lo, hi = _unpack_words(xs_ref[pl.ds(j, rows, stride=ROW_TILES), :])
                xb_ref[0:rows, j * LANES:(j + 1) * LANES] = lo
                xb_ref[0:rows, D_MODEL // 2 + j * LANES:D_MODEL // 2 + (j + 1) * LANES] = hi
            for c in range(D_FF // FF_CHUNK):
                f0 = c * FF_CHUNK
                xb = xb_ref[0:rows, :]
                gate = _dot(xb, wgu_bf[:, f0:f0 + FF_CHUNK]) + bgu_ref[0, :, f0:f0 + FF_CHUNK]
                up = (_dot(xb, wgu_bf[:, D_FF + f0:D_FF + f0 + FF_CHUNK])
                      + bgu_ref[0, :, D_FF + f0:D_FF + f0 + FF_CHUNK])
                gate = jnp.minimum(gate, SWIGLU_LIMIT)
                up = jnp.clip(up, -SWIGLU_LIMIT, SWIGLU_LIMIT)
                glu = gate * jax.nn.sigmoid(gate * SWIGLU_ALPHA)
                act_ref[0:rows, f0:f0 + FF_CHUNK] = ((up + 1.0) * glu).astype(BF16)
            out = _pack_rows(_dot(act_ref[0:rows, :], wd_bf[...]) + bd_ref[0])
            for j in range(ROW_TILES):
                ys_ref[pl.ds(j, rows, stride=ROW_TILES), :] = out[:, j * LANES:(j + 1) * LANES]

        n_routed = nval_ref[i]
        for rows in range(TAIL_ROWS, bm + 1, TAIL_ROWS):
            @pl.when((n_routed > rows - TAIL_ROWS) & (n_routed <= rows))
            def _(rows=rows):
                mlp(rows)
                if rows < bm:
                    ys_ref[rows * ROW_TILES:bm * ROW_TILES, :] = jnp.zeros(
                        ((bm - rows) * ROW_TILES, LANES), U32)


def _experts_call(bexp, first, nxt, nval, nreal, xs, wgu, bgu, wd, bd):
    bm = MOE_BLOCK
    n_blocks = xs.shape[0] // (bm * ROW_TILES)
    bgu3 = bgu.reshape(N_EXPERTS, 1, 2 * D_FF)
    bd3 = bd.reshape(N_EXPERTS, 1, D_MODEL)
    grid_spec = pltpu.PrefetchScalarGridSpec(
        num_scalar_prefetch=5,
        grid=(n_blocks,),
        in_specs=[
            pl.BlockSpec((bm * ROW_TILES, LANES),
                         lambda i, be, fi, nx, nv, nr: (jnp.minimum(i, nr[0] - 1), 0)),
            pl.BlockSpec(memory_space=pl.ANY),
            pl.BlockSpec((1, 1, 2 * D_FF), lambda i, be, fi, nx, nv, nr: (be[i], 0, 0)),
            pl.BlockSpec(memory_space=pl.ANY),
            pl.BlockSpec((1, 1, D_MODEL), lambda i, be, fi, nx, nv, nr: (be[i], 0, 0)),
        ],
        out_specs=pl.BlockSpec((bm * ROW_TILES, LANES), lambda i, be, fi, nx, nv, nr: (i, 0)),
        scratch_shapes=[pltpu.VMEM((bm, D_MODEL), BF16),
                        pltpu.VMEM((bm, D_FF), BF16),
                        pltpu.VMEM((D_MODEL, 2 * D_FF), F32),
                        pltpu.VMEM((D_FF, D_MODEL), F32),
                        pltpu.VMEM((D_MODEL, 2 * D_FF), BF16),
                        pltpu.VMEM((D_FF, D_MODEL), BF16),
                        pltpu.SemaphoreType.DMA((2,))],
    )
    return pl.pallas_call(
        _experts_kernel,
        grid_spec=grid_spec,
        out_shape=jax.ShapeDtypeStruct(xs.shape, U32),
        compiler_params=pltpu.CompilerParams(
            dimension_semantics=("arbitrary",), vmem_limit_bytes=VMEM_LIMIT),
        name="experts",
    )(bexp, first, nxt, nval, nreal, xs, wgu, bgu3, wd, bd3)


def _combine_kernel(src_ref, dst_ref, len_ref, x1_ref, meta_ref, g_ref, ys_hbm, out_ref,
                    ybuf, ysort_ref, sem):
    i = pl.program_id(0)
    n = pl.num_programs(0)
    tb = SORT_TILE
    n_rows = TOP_K * tb
    slot = i % 2

    def start_runs(step, buf_slot):
        for u in range(COMBINE_TILES):
            def body(pair, carry, u=u):
                for queue in range(DMA_QUEUES):
                    g = (step * COMBINE_TILES + u) * N_EXPERTS + pair * DMA_QUEUES + queue
                    ln = pl.multiple_of(len_ref[g], ROW_TILES)

                    @pl.when(ln > 0)
                    def _(g=g, ln=ln, queue=queue):
                        src = pl.multiple_of(src_ref[g], ROW_TILES)
                        dst = pl.multiple_of(dst_ref[g], ROW_TILES)
                        pltpu.make_async_copy(ys_hbm.at[pl.ds(src, ln), :],
                                              ybuf.at[buf_slot, u, pl.ds(dst, ln), :],
                                              sem.at[buf_slot]).start(priority=queue)
                return carry
            lax.fori_loop(0, N_EXPERTS // DMA_QUEUES, body, 0)

    @pl.when(i == 0)
    def _():
        start_runs(0, 0)

    @pl.when(i + 1 < n)
    def _():
        start_runs(i + 1, 1 - slot)

    for u in range(COMBINE_TILES):
        pltpu.make_async_copy(ys_hbm.at[pl.ds(0, n_rows * ROW_TILES), :], ybuf.at[slot, u],
                              sem.at[slot]).wait()

    for u in range(COMBINE_TILES):
        rs = slice(u * tb, (u + 1) * tb)
        for j in range(ROW_TILES):
            lo, hi = _unpack_words(ybuf[slot, u, pl.ds(j, n_rows, stride=ROW_TILES), :])
            ysort_ref[u, :, j * LANES:(j + 1) * LANES] = lo
            ysort_ref[u, :, D_MODEL // 2 + j * LANES:D_MODEL // 2 + (j + 1) * LANES] = hi
        meta = meta_ref[rs, :]
        col = lax.broadcasted_iota(I32, (tb, n_rows), 1).astype(F32)
        weights = jnp.zeros((tb, n_rows), F32)
        for kk in range(TOP_K):
            weights = jnp.where(col == meta[:, 3 * TOP_K + kk:3 * TOP_K + kk + 1],
                                meta[:, TOP_K + kk:TOP_K + kk + 1], weights)
        acc = x1_ref[rs, :] + _dot(weights.astype(BF16), ysort_ref[u])
        out_ref[rs, :] = _rms(acc, g_ref[...])


def _combine_call(seg_src, seg_dst, seg_len, x1, meta, g, ys):
    t, d = x1.shape
    tb = SORT_TILE * COMBINE_TILES
    nb = t // tb
    grid_spec = pltpu.PrefetchScalarGridSpec(
        num_scalar_prefetch=3,
        grid=(nb,),
        in_specs=[pl.BlockSpec((tb, d), lambda i, a, b, c: (i, 0)),
                  pl.BlockSpec((tb, LANES), lambda i, a, b, c: (i, 0)),
                  pl.BlockSpec((1, d), lambda i, a, b, c: (0, 0)),
                  pl.BlockSpec(memory_space=pl.ANY)],
        out_specs=pl.BlockSpec((tb, d), lambda i, a, b, c: (i, 0)),
        scratch_shapes=[pltpu.VMEM((2, COMBINE_TILES, TOP_K * SORT_TILE * ROW_TILES, LANES), U32),
                        pltpu.VMEM((COMBINE_TILES, TOP_K * SORT_TILE, D_MODEL), BF16),
                        pltpu.SemaphoreType.DMA((2,))],
    )
    return pl.pallas_call(
        _combine_kernel,
        grid_spec=grid_spec,
        out_shape=jax.ShapeDtypeStruct((t, d), F32),
        compiler_params=pltpu.CompilerParams(
            dimension_semantics=("arbitrary",), vmem_limit_bytes=VMEM_LIMIT),
        name="combine",
    )(seg_src, seg_dst, seg_len, x1, meta, g, ys)


def _routing_tables(counts, t):
    bm = MOE_BLOCK
    n_blocks = t * TOP_K // bm + N_EXPERTS
    eids = jnp.arange(N_EXPERTS, dtype=I32)
    nblk_e = (counts + bm - 1) // bm
    blk_end = jnp.sum(jnp.where(eids[None, :] <= eids[:, None], nblk_e[None, :], 0), axis=1)
    blk_start = blk_end - nblk_e
    nreal = blk_end[N_EXPERTS - 1]
    pad_start = blk_start * bm
    blk = jnp.arange(n_blocks, dtype=I32)
    bexp = jnp.minimum(jnp.sum((blk_end[None, :] <= blk[:, None]).astype(I32), axis=1),
                       N_EXPERTS - 1)
    blk_is_e = bexp[:, None] == eids[None, :]
    pick = lambda tab: jnp.sum(jnp.where(blk_is_e, tab[None, :], 0), axis=1)
    first = (blk == pick(blk_start)).astype(I32)
    nxt_e = jnp.sum((blk_end[None, :] <= blk_end[:, None]).astype(I32), axis=1)
    nxt_e = jnp.where(blk_end < nreal, jnp.minimum(nxt_e, N_EXPERTS - 1), -1)
    nxt = pick(nxt_e)
    nval = jnp.clip(pick(counts) - (blk - pick(blk_start)) * bm, 0, bm)
    tail_blk = jnp.arange(N_EXPERTS, dtype=I32) + nreal
    pad_lo = jnp.concatenate([pad_start + counts, jnp.minimum(tail_blk, n_blocks - 1) * bm])
    pad_n = jnp.concatenate([nblk_e * bm - counts, jnp.where(tail_blk < n_blocks, bm, 0)])
    return (pad_start, bexp, first, nxt, nval, nreal.reshape(1).astype(I32), n_blocks,
            pad_lo * ROW_TILES, pad_n * ROW_TILES)


def kernel(x, norm_mix_g, w_in, conv_w, w_gk_up, b_gk_up, gla_norm_g, w_out, norm_ffn_g,
           w_router, b_router, w_gate_up, b_gate_up, w_down, b_down, norm_final_g):
    bsz, seq, d = x.shape
    t = bsz * seq
    assert w_in.shape[0] == 1, "single-layer trunk only"
    assert conv_w.shape[1] == CONV_K and ROUTE_ROWS % SUBLANES == 0
    l = 0
    d_in = w_in.shape[-1]
    win = jnp.pad(w_in[l].astype(BF16), ((0, 0), (0, D_IN_PAD - d_in)))
    wgk = jnp.pad(w_gk_up[l], ((0, LANES - GLA_RANK), (0, 0))).astype(BF16)
    wr = jnp.pad(w_router[l], ((0, 0), (0, LANES - N_EXPERTS)))
    wrh = wr.astype(BF16)
    wrhl = jnp.concatenate([wrh, (wr - wrh.astype(F32)).astype(BF16)], axis=1)
    br = jnp.pad(b_router[l], (0, LANES - N_EXPERTS), constant_values=NEG_BIG).reshape(1, LANES)

    x1, h2, meta, route, seg, cnt = _mixer_call(
        x, norm_mix_g[l].reshape(1, d), win, conv_w[l], wgk, b_gk_up[l].reshape(1, GLA_QK),
        gla_norm_g[l].reshape(1, GLA_DV), w_out[l].astype(BF16), norm_ffn_g[l].reshape(1, d),
        wrhl, br)

    meta2 = meta.reshape(t, LANES)
    counts = cnt[:, 0].astype(I32)
    (pad_start, bexp, first, nxt, nval, nreal, n_blocks, pad_lo,
     pad_n) = _routing_tables(counts, t)
    seg_src = ((pad_start[None, :] + seg[:, :, 0].astype(I32)) * ROW_TILES).reshape(-1)
    seg_len = (seg[:, :, 1].astype(I32) * ROW_TILES).reshape(-1)
    seg_dst = (seg[:, :, 2].astype(I32) * ROW_TILES).reshape(-1)
    xs = _dispatch_call(seg_src, seg_dst, seg_len, pad_lo, pad_n, route, h2.reshape(t, d),
                        n_blocks * MOE_BLOCK)
    ys = _experts_call(bexp, first, nxt, nval, nreal, xs, w_gate_up[l], b_gate_up[l], w_down[l],
                       b_down[l])
    out = _combine_call(seg_src, seg_dst, seg_len, x1.reshape(t, d), meta2,
                        norm_final_g.reshape(1, d), ys)
    return out.reshape(bsz, seq, d)
```

```python
import jax
import jax.numpy as jnp
from jax import lax
from jax.experimental import pallas as pl
from jax.experimental.pallas import tpu as pltpu

F32 = jnp.float32
BF16 = jnp.bfloat16
I32 = jnp.int32
U32 = jnp.uint32

D_MODEL = 1024
CONV_WIDTH = 512
CONV_K = 3
GLA_WIDTH = 512
GLA_HEADS = 4
GLA_DV = 128
GLA_DK = 64
GLA_QK = GLA_HEADS * GLA_DK
GLA_RANK = 16
GLA_NORMALIZER = 16.0
GLA_CHUNK = 64
N_EXPERTS = 32
TOP_K = 4
D_FF = 1024
SWIGLU_LIMIT = 7.0
SWIGLU_ALPHA = 1.702
RMS_EPS = 1e-5

LANES = 128
SUBLANES = 8
ROW_TILES = D_MODEL // LANES // 2
ROUTE_ROWS = 4 * TOP_K

OFF_UH = 0
OFF_GB = OFF_UH + CONV_WIDTH
OFF_GC = OFF_GB + CONV_WIDTH
OFF_Q = OFF_GC + CONV_WIDTH
OFF_K = OFF_Q + GLA_QK
OFF_V = OFF_K + GLA_QK
OFF_GO = OFF_V + GLA_WIDTH
OFF_GKL = OFF_GO + GLA_WIDTH
D_IN_PAD = OFF_GKL + LANES

SEQ_TILE = 512
SORT_TILE = 256
STEP_TILES = 4
COMBINE_TILES = 4
DMA_QUEUES = 2
MOE_BLOCK = 512
FF_CHUNK = 256
TAIL_ROWS = 128
CAST_ROWS = 128
NEG_BIG = -1e30
VMEM_LIMIT = 56 * 1024 * 1024


def _rms(x, g):
    return x * lax.rsqrt(jnp.mean(x * x, axis=-1, keepdims=True) + RMS_EPS) * g


def _dot(a, b):
    return jnp.dot(a, b, preferred_element_type=F32)


def _dot_nt(a, b):
    return lax.dot_general(a, b, (((1,), (1,)), ((), ())), preferred_element_type=F32)


def _pack_rows(x):
    half = x.shape[1] // 2
    xr = x.astype(BF16).astype(F32)
    lo = lax.bitcast_convert_type(xr[:, :half], U32) >> 16
    hi = lax.bitcast_convert_type(xr[:, half:], U32) & jnp.uint32(0xFFFF0000)
    return hi | lo


def _unpack_words(w):
    lo = lax.bitcast_convert_type(w << 16, F32).astype(BF16)
    hi = lax.bitcast_convert_type(w & jnp.uint32(0xFFFF0000), F32).astype(BF16)
    return lo, hi


def _split_bf16(x):
    hi = x.astype(BF16)
    lo = (x - hi.astype(F32)).astype(BF16)
    return hi, lo


def _mixer_kernel(x_ref, g1_ref, win_ref, convw_ref, wgk_ref, bgk_ref, gng_ref, wout_ref,
                  g2_ref, wrhl_ref, br_ref,
                  x1_ref, h2_ref, meta_ref, route_ref, seg_ref, cnt_ref,
                  proj_ref, ubuf_ref, la_ref, ycat_ref, state_ref, carry_ref):
    ts = x_ref.shape[1]
    b_idx = pl.program_id(0)
    s_idx = pl.program_id(1)

    @pl.when(s_idx == 0)
    def _():
        state_ref[...] = jnp.zeros_like(state_ref)
        ubuf_ref[0:SUBLANES, :] = jnp.zeros((SUBLANES, CONV_WIDTH), F32)

    @pl.when((s_idx == 0) & (b_idx == 0))
    def _():
        carry_ref[...] = jnp.zeros_like(carry_ref)

    x = x_ref[0]
    h = _rms(x, g1_ref[...]).astype(BF16)
    proj_ref[...] = _dot(h, win_ref[...])

    u = proj_ref[:, OFF_GC:OFF_GC + CONV_WIDTH] * proj_ref[:, OFF_UH:OFF_UH + CONV_WIDTH]
    ubuf_ref[SUBLANES:SUBLANES + ts, :] = u
    u1 = ubuf_ref[pl.ds(SUBLANES - 1, ts), :]
    u2 = ubuf_ref[pl.ds(SUBLANES - 2, ts), :]
    conv = convw_ref[0:1, :] * u2 + convw_ref[1:2, :] * u1 + convw_ref[2:3, :] * u
    ycat_ref[:, 0:CONV_WIDTH] = (proj_ref[:, OFF_GB:OFF_GB + CONV_WIDTH] * conv).astype(BF16)
    ubuf_ref[0:SUBLANES, :] = ubuf_ref[ts:ts + SUBLANES, :]

    gk = _dot(proj_ref[:, OFF_GKL:OFF_GKL + LANES].astype(BF16), wgk_ref[...]) + bgk_ref[...]
    log_sig = jnp.minimum(gk, 0.0) - jnp.log1p(jnp.exp(-jnp.abs(gk)))
    la_ref[...] = log_sig / GLA_NORMALIZER

    ci = lax.broadcasted_iota(I32, (GLA_CHUNK, GLA_CHUNK), 0)
    cj = lax.broadcasted_iota(I32, (GLA_CHUNK, GLA_CHUNK), 1)
    tri_incl = (cj <= ci).astype(BF16)
    causal = cj <= ci
    causal4 = jnp.concatenate([causal] * GLA_HEADS, axis=0)
    lane_qk = lax.broadcasted_iota(I32, (1, GLA_QK), 1)
    head_masks = [((lane_qk >= hd * GLA_DK) & (lane_qk < (hd + 1) * GLA_DK)).astype(F32)
                  for hd in range(GLA_HEADS)]
    gng = gng_ref[...]

    n_chunks = ts // GLA_CHUNK
    chunk_rows = [pl.ds(c * GLA_CHUNK, GLA_CHUNK) for c in range(n_chunks)]
    lane_c = lax.broadcasted_iota(I32, (GLA_QK, 2 * GLA_CHUNK), 1)
    qd_all, kd_all, kr_all, bl_all, v_all = [], [], [], [], []
    for rows in chunk_rows:
        la_hi, la_lo = _split_bf16(la_ref[rows, :])
        bcum = _dot(tri_incl, la_hi) + _dot(tri_incl, la_lo)
        blast = bcum[GLA_CHUNK - 1:GLA_CHUNK, :]
        q = proj_ref[rows, OFF_Q:OFF_Q + GLA_QK] * (GLA_DK ** -0.5)
        k = proj_ref[rows, OFF_K:OFF_K + GLA_QK]
        qd_all.append(q * jnp.exp(bcum))
        kd_all.append((k * jnp.exp(-bcum)).astype(BF16))
        kr_all.append(k * jnp.exp(blast - bcum))
        bl_all.append(blast)
        v_all.append(proj_ref[rows, OFF_V:OFF_V + GLA_WIDTH].astype(BF16))

    scores_all = []
    for c in range(n_chunks):
        q_stack = jnp.concatenate([qd_all[c] * m for m in head_masks], axis=0).astype(BF16)
        scores_all.append(
            jnp.where(causal4, _dot_nt(q_stack, kd_all[c]), 0.0).astype(BF16))
    o_intra_all = []
    for c in range(n_chunks):
        o_intra_all.append(jnp.concatenate(
            [_dot(scores_all[c][hd * GLA_CHUNK:(hd + 1) * GLA_CHUNK, :],
                  v_all[c][:, hd * GLA_DV:(hd + 1) * GLA_DV]) for hd in range(GLA_HEADS)], axis=1))
    kv_all, dcol_all = [], []
    for c in range(n_chunks):
        kt = jnp.concatenate(
            [kr_all[c], jnp.broadcast_to(bl_all[c], (GLA_CHUNK, GLA_QK))], axis=0).T
        dcol_all.append(jnp.exp(kt[:, GLA_CHUNK:GLA_CHUNK + 1]))
        kt_b = jnp.where(lane_c < GLA_CHUNK, kt, 0.0).astype(BF16)
        v_pad = jnp.concatenate([v_all[c], jnp.zeros_like(v_all[c])], axis=0)
        kv_all.append([_dot(kt_b[hd * GLA_DK:(hd + 1) * GLA_DK, :],
                            v_pad[:, hd * GLA_DV:(hd + 1) * GLA_DV]) for hd in range(GLA_HEADS)])

    o_all = []
    for c in range(n_chunks):
        state = state_ref[...]
        o_all.append(_dot(qd_all[c].astype(BF16), state.astype(BF16)) + o_intra_all[c])
        for hd in range(GLA_HEADS):
            rs = slice(hd * GLA_DK, (hd + 1) * GLA_DK)
            cs = slice(hd * GLA_DV, (hd + 1) * GLA_DV)
            state_ref[rs, cs] = dcol_all[c][rs, :] * state[rs, cs] + kv_all[c][hd]

    for c, rows in enumerate(chunk_rows):
        o = o_all[c]
        g_out = proj_ref[rows, OFF_GO:OFF_GO + GLA_WIDTH]
        o_n = jnp.concatenate(
            [_rms(o[:, hd * GLA_DV:(hd + 1) * GLA_DV], gng) for hd in range(GLA_HEADS)], axis=1)
        y = o_n * (g_out * jax.nn.sigmoid(g_out))
        ycat_ref[rows, CONV_WIDTH:CONV_WIDTH + GLA_WIDTH] = y.astype(BF16)

    x1 = x + _dot(ycat_ref[...], wout_ref[...])
    x1_ref[0] = x1
    h2 = _rms(x1, g2_ref[...])
    h2_hi, h2_lo = _split_bf16(h2)
    h2_ref[0] = h2_hi
    hi_terms = _dot(h2_hi, wrhl_ref[...])
    logits = (hi_terms[:, 0:LANES] + hi_terms[:, LANES:2 * LANES]
              + _dot(h2_lo, wrhl_ref[:, 0:LANES]) + br_ref[...])

    lt = logits.T[0:N_EXPERTS, :]
    erow = lax.broadcasted_iota(I32, (N_EXPERTS, ts), 0).astype(F32)
    work = lt
    sel = jnp.zeros((N_EXPERTS, ts), F32)
    top_v, top_i, top_oh = [], [], []
    for _ in range(TOP_K):
        m = jnp.max(work, axis=0, keepdims=True)
        idx = jnp.min(jnp.where(work == m, erow, float(N_EXPERTS)), axis=0, keepdims=True)
        oh = erow == idx
        top_v.append(m)
        top_i.append(idx)
        top_oh.append(oh)
        sel = sel + oh.astype(F32)
        work = jnp.where(oh, -jnp.inf, work)
    exps = [jnp.exp(tv - top_v[0]) for tv in top_v]
    denom = exps[0] + exps[1] + exps[2] + exps[3]
    gates = [e / denom for e in exps]

    tsrc = lax.broadcasted_iota(I32, (ts, ts), 0)
    tdst = lax.broadcasted_iota(I32, (ts, ts), 1)
    earlier = (tsrc < tdst).astype(BF16)
    local = _dot(sel.astype(BF16), earlier)
    carry = carry_ref[:, 0:1]
    ranks = [jnp.sum(jnp.where(oh, local + carry, 0.0), axis=0, keepdims=True) for oh in top_oh]

    ei = lax.broadcasted_iota(I32, (N_EXPERTS, N_EXPERTS), 0)
    ej = lax.broadcasted_iota(I32, (N_EXPERTS, N_EXPERTS), 1)
    lower_experts = (ej < ei).astype(BF16)
    lane_f = lax.broadcasted_iota(I32, (N_EXPERTS, LANES), 1)
    slot_base = []
    before = jnp.zeros((N_EXPERTS, 1), F32)
    for u in range(ts // SORT_TILE):
        sub_cnt = jnp.sum(sel[:, u * SORT_TILE:(u + 1) * SORT_TILE], axis=1, keepdims=True)
        seg_start = _dot(lower_experts,
                         jnp.broadcast_to(sub_cnt, (N_EXPERTS, LANES)).astype(BF16))[:, 0:1]
        seg_ref[u] = jnp.where(lane_f == 0, carry + before,
                               jnp.where(lane_f == 1, sub_cnt,
                                         jnp.where(lane_f == 2, seg_start, 0.0)))
        slot_base.append(jnp.broadcast_to(seg_start - before, (N_EXPERTS, SORT_TILE)))
        before = before + sub_cnt
    slot_base = jnp.concatenate(slot_base, axis=1)
    slots = [jnp.sum(jnp.where(oh, local + slot_base, 0.0), axis=0, keepdims=True)
             for oh in top_oh]
    new_carry = carry + before
    carry_ref[...] = jnp.broadcast_to(new_carry, carry_ref.shape)
    cnt_ref[...] = jnp.broadcast_to(new_carry, cnt_ref.shape)

    route = jnp.concatenate(top_i + gates + ranks + slots, axis=0)
    route_ref[...] = route
    meta_ref[0] = jnp.concatenate(
        [route, jnp.zeros((LANES - ROUTE_ROWS, ts), F32)], axis=0).T


def _mixer_call(x, g1, win, convw, wgk, bgk, gng, wout, g2, wrhl, br):
    bsz, seq, d = x.shape
    ts = SEQ_TILE
    grid = (bsz, seq // ts)

    def const(shape):
        return pl.BlockSpec(shape, lambda b, s: (0,) * len(shape))

    tile = lambda w: pl.BlockSpec((1, ts, w), lambda b, s: (b, s, 0))
    return pl.pallas_call(
        _mixer_kernel,
        grid=grid,
        in_specs=[tile(d), const(g1.shape), const(win.shape), const(convw.shape),
                  const(wgk.shape), const(bgk.shape), const(gng.shape), const(wout.shape),
                  const(g2.shape), const(wrhl.shape), const(br.shape)],
        out_specs=[tile(d),
                   tile(d),
                   tile(LANES),
                   pl.BlockSpec((ROUTE_ROWS, ts), lambda b, s: (0, b * (seq // ts) + s)),
                   pl.BlockSpec((ts // SORT_TILE, N_EXPERTS, LANES),
                                lambda b, s: (b * (seq // ts) + s, 0, 0)),
                   const((N_EXPERTS, LANES))],
        out_shape=[jax.ShapeDtypeStruct((bsz, seq, d), F32),
                   jax.ShapeDtypeStruct((bsz, seq, d), BF16),
                   jax.ShapeDtypeStruct((bsz, seq, LANES), F32),
                   jax.ShapeDtypeStruct((ROUTE_ROWS, bsz * seq), F32),
                   jax.ShapeDtypeStruct((bsz * seq // SORT_TILE, N_EXPERTS, LANES), F32),
                   jax.ShapeDtypeStruct((N_EXPERTS, LANES), F32)],
        scratch_shapes=[pltpu.VMEM((ts, D_IN_PAD), F32),
                        pltpu.VMEM((ts + SUBLANES, CONV_WIDTH), F32),
                        pltpu.VMEM((ts, GLA_QK), F32),
                        pltpu.VMEM((ts, D_MODEL), BF16),
                        pltpu.VMEM((GLA_QK, GLA_WIDTH), F32),
                        pltpu.VMEM((N_EXPERTS, LANES), F32)],
        compiler_params=pltpu.CompilerParams(
            dimension_semantics=("arbitrary", "arbitrary"), vmem_limit_bytes=VMEM_LIMIT),
        name="mixer",
    )(x, g1, win, convw, wgk, bgk, gng, wout, g2, wrhl, br)


def _dispatch_kernel(run_xs_ref, run_buf_ref, run_len_ref, pad_lo_ref, pad_n_ref,
                     route_ref, h2_ref, xs_hbm, sbuf, zbuf, sem, zsem):
    i = pl.program_id(0)
    n = pl.num_programs(0)
    td = SORT_TILE
    n_rows = TOP_K * td

    def zero_fill(wait):
        def fill(g, carry):
            ln = pl.multiple_of(pad_n_ref[g], ROW_TILES)

            @pl.when(ln > 0)
            def _():
                lo = pl.multiple_of(pad_lo_ref[g], ROW_TILES)
                cp = pltpu.make_async_copy(zbuf.at[pl.ds(0, ln), :], xs_hbm.at[pl.ds(lo, ln), :],
                                           zsem.at[0])
                if wait:
                    cp.wait()
                else:
                    cp.start()
            return carry

        lax.fori_loop(0, pad_lo_ref.shape[0], fill, 0)

    def wait_runs(buf_slot):
        pltpu.make_async_copy(sbuf.at[buf_slot], xs_hbm.at[pl.ds(0, n_rows * ROW_TILES), :],
                              sem.at[buf_slot]).wait()

    @pl.when(i == 0)
    def _():
        zbuf[...] = jnp.zeros_like(zbuf)
        zero_fill(wait=False)

    base = (i % 2) * STEP_TILES

    @pl.when(i >= 2)
    def _():
        for u in range(STEP_TILES):
            wait_runs(base + u)

    for u in range(STEP_TILES):
        route = route_ref[:, u * td:(u + 1) * td]
        row = lax.broadcasted_iota(I32, (n_rows, td), 0).astype(F32)
        pick = jnp.zeros((n_rows, td), F32)
        for kk in range(TOP_K):
            pick = jnp.where(row == route[3 * TOP_K + kk:3 * TOP_K + kk + 1, :], 1.0, pick)
        rows_sorted = _dot(pick.astype(BF16), h2_ref[u * td:(u + 1) * td, :])
        words = _pack_rows(rows_sorted)
        for j in range(ROW_TILES):
            sbuf[base + u, pl.ds(j, n_rows, stride=ROW_TILES), :] = words[:, j * LANES:(j + 1) * LANES]

    for u in range(STEP_TILES):
        def body(pair, carry, u=u):
            for queue in range(DMA_QUEUES):
                g = (i * STEP_TILES + u) * N_EXPERTS + pair * DMA_QUEUES + queue
                ln = pl.multiple_of(run_len_ref[g], ROW_TILES)

                @pl.when(ln > 0)
                def _(g=g, ln=ln, queue=queue):
                    src = pl.multiple_of(run_buf_ref[g], ROW_TILES)
                    dst = pl.multiple_of(run_xs_ref[g], ROW_TILES)
                    pltpu.make_async_copy(sbuf.at[base + u, pl.ds(src, ln), :],
                                          xs_hbm.at[pl.ds(dst, ln), :],
                                          sem.at[base + u]).start(priority=queue)
            return carry

        lax.fori_loop(0, N_EXPERTS // DMA_QUEUES, body, 0)

    @pl.when(i == n - 1)
    def _():
        for u in range(STEP_TILES):
            wait_runs(base + u)

        @pl.when(i >= 1)
        def _():
            for u in range(STEP_TILES):
                wait_runs(STEP_TILES - base + u)
        zero_fill(wait=True)


def _dispatch_call(run_xs, run_buf, run_len, pad_lo, pad_n, route, h2, n_rows):
    t, d = h2.shape
    td = SORT_TILE * STEP_TILES
    grid_spec = pltpu.PrefetchScalarGridSpec(
        num_scalar_prefetch=5,
        grid=(t // td,),
        in_specs=[pl.BlockSpec((ROUTE_ROWS, td), lambda i, *_: (0, i)),
                  pl.BlockSpec((td, d), lambda i, *_: (i, 0))],
        out_specs=pl.BlockSpec(memory_space=pl.ANY),
        scratch_shapes=[pltpu.VMEM((2 * STEP_TILES, TOP_K * SORT_TILE * ROW_TILES, LANES), U32),
                        pltpu.VMEM((MOE_BLOCK * ROW_TILES, LANES), U32),
                        pltpu.SemaphoreType.DMA((2 * STEP_TILES,)),
                        pltpu.SemaphoreType.DMA((1,))],
    )
    return pl.pallas_call(
        _dispatch_kernel,
        grid_spec=grid_spec,
        out_shape=jax.ShapeDtypeStruct((n_rows * ROW_TILES, LANES), U32),
        compiler_params=pltpu.CompilerParams(
            dimension_semantics=("arbitrary",), vmem_limit_bytes=VMEM_LIMIT,
            has_side_effects=True),
        name="dispatch",
    )(run_xs, run_buf, run_len, pad_lo, pad_n, route, h2)


def _experts_kernel(bexp_ref, first_ref, nxt_ref, nval_ref, nreal_ref,
                    xs_ref, wgu_hbm, bgu_ref, wd_hbm, bd_ref,
                    ys_ref,
                    xb_ref, act_ref, wgu_stage, wd_stage, wgu_bf, wd_bf, wsem):
    i = pl.program_id(0)
    nreal = nreal_ref[0]
    bm = xb_ref.shape[0]

    def weight_copies(e):
        return (pltpu.make_async_copy(wgu_hbm.at[e], wgu_stage, wsem.at[0]),
                pltpu.make_async_copy(wd_hbm.at[e], wd_stage, wsem.at[1]))

    @pl.when(i >= nreal)
    def _():
        ys_ref[...] = jnp.zeros_like(ys_ref)

    @pl.when(i < nreal)
    def _():
        e = bexp_ref[i]

        @pl.when(i == 0)
        def _():
            for cp in weight_copies(e):
                cp.start(priority=1)

        @pl.when(first_ref[i] == 1)
        def _():
            for cp in weight_copies(e):
                cp.wait()

            def cast_gu(r, carry):
                rows = pl.ds(pl.multiple_of(r * CAST_ROWS, CAST_ROWS), CAST_ROWS)
                wgu_bf[rows, :] = wgu_stage[rows, :].astype(BF16)
                return carry

            def cast_d(r, carry):
                rows = pl.ds(pl.multiple_of(r * CAST_ROWS, CAST_ROWS), CAST_ROWS)
                wd_bf[rows, :] = wd_stage[rows, :].astype(BF16)
                return carry

            lax.fori_loop(0, D_MODEL // CAST_ROWS, cast_gu, 0)
            lax.fori_loop(0, D_FF // CAST_ROWS, cast_d, 0)

            @pl.when(nxt_ref[i] >= 0)
            def _():
                for cp in weight_copies(nxt_ref[i]):
                    cp.start(priority=1)

        def mlp(rows):
            for j in range(ROW_TILES):
                lo, hi = _unpack_words(xs_ref[pl.ds(j, rows, stride=ROW_TILES), :])
                xb_ref[0:rows, j * LANES:(j + 1) * LANES] = lo
                xb_ref[0:rows, D_MODEL // 2 + j * LANES:D_MODEL // 2 + (j + 1) * LANES] = hi
            for c in range(D_FF // FF_CHUNK):
                f0 = c * FF_CHUNK
                xb = xb_ref[0:rows, :]
                gate = _dot(xb, wgu_bf[:, f0:f0 + FF_CHUNK]) + bgu_ref[0, :, f0:f0 + FF_CHUNK]
                up = (_dot(xb, wgu_bf[:, D_FF + f0:D_FF + f0 + FF_CHUNK])
                      + bgu_ref[0, :, D_FF + f0:D_FF + f0 + FF_CHUNK])
                gate = jnp.minimum(gate, SWIGLU_LIMIT)
                up = jnp.clip(up, -SWIGLU_LIMIT, SWIGLU_LIMIT)
                glu = gate * jax.nn.sigmoid(gate * SWIGLU_ALPHA)
                act_ref[0:rows, f0:f0 + FF_CHUNK] = ((up + 1.0) * glu).astype(BF16)
            out = _pack_rows(_dot(act_ref[0:rows, :], wd_bf[...]) + bd_ref[0])
            for j in range(ROW_TILES):
                ys_ref[pl.ds(j, rows, stride=ROW_TILES), :] = out[:, j * LANES:(j + 1) * LANES]

        n_routed = nval_ref[i]
        for rows in range(TAIL_ROWS, bm + 1, TAIL_ROWS):
            @pl.when((n_routed > rows - TAIL_ROWS) & (n_routed <= rows))
            def _(rows=rows):
                mlp(rows)
                if rows < bm:
                    ys_ref[rows * ROW_TILES:bm * ROW_TILES, :] = jnp.zeros(
                        ((bm - rows) * ROW_TILES, LANES), U32)


def _experts_call(bexp, first, nxt, nval, nreal, xs, wgu, bgu, wd, bd):
    bm = MOE_BLOCK
    n_blocks = xs.shape[0] // (bm * ROW_TILES)
    bgu3 = bgu.reshape(N_EXPERTS, 1, 2 * D_FF)
    bd3 = bd.reshape(N_EXPERTS, 1, D_MODEL)
    grid_spec = pltpu.PrefetchScalarGridSpec(
        num_scalar_prefetch=5,
        grid=(n_blocks,),
        in_specs=[
            pl.BlockSpec((bm * ROW_TILES, LANES),
                         lambda i, be, fi, nx, nv, nr: (jnp.minimum(i, nr[0] - 1), 0)),
            pl.BlockSpec(memory_space=pl.ANY),
            pl.BlockSpec((1, 1, 2 * D_FF), lambda i, be, fi, nx, nv, nr: (be[i], 0, 0)),
            pl.BlockSpec(memory_space=pl.ANY),
            pl.BlockSpec((1, 1, D_MODEL), lambda i, be, fi, nx, nv, nr: (be[i], 0, 0)),
        ],
        out_specs=pl.BlockSpec((bm * ROW_TILES, LANES), lambda i, be, fi, nx, nv, nr: (i, 0)),
        scratch_shapes=[pltpu.VMEM((bm, D_MODEL), BF16),
                        pltpu.VMEM((bm, D_FF), BF16),
                        pltpu.VMEM((D_MODEL, 2 * D_FF), F32),
                        pltpu.VMEM((D_FF, D_MODEL), F32),
                        pltpu.VMEM((D_MODEL, 2 * D_FF), BF16),
                        pltpu.VMEM((D_FF, D_MODEL), BF16),
                        pltpu.SemaphoreType.DMA((2,))],
    )
    return pl.pallas_call(
        _experts_kernel,
        grid_spec=grid_spec,
        out_shape=jax.ShapeDtypeStruct(xs.shape, U32),
        compiler_params=pltpu.CompilerParams(
            dimension_semantics=("arbitrary",), vmem_limit_bytes=VMEM_LIMIT),
        name="experts",
    )(bexp, first, nxt, nval, nreal, xs, wgu, bgu3, wd, bd3)


def _combine_kernel(src_ref, dst_ref, len_ref, x1_ref, meta_ref, g_ref, ys_hbm, out_ref,
                    ybuf, ysort_ref, sem):
    i = pl.program_id(0)
    n = pl.num_programs(0)
    tb = SORT_TILE
    n_rows = TOP_K * tb
    slot = i % 2

    def start_runs(step, buf_slot):
        for u in range(COMBINE_TILES):
            def body(pair, carry, u=u):
                for half in range(DMA_QUEUES):
                    queue = DMA_QUEUES - 1
                    g = (step * COMBINE_TILES + u) * N_EXPERTS + pair * DMA_QUEUES + half
                    ln = pl.multiple_of(len_ref[g], ROW_TILES)

                    @pl.when(ln > 0)
                    def _(g=g, ln=ln, queue=queue):
                        src = pl.multiple_of(src_ref[g], ROW_TILES)
                        dst = pl.multiple_of(dst_ref[g], ROW_TILES)
                        pltpu.make_async_copy(ys_hbm.at[pl.ds(src, ln), :],
                                              ybuf.at[buf_slot, u, pl.ds(dst, ln), :],
                                              sem.at[buf_slot]).start(priority=queue)
                return carry
            lax.fori_loop(0, N_EXPERTS // DMA_QUEUES, body, 0)

    @pl.when(i == 0)
    def _():
        start_runs(0, 0)

    @pl.when(i + 1 < n)
    def _():
        start_runs(i + 1, 1 - slot)

    for u in range(COMBINE_TILES):
        pltpu.make_async_copy(ys_hbm.at[pl.ds(0, n_rows * ROW_TILES), :], ybuf.at[slot, u],
                              sem.at[slot]).wait()

    for u in range(COMBINE_TILES):
        rs = slice(u * tb, (u + 1) * tb)
        for j in range(ROW_TILES):
            lo, hi = _unpack_words(ybuf[slot, u, pl.ds(j, n_rows, stride=ROW_TILES), :])
            ysort_ref[u, :, j * LANES:(j + 1) * LANES] = lo
            ysort_ref[u, :, D_MODEL // 2 + j * LANES:D_MODEL // 2 + (j + 1) * LANES] = hi
        meta = meta_ref[rs, :]
        col = lax.broadcasted_iota(I32, (tb, n_rows), 1).astype(F32)
        weights = jnp.zeros((tb, n_rows), F32)
        for kk in range(TOP_K):
            weights = jnp.where(col == meta[:, 3 * TOP_K + kk:3 * TOP_K + kk + 1],
                                meta[:, TOP_K + kk:TOP_K + kk + 1], weights)
        acc = x1_ref[rs, :] + _dot(weights.astype(BF16), ysort_ref[u])
        out_ref[rs, :] = _rms(acc, g_ref[...])


def _combine_call(seg_src, seg_dst, seg_len, x1, meta, g, ys):
    t, d = x1.shape
    tb = SORT_TILE * COMBINE_TILES
    nb = t // tb
    grid_spec = pltpu.PrefetchScalarGridSpec(
        num_scalar_prefetch=3,
        grid=(nb,),
        in_specs=[pl.BlockSpec((tb, d), lambda i, a, b, c: (i, 0)),
                  pl.BlockSpec((tb, LANES), lambda i, a, b, c: (i, 0)),
                  pl.BlockSpec((1, d), lambda i, a, b, c: (0, 0)),
                  pl.BlockSpec(memory_space=pl.ANY)],
        out_specs=pl.BlockSpec((tb, d), lambda i, a, b, c: (i, 0)),
        scratch_shapes=[pltpu.VMEM((2, COMBINE_TILES, TOP_K * SORT_TILE * ROW_TILES, LANES), U32),
                        pltpu.VMEM((COMBINE_TILES, TOP_K * SORT_TILE, D_MODEL), BF16),
                        pltpu.SemaphoreType.DMA((2,))],
    )
    return pl.pallas_call(
        _combine_kernel,
        grid_spec=grid_spec,
        out_shape=jax.ShapeDtypeStruct((t, d), F32),
        compiler_params=pltpu.CompilerParams(
            dimension_semantics=("arbitrary",), vmem_limit_bytes=VMEM_LIMIT),
        name="combine",
    )(seg_src, seg_dst, seg_len, x1, meta, g, ys)


def _routing_tables(counts, t):
    bm = MOE_BLOCK
    n_blocks = t * TOP_K // bm + N_EXPERTS
    eids = jnp.arange(N_EXPERTS, dtype=I32)
    nblk_e = (counts + bm - 1) // bm
    blk_end = jnp.sum(jnp.where(eids[None, :] <= eids[:, None], nblk_e[None, :], 0), axis=1)
    blk_start = blk_end - nblk_e
    nreal = blk_end[N_EXPERTS - 1]
    pad_start = blk_start * bm
    blk = jnp.arange(n_blocks, dtype=I32)
    bexp = jnp.minimum(jnp.sum((blk_end[None, :] <= blk[:, None]).astype(I32), axis=1),
                       N_EXPERTS - 1)
    blk_is_e = bexp[:, None] == eids[None, :]
    pick = lambda tab: jnp.sum(jnp.where(blk_is_e, tab[None, :], 0), axis=1)
    first = (blk == pick(blk_start)).astype(I32)
    nxt_e = jnp.sum((blk_end[None, :] <= blk_end[:, None]).astype(I32), axis=1)
    nxt_e = jnp.where(blk_end < nreal, jnp.minimum(nxt_e, N_EXPERTS - 1), -1)
    nxt = pick(nxt_e)
    nval = jnp.clip(pick(counts) - (blk - pick(blk_start)) * bm, 0, bm)
    tail_blk = jnp.arange(N_EXPERTS, dtype=I32) + nreal
    pad_lo = jnp.concatenate([pad_start + counts, jnp.minimum(tail_blk, n_blocks - 1) * bm])
    pad_n = jnp.concatenate([nblk_e * bm - counts, jnp.where(tail_blk < n_blocks, bm, 0)])
    return (pad_start, bexp, first, nxt, nval, nreal.reshape(1).astype(I32), n_blocks,
            pad_lo * ROW_TILES, pad_n * ROW_TILES)


def kernel(x, norm_mix_g, w_in, conv_w, w_gk_up, b_gk_up, gla_norm_g, w_out, norm_ffn_g,
           w_router, b_router, w_gate_up, b_gate_up, w_down, b_down, norm_final_g):
    bsz, seq, d = x.shape
    t = bsz * seq
    assert w_in.shape[0] == 1, "single-layer trunk only"
    assert conv_w.shape[1] == CONV_K and ROUTE_ROWS % SUBLANES == 0
    l = 0
    d_in = w_in.shape[-1]
    win = jnp.pad(w_in[l].astype(BF16), ((0, 0), (0, D_IN_PAD - d_in)))
    wgk = jnp.pad(w_gk_up[l], ((0, LANES - GLA_RANK), (0, 0))).astype(BF16)
    wr = jnp.pad(w_router[l], ((0, 0), (0, LANES - N_EXPERTS)))
    wrh = wr.astype(BF16)
    wrhl = jnp.concatenate([wrh, (wr - wrh.astype(F32)).astype(BF16)], axis=1)
    br = jnp.pad(b_router[l], (0, LANES - N_EXPERTS), constant_values=NEG_BIG).reshape(1, LANES)

    x1, h2, meta, route, seg, cnt = _mixer_call(
        x, norm_mix_g[l].reshape(1, d), win, conv_w[l], wgk, b_gk_up[l].reshape(1, GLA_QK),
        gla_norm_g[l].reshape(1, GLA_DV), w_out[l].astype(BF16), norm_ffn_g[l].reshape(1, d),
        wrhl, br)

    meta2 = meta.reshape(t, LANES)
    counts = cnt[:, 0].astype(I32)
    (pad_start, bexp, first, nxt, nval, nreal, n_blocks, pad_lo,
     pad_n) = _routing_tables(counts, t)
    seg_src = ((pad_start[None, :] + seg[:, :, 0].astype(I32)) * ROW_TILES).reshape(-1)
    seg_len = (seg[:, :, 1].astype(I32) * ROW_TILES).reshape(-1)
    seg_dst = (seg[:, :, 2].astype(I32) * ROW_TILES).reshape(-1)
    xs = _dispatch_call(seg_src, seg_dst, seg_len, pad_lo, pad_n, route, h2.reshape(t, d),
                        n_blocks * MOE_BLOCK)
    ys = _experts_call(bexp, first, nxt, nval, nreal, xs, w_gate_up[l], b_gate_up[l], w_down[l],
                       b_down[l])
    out = _combine_call(seg_src, seg_dst, seg_len, x1.reshape(t, d), meta2,
                        norm_final_g.reshape(1, d), ys)
    return out.reshape(bsz, seq, d)
```
